```python
import math
import jax, jax.numpy as jnp
from jax import lax
import numpy as np

D_MODEL = 1024
BATCH = 2
SEQ = 16384
DEPTH = 2

N_A_LAYERS = DEPTH // 2
N_B_LAYERS = DEPTH - N_A_LAYERS

SSD_EXPAND = 2
SSD_D_INNER = SSD_EXPAND * D_MODEL
SSD_HEADDIM = 64
SSD_HEADS = SSD_D_INNER // SSD_HEADDIM
SSD_GROUPS = 8
SSD_HEADS_PER_GROUP = SSD_HEADS // SSD_GROUPS
SSD_STATE = 128
SSD_CONV = 4
SSD_CHUNK = 128
SSD_CONV_DIM = SSD_D_INNER + 2 * SSD_GROUPS * SSD_STATE
SSD_IN_DIM = SSD_D_INNER + SSD_CONV_DIM + SSD_HEADS

NSA_HEADS = 16
NSA_KV_HEADS = 4
NSA_GROUP = NSA_HEADS // NSA_KV_HEADS
NSA_HEAD_DIM = 64
CMP_BLOCK = 32
CMP_STRIDE = 16
CMP_HIDDEN = 256
SLC_BLOCK = 64
SLC_TOPN = 16
SLC_FORCE = 1e4
WINDOW = 512
Q_BLOCK = 128
NSA_IN_DIM = NSA_HEADS * NSA_HEAD_DIM + 3 * NSA_HEADS
KV_DIM = 6 * NSA_KV_HEADS * NSA_HEAD_DIM

REL_BUCKETS = 32
REL_MAX_DISTANCE = 4096

PEER_HEADS = 8
PEER_N_KEYS = 128
PEER_EXPERTS = PEER_N_KEYS * PEER_N_KEYS
PEER_QDIM = 256
PEER_TOPK = 16
PEER_TOKEN_CHUNK = 128
PEER_V_SCALE = PEER_HEADS ** -0.5

PLE_DIM = 256
NORM_EPS = 1e-6
NEG_INF = -1e30

kernel_name = 'hybrid_ssd_nsa_peer'


def rms_norm(x, g):
    xf = x.astype(jnp.float32)
    y = xf * lax.rsqrt(jnp.mean(xf * xf, axis=-1, keepdims=True) + NORM_EPS)
    return (y * g.astype(jnp.float32)).astype(x.dtype)


def masked_softmax(s, mask):
    s = jnp.where(mask, s.astype(jnp.float32), NEG_INF)
    p = jax.nn.softmax(s, axis=-1)
    return jnp.where(mask, p, 0.0)


def rel_bucket(dist):
    dist = jnp.maximum(dist, 0)
    max_exact = REL_BUCKETS // 2
    d = jnp.maximum(dist, 1).astype(jnp.float32)
    large = max_exact + (jnp.log(d / max_exact) / math.log(REL_MAX_DISTANCE / max_exact)
                         * (REL_BUCKETS - max_exact)).astype(jnp.int32)
    large = jnp.minimum(large, REL_BUCKETS - 1)
    return jnp.where(dist < max_exact, dist, large)


def ssd_mixer(u, w_in, conv_w, conv_b, dt_bias, a_log, d_skip, gnorm_g, w_out):
    b, s, _ = u.shape
    G, R, P, N, L = SSD_GROUPS, SSD_HEADS_PER_GROUP, SSD_HEADDIM, SSD_STATE, SSD_CHUNK
    nc = s // L
    zxbcdt = u @ w_in
    z = zxbcdt[..., :SSD_D_INNER]
    xbc = zxbcdt[..., SSD_D_INNER:SSD_D_INNER + SSD_CONV_DIM]
    dt_raw = zxbcdt[..., SSD_D_INNER + SSD_CONV_DIM:]
    xbc = lax.conv_general_dilated(xbc, conv_w[:, None, :], window_strides=(1,),
                                   padding=[(SSD_CONV - 1, 0)],
                                   dimension_numbers=('NWC', 'WIO', 'NWC'),
                                   feature_group_count=SSD_CONV_DIM) + conv_b
    xbc = jax.nn.silu(xbc)
    x = xbc[..., :SSD_D_INNER].reshape(b, nc, L, G, R, P)
    bm = xbc[..., SSD_D_INNER:SSD_D_INNER + G * N].reshape(b, nc, L, G, N)
    cm = xbc[..., SSD_D_INNER + G * N:].reshape(b, nc, L, G, N)
    dt = jax.nn.softplus((dt_raw + dt_bias).astype(jnp.float32)).reshape(b, nc, L, G, R)
    a = -jnp.exp(a_log.astype(jnp.float32)).reshape(G, R)
    xd = x * dt[..., None].astype(x.dtype)
    a_dt = jnp.transpose(dt * a, (0, 3, 4, 1, 2))
    a_cs = jnp.cumsum(a_dt, axis=-1)
    causal = jnp.tril(jnp.ones((L, L), dtype=bool))
    seg = a_cs[..., :, None] - a_cs[..., None, :]
    decay_in = jnp.exp(jnp.where(causal, seg, -jnp.inf)).astype(x.dtype)
    cb = jnp.einsum('bclgn,bcsgn->bgcls', cm, bm)
    y_diag = jnp.einsum('bgrcls,bcsgrp->bclgrp', cb[:, :, None] * decay_in, xd)
    decay_to_end = jnp.exp(a_cs[..., -1:] - a_cs).astype(x.dtype)
    states = jnp.einsum('bcsgn,bgrcs,bcsgrp->bcgrpn', bm, decay_to_end, xd)
    chunk_decay = jnp.exp(a_cs[..., -1]).astype(x.dtype)

    def step(h, inp):
        st, dec = inp
        return h * dec[..., None, None] + st, h

    h0 = jnp.zeros((b, G, R, P, N), x.dtype)
    _, prev = lax.scan(step, h0, (jnp.moveaxis(states, 1, 0), jnp.moveaxis(chunk_decay, -1, 0)))
    prev = jnp.moveaxis(prev, 0, 1)
    y_off = jnp.einsum('bclgn,bcgrpn,bgrcl->bclgrp', cm, prev, jnp.exp(a_cs).astype(x.dtype))
    y = (y_diag + y_off + x * d_skip.reshape(G, R, 1)).reshape(b, s, SSD_D_INNER)
    yg = (y * jax.nn.silu(z)).reshape(b, s, G, SSD_D_INNER // G)
    yg = rms_norm(yg, gnorm_g.reshape(G, -1)).reshape(b, s, SSD_D_INNER)
    return yg @ w_out


def compress_blocks(t, pos, w1, w2):
    b, g, s, d = t.shape
    halves = t.reshape(b, g, s // CMP_STRIDE, CMP_STRIDE, d)
    blocks = jnp.concatenate([halves[:, :, :-1], halves[:, :, 1:]], axis=3) + pos
    flat = blocks.reshape(b, g, -1, CMP_BLOCK * d)
    return jax.nn.gelu(flat @ w1) @ w2


def nsa_shared_kv(h, kv_norm_g, w_kv, cmp_pos_k, cmp_pos_v, cmp_w1_k, cmp_w2_k, cmp_w1_v, cmp_w2_v):
    b, s, _ = h.shape
    kv = rms_norm(h, kv_norm_g) @ w_kv
    kv = kv.reshape(b, s, 6, NSA_KV_HEADS, NSA_HEAD_DIM).transpose(2, 0, 3, 1, 4)
    k_c, v_c, k_s, v_s, k_w, v_w = kv[0], kv[1], kv[2], kv[3], kv[4], kv[5]
    k_cmp = compress_blocks(k_c, cmp_pos_k, cmp_w1_k, cmp_w2_k)
    v_cmp = compress_blocks(v_c, cmp_pos_v, cmp_w1_v, cmp_w2_v)
    nsel = s // SLC_BLOCK
    k_slc = k_s.reshape(b, NSA_KV_HEADS, nsel, SLC_BLOCK, NSA_HEAD_DIM)
    v_slc = v_s.reshape(b, NSA_KV_HEADS, nsel, SLC_BLOCK, NSA_HEAD_DIM)
    pad = ((0, 0), (0, 0), (WINDOW, 0), (0, 0))
    return (k_cmp, v_cmp, k_slc, v_slc, jnp.pad(k_w, pad), jnp.pad(v_w, pad))


def nsa_mixer(u, w_in, w_out, rel_bias, k_cmp, v_cmp, k_slc, v_slc, k_win, v_win):
    b, s, _ = u.shape
    H, G, R, DH = NSA_HEADS, NSA_KV_HEADS, NSA_GROUP, NSA_HEAD_DIM
    nb = s // Q_BLOCK
    proj = u @ w_in
    q = proj[..., :H * DH].reshape(b, nb, Q_BLOCK, G, R, DH).transpose(1, 0, 3, 4, 2, 5)
    gates = jax.nn.sigmoid(proj[..., H * DH:]).reshape(b, nb, Q_BLOCK, G, R, 3).transpose(1, 0, 3, 4, 2, 5)
    n_cmp = k_cmp.shape[2]
    n_sel = k_slc.shape[2]
    topn = min(SLC_TOPN, n_sel)
    cmp_end = jnp.arange(n_cmp) * CMP_STRIDE + CMP_BLOCK - 1
    n_span = CMP_BLOCK // CMP_STRIDE
    per = SLC_BLOCK // CMP_STRIDE
    raw = np.arange(n_sel)[:, None] * per + np.arange(-(n_span - 1), per)[None, :]
    imp_idx = jnp.asarray(np.clip(raw, 0, n_cmp - 1), jnp.int32)
    imp_valid = jnp.asarray((raw >= 0) & (raw < n_cmp), jnp.float32)
    scale = DH ** -0.5
    bidx = jnp.arange(b)[:, None, None, None]
    gidx = jnp.arange(G)[None, :, None, None]
    table_g = rel_bias.reshape(REL_BUCKETS, G, R).transpose(1, 0, 2)
    blk = jnp.arange(n_sel)

    def head_bias(dist):
        bb = rel_bias[rel_bucket(dist)].reshape(dist.shape + (G, R))
        return jnp.moveaxis(bb, (-2, -1), (0, 1))

    def block(args):
        qb_i, q_blk, g_blk = args
        q0 = qb_i * Q_BLOCK
        t = q0 + jnp.arange(Q_BLOCK)
        s_c = jnp.einsum('bgrqd,bgkd->bgrqk', q_blk, k_cmp) * scale + head_bias(t[:, None] - cmp_end[None, :])
        p_c = masked_softmax(s_c, cmp_end[None, :] <= t[:, None])
        o_c = jnp.einsum('bgrqk,bgkd->bgrqd', p_c.astype(v_cmp.dtype), v_cmp)
        imp = jnp.sum(jnp.take(p_c, imp_idx, axis=-1) * imp_valid, axis=(2, -1))
        cur = t // SLC_BLOCK
        valid_blk = blk[None, :] <= cur[:, None]
        forced = (blk[None, :] == 0) | (blk[None, :] == cur[:, None]) | (blk[None, :] == cur[:, None] - 1)
        score = jnp.where(forced, SLC_FORCE, jnp.where(valid_blk, imp, -SLC_FORCE))
        _, sel = lax.top_k(score, topn)
        k_sel = k_slc[bidx, gidx, sel]
        v_sel = v_slc[bidx, gidx, sel]
        kpos = sel[..., None] * SLC_BLOCK + jnp.arange(SLC_BLOCK)
        dist = t[:, None, None] - kpos
        bias_s = table_g[gidx[..., None], rel_bucket(dist)]
        s_s = jnp.einsum('bgrqd,bgqnjd->bgrqnj', q_blk, k_sel) * scale + jnp.moveaxis(bias_s, -1, 2)
        s_s = s_s.reshape(b, G, R, Q_BLOCK, topn * SLC_BLOCK)
        p_s = masked_softmax(s_s, (dist >= 0).reshape(b, G, 1, Q_BLOCK, topn * SLC_BLOCK))
        o_s = jnp.einsum('bgrqm,bgqmd->bgrqd', p_s.astype(v_sel.dtype),
                         v_sel.reshape(b, G, Q_BLOCK, topn * SLC_BLOCK, DH))
        k_w = lax.dynamic_slice_in_dim(k_win, q0, WINDOW + Q_BLOCK, axis=2)
        v_w = lax.dynamic_slice_in_dim(v_win, q0, WINDOW + Q_BLOCK, axis=2)
        kpos_w = q0 - WINDOW + jnp.arange(WINDOW + Q_BLOCK)
        dist_w = t[:, None] - kpos_w[None, :]
        mask_w = (kpos_w[None, :] >= 0) & (dist_w >= 0) & (dist_w < WINDOW)
        s_w = jnp.einsum('bgrqd,bgkd->bgrqk', q_blk, k_w) * scale + head_bias(dist_w)
        p_w = masked_softmax(s_w, mask_w)
        o_w = jnp.einsum('bgrqk,bgkd->bgrqd', p_w.astype(v_w.dtype), v_w)
        return g_blk[..., 0:1] * o_c + g_blk[..., 1:2] * o_s + g_blk[..., 2:3] * o_w

    o = lax.map(block, (jnp.arange(nb), q, gates))
    o = jnp.transpose(o, (1, 0, 4, 2, 3, 5)).reshape(b, s, H * DH)
    return o @ w_out


def peer_ffn(u, w_q, keys1, keys2, u_tab, v_tab):
    b, s, d = u.shape
    T = b * s
    xt = u.reshape(T, d)
    q = (xt @ w_q).reshape(T, PEER_HEADS, 2, PEER_QDIM // 2)
    s1 = jnp.einsum('thd,hkd->thk', q[:, :, 0], keys1)
    s2 = jnp.einsum('thd,hkd->thk', q[:, :, 1], keys2)
    v1, i1 = lax.top_k(s1, PEER_TOPK)
    v2, i2 = lax.top_k(s2, PEER_TOPK)
    cand = (v1[..., :, None] + v2[..., None, :]).reshape(T, PEER_HEADS, PEER_TOPK * PEER_TOPK)
    cand_id = (i1[..., :, None] * PEER_N_KEYS + i2[..., None, :]).reshape(T, PEER_HEADS, PEER_TOPK * PEER_TOPK)
    top_s, pos = lax.top_k(cand, PEER_TOPK)
    ids = jnp.take_along_axis(cand_id, pos, axis=-1)
    gate = jax.nn.softmax(top_s.astype(jnp.float32), axis=-1).astype(u.dtype)
    nch = T // PEER_TOKEN_CHUNK

    def chunk(args):
        xc, idc, gc = args
        act = jax.nn.gelu(jnp.einsum('cd,chkd->chk', xc, u_tab[idc]))
        return jnp.einsum('chk,chkd->cd', gc * act, v_tab[idc])

    out = lax.map(chunk, (xt.reshape(nch, PEER_TOKEN_CHUNK, d),
                          ids.reshape(nch, PEER_TOKEN_CHUNK, PEER_HEADS, PEER_TOPK),
                          gate.reshape(nch, PEER_TOKEN_CHUNK, PEER_HEADS, PEER_TOPK)))
    return out.reshape(b, s, d)


def per_layer_embed(h, p_i, norm_g, w_up, w_gate):
    return (p_i @ w_up) * jax.nn.sigmoid(rms_norm(h, norm_g) @ w_gate)


def setup_inputs(seed: int = 0) -> dict:
    key = jax.random.key(seed)
    ks = iter(jax.random.split(key, 48))

    def nrm(shape, scale):
        return jax.random.normal(next(ks), shape, jnp.float32) * scale

    def gain(shape):
        return 1.0 + 0.02 * jax.random.normal(next(ks), shape, jnp.float32)

    dt = jnp.exp(jax.random.uniform(next(ks), (N_A_LAYERS, SSD_HEADS), jnp.float32,
                                    minval=math.log(1e-3), maxval=math.log(1e-1)))
    return {
        'x': nrm((BATCH, SEQ, D_MODEL), 1.0),
        'p': nrm((DEPTH, BATCH, SEQ, PLE_DIM), 1.0),
        'a_norm_g': gain((N_A_LAYERS, D_MODEL)),
        'a_w_in': nrm((N_A_LAYERS, D_MODEL, SSD_IN_DIM), D_MODEL ** -0.5),
        'a_conv_w': nrm((N_A_LAYERS, SSD_CONV, SSD_CONV_DIM), SSD_CONV ** -0.5),
        'a_conv_b': nrm((N_A_LAYERS, SSD_CONV_DIM), 0.02),
        'a_dt_bias': dt + jnp.log(-jnp.expm1(-dt)),
        'a_log': jnp.log(jax.random.uniform(next(ks), (N_A_LAYERS, SSD_HEADS), jnp.float32, minval=1.0, maxval=16.0)),
        'a_d_skip': 1.0 + 0.1 * jax.random.normal(next(ks), (N_A_LAYERS, SSD_HEADS), jnp.float32),
        'a_gnorm_g': gain((N_A_LAYERS, SSD_D_INNER)),
        'a_w_out': nrm((N_A_LAYERS, SSD_D_INNER, D_MODEL), SSD_D_INNER ** -0.5),
        'kv_norm_g': gain((D_MODEL,)),
        'w_kv': nrm((D_MODEL, KV_DIM), D_MODEL ** -0.5),
        'cmp_pos_k': nrm((CMP_BLOCK, NSA_HEAD_DIM), 0.02),
        'cmp_pos_v': nrm((CMP_BLOCK, NSA_HEAD_DIM), 0.02),
        'cmp_w1_k': nrm((CMP_BLOCK * NSA_HEAD_DIM, CMP_HIDDEN), (CMP_BLOCK * NSA_HEAD_DIM) ** -0.5),
        'cmp_w2_k': nrm((CMP_HIDDEN, NSA_HEAD_DIM), CMP_HIDDEN ** -0.5),
        'cmp_w1_v': nrm((CMP_BLOCK * NSA_HEAD_DIM, CMP_HIDDEN), (CMP_BLOCK * NSA_HEAD_DIM) ** -0.5),
        'cmp_w2_v': nrm((CMP_HIDDEN, NSA_HEAD_DIM), CMP_HIDDEN ** -0.5),
        'rel_bias': nrm((REL_BUCKETS, NSA_HEADS), 0.2),
        'b_norm_g': gain((N_B_LAYERS, D_MODEL)),
        'b_w_in': nrm((N_B_LAYERS, D_MODEL, NSA_IN_DIM), D_MODEL ** -0.5),
        'b_w_out': nrm((N_B_LAYERS, NSA_HEADS * NSA_HEAD_DIM, D_MODEL), (NSA_HEADS * NSA_HEAD_DIM) ** -0.5),
        'c_norm_g': gain((DEPTH, D_MODEL)),
        'c_w_q': nrm((DEPTH, D_MODEL, PEER_HEADS * PEER_QDIM), D_MODEL ** -0.5),
        'c_keys1': nrm((DEPTH, PEER_HEADS, PEER_N_KEYS, PEER_QDIM // 2), (PEER_QDIM // 2) ** -0.5),
        'c_keys2': nrm((DEPTH, PEER_HEADS, PEER_N_KEYS, PEER_QDIM // 2), (PEER_QDIM // 2) ** -0.5),
        'c_u': nrm((DEPTH, PEER_EXPERTS, D_MODEL), D_MODEL ** -0.5),
        'c_v': nrm((DEPTH, PEER_EXPERTS, D_MODEL), PEER_V_SCALE),
        'e_norm_g': gain((DEPTH, D_MODEL)),
        'e_w_up': nrm((DEPTH, PLE_DIM, D_MODEL), PLE_DIM ** -0.5),
        'e_w_gate': nrm((DEPTH, D_MODEL, D_MODEL), D_MODEL ** -0.5),
        'final_g': gain((D_MODEL,)),
    }


def reference(x, p, a_norm_g, a_w_in, a_conv_w, a_conv_b, a_dt_bias, a_log, a_d_skip, a_gnorm_g, a_w_out,
              kv_norm_g, w_kv, cmp_pos_k, cmp_pos_v, cmp_w1_k, cmp_w2_k, cmp_w1_v, cmp_w2_v, rel_bias,
              b_norm_g, b_w_in, b_w_out, c_norm_g, c_w_q, c_keys1, c_keys2, c_u, c_v,
              e_norm_g, e_w_up, e_w_gate, final_g):
    h = x
    shared = None
    for i in range(DEPTH):
        if i < N_A_LAYERS:
            h = h + ssd_mixer(rms_norm(h, a_norm_g[i]), a_w_in[i], a_conv_w[i], a_conv_b[i], a_dt_bias[i],
                              a_log[i], a_d_skip[i], a_gnorm_g[i], a_w_out[i])
        else:
            if shared is None:
                shared = nsa_shared_kv(h, kv_norm_g, w_kv, cmp_pos_k, cmp_pos_v,
                                       cmp_w1_k, cmp_w2_k, cmp_w1_v, cmp_w2_v)
            j = i - N_A_LAYERS
            h = h + nsa_mixer(rms_norm(h, b_norm_g[j]), b_w_in[j], b_w_out[j], rel_bias, *shared)
        h = h + peer_ffn(rms_norm(h, c_norm_g[i]), c_w_q[i], c_keys1[i], c_keys2[i], c_u[i], c_v[i])
        h = h + per_layer_embed(h, p[i], e_norm_g[i], e_w_up[i], e_w_gate[i])
    return rms_norm(h, final_g)
```

```python
import functools
import math

import jax
import jax.numpy as jnp
import numpy as np
from jax import lax
from jax.experimental import pallas as pl
from jax.experimental.pallas import tpu as pltpu

D_MODEL = 1024
BATCH = 2
SEQ = 16384
DEPTH = 2
N_A_LAYERS = DEPTH // 2
N_B_LAYERS = DEPTH - N_A_LAYERS

SSD_D_INNER = 2 * D_MODEL
SSD_HEADDIM = 64
SSD_HEADS = SSD_D_INNER // SSD_HEADDIM
SSD_GROUPS = 8
SSD_HEADS_PER_GROUP = SSD_HEADS // SSD_GROUPS
SSD_STATE = 128
SSD_CONV = 4
SSD_CHUNK = 128
SSD_CONV_DIM = SSD_D_INNER + 2 * SSD_GROUPS * SSD_STATE
SSD_IN_DIM = SSD_D_INNER + SSD_CONV_DIM + SSD_HEADS

NSA_HEADS = 16
NSA_KV_HEADS = 4
NSA_GROUP = NSA_HEADS // NSA_KV_HEADS
NSA_HEAD_DIM = 64
CMP_BLOCK = 32
CMP_STRIDE = 16
CMP_HIDDEN = 256
SLC_BLOCK = 64
SLC_TOPN = 16
SLC_FORCE = 1e4
WINDOW = 512
Q_BLOCK = 128
NSA_IN_DIM = NSA_HEADS * NSA_HEAD_DIM + 3 * NSA_HEADS
KV_DIM = 6 * NSA_KV_HEADS * NSA_HEAD_DIM

REL_BUCKETS = 32
REL_MAX_DISTANCE = 4096

PEER_HEADS = 8
PEER_N_KEYS = 128
PEER_EXPERTS = PEER_N_KEYS * PEER_N_KEYS
PEER_QDIM = 256
PEER_TOPK = 16
PEER_TOKEN_CHUNK = 128
PEER_V_SCALE = PEER_HEADS ** -0.5

PLE_DIM = 256
NORM_EPS = 1e-6
NEG_INF = -1e30

V7X_LANES = 128
V7X_VMEM_LIMIT_BYTES = 56 * 1024 * 1024


def _norm_matmul_kernel(x_ref, g_ref, w_ref, o_ref, *, normalize):
    x = x_ref[...]
    if normalize:
        x = x * lax.rsqrt(jnp.mean(x * x, axis=-1, keepdims=True) + NORM_EPS) * g_ref[...]
    o_ref[...] = jnp.dot(x.astype(jnp.bfloat16), w_ref[...], preferred_element_type=jnp.float32)


def norm_matmul(x, g, w, *, normalize, tm=512, tn=None):
    m, k = x.shape
    n = w.shape[1]
    n_pad = -(-n // V7X_LANES) * V7X_LANES
    wb = w.astype(jnp.bfloat16)
    if n_pad != n:
        wb = jnp.pad(wb, ((0, 0), (0, n_pad - n)))
    if tn is None:
        tn = n_pad
    assert m % tm == 0 and n_pad % tn == 0
    out = pl.pallas_call(
        functools.partial(_norm_matmul_kernel, normalize=normalize),
        grid=(m // tm, n_pad // tn),
        in_specs=[
            pl.BlockSpec((tm, k), lambda i, j: (i, 0)),
            pl.BlockSpec((1, k), lambda i, j: (0, 0)),
            pl.BlockSpec((k, tn), lambda i, j: (0, j)),
        ],
        out_specs=pl.BlockSpec((tm, tn), lambda i, j: (i, j)),
        out_shape=jax.ShapeDtypeStruct((m, n_pad), jnp.float32),
        compiler_params=pltpu.CompilerParams(
            dimension_semantics=("arbitrary", "arbitrary"),
            vmem_limit_bytes=V7X_VMEM_LIMIT_BYTES),
        name="norm_matmul" if normalize else "matmul",
    )(x, g.reshape(1, k), wb)
    return out[:, :n] if n_pad != n else out


def _mm(x3, g, w, normalize, **kw):
    b, s, k = x3.shape
    if g is None:
        g = jnp.ones((k,), jnp.float32)
    return norm_matmul(x3.reshape(b * s, k), g, w, normalize=normalize, **kw).reshape(b, s, w.shape[1])


def rms_norm(x, g):
    xf = x.astype(jnp.float32)
    y = xf * lax.rsqrt(jnp.mean(xf * xf, axis=-1, keepdims=True) + NORM_EPS)
    return (y * g.astype(jnp.float32)).astype(x.dtype)


def masked_softmax(s, mask):
    s = jnp.where(mask, s.astype(jnp.float32), NEG_INF)
    p = jax.nn.softmax(s, axis=-1)
    return jnp.where(mask, p, 0.0)


def rel_bucket(dist):
    dist = jnp.maximum(dist, 0)
    max_exact = REL_BUCKETS // 2
    d = jnp.maximum(dist, 1).astype(jnp.float32)
    large = max_exact + (jnp.log(d / max_exact) / math.log(REL_MAX_DISTANCE / max_exact)
                         * (REL_BUCKETS - max_exact)).astype(jnp.int32)
    large = jnp.minimum(large, REL_BUCKETS - 1)
    return jnp.where(dist < max_exact, dist, large)


def ssd_mixer(h, norm_g, w_in, conv_w, conv_b, dt_bias, a_log, d_skip, gnorm_g, w_out):
    b, s, _ = h.shape
    G, R, P, N, L = SSD_GROUPS, SSD_HEADS_PER_GROUP, SSD_HEADDIM, SSD_STATE, SSD_CHUNK
    nc = s // L
    zxbcdt = _mm(h, norm_g, w_in, True, tm=256)
    z = zxbcdt[..., :SSD_D_INNER]
    xbc = zxbcdt[..., SSD_D_INNER:SSD_D_INNER + SSD_CONV_DIM]
    dt_raw = zxbcdt[..., SSD_D_INNER + SSD_CONV_DIM:]
    xbc = lax.conv_general_dilated(xbc, conv_w[:, None, :], window_strides=(1,),
                                   padding=[(SSD_CONV - 1, 0)],
                                   dimension_numbers=('NWC', 'WIO', 'NWC'),
                                   feature_group_count=SSD_CONV_DIM) + conv_b
    xbc = jax.nn.silu(xbc)
    x = xbc[..., :SSD_D_INNER].reshape(b, nc, L, G, R, P)
    bm = xbc[..., SSD_D_INNER:SSD_D_INNER + G * N].reshape(b, nc, L, G, N)
    cm = xbc[..., SSD_D_INNER + G * N:].reshape(b, nc, L, G, N)
    dt = jax.nn.softplus((dt_raw + dt_bias).astype(jnp.float32)).reshape(b, nc, L, G, R)
    a = -jnp.exp(a_log.astype(jnp.float32)).reshape(G, R)
    xd = x * dt[..., None].astype(x.dtype)
    a_dt = jnp.transpose(dt * a, (0, 3, 4, 1, 2))
    a_cs = jnp.cumsum(a_dt, axis=-1)
    causal = jnp.tril(jnp.ones((L, L), dtype=bool))
    seg = a_cs[..., :, None] - a_cs[..., None, :]
    decay_in = jnp.exp(jnp.where(causal, seg, -jnp.inf)).astype(x.dtype)
    cb = jnp.einsum('bclgn,bcsgn->bgcls', cm, bm)
    y_diag = jnp.einsum('bgrcls,bcsgrp->bclgrp', cb[:, :, None] * decay_in, xd)
    decay_to_end = jnp.exp(a_cs[..., -1:] - a_cs).astype(x.dtype)
    states = jnp.einsum('bcsgn,bgrcs,bcsgrp->bcgrpn', bm, decay_to_end, xd)
    chunk_decay = jnp.exp(a_cs[..., -1]).astype(x.dtype)

    def step(hh, inp):
        st, dec = inp
        return hh * dec[..., None, None] + st, hh

    h0 = jnp.zeros((b, G, R, P, N), x.dtype)
    _, prev = lax.scan(step, h0, (jnp.moveaxis(states, 1, 0), jnp.moveaxis(chunk_decay, -1, 0)))
    prev = jnp.moveaxis(prev, 0, 1)
    y_off = jnp.einsum('bclgn,bcgrpn,bgrcl->bclgrp', cm, prev, jnp.exp(a_cs).astype(x.dtype))
    y = (y_diag + y_off + x * d_skip.reshape(G, R, 1)).reshape(b, s, SSD_D_INNER)
    yg = (y * jax.nn.silu(z)).reshape(b, s, G, SSD_D_INNER // G)
    yg = rms_norm(yg, gnorm_g.reshape(G, -1)).reshape(b, s, SSD_D_INNER)
    return _mm(yg, None, w_out, False)


def compress_blocks(t, pos, w1, w2):
    b, g, s, d = t.shape
    halves = t.reshape(b, g, s // CMP_STRIDE, CMP_STRIDE, d)
    blocks = jnp.concatenate([halves[:, :, :-1], halves[:, :, 1:]], axis=3) + pos
    flat = blocks.reshape(b, g, -1, CMP_BLOCK * d)
    return jax.nn.gelu(flat @ w1) @ w2


def nsa_shared_kv(h, kv_norm_g, w_kv, cmp_pos_k, cmp_pos_v, cmp_w1_k, cmp_w2_k, cmp_w1_v, cmp_w2_v):
    b, s, _ = h.shape
    kv = _mm(h, kv_norm_g, w_kv, True)
    kv = kv.reshape(b, s, 6, NSA_KV_HEADS, NSA_HEAD_DIM).transpose(2, 0, 3, 1, 4)
    k_c, v_c, k_s, v_s, k_w, v_w = kv[0], kv[1], kv[2], kv[3], kv[4], kv[5]
    k_cmp = compress_blocks(k_c, cmp_pos_k, cmp_w1_k, cmp_w2_k)
    v_cmp = compress_blocks(v_c, cmp_pos_v, cmp_w1_v, cmp_w2_v)
    nsel = s // SLC_BLOCK
    k_slc = k_s.reshape(b, NSA_KV_HEADS, nsel, SLC_BLOCK, NSA_HEAD_DIM)
    v_slc = v_s.reshape(b, NSA_KV_HEADS, nsel, SLC_BLOCK, NSA_HEAD_DIM)
    pad = ((0, 0), (0, 0), (WINDOW, 0), (0, 0))
    return (k_cmp, v_cmp, k_slc, v_slc, jnp.pad(k_w, pad), jnp.pad(v_w, pad))


def nsa_mixer(h, norm_g, w_in, w_out, rel_bias, k_cmp, v_cmp, k_slc, v_slc, k_win, v_win):
    b, s, _ = h.shape
    H, G, R, DH = NSA_HEADS, NSA_KV_HEADS, NSA_GROUP, NSA_HEAD_DIM
    nb = s // Q_BLOCK
    proj = _mm(h, norm_g, w_in, True)
    q = proj[..., :H * DH].reshape(b, nb, Q_BLOCK, G, R, DH).transpose(1, 0, 3, 4, 2, 5)
    gates = jax.nn.sigmoid(proj[..., H * DH:]).reshape(b, nb, Q_BLOCK, G, R, 3).transpose(1, 0, 3, 4, 2, 5)
    n_cmp = k_cmp.shape[2]
    n_sel = k_slc.shape[2]
    topn = min(SLC_TOPN, n_sel)
    cmp_end = jnp.arange(n_cmp) * CMP_STRIDE + CMP_BLOCK - 1
    n_span = CMP_BLOCK // CMP_STRIDE
    per = SLC_BLOCK // CMP_STRIDE
    raw = np.arange(n_sel)[:, None] * per + np.arange(-(n_span - 1), per)[None, :]
    imp_idx = jnp.asarray(np.clip(raw, 0, n_cmp - 1), jnp.int32)
    imp_valid = jnp.asarray((raw >= 0) & (raw < n_cmp), jnp.float32)
    scale = DH ** -0.5
    bidx = jnp.arange(b)[:, None, None, None]
    gidx = jnp.arange(G)[None, :, None, None]
    table_g = rel_bias.reshape(REL_BUCKETS, G, R).transpose(1, 0, 2)
    blk = jnp.arange(n_sel)

    def head_bias(dist):
        bb = rel_bias[rel_bucket(dist)].reshape(dist.shape + (G, R))
        return jnp.moveaxis(bb, (-2, -1), (0, 1))

    def block(args):
        qb_i, q_blk, g_blk = args
        q0 = qb_i * Q_BLOCK
        t = q0 + jnp.arange(Q_BLOCK)
        s_c = jnp.einsum('bgrqd,bgkd->bgrqk', q_blk, k_cmp) * scale + head_bias(t[:, None] - cmp_end[None, :])
        p_c = masked_softmax(s_c, cmp_end[None, :] <= t[:, None])
        o_c = jnp.einsum('bgrqk,bgkd->bgrqd', p_c.astype(v_cmp.dtype), v_cmp)
        imp = jnp.sum(jnp.take(p_c, imp_idx, axis=-1) * imp_valid, axis=(2, -1))
        cur = t // SLC_BLOCK
        valid_blk = blk[None, :] <= cur[:, None]
        forced = (blk[None, :] == 0) | (blk[None, :] == cur[:, None]) | (blk[None, :] == cur[:, None] - 1)
        score = jnp.where(forced, SLC_FORCE, jnp.where(valid_blk, imp, -SLC_FORCE))
        _, sel = lax.top_k(score, topn)
        k_sel = k_slc[bidx, gidx, sel]
        v_sel = v_slc[bidx, gidx, sel]
        kpos = sel[..., None] * SLC_BLOCK + jnp.arange(SLC_BLOCK)
        dist = t[:, None, None] - kpos
        bias_s = table_g[gidx[..., None], rel_bucket(dist)]
        s_s = jnp.einsum('bgrqd,bgqnjd->bgrqnj', q_blk, k_sel) * scale + jnp.moveaxis(bias_s, -1, 2)
        s_s = s_s.reshape(b, G, R, Q_BLOCK, topn * SLC_BLOCK)
        p_s = masked_softmax(s_s, (dist >= 0).reshape(b, G, 1, Q_BLOCK, topn * SLC_BLOCK))
        o_s = jnp.einsum('bgrqm,bgqmd->bgrqd', p_s.astype(v_sel.dtype),
                         v_sel.reshape(b, G, Q_BLOCK, topn * SLC_BLOCK, DH))
        k_w = lax.dynamic_slice_in_dim(k_win, q0, WINDOW + Q_BLOCK, axis=2)
        v_w = lax.dynamic_slice_in_dim(v_win, q0, WINDOW + Q_BLOCK, axis=2)
        kpos_w = q0 - WINDOW + jnp.arange(WINDOW + Q_BLOCK)
        dist_w = t[:, None] - kpos_w[None, :]
        mask_w = (kpos_w[None, :] >= 0) & (dist_w >= 0) & (dist_w < WINDOW)
        s_w = jnp.einsum('bgrqd,bgkd->bgrqk', q_blk, k_w) * scale + head_bias(dist_w)
        p_w = masked_softmax(s_w, mask_w)
        o_w = jnp.einsum('bgrqk,bgkd->bgrqd', p_w.astype(v_w.dtype), v_w)
        return g_blk[..., 0:1] * o_c + g_blk[..., 1:2] * o_s + g_blk[..., 2:3] * o_w

    o = lax.map(block, (jnp.arange(nb), q, gates))
    o = jnp.transpose(o, (1, 0, 4, 2, 3, 5)).reshape(b, s, H * DH)
    return _mm(o, None, w_out, False)


def peer_ffn(h, norm_g, w_q, keys1, keys2, u_tab, v_tab):
    b, s, d = h.shape
    T = b * s
    xt = rms_norm(h, norm_g).reshape(T, d)
    q = _mm(h, norm_g, w_q, True).reshape(T, PEER_HEADS, 2, PEER_QDIM // 2)
    s1 = jnp.einsum('thd,hkd->thk', q[:, :, 0], keys1)
    s2 = jnp.einsum('thd,hkd->thk', q[:, :, 1], keys2)
    v1, i1 = lax.top_k(s1, PEER_TOPK)
    v2, i2 = lax.top_k(s2, PEER_TOPK)
    cand = (v1[..., :, None] + v2[..., None, :]).reshape(T, PEER_HEADS, PEER_TOPK * PEER_TOPK)
    cand_id = (i1[..., :, None] * PEER_N_KEYS + i2[..., None, :]).reshape(T, PEER_HEADS, PEER_TOPK * PEER_TOPK)
    top_s, pos = lax.top_k(cand, PEER_TOPK)
    ids = jnp.take_along_axis(cand_id, pos, axis=-1)
    gate = jax.nn.softmax(top_s.astype(jnp.float32), axis=-1).astype(h.dtype)
    nch = T // PEER_TOKEN_CHUNK

    def chunk(args):
        xc, idc, gc = args
        act = jax.nn.gelu(jnp.einsum('cd,chkd->chk', xc, u_tab[idc]))
        return jnp.einsum('chk,chkd->cd', gc * act, v_tab[idc])

    out = lax.map(chunk, (xt.reshape(nch, PEER_TOKEN_CHUNK, d),
                          ids.reshape(nch, PEER_TOKEN_CHUNK, PEER_HEADS, PEER_TOPK),
                          gate.reshape(nch, PEER_TOKEN_CHUNK, PEER_HEADS, PEER_TOPK)))
    return out.reshape(b, s, d)


def per_layer_embed(h, p_i, norm_g, w_up, w_gate):
    return _mm(p_i, None, w_up, False) * jax.nn.sigmoid(_mm(h, norm_g, w_gate, True))


def kernel(x, p, a_norm_g, a_w_in, a_conv_w, a_conv_b, a_dt_bias, a_log, a_d_skip, a_gnorm_g, a_w_out,
           kv_norm_g, w_kv, cmp_pos_k, cmp_pos_v, cmp_w1_k, cmp_w2_k, cmp_w1_v, cmp_w2_v, rel_bias,
           b_norm_g, b_w_in, b_w_out, c_norm_g, c_w_q, c_keys1, c_keys2, c_u, c_v,
           e_norm_g, e_w_up, e_w_gate, final_g):
    h = x
    shared = None
    for i in range(DEPTH):
        if i < N_A_LAYERS:
            h = h + ssd_mixer(h, a_norm_g[i], a_w_in[i], a_conv_w[i], a_conv_b[i], a_dt_bias[i],
                              a_log[i], a_d_skip[i], a_gnorm_g[i], a_w_out[i])
        else:
            if shared is None:
                shared = nsa_shared_kv(h, kv_norm_g, w_kv, cmp_pos_k, cmp_pos_v,
                                       cmp_w1_k, cmp_w2_k, cmp_w1_v, cmp_w2_v)
            j = i - N_A_LAYERS
            h = h + nsa_mixer(h, b_norm_g[j], b_w_in[j], b_w_out[j], rel_bias, *shared)
        h = h + peer_ffn(h, c_norm_g[i], c_w_q[i], c_keys1[i], c_keys2[i], c_u[i], c_v[i])
        h = h + per_layer_embed(h, p[i], e_norm_g[i], e_w_up[i], e_w_gate[i])
    return rms_norm(h, final_g)
```

```python
import functools
import math

import jax
import jax.numpy as jnp
import numpy as np
from jax import lax
from jax.experimental import pallas as pl
from jax.experimental.pallas import tpu as pltpu

D_MODEL = 1024
BATCH = 2
SEQ = 16384
DEPTH = 2
N_A_LAYERS = DEPTH // 2
N_B_LAYERS = DEPTH - N_A_LAYERS

SSD_D_INNER = 2 * D_MODEL
SSD_HEADDIM = 64
SSD_HEADS = SSD_D_INNER // SSD_HEADDIM
SSD_GROUPS = 8
SSD_HEADS_PER_GROUP = SSD_HEADS // SSD_GROUPS
SSD_STATE = 128
SSD_CONV = 4
SSD_CHUNK = 128
SSD_CONV_DIM = SSD_D_INNER + 2 * SSD_GROUPS * SSD_STATE
SSD_IN_DIM = SSD_D_INNER + SSD_CONV_DIM + SSD_HEADS

NSA_HEADS = 16
NSA_KV_HEADS = 4
NSA_GROUP = NSA_HEADS // NSA_KV_HEADS
NSA_HEAD_DIM = 64
CMP_BLOCK = 32
CMP_STRIDE = 16
CMP_HIDDEN = 256
SLC_BLOCK = 64
SLC_TOPN = 16
SLC_FORCE = 1e4
WINDOW = 512
Q_BLOCK = 128
NSA_IN_DIM = NSA_HEADS * NSA_HEAD_DIM + 3 * NSA_HEADS
KV_DIM = 6 * NSA_KV_HEADS * NSA_HEAD_DIM

REL_BUCKETS = 32
REL_MAX_DISTANCE = 4096

PEER_HEADS = 8
PEER_N_KEYS = 128
PEER_EXPERTS = PEER_N_KEYS * PEER_N_KEYS
PEER_QDIM = 256
PEER_TOPK = 16
PEER_TOKEN_CHUNK = 128
PEER_V_SCALE = PEER_HEADS ** -0.5

PLE_DIM = 256
NORM_EPS = 1e-6
NEG_INF = -1e30

V7X_LANES = 128
V7X_VMEM_LIMIT_BYTES = 56 * 1024 * 1024


def _norm_matmul_kernel(x_ref, g_ref, w_ref, o_ref, *, normalize):
    x = x_ref[...]
    if normalize:
        x = x * lax.rsqrt(jnp.mean(x * x, axis=-1, keepdims=True) + NORM_EPS) * g_ref[...]
    o_ref[...] = jnp.dot(x.astype(jnp.bfloat16), w_ref[...], preferred_element_type=jnp.float32)


def norm_matmul(x, g, w, *, normalize, tm=512, tn=None):
    m, k = x.shape
    n = w.shape[1]
    n_pad = -(-n // V7X_LANES) * V7X_LANES
    wb = w.astype(jnp.bfloat16)
    if n_pad != n:
        wb = jnp.pad(wb, ((0, 0), (0, n_pad - n)))
    if tn is None:
        tn = n_pad
    assert m % tm == 0 and n_pad % tn == 0
    out = pl.pallas_call(
        functools.partial(_norm_matmul_kernel, normalize=normalize),
        grid=(m // tm, n_pad // tn),
        in_specs=[
            pl.BlockSpec((tm, k), lambda i, j: (i, 0)),
            pl.BlockSpec((1, k), lambda i, j: (0, 0)),
            pl.BlockSpec((k, tn), lambda i, j: (0, j)),
        ],
        out_specs=pl.BlockSpec((tm, tn), lambda i, j: (i, j)),
        out_shape=jax.ShapeDtypeStruct((m, n_pad), jnp.float32),
        compiler_params=pltpu.CompilerParams(
            dimension_semantics=("arbitrary", "arbitrary"),
            vmem_limit_bytes=V7X_VMEM_LIMIT_BYTES),
        name="norm_matmul" if normalize else "matmul",
    )(x, g.reshape(1, k), wb)
    return out[:, :n] if n_pad != n else out


def _mm(x3, g, w, normalize, **kw):
    b, s, k = x3.shape
    if g is None:
        g = jnp.ones((k,), jnp.float32)
    return norm_matmul(x3.reshape(b * s, k), g, w, normalize=normalize, **kw).reshape(b, s, w.shape[1])


def rms_norm(x, g):
    xf = x.astype(jnp.float32)
    y = xf * lax.rsqrt(jnp.mean(xf * xf, axis=-1, keepdims=True) + NORM_EPS)
    return (y * g.astype(jnp.float32)).astype(x.dtype)


def rel_bucket(dist):
    dist = jnp.maximum(dist, 0)
    max_exact = REL_BUCKETS // 2
    d = jnp.maximum(dist, 1).astype(jnp.float32)
    large = max_exact + (jnp.log(d / max_exact) / math.log(REL_MAX_DISTANCE / max_exact)
                         * (REL_BUCKETS - max_exact)).astype(jnp.int32)
    large = jnp.minimum(large, REL_BUCKETS - 1)
    return jnp.where(dist < max_exact, dist, large)


def ssd_mixer(h, norm_g, w_in, conv_w, conv_b, dt_bias, a_log, d_skip, gnorm_g, w_out):
    b, s, _ = h.shape
    G, R, P, N, L = SSD_GROUPS, SSD_HEADS_PER_GROUP, SSD_HEADDIM, SSD_STATE, SSD_CHUNK
    nc = s // L
    zxbcdt = _mm(h, norm_g, w_in, True, tm=256)
    z = zxbcdt[..., :SSD_D_INNER]
    xbc = zxbcdt[..., SSD_D_INNER:SSD_D_INNER + SSD_CONV_DIM]
    dt_raw = zxbcdt[..., SSD_D_INNER + SSD_CONV_DIM:]
    xbc = lax.conv_general_dilated(xbc, conv_w[:, None, :], window_strides=(1,),
                                   padding=[(SSD_CONV - 1, 0)],
                                   dimension_numbers=('NWC', 'WIO', 'NWC'),
                                   feature_group_count=SSD_CONV_DIM) + conv_b
    xbc = jax.nn.silu(xbc)
    x = xbc[..., :SSD_D_INNER].reshape(b, nc, L, G, R, P)
    bm = xbc[..., SSD_D_INNER:SSD_D_INNER + G * N].reshape(b, nc, L, G, N)
    cm = xbc[..., SSD_D_INNER + G * N:].reshape(b, nc, L, G, N)
    dt = jax.nn.softplus((dt_raw + dt_bias).astype(jnp.float32)).reshape(b, nc, L, G, R)
    a = -jnp.exp(a_log.astype(jnp.float32)).reshape(G, R)
    xd = x * dt[..., None].astype(x.dtype)
    a_dt = jnp.transpose(dt * a, (0, 3, 4, 1, 2))
    a_cs = jnp.cumsum(a_dt, axis=-1)
    causal = jnp.tril(jnp.ones((L, L), dtype=bool))
    seg = a_cs[..., :, None] - a_cs[..., None, :]
    decay_in = jnp.exp(jnp.where(causal, seg, -jnp.inf)).astype(x.dtype)
    cb = jnp.einsum('bclgn,bcsgn->bgcls', cm, bm)
    y_diag = jnp.einsum('bgrcls,bcsgrp->bclgrp', cb[:, :, None] * decay_in, xd)
    decay_to_end = jnp.exp(a_cs[..., -1:] - a_cs).astype(x.dtype)
    states = jnp.einsum('bcsgn,bgrcs,bcsgrp->bcgrpn', bm, decay_to_end, xd)
    chunk_decay = jnp.exp(a_cs[..., -1]).astype(x.dtype)

    def step(hh, inp):
        st, dec = inp
        return hh * dec[..., None, None] + st, hh

    h0 = jnp.zeros((b, G, R, P, N), x.dtype)
    _, prev = lax.scan(step, h0, (jnp.moveaxis(states, 1, 0), jnp.moveaxis(chunk_decay, -1, 0)))
    prev = jnp.moveaxis(prev, 0, 1)
    y_off = jnp.einsum('bclgn,bcgrpn,bgrcl->bclgrp', cm, prev, jnp.exp(a_cs).astype(x.dtype))
    y = (y_diag + y_off + x * d_skip.reshape(G, R, 1)).reshape(b, s, SSD_D_INNER)
    yg = (y * jax.nn.silu(z)).reshape(b, s, G, SSD_D_INNER // G)
    yg = rms_norm(yg, gnorm_g.reshape(G, -1)).reshape(b, s, SSD_D_INNER)
    return _mm(yg, None, w_out, False)


def _mlp_kernel(x_ref, w1_ref, w2_ref, o_ref):
    hid = jnp.dot(x_ref[...].astype(jnp.bfloat16), w1_ref[...], preferred_element_type=jnp.float32)
    hid = jax.nn.gelu(hid)
    o_ref[...] = jnp.dot(hid.astype(jnp.bfloat16), w2_ref[...], preferred_element_type=jnp.float32)


def mlp2(x, w1, w2, *, tm=512):
    m, k = x.shape
    hdim, n = w2.shape
    assert m % tm == 0
    return pl.pallas_call(
        _mlp_kernel,
        grid=(m // tm,),
        in_specs=[pl.BlockSpec((tm, k), lambda i: (i, 0)),
                  pl.BlockSpec((k, hdim), lambda i: (0, 0)),
                  pl.BlockSpec((hdim, n), lambda i: (0, 0))],
        out_specs=pl.BlockSpec((tm, n), lambda i: (i, 0)),
        out_shape=jax.ShapeDtypeStruct((m, n), jnp.float32),
        compiler_params=pltpu.CompilerParams(dimension_semantics=("arbitrary",),
                                             vmem_limit_bytes=V7X_VMEM_LIMIT_BYTES),
        name="cmp_mlp",
    )(x, w1.astype(jnp.bfloat16), w2.astype(jnp.bfloat16))


def compress_blocks(t, pos, w1, w2):
    b, s, g, d = t.shape
    halves = t.transpose(0, 2, 1, 3).reshape(b, g, s // CMP_STRIDE, CMP_STRIDE, d)
    nxt = jnp.concatenate([halves[:, :, 1:], jnp.zeros_like(halves[:, :, :1])], axis=2)
    blocks = jnp.concatenate([halves, nxt], axis=3) + pos
    flat = blocks.reshape(b * g * (s // CMP_STRIDE), CMP_BLOCK * d)
    return mlp2(flat, w1, w2).reshape(b, g, s // CMP_STRIDE, d)


N_SEL = SEQ // SLC_BLOCK
N_KT = SEQ // Q_BLOCK
CMP_FRONT = 256
CMP_ROWS = CMP_FRONT + SEQ // CMP_STRIDE
CMP_WIN = 256
REL_TABLE = 4096
NEAR_TILES = 24
WIN_TILES = WINDOW // Q_BLOCK + 1
QL = NSA_GROUP * Q_BLOCK


def _nsa_kernel(qT_ref, gT_ref, kc_ref, vcT_ref, kk_ref, vsT_ref, vwT_ref, at_ref, bcT_ref, bT_ref, bTw_ref,
                o_ref, s_scr, sel_scr):
    f32, bf16 = jnp.float32, jnp.bfloat16
    i = pl.program_id(2)
    qT = (qT_ref[0, 0, 0] * (NSA_HEAD_DIM ** -0.5)).astype(bf16)
    zq = jnp.zeros_like(qT)
    q_sel = jnp.concatenate([qT, zq], axis=0)
    q_win = jnp.concatenate([zq, qT], axis=0)

    r0 = pl.multiple_of(8 * i + 8, 8)
    s = jnp.dot(kc_ref[0, 0], qT, preferred_element_type=f32)
    rio = lax.broadcasted_iota(jnp.int32, (CMP_ROWS, Q_BLOCK), 0)
    rowmask = jnp.where((rio >= CMP_FRONT) & (rio < r0 + CMP_WIN), 0.0, NEG_INF)
    s_scr[...] = s + jnp.concatenate([rowmask] * NSA_GROUP, axis=1)
    s_scr[pl.ds(r0, CMP_WIN), :] = s_scr[pl.ds(r0, CMP_WIN), :] + bcT_ref[0]
    s = s_scr[...]
    m = jnp.max(s, axis=0, keepdims=True)
    p = jnp.exp(s - m)
    l = jnp.sum(p, axis=0, keepdims=True)
    inv = jnp.where(m > 0.1 * NEG_INF, 1.0 / l, 0.0)
    pn = p * inv
    o_c = jnp.dot(vcT_ref[0, 0], pn.astype(bf16), preferred_element_type=f32)

    psum = pn[:, 0:Q_BLOCK]
    for r in range(1, NSA_GROUP):
        psum = psum + pn[:, r * Q_BLOCK:(r + 1) * Q_BLOCK]
    hi = psum.astype(bf16)
    lo = (psum - hi.astype(f32)).astype(bf16)
    imp = (jnp.dot(at_ref[...], hi, preferred_element_type=f32)
           + jnp.dot(at_ref[...], lo, preferred_element_type=f32))
    blk = lax.broadcasted_iota(jnp.int32, (N_SEL, Q_BLOCK), 0)
    qi = lax.broadcasted_iota(jnp.int32, (N_SEL, Q_BLOCK), 1)
    cur = 2 * i + (qi >= SLC_BLOCK).astype(jnp.int32)
    forced = (blk == 0) | (blk == cur) | (blk == cur - 1)
    valid = blk <= cur
    score = jnp.where(forced, SLC_FORCE, jnp.where(valid, imp, -SLC_FORCE))
    blkf = blk.astype(f32)
    chosen = jnp.zeros((N_SEL, Q_BLOCK), f32)
    for _ in range(SLC_TOPN):
        mx = jnp.max(score, axis=0, keepdims=True)
        first = jnp.min(jnp.where(score == mx, blkf, float(N_SEL)), axis=0, keepdims=True)
        hit = blkf == first
        chosen = jnp.where(hit, 1.0, chosen)
        score = jnp.where(hit, -jnp.inf, score)
    sel_scr[...] = jnp.where((chosen > 0.5) & valid, 0.0, NEG_INF)

    def attend(kt, carry, q_aug, vT_ref, bias):
        m_run, l_run, acc = carry
        sc = jnp.dot(kk_ref[0, 0, kt], q_aug, preferred_element_type=f32) + bias
        m_new = jnp.maximum(m_run, jnp.max(sc, axis=0, keepdims=True))
        alpha = jnp.exp(m_run - m_new)
        pt = jnp.exp(sc - m_new)
        l_new = alpha * l_run + jnp.sum(pt, axis=0, keepdims=True)
        acc_new = alpha * acc + jnp.dot(vT_ref[0, 0, kt], pt.astype(bf16), preferred_element_type=f32)
        return m_new, l_new, acc_new

    def sel_mask(kt):
        m0 = jnp.broadcast_to(sel_scr[pl.ds(2 * kt, 1), :], (SLC_BLOCK, Q_BLOCK))
        m1 = jnp.broadcast_to(sel_scr[pl.ds(2 * kt + 1, 1), :], (SLC_BLOCK, Q_BLOCK))
        mk = jnp.concatenate([m0, m1], axis=0)
        return jnp.concatenate([mk] * NSA_GROUP, axis=1)

    init = (jnp.full((1, QL), NEG_INF, f32), jnp.zeros((1, QL), f32), jnp.zeros((NSA_HEAD_DIM, QL), f32))
    first_near = jnp.maximum(i - (NEAR_TILES - 1), 0)
    carry = lax.fori_loop(0, first_near,
                          lambda kt, c: attend(kt, c, q_sel, vsT_ref, sel_mask(kt)), init)
    _, l_s, acc_s = lax.fori_loop(first_near, i + 1,
                                  lambda kt, c: attend(kt, c, q_sel, vsT_ref, sel_mask(kt) + bT_ref[0, i - kt]),
                                  carry)
    o_s = acc_s / l_s

    first_win = jnp.maximum(i - (WIN_TILES - 1), 0)
    _, l_w, acc_w = lax.fori_loop(first_win, i + 1,
                                  lambda kt, c: attend(kt, c, q_win, vwT_ref, bTw_ref[0, i - kt]), init)
    o_w = acc_w / l_w

    gate = jax.nn.sigmoid(gT_ref[0, 0, 0])
    o_ref[0, 0, 0] = gate[0:1] * o_c + gate[1:2] * o_s + gate[2:3] * o_w


def _bias_tables(rel_bias):
    G, R = NSA_KV_HEADS, NSA_GROUP
    bd = rel_bias[rel_bucket(jnp.arange(REL_TABLE))]
    bd = (bd - bd[REL_TABLE - 1]).T

    def tiles(dist, ok):
        vals = jnp.take(bd, jnp.asarray(np.clip(dist, 0, REL_TABLE - 1), jnp.int32), axis=1)
        return jnp.where(jnp.asarray(ok), vals, NEG_INF)

    kj = np.arange(Q_BLOCK)[None, :, None]
    qi = np.arange(Q_BLOCK)[None, None, :]
    d_near = Q_BLOCK * np.arange(NEAR_TILES)[:, None, None] + qi - kj
    b_near = tiles(d_near, d_near >= 0)
    d_win = d_near[:WIN_TILES]
    b_win = tiles(d_win, (d_win >= 0) & (d_win < WINDOW))
    row = np.arange(CMP_WIN)[:, None]
    d_cmp = np.arange(Q_BLOCK)[None, :] - CMP_STRIDE * (row - (CMP_WIN - 8)) - (CMP_BLOCK - 1)
    b_cmp = tiles(d_cmp, d_cmp >= 0)

    def lanes(x):
        x = x.reshape((G, R) + x.shape[1:])
        x = jnp.moveaxis(x, 1, -2)
        return x.reshape(x.shape[:-2] + (QL,))

    return lanes(b_cmp), lanes(b_near), lanes(b_win)


def _pool_matrix():
    per = SLC_BLOCK // CMP_STRIDE
    n_span = CMP_BLOCK // CMP_STRIDE
    k = np.arange(CMP_ROWS)[None, :] - CMP_FRONT
    j = np.arange(N_SEL)[:, None]
    return jnp.asarray((k >= per * j - (n_span - 1)) & (k <= per * j + per - 1), jnp.bfloat16)


def nsa_shared_kv(h, kv_norm_g, w_kv, cmp_pos_k, cmp_pos_v, cmp_w1_k, cmp_w2_k, cmp_w1_v, cmp_w2_v):
    b, s, _ = h.shape
    G, DH = NSA_KV_HEADS, NSA_HEAD_DIM
    bf16 = jnp.bfloat16
    kv = _mm(h, kv_norm_g, w_kv, True).reshape(b, s, 6, G, DH)
    k_cmp = compress_blocks(kv[:, :, 0], cmp_pos_k, cmp_w1_k, cmp_w2_k)
    v_cmp = compress_blocks(kv[:, :, 1], cmp_pos_v, cmp_w1_v, cmp_w2_v)
    front = jnp.zeros((b, G, CMP_FRONT, DH), bf16)
    kc = jnp.concatenate([front, k_cmp.astype(bf16)], axis=2)
    vcT = jnp.concatenate([front, v_cmp.astype(bf16)], axis=2).transpose(0, 1, 3, 2)
    kk = jnp.concatenate([kv[:, :, 2], kv[:, :, 4]], axis=-1).astype(bf16)
    kk = kk.transpose(0, 2, 1, 3).reshape(b, G, N_KT, Q_BLOCK, 2 * DH)

    def vt(v):
        return v.astype(bf16).transpose(0, 2, 1, 3).reshape(b, G, N_KT, Q_BLOCK, DH).transpose(0, 1, 2, 4, 3)

    return kc, vcT, kk, vt(kv[:, :, 3]), vt(kv[:, :, 5])


def nsa_mixer(h, norm_g, w_in, w_out, rel_bias, kc, vcT, kk, vsT, vwT):
    b, s, _ = h.shape
    G, R, DH = NSA_KV_HEADS, NSA_GROUP, NSA_HEAD_DIM
    nb = s // Q_BLOCK
    proj = _mm(h, norm_g, w_in, True)
    qT = proj[..., :NSA_HEADS * DH].reshape(b, nb, Q_BLOCK, G, R, DH).transpose(0, 3, 1, 5, 4, 2)
    qT = qT.reshape(b, G, nb, DH, QL)
    gT = proj[..., NSA_HEADS * DH:].reshape(b, nb, Q_BLOCK, G, R, 3).transpose(0, 3, 1, 5, 4, 2)
    gT = gT.reshape(b, G, nb, 3, QL)
    b_cmp, b_near, b_win = _bias_tables(rel_bias)
    at = _pool_matrix()

    per_bg = lambda *blk: pl.BlockSpec((1, 1) + blk, lambda bi, gi, i: (bi, gi) + (0,) * len(blk))
    per_g = lambda *blk: pl.BlockSpec((1,) + blk, lambda bi, gi, i: (gi,) + (0,) * len(blk))
    oT = pl.pallas_call(
        _nsa_kernel,
        grid=(b, G, nb),
        in_specs=[
            pl.BlockSpec((1, 1, 1, DH, QL), lambda bi, gi, i: (bi, gi, i, 0, 0)),
            pl.BlockSpec((1, 1, 1, 3, QL), lambda bi, gi, i: (bi, gi, i, 0, 0)),
            per_bg(CMP_ROWS, DH),
            per_bg(DH, CMP_ROWS),
            per_bg(N_KT, Q_BLOCK, 2 * DH),
            per_bg(N_KT, DH, Q_BLOCK),
            per_bg(N_KT, DH, Q_BLOCK),
            pl.BlockSpec((N_SEL, CMP_ROWS), lambda bi, gi, i: (0, 0)),
            per_g(CMP_WIN, QL),
            per_g(NEAR_TILES, Q_BLOCK, QL),
            per_g(WIN_TILES, Q_BLOCK, QL),
        ],
        out_specs=pl.BlockSpec((1, 1, 1, DH, QL), lambda bi, gi, i: (bi, gi, i, 0, 0)),
        out_shape=jax.ShapeDtypeStruct((b, G, nb, DH, QL), jnp.float32),
        scratch_shapes=[pltpu.VMEM((CMP_ROWS, QL), jnp.float32), pltpu.VMEM((N_SEL, Q_BLOCK), jnp.float32)],
        compiler_params=pltpu.CompilerParams(
            dimension_semantics=("arbitrary", "arbitrary", "arbitrary"),
            vmem_limit_bytes=V7X_VMEM_LIMIT_BYTES),
        name="nsa_attention",
    )(qT, gT, kc, vcT, kk, vsT, vwT, at, b_cmp, b_near, b_win)
    o = oT.reshape(b, G, nb, DH, R, Q_BLOCK).transpose(0, 2, 5, 1, 4, 3).reshape(b, s, NSA_HEADS * DH)
    return _mm(o, None, w_out, False)


def peer_ffn(h, norm_g, w_q, keys1, keys2, u_tab, v_tab):
    b, s, d = h.shape
    T = b * s
    xt = rms_norm(h, norm_g).reshape(T, d)
    q = _mm(h, norm_g, w_q, True).reshape(T, PEER_HEADS, 2, PEER_QDIM // 2)
    s1 = jnp.einsum('thd,hkd->thk', q[:, :, 0], keys1)
    s2 = jnp.einsum('thd,hkd->thk', q[:, :, 1], keys2)
    v1, i1 = lax.top_k(s1, PEER_TOPK)
    v2, i2 = lax.top_k(s2, PEER_TOPK)
    cand = (v1[..., :, None] + v2[..., None, :]).reshape(T, PEER_HEADS, PEER_TOPK * PEER_TOPK)
    cand_id = (i1[..., :, None] * PEER_N_KEYS + i2[..., None, :]).reshape(T, PEER_HEADS, PEER_TOPK * PEER_TOPK)
    top_s, pos = lax.top_k(cand, PEER_TOPK)
    ids = jnp.take_along_axis(cand_id, pos, axis=-1)
    gate = jax.nn.softmax(top_s.astype(jnp.float32), axis=-1).astype(h.dtype)
    nch = T // PEER_TOKEN_CHUNK

    def chunk(args):
        xc, idc, gc = args
        act = jax.nn.gelu(jnp.einsum('cd,chkd->chk', xc, u_tab[idc]))
        return jnp.einsum('chk,chkd->cd', gc * act, v_tab[idc])

    out = lax.map(chunk, (xt.reshape(nch, PEER_TOKEN_CHUNK, d),
                          ids.reshape(nch, PEER_TOKEN_CHUNK, PEER_HEADS, PEER_TOPK),
                          gate.reshape(nch, PEER_TOKEN_CHUNK, PEER_HEADS, PEER_TOPK)))
    return out.reshape(b, s, d)


def per_layer_embed(h, p_i, norm_g, w_up, w_gate):
    return _mm(p_i, None, w_up, False) * jax.nn.sigmoid(_mm(h, norm_g, w_gate, True))


def kernel(x, p, a_norm_g, a_w_in, a_conv_w, a_conv_b, a_dt_bias, a_log, a_d_skip, a_gnorm_g, a_w_out,
           kv_norm_g, w_kv, cmp_pos_k, cmp_pos_v, cmp_w1_k, cmp_w2_k, cmp_w1_v, cmp_w2_v, rel_bias,
           b_norm_g, b_w_in, b_w_out, c_norm_g, c_w_q, c_keys1, c_keys2, c_u, c_v,
           e_norm_g, e_w_up, e_w_gate, final_g):
    h = x
    shared = None
    for i in range(DEPTH):
        if i < N_A_LAYERS:
            h = h + ssd_mixer(h, a_norm_g[i], a_w_in[i], a_conv_w[i], a_conv_b[i], a_dt_bias[i],
                              a_log[i], a_d_skip[i], a_gnorm_g[i], a_w_out[i])
        else:
            if shared is None:
                shared = nsa_shared_kv(h, kv_norm_g, w_kv, cmp_pos_k, cmp_pos_v,
                                       cmp_w1_k, cmp_w2_k, cmp_w1_v, cmp_w2_v)
            j = i - N_A_LAYERS
            h = h + nsa_mixer(h, b_norm_g[j], b_w_in[j], b_w_out[j], rel_bias, *shared)
        h = h + peer_ffn(h, c_norm_g[i], c_w_q[i], c_keys1[i], c_keys2[i], c_u[i], c_v[i])
        h = h + per_layer_embed(h, p[i], e_norm_g[i], e_w_up[i], e_w_gate[i])
    return rms_norm(h, final_g)
```

```python
import functools
import math

import jax
import jax.numpy as jnp
import numpy as np
from jax import lax
from jax.experimental import pallas as pl
from jax.experimental.pallas import tpu as pltpu

D_MODEL = 1024
BATCH = 2
SEQ = 16384
DEPTH = 2
N_A_LAYERS = DEPTH // 2
N_B_LAYERS = DEPTH - N_A_LAYERS

SSD_D_INNER = 2 * D_MODEL
SSD_HEADDIM = 64
SSD_HEADS = SSD_D_INNER // SSD_HEADDIM
SSD_GROUPS = 8
SSD_HEADS_PER_GROUP = SSD_HEADS // SSD_GROUPS
SSD_STATE = 128
SSD_CONV = 4
SSD_CHUNK = 128
SSD_CONV_DIM = SSD_D_INNER + 2 * SSD_GROUPS * SSD_STATE
SSD_IN_DIM = SSD_D_INNER + SSD_CONV_DIM + SSD_HEADS

NSA_HEADS = 16
NSA_KV_HEADS = 4
NSA_GROUP = NSA_HEADS // NSA_KV_HEADS
NSA_HEAD_DIM = 64
CMP_BLOCK = 32
CMP_STRIDE = 16
CMP_HIDDEN = 256
SLC_BLOCK = 64
SLC_TOPN = 16
SLC_FORCE = 1e4
WINDOW = 512
Q_BLOCK = 128
NSA_IN_DIM = NSA_HEADS * NSA_HEAD_DIM + 3 * NSA_HEADS
KV_DIM = 6 * NSA_KV_HEADS * NSA_HEAD_DIM

REL_BUCKETS = 32
REL_MAX_DISTANCE = 4096

PEER_HEADS = 8
PEER_N_KEYS = 128
PEER_EXPERTS = PEER_N_KEYS * PEER_N_KEYS
PEER_QDIM = 256
PEER_TOPK = 16
PEER_TOKEN_CHUNK = 128
PEER_V_SCALE = PEER_HEADS ** -0.5

PLE_DIM = 256
NORM_EPS = 1e-6
NEG_INF = -1e30

V7X_LANES = 128
V7X_VMEM_LIMIT_BYTES = 56 * 1024 * 1024


def _norm_matmul_kernel(x_ref, g_ref, w_ref, o_ref, *, normalize):
    x = x_ref[...]
    if normalize:
        x = x * lax.rsqrt(jnp.mean(x * x, axis=-1, keepdims=True) + NORM_EPS) * g_ref[...]
    o_ref[...] = jnp.dot(x.astype(jnp.bfloat16), w_ref[...], preferred_element_type=jnp.float32)


def norm_matmul(x, g, w, *, normalize, tm=512, tn=None):
    m, k = x.shape
    n = w.shape[1]
    n_pad = -(-n // V7X_LANES) * V7X_LANES
    wb = w.astype(jnp.bfloat16)
    if n_pad != n:
        wb = jnp.pad(wb, ((0, 0), (0, n_pad - n)))
    if tn is None:
        tn = n_pad
    assert m % tm == 0 and n_pad % tn == 0
    out = pl.pallas_call(
        functools.partial(_norm_matmul_kernel, normalize=normalize),
        grid=(m // tm, n_pad // tn),
        in_specs=[
            pl.BlockSpec((tm, k), lambda i, j: (i, 0)),
            pl.BlockSpec((1, k), lambda i, j: (0, 0)),
            pl.BlockSpec((k, tn), lambda i, j: (0, j)),
        ],
        out_specs=pl.BlockSpec((tm, tn), lambda i, j: (i, j)),
        out_shape=jax.ShapeDtypeStruct((m, n_pad), jnp.float32),
        compiler_params=pltpu.CompilerParams(
            dimension_semantics=("arbitrary", "arbitrary"),
            vmem_limit_bytes=V7X_VMEM_LIMIT_BYTES),
        name="norm_matmul" if normalize else "matmul",
    )(x, g.reshape(1, k), wb)
    return out[:, :n] if n_pad != n else out


def _mm(x3, g, w, normalize, **kw):
    b, s, k = x3.shape
    if g is None:
        g = jnp.ones((k,), jnp.float32)
    return norm_matmul(x3.reshape(b * s, k), g, w, normalize=normalize, **kw).reshape(b, s, w.shape[1])


def rms_norm(x, g):
    xf = x.astype(jnp.float32)
    y = xf * lax.rsqrt(jnp.mean(xf * xf, axis=-1, keepdims=True) + NORM_EPS)
    return (y * g.astype(jnp.float32)).astype(x.dtype)


def rel_bucket(dist):
    dist = jnp.maximum(dist, 0)
    max_exact = REL_BUCKETS // 2
    d = jnp.maximum(dist, 1).astype(jnp.float32)
    large = max_exact + (jnp.log(d / max_exact) / math.log(REL_MAX_DISTANCE / max_exact)
                         * (REL_BUCKETS - max_exact)).astype(jnp.int32)
    large = jnp.minimum(large, REL_BUCKETS - 1)
    return jnp.where(dist < max_exact, dist, large)


def ssd_mixer(h, norm_g, w_in, conv_w, conv_b, dt_bias, a_log, d_skip, gnorm_g, w_out):
    b, s, _ = h.shape
    G, R, P, N, L = SSD_GROUPS, SSD_HEADS_PER_GROUP, SSD_HEADDIM, SSD_STATE, SSD_CHUNK
    nc = s // L
    zxbcdt = _mm(h, norm_g, w_in, True, tm=256)
    z = zxbcdt[..., :SSD_D_INNER]
    xbc = zxbcdt[..., SSD_D_INNER:SSD_D_INNER + SSD_CONV_DIM]
    dt_raw = zxbcdt[..., SSD_D_INNER + SSD_CONV_DIM:]
    xbc = lax.conv_general_dilated(xbc, conv_w[:, None, :], window_strides=(1,),
                                   padding=[(SSD_CONV - 1, 0)],
                                   dimension_numbers=('NWC', 'WIO', 'NWC'),
                                   feature_group_count=SSD_CONV_DIM) + conv_b
    xbc = jax.nn.silu(xbc)
    x = xbc[..., :SSD_D_INNER].reshape(b, nc, L, G, R, P)
    bm = xbc[..., SSD_D_INNER:SSD_D_INNER + G * N].reshape(b, nc, L, G, N)
    cm = xbc[..., SSD_D_INNER + G * N:].reshape(b, nc, L, G, N)
    dt = jax.nn.softplus((dt_raw + dt_bias).astype(jnp.float32)).reshape(b, nc, L, G, R)
    a = -jnp.exp(a_log.astype(jnp.float32)).reshape(G, R)
    xd = x * dt[..., None].astype(x.dtype)
    a_dt = jnp.transpose(dt * a, (0, 3, 4, 1, 2))
    a_cs = jnp.cumsum(a_dt, axis=-1)
    causal = jnp.tril(jnp.ones((L, L), dtype=bool))
    seg = a_cs[..., :, None] - a_cs[..., None, :]
    decay_in = jnp.exp(jnp.where(causal, seg, -jnp.inf)).astype(x.dtype)
    cb = jnp.einsum('bclgn,bcsgn->bgcls', cm, bm)
    y_diag = jnp.einsum('bgrcls,bcsgrp->bclgrp', cb[:, :, None] * decay_in, xd)
    decay_to_end = jnp.exp(a_cs[..., -1:] - a_cs).astype(x.dtype)
    states = jnp.einsum('bcsgn,bgrcs,bcsgrp->bcgrpn', bm, decay_to_end, xd)
    chunk_decay = jnp.exp(a_cs[..., -1]).astype(x.dtype)

    def step(hh, inp):
        st, dec = inp
        return hh * dec[..., None, None] + st, hh

    h0 = jnp.zeros((b, G, R, P, N), x.dtype)
    _, prev = lax.scan(step, h0, (jnp.moveaxis(states, 1, 0), jnp.moveaxis(chunk_decay, -1, 0)))
    prev = jnp.moveaxis(prev, 0, 1)
    y_off = jnp.einsum('bclgn,bcgrpn,bgrcl->bclgrp', cm, prev, jnp.exp(a_cs).astype(x.dtype))
    y = (y_diag + y_off + x * d_skip.reshape(G, R, 1)).reshape(b, s, SSD_D_INNER)
    yg = (y * jax.nn.silu(z)).reshape(b, s, G, SSD_D_INNER // G)
    yg = rms_norm(yg, gnorm_g.reshape(G, -1)).reshape(b, s, SSD_D_INNER)
    return _mm(yg, None, w_out, False)


def _mlp_kernel(x_ref, w1_ref, w2_ref, o_ref):
    hid = jnp.dot(x_ref[...].astype(jnp.bfloat16), w1_ref[...], preferred_element_type=jnp.float32)
    hid = jax.nn.gelu(hid)
    o_ref[...] = jnp.dot(hid.astype(jnp.bfloat16), w2_ref[...], preferred_element_type=jnp.float32)


def mlp2(x, w1, w2, *, tm=512):
    m, k = x.shape
    hdim, n = w2.shape
    assert m % tm == 0
    return pl.pallas_call(
        _mlp_kernel,
        grid=(m // tm,),
        in_specs=[pl.BlockSpec((tm, k), lambda i: (i, 0)),
                  pl.BlockSpec((k, hdim), lambda i: (0, 0)),
                  pl.BlockSpec((hdim, n), lambda i: (0, 0))],
        out_specs=pl.BlockSpec((tm, n), lambda i: (i, 0)),
        out_shape=jax.ShapeDtypeStruct((m, n), jnp.float32),
        compiler_params=pltpu.CompilerParams(dimension_semantics=("arbitrary",),
                                             vmem_limit_bytes=V7X_VMEM_LIMIT_BYTES),
        name="cmp_mlp",
    )(x, w1.astype(jnp.bfloat16), w2.astype(jnp.bfloat16))


def compress_blocks(t, pos, w1, w2):
    b, s, g, d = t.shape
    halves = t.transpose(0, 2, 1, 3).reshape(b, g, s // CMP_STRIDE, CMP_STRIDE, d)
    nxt = jnp.concatenate([halves[:, :, 1:], jnp.zeros_like(halves[:, :, :1])], axis=2)
    blocks = jnp.concatenate([halves, nxt], axis=3) + pos
    flat = blocks.reshape(b * g * (s // CMP_STRIDE), CMP_BLOCK * d)
    return mlp2(flat, w1, w2).reshape(b, g, s // CMP_STRIDE, d)


N_SEL = SEQ // SLC_BLOCK
N_KT = SEQ // Q_BLOCK
CMP_FRONT = 256
CMP_ROWS = CMP_FRONT + SEQ // CMP_STRIDE
CMP_WIN = 256
REL_TABLE = 4096
NEAR_TILES = 24
WIN_TILES = WINDOW // Q_BLOCK + 1
QL = NSA_GROUP * Q_BLOCK


def _nsa_kernel(qT_ref, gT_ref, kc_ref, vcT_ref, kk_ref, vsT_ref, vwT_ref, at_ref, bcT_ref, bT_ref, bTw_ref,
                o_ref, s_scr, sel_scr):
    f32, bf16 = jnp.float32, jnp.bfloat16
    i = pl.program_id(2)
    qT = (qT_ref[0, 0, 0] * (NSA_HEAD_DIM ** -0.5)).astype(bf16)
    zq = jnp.zeros_like(qT)
    q_sel = jnp.concatenate([qT, zq], axis=0)
    q_win = jnp.concatenate([zq, qT], axis=0)

    r0 = pl.multiple_of(8 * i + 8, 8)
    s = jnp.dot(kc_ref[0, 0], qT, preferred_element_type=f32)
    rio = lax.broadcasted_iota(jnp.int32, (CMP_ROWS, Q_BLOCK), 0)
    rowmask = jnp.where((rio >= CMP_FRONT) & (rio < r0 + CMP_WIN), 0.0, NEG_INF)
    s_scr[...] = s + jnp.concatenate([rowmask] * NSA_GROUP, axis=1)
    s_scr[pl.ds(r0, CMP_WIN), :] = s_scr[pl.ds(r0, CMP_WIN), :] + bcT_ref[0]
    s = s_scr[...]
    m = jnp.max(s, axis=0, keepdims=True)
    p = jnp.exp(s - m)
    l = jnp.sum(p, axis=0, keepdims=True)
    inv = jnp.where(m > 0.1 * NEG_INF, 1.0 / l, 0.0)
    pn = p * inv
    o_c = jnp.dot(vcT_ref[0, 0], pn.astype(bf16), preferred_element_type=f32)

    psum = pn[:, 0:Q_BLOCK]
    for r in range(1, NSA_GROUP):
        psum = psum + pn[:, r * Q_BLOCK:(r + 1) * Q_BLOCK]
    hi = psum.astype(bf16)
    lo = (psum - hi.astype(f32)).astype(bf16)
    imp = (jnp.dot(at_ref[...], hi, preferred_element_type=f32)
           + jnp.dot(at_ref[...], lo, preferred_element_type=f32))
    blk = lax.broadcasted_iota(jnp.int32, (N_SEL, Q_BLOCK), 0)
    qi = lax.broadcasted_iota(jnp.int32, (N_SEL, Q_BLOCK), 1)
    cur = 2 * i + (qi >= SLC_BLOCK).astype(jnp.int32)
    forced = (blk == 0) | (blk == cur) | (blk == cur - 1)
    valid = blk <= cur
    score = jnp.where(forced, SLC_FORCE, jnp.where(valid, imp, -SLC_FORCE))
    blkf = blk.astype(f32)
    chosen = jnp.zeros((N_SEL, Q_BLOCK), f32)
    for _ in range(SLC_TOPN):
        mx = jnp.max(score, axis=0, keepdims=True)
        first = jnp.min(jnp.where(score == mx, blkf, float(N_SEL)), axis=0, keepdims=True)
        hit = blkf == first
        chosen = jnp.where(hit, 1.0, chosen)
        score = jnp.where(hit, -jnp.inf, score)
    sel_scr[...] = jnp.where((chosen > 0.5) & valid, 0.0, NEG_INF)

    def attend(kt, carry, q_aug, vT_ref, bias):
        m_run, l_run, acc = carry
        sc = jnp.dot(kk_ref[0, 0, kt], q_aug, preferred_element_type=f32) + bias
        m_new = jnp.maximum(m_run, jnp.max(sc, axis=0, keepdims=True))
        alpha = jnp.exp(m_run - m_new)
        pt = jnp.exp(sc - m_new)
        l_new = alpha * l_run + jnp.sum(pt, axis=0, keepdims=True)
        acc_new = alpha * acc + jnp.dot(vT_ref[0, 0, kt], pt.astype(bf16), preferred_element_type=f32)
        return m_new, l_new, acc_new

    def sel_mask(kt):
        m0 = jnp.broadcast_to(sel_scr[pl.ds(2 * kt, 1), :], (SLC_BLOCK, Q_BLOCK))
        m1 = jnp.broadcast_to(sel_scr[pl.ds(2 * kt + 1, 1), :], (SLC_BLOCK, Q_BLOCK))
        mk = jnp.concatenate([m0, m1], axis=0)
        return jnp.concatenate([mk] * NSA_GROUP, axis=1)

    init = (jnp.full((1, QL), NEG_INF, f32), jnp.zeros((1, QL), f32), jnp.zeros((NSA_HEAD_DIM, QL), f32))
    first_near = jnp.maximum(i - (NEAR_TILES - 1), 0)
    carry = lax.fori_loop(0, first_near,
                          lambda kt, c: attend(kt, c, q_sel, vsT_ref, sel_mask(kt)), init)
    _, l_s, acc_s = lax.fori_loop(first_near, i + 1,
                                  lambda kt, c: attend(kt, c, q_sel, vsT_ref, sel_mask(kt) + bT_ref[0, i - kt]),
                                  carry)
    o_s = acc_s / l_s

    first_win = jnp.maximum(i - (WIN_TILES - 1), 0)
    _, l_w, acc_w = lax.fori_loop(first_win, i + 1,
                                  lambda kt, c: attend(kt, c, q_win, vwT_ref, bTw_ref[0, i - kt]), init)
    o_w = acc_w / l_w

    gate = jax.nn.sigmoid(gT_ref[0, 0, 0])
    o_ref[0, 0, 0] = gate[0:1] * o_c + gate[1:2] * o_s + gate[2:3] * o_w


def _bias_tables(rel_bias):
    G, R = NSA_KV_HEADS, NSA_GROUP
    bd = rel_bias[rel_bucket(jnp.arange(REL_TABLE))]
    bd = (bd - bd[REL_TABLE - 1]).T

    def tiles(dist, ok):
        vals = jnp.take(bd, jnp.asarray(np.clip(dist, 0, REL_TABLE - 1), jnp.int32), axis=1)
        return jnp.where(jnp.asarray(ok), vals, NEG_INF)

    kj = np.arange(Q_BLOCK)[None, :, None]
    qi = np.arange(Q_BLOCK)[None, None, :]
    d_near = Q_BLOCK * np.arange(NEAR_TILES)[:, None, None] + qi - kj
    b_near = tiles(d_near, d_near >= 0)
    d_win = d_near[:WIN_TILES]
    b_win = tiles(d_win, (d_win >= 0) & (d_win < WINDOW))
    row = np.arange(CMP_WIN)[:, None]
    d_cmp = np.arange(Q_BLOCK)[None, :] - CMP_STRIDE * (row - (CMP_WIN - 8)) - (CMP_BLOCK - 1)
    b_cmp = tiles(d_cmp, d_cmp >= 0)

    def lanes(x):
        x = x.reshape((G, R) + x.shape[1:])
        x = jnp.moveaxis(x, 1, -2)
        return x.reshape(x.shape[:-2] + (QL,))

    return lanes(b_cmp), lanes(b_near), lanes(b_win)


def _pool_matrix():
    per = SLC_BLOCK // CMP_STRIDE
    n_span = CMP_BLOCK // CMP_STRIDE
    k = np.arange(CMP_ROWS)[None, :] - CMP_FRONT
    j = np.arange(N_SEL)[:, None]
    return jnp.asarray((k >= per * j - (n_span - 1)) & (k <= per * j + per - 1), jnp.bfloat16)


def nsa_shared_kv(h, kv_norm_g, w_kv, cmp_pos_k, cmp_pos_v, cmp_w1_k, cmp_w2_k, cmp_w1_v, cmp_w2_v):
    b, s, _ = h.shape
    G, DH = NSA_KV_HEADS, NSA_HEAD_DIM
    bf16 = jnp.bfloat16
    kv = _mm(h, kv_norm_g, w_kv, True).reshape(b, s, 6, G, DH)
    k_cmp = compress_blocks(kv[:, :, 0], cmp_pos_k, cmp_w1_k, cmp_w2_k)
    v_cmp = compress_blocks(kv[:, :, 1], cmp_pos_v, cmp_w1_v, cmp_w2_v)
    front = jnp.zeros((b, G, CMP_FRONT, DH), bf16)
    kc = jnp.concatenate([front, k_cmp.astype(bf16)], axis=2)
    vcT = jnp.concatenate([front, v_cmp.astype(bf16)], axis=2).transpose(0, 1, 3, 2)
    kk = jnp.concatenate([kv[:, :, 2], kv[:, :, 4]], axis=-1).astype(bf16)
    kk = kk.transpose(0, 2, 1, 3).reshape(b, G, N_KT, Q_BLOCK, 2 * DH)

    def vt(v):
        return v.astype(bf16).transpose(0, 2, 1, 3).reshape(b, G, N_KT, Q_BLOCK, DH).transpose(0, 1, 2, 4, 3)

    return kc, vcT, kk, vt(kv[:, :, 3]), vt(kv[:, :, 5])


def nsa_mixer(h, norm_g, w_in, w_out, rel_bias, kc, vcT, kk, vsT, vwT):
    b, s, _ = h.shape
    G, R, DH = NSA_KV_HEADS, NSA_GROUP, NSA_HEAD_DIM
    nb = s // Q_BLOCK
    proj = _mm(h, norm_g, w_in, True)
    qT = proj[..., :NSA_HEADS * DH].reshape(b, nb, Q_BLOCK, G, R, DH).transpose(0, 3, 1, 5, 4, 2)
    qT = qT.reshape(b, G, nb, DH, QL)
    gT = proj[..., NSA_HEADS * DH:].reshape(b, nb, Q_BLOCK, G, R, 3).transpose(0, 3, 1, 5, 4, 2)
    gT = gT.reshape(b, G, nb, 3, QL)
    b_cmp, b_near, b_win = _bias_tables(rel_bias)
    at = _pool_matrix()

    per_bg = lambda *blk: pl.BlockSpec((1, 1) + blk, lambda bi, gi, i: (bi, gi) + (0,) * len(blk))
    per_g = lambda *blk: pl.BlockSpec((1,) + blk, lambda bi, gi, i: (gi,) + (0,) * len(blk))
    oT = pl.pallas_call(
        _nsa_kernel,
        grid=(b, G, nb),
        in_specs=[
            pl.BlockSpec((1, 1, 1, DH, QL), lambda bi, gi, i: (bi, gi, i, 0, 0)),
            pl.BlockSpec((1, 1, 1, 3, QL), lambda bi, gi, i: (bi, gi, i, 0, 0)),
            per_bg(CMP_ROWS, DH),
            per_bg(DH, CMP_ROWS),
            per_bg(N_KT, Q_BLOCK, 2 * DH),
            per_bg(N_KT, DH, Q_BLOCK),
            per_bg(N_KT, DH, Q_BLOCK),
            pl.BlockSpec((N_SEL, CMP_ROWS), lambda bi, gi, i: (0, 0)),
            per_g(CMP_WIN, QL),
            per_g(NEAR_TILES, Q_BLOCK, QL),
            per_g(WIN_TILES, Q_BLOCK, QL),
        ],
        out_specs=pl.BlockSpec((1, 1, 1, DH, QL), lambda bi, gi, i: (bi, gi, i, 0, 0)),
        out_shape=jax.ShapeDtypeStruct((b, G, nb, DH, QL), jnp.float32),
        scratch_shapes=[pltpu.VMEM((CMP_ROWS, QL), jnp.float32), pltpu.VMEM((N_SEL, Q_BLOCK), jnp.float32)],
        compiler_params=pltpu.CompilerParams(
            dimension_semantics=("arbitrary", "arbitrary", "arbitrary"),
            vmem_limit_bytes=V7X_VMEM_LIMIT_BYTES),
        name="nsa_attention",
    )(qT, gT, kc, vcT, kk, vsT, vwT, at, b_cmp, b_near, b_win)
    o = oT.reshape(b, G, nb, DH, R, Q_BLOCK).transpose(0, 2, 5, 1, 4, 3).reshape(b, s, NSA_HEADS * DH)
    return _mm(o, None, w_out, False)


PEER_SLOTS = PEER_HEADS * PEER_TOPK
PEER_ROUTE_TM = 256
PEER_TT = 8
PEER_CHUNKS = D_MODEL // V7X_LANES
PEER_SLAB = 2 * PEER_CHUNKS
PEER_SLAB_STRIDE = 24


def _top_rows(s, rowid, n, payload=None):
    vals, picks = [], []
    for _ in range(n):
        mx = jnp.max(s, axis=0, keepdims=True)
        first = jnp.min(jnp.where(s == mx, rowid, float(s.shape[0])), axis=0, keepdims=True)
        hit = rowid == first
        vals.append(mx)
        picks.append(first if payload is None else jnp.max(jnp.where(hit, payload, -1.0), axis=0, keepdims=True))
        s = jnp.where(hit, -jnp.inf, s)
    return vals, picks


def _peer_route_kernel(h_ref, g_ref, wqT_ref, k1_ref, k2_ref, xn_ref, ids_ref, gate_ref):
    f32, bf16 = jnp.float32, jnp.bfloat16
    x = h_ref[...]
    xn = x * lax.rsqrt(jnp.mean(x * x, axis=-1, keepdims=True) + NORM_EPS) * g_ref[...]
    xn_ref[...] = xn
    qT = lax.dot_general(wqT_ref[...], xn.astype(bf16), (((1,), (1,)), ((), ())), preferred_element_type=f32)
    half = PEER_QDIM // 2
    tm = x.shape[0]
    key_id = lax.broadcasted_iota(jnp.int32, (PEER_N_KEYS, tm), 0).astype(f32)
    cand_pos = lax.broadcasted_iota(jnp.int32, (PEER_TOPK * PEER_TOPK, tm), 0).astype(f32)
    for hd in range(PEER_HEADS):
        q1 = qT[hd * PEER_QDIM: hd * PEER_QDIM + half].astype(bf16)
        q2 = qT[hd * PEER_QDIM + half: (hd + 1) * PEER_QDIM].astype(bf16)
        s1 = jnp.dot(k1_ref[hd], q1, preferred_element_type=f32)
        s2 = jnp.dot(k2_ref[hd], q2, preferred_element_type=f32)
        v1, i1 = _top_rows(s1, key_id, PEER_TOPK)
        v2, i2 = _top_rows(s2, key_id, PEER_TOPK)
        v2m = jnp.concatenate(v2, axis=0)
        i2m = jnp.concatenate(i2, axis=0)
        cand = jnp.concatenate([v1[a] + v2m for a in range(PEER_TOPK)], axis=0)
        cand_id = jnp.concatenate([i1[a] * float(PEER_N_KEYS) + i2m for a in range(PEER_TOPK)], axis=0)
        top_s, ids = _top_rows(cand, cand_pos, PEER_TOPK, payload=cand_id)
        top_s = jnp.concatenate(top_s, axis=0)
        e = jnp.exp(top_s - top_s[0:1])
        gate_ref[hd * PEER_TOPK:(hd + 1) * PEER_TOPK, :] = e / jnp.sum(e, axis=0, keepdims=True)
        ids_ref[hd * PEER_TOPK:(hd + 1) * PEER_TOPK, :] = jnp.concatenate(ids, axis=0).astype(jnp.int32)


def peer_route(h2, norm_g, w_q, keys1, keys2):
    T, d = h2.shape
    tm = PEER_ROUTE_TM
    bf16 = jnp.bfloat16
    nq = PEER_HEADS * PEER_QDIM
    half = PEER_QDIM // 2
    return pl.pallas_call(
        _peer_route_kernel,
        grid=(T // tm,),
        in_specs=[pl.BlockSpec((tm, d), lambda i: (i, 0)),
                  pl.BlockSpec((1, d), lambda i: (0, 0)),
                  pl.BlockSpec((nq, d), lambda i: (0, 0)),
                  pl.BlockSpec((PEER_HEADS, PEER_N_KEYS, half), lambda i: (0, 0, 0)),
                  pl.BlockSpec((PEER_HEADS, PEER_N_KEYS, half), lambda i: (0, 0, 0))],
        out_specs=[pl.BlockSpec((tm, d), lambda i: (i, 0)),
                   pl.BlockSpec((PEER_SLOTS, tm), lambda i: (0, i)),
                   pl.BlockSpec((PEER_SLOTS, tm), lambda i: (0, i))],
        out_shape=[jax.ShapeDtypeStruct((T, d), jnp.float32),
                   jax.ShapeDtypeStruct((PEER_SLOTS, T), jnp.int32),
                   jax.ShapeDtypeStruct((PEER_SLOTS, T), jnp.float32)],
        compiler_params=pltpu.CompilerParams(dimension_semantics=("arbitrary",),
                                             vmem_limit_bytes=V7X_VMEM_LIMIT_BYTES),
        name="peer_route",
    )(h2, norm_g.reshape(1, d), w_q.T.astype(bf16), keys1.astype(bf16), keys2.astype(bf16))


def _peer_slab(buf, slot, r):
    return buf.at[slot, pl.ds(pl.multiple_of(r * PEER_SLAB_STRIDE, 8), PEER_SLAB)]


def _peer_expert_kernel(ids0_ref, idsn_ref, xn_ref, gt_ref, uv_hbm, o_ref, buf, sem):
    i = pl.program_id(0)
    n = pl.num_programs(0)
    slot = i % 2
    groups = PEER_TT * PEER_SLOTS // 8

    def issue(ids_ref, dst_slot):
        def group(gidx, carry):
            t = gidx // (PEER_SLOTS // 8)
            k0 = (gidx % (PEER_SLOTS // 8)) * 8
            for u in range(8):
                e = ids_ref[t, k0 + u]
                pltpu.make_async_copy(uv_hbm.at[e], _peer_slab(buf, dst_slot, gidx * 8 + u), sem.at[dst_slot]).start()
            return carry
        lax.fori_loop(0, groups, group, 0)

    @pl.when(i == 0)
    def _():
        issue(ids0_ref, 0)

    @pl.when(i + 1 < n)
    def _():
        issue(idsn_ref, 1 - slot)

    def wait_group(gidx, carry):
        for u in range(8):
            pltpu.make_async_copy(uv_hbm.at[0], _peer_slab(buf, slot, gidx * 8 + u), sem.at[slot]).wait()
        return carry
    lax.fori_loop(0, groups, wait_group, 0)

    for t in range(PEER_TT):
        base = t * PEER_SLOTS * PEER_SLAB_STRIDE
        xrow = xn_ref[t:t + 1, :]
        acc = jnp.zeros((PEER_SLOTS, V7X_LANES), jnp.float32)
        for j in range(PEER_CHUNKS):
            tile = buf[slot, pl.ds(base + j, PEER_SLOTS, stride=PEER_SLAB_STRIDE), :]
            acc = acc + tile * xrow[:, j * V7X_LANES:(j + 1) * V7X_LANES]
        act = jnp.sum(acc, axis=1, keepdims=True)
        w = gt_ref[0, :, t:t + 1] * jax.nn.gelu(act)
        for j in range(PEER_CHUNKS):
            tile = buf[slot, pl.ds(base + PEER_CHUNKS + j, PEER_SLOTS, stride=PEER_SLAB_STRIDE), :]
            o_ref[t:t + 1, j * V7X_LANES:(j + 1) * V7X_LANES] = jnp.sum(tile * w, axis=0, keepdims=True)


def peer_experts(xn, idsT, gateT, u_tab, v_tab):
    T, d = xn.shape
    tt = PEER_TT
    nt = T // tt
    uv = jnp.concatenate([u_tab.reshape(PEER_EXPERTS, PEER_CHUNKS, V7X_LANES),
                          v_tab.reshape(PEER_EXPERTS, PEER_CHUNKS, V7X_LANES)], axis=1)
    ids = idsT.T
    gt = gateT.reshape(PEER_SLOTS, nt, tt).transpose(1, 0, 2)
    return pl.pallas_call(
        _peer_expert_kernel,
        grid=(nt,),
        in_specs=[pl.BlockSpec((tt, PEER_SLOTS), lambda i: (0, 0), memory_space=pltpu.SMEM),
                  pl.BlockSpec((tt, PEER_SLOTS), lambda i: (jnp.minimum(i + 1, nt - 1), 0), memory_space=pltpu.SMEM),
                  pl.BlockSpec((tt, d), lambda i: (i, 0)),
                  pl.BlockSpec((1, PEER_SLOTS, tt), lambda i: (i, 0, 0)),
                  pl.BlockSpec(memory_space=pl.ANY)],
        out_specs=pl.BlockSpec((tt, d), lambda i: (i, 0)),
        out_shape=jax.ShapeDtypeStruct((T, d), jnp.float32),
        scratch_shapes=[pltpu.VMEM((2, tt * PEER_SLOTS * PEER_SLAB_STRIDE, V7X_LANES), jnp.float32),
                        pltpu.SemaphoreType.DMA((2,))],
        compiler_params=pltpu.CompilerParams(dimension_semantics=("arbitrary",),
                                             vmem_limit_bytes=V7X_VMEM_LIMIT_BYTES),
        name="peer_experts",
    )(ids, ids, xn, gt, uv)


def peer_ffn(h, norm_g, w_q, keys1, keys2, u_tab, v_tab):
    b, s, d = h.shape
    xn, idsT, gateT = peer_route(h.reshape(b * s, d), norm_g, w_q, keys1, keys2)
    return peer_experts(xn, idsT, gateT, u_tab, v_tab).reshape(b, s, d)


def per_layer_embed(h, p_i, norm_g, w_up, w_gate):
    return _mm(p_i, None, w_up, False) * jax.nn.sigmoid(_mm(h, norm_g, w_gate, True))


def kernel(x, p, a_norm_g, a_w_in, a_conv_w, a_conv_b, a_dt_bias, a_log, a_d_skip, a_gnorm_g, a_w_out,
           kv_norm_g, w_kv, cmp_pos_k, cmp_pos_v, cmp_w1_k, cmp_w2_k, cmp_w1_v, cmp_w2_v, rel_bias,
           b_norm_g, b_w_in, b_w_out, c_norm_g, c_w_q, c_keys1, c_keys2, c_u, c_v,
           e_norm_g, e_w_up, e_w_gate, final_g):
    h = x
    shared = None
    for i in range(DEPTH):
        if i < N_A_LAYERS:
            h = h + ssd_mixer(h, a_norm_g[i], a_w_in[i], a_conv_w[i], a_conv_b[i], a_dt_bias[i],
                              a_log[i], a_d_skip[i], a_gnorm_g[i], a_w_out[i])
        else:
            if shared is None:
                shared = nsa_shared_kv(h, kv_norm_g, w_kv, cmp_pos_k, cmp_pos_v,
                                       cmp_w1_k, cmp_w2_k, cmp_w1_v, cmp_w2_v)
            j = i - N_A_LAYERS
            h = h + nsa_mixer(h, b_norm_g[j], b_w_in[j], b_w_out[j], rel_bias, *shared)
        h = h + peer_ffn(h, c_norm_g[i], c_w_q[i], c_keys1[i], c_keys2[i], c_u[i], c_v[i])
        h = h + per_layer_embed(h, p[i], e_norm_g[i], e_w_up[i], e_w_gate[i])
    return rms_norm(h, final_g)
```

```python
import functools
import math

import jax
import jax.numpy as jnp
import numpy as np
from jax import lax
from jax.experimental import pallas as pl
from jax.experimental.pallas import tpu as pltpu

D_MODEL = 1024
BATCH = 2
SEQ = 16384
DEPTH = 2
N_A_LAYERS = DEPTH // 2
N_B_LAYERS = DEPTH - N_A_LAYERS

SSD_D_INNER = 2 * D_MODEL
SSD_HEADDIM = 64
SSD_HEADS = SSD_D_INNER // SSD_HEADDIM
SSD_GROUPS = 8
SSD_HEADS_PER_GROUP = SSD_HEADS // SSD_GROUPS
SSD_STATE = 128
SSD_CONV = 4
SSD_CHUNK = 128
SSD_CONV_DIM = SSD_D_INNER + 2 * SSD_GROUPS * SSD_STATE
SSD_IN_DIM = SSD_D_INNER + SSD_CONV_DIM + SSD_HEADS

NSA_HEADS = 16
NSA_KV_HEADS = 4
NSA_GROUP = NSA_HEADS // NSA_KV_HEADS
NSA_HEAD_DIM = 64
CMP_BLOCK = 32
CMP_STRIDE = 16
CMP_HIDDEN = 256
SLC_BLOCK = 64
SLC_TOPN = 16
SLC_FORCE = 1e4
WINDOW = 512
Q_BLOCK = 128
NSA_IN_DIM = NSA_HEADS * NSA_HEAD_DIM + 3 * NSA_HEADS
KV_DIM = 6 * NSA_KV_HEADS * NSA_HEAD_DIM

REL_BUCKETS = 32
REL_MAX_DISTANCE = 4096

PEER_HEADS = 8
PEER_N_KEYS = 128
PEER_EXPERTS = PEER_N_KEYS * PEER_N_KEYS
PEER_QDIM = 256
PEER_TOPK = 16
PEER_TOKEN_CHUNK = 128
PEER_V_SCALE = PEER_HEADS ** -0.5

PLE_DIM = 256
NORM_EPS = 1e-6
NEG_INF = -1e30

V7X_LANES = 128
V7X_VMEM_LIMIT_BYTES = 56 * 1024 * 1024


def _norm_matmul_kernel(x_ref, g_ref, w_ref, o_ref, *, normalize):
    x = x_ref[...]
    if normalize:
        x = x * lax.rsqrt(jnp.mean(x * x, axis=-1, keepdims=True) + NORM_EPS) * g_ref[...]
    o_ref[...] = jnp.dot(x.astype(jnp.bfloat16), w_ref[...], preferred_element_type=jnp.float32)


def norm_matmul(x, g, w, *, normalize, tm=512, tn=None):
    m, k = x.shape
    n = w.shape[1]
    n_pad = -(-n // V7X_LANES) * V7X_LANES
    wb = w.astype(jnp.bfloat16)
    if n_pad != n:
        wb = jnp.pad(wb, ((0, 0), (0, n_pad - n)))
    if tn is None:
        tn = n_pad
    assert m % tm == 0 and n_pad % tn == 0
    out = pl.pallas_call(
        functools.partial(_norm_matmul_kernel, normalize=normalize),
        grid=(m // tm, n_pad // tn),
        in_specs=[
            pl.BlockSpec((tm, k), lambda i, j: (i, 0)),
            pl.BlockSpec((1, k), lambda i, j: (0, 0)),
            pl.BlockSpec((k, tn), lambda i, j: (0, j)),
        ],
        out_specs=pl.BlockSpec((tm, tn), lambda i, j: (i, j)),
        out_shape=jax.ShapeDtypeStruct((m, n_pad), jnp.float32),
        compiler_params=pltpu.CompilerParams(
            dimension_semantics=("arbitrary", "arbitrary"),
            vmem_limit_bytes=V7X_VMEM_LIMIT_BYTES),
        name="norm_matmul" if normalize else "matmul",
    )(x, g.reshape(1, k), wb)
    return out[:, :n] if n_pad != n else out


def _mm(x3, g, w, normalize, **kw):
    b, s, k = x3.shape
    if g is None:
        g = jnp.ones((k,), jnp.float32)
    return norm_matmul(x3.reshape(b * s, k), g, w, normalize=normalize, **kw).reshape(b, s, w.shape[1])


def rms_norm(x, g):
    xf = x.astype(jnp.float32)
    y = xf * lax.rsqrt(jnp.mean(xf * xf, axis=-1, keepdims=True) + NORM_EPS)
    return (y * g.astype(jnp.float32)).astype(x.dtype)


def rel_bucket(dist):
    dist = jnp.maximum(dist, 0)
    max_exact = REL_BUCKETS // 2
    d = jnp.maximum(dist, 1).astype(jnp.float32)
    large = max_exact + (jnp.log(d / max_exact) / math.log(REL_MAX_DISTANCE / max_exact)
                         * (REL_BUCKETS - max_exact)).astype(jnp.int32)
    large = jnp.minimum(large, REL_BUCKETS - 1)
    return jnp.where(dist < max_exact, dist, large)


def ssd_mixer(h, norm_g, w_in, conv_w, conv_b, dt_bias, a_log, d_skip, gnorm_g, w_out):
    b, s, _ = h.shape
    G, R, P, N, L = SSD_GROUPS, SSD_HEADS_PER_GROUP, SSD_HEADDIM, SSD_STATE, SSD_CHUNK
    nc = s // L
    zxbcdt = _mm(h, norm_g, w_in, True, tm=256)
    z = zxbcdt[..., :SSD_D_INNER]
    xbc = zxbcdt[..., SSD_D_INNER:SSD_D_INNER + SSD_CONV_DIM]
    dt_raw = zxbcdt[..., SSD_D_INNER + SSD_CONV_DIM:]
    xbc = lax.conv_general_dilated(xbc, conv_w[:, None, :], window_strides=(1,),
                                   padding=[(SSD_CONV - 1, 0)],
                                   dimension_numbers=('NWC', 'WIO', 'NWC'),
                                   feature_group_count=SSD_CONV_DIM) + conv_b
    xbc = jax.nn.silu(xbc)
    x = xbc[..., :SSD_D_INNER].reshape(b, nc, L, G, R, P)
    bm = xbc[..., SSD_D_INNER:SSD_D_INNER + G * N].reshape(b, nc, L, G, N)
    cm = xbc[..., SSD_D_INNER + G * N:].reshape(b, nc, L, G, N)
    dt = jax.nn.softplus((dt_raw + dt_bias).astype(jnp.float32)).reshape(b, nc, L, G, R)
    a = -jnp.exp(a_log.astype(jnp.float32)).reshape(G, R)
    xd = x * dt[..., None].astype(x.dtype)
    a_dt = jnp.transpose(dt * a, (0, 3, 4, 1, 2))
    a_cs = jnp.cumsum(a_dt, axis=-1)
    causal = jnp.tril(jnp.ones((L, L), dtype=bool))
    seg = a_cs[..., :, None] - a_cs[..., None, :]
    decay_in = jnp.exp(jnp.where(causal, seg, -jnp.inf)).astype(x.dtype)
    cb = jnp.einsum('bclgn,bcsgn->bgcls', cm, bm)
    y_diag = jnp.einsum('bgrcls,bcsgrp->bclgrp', cb[:, :, None] * decay_in, xd)
    decay_to_end = jnp.exp(a_cs[..., -1:] - a_cs).astype(x.dtype)
    states = jnp.einsum('bcsgn,bgrcs,bcsgrp->bcgrpn', bm, decay_to_end, xd)
    chunk_decay = jnp.exp(a_cs[..., -1]).astype(x.dtype)

    def step(hh, inp):
        st, dec = inp
        return hh * dec[..., None, None] + st, hh

    h0 = jnp.zeros((b, G, R, P, N), x.dtype)
    _, prev = lax.scan(step, h0, (jnp.moveaxis(states, 1, 0), jnp.moveaxis(chunk_decay, -1, 0)))
    prev = jnp.moveaxis(prev, 0, 1)
    y_off = jnp.einsum('bclgn,bcgrpn,bgrcl->bclgrp', cm, prev, jnp.exp(a_cs).astype(x.dtype))
    y = (y_diag + y_off + x * d_skip.reshape(G, R, 1)).reshape(b, s, SSD_D_INNER)
    yg = (y * jax.nn.silu(z)).reshape(b, s, G, SSD_D_INNER // G)
    yg = rms_norm(yg, gnorm_g.reshape(G, -1)).reshape(b, s, SSD_D_INNER)
    return _mm(yg, None, w_out, False)


def _mlp_kernel(x_ref, w1_ref, w2_ref, o_ref):
    hid = jnp.dot(x_ref[...].astype(jnp.bfloat16), w1_ref[...], preferred_element_type=jnp.float32)
    hid = jax.nn.gelu(hid)
    o_ref[...] = jnp.dot(hid.astype(jnp.bfloat16), w2_ref[...], preferred_element_type=jnp.float32)


def mlp2(x, w1, w2, *, tm=512):
    m, k = x.shape
    hdim, n = w2.shape
    assert m % tm == 0
    return pl.pallas_call(
        _mlp_kernel,
        grid=(m // tm,),
        in_specs=[pl.BlockSpec((tm, k), lambda i: (i, 0)),
                  pl.BlockSpec((k, hdim), lambda i: (0, 0)),
                  pl.BlockSpec((hdim, n), lambda i: (0, 0))],
        out_specs=pl.BlockSpec((tm, n), lambda i: (i, 0)),
        out_shape=jax.ShapeDtypeStruct((m, n), jnp.float32),
        compiler_params=pltpu.CompilerParams(dimension_semantics=("arbitrary",),
                                             vmem_limit_bytes=V7X_VMEM_LIMIT_BYTES),
        name="cmp_mlp",
    )(x, w1.astype(jnp.bfloat16), w2.astype(jnp.bfloat16))


def compress_blocks(t, pos, w1, w2):
    b, s, g, d = t.shape
    halves = t.transpose(0, 2, 1, 3).reshape(b, g, s // CMP_STRIDE, CMP_STRIDE, d)
    nxt = jnp.concatenate([halves[:, :, 1:], jnp.zeros_like(halves[:, :, :1])], axis=2)
    blocks = jnp.concatenate([halves, nxt], axis=3) + pos
    flat = blocks.reshape(b * g * (s // CMP_STRIDE), CMP_BLOCK * d)
    return mlp2(flat, w1, w2).reshape(b, g, s // CMP_STRIDE, d)


N_SEL = SEQ // SLC_BLOCK
N_KT = SEQ // Q_BLOCK
CMP_FRONT = 256
CMP_ROWS = CMP_FRONT + SEQ // CMP_STRIDE
CMP_WIN = 256
REL_TABLE = 4096
NEAR_TILES = 24
WIN_TILES = WINDOW // Q_BLOCK + 1
SEL_GROUP = 4
QL = NSA_GROUP * Q_BLOCK


def _nsa_kernel(qT_ref, gT_ref, kc_ref, vcT_ref, kk_ref, vsT_ref, vwT_ref, at_ref, bcT_ref, bT_ref, bTw_ref,
                o_ref, s_scr, sel_scr):
    f32, bf16 = jnp.float32, jnp.bfloat16
    i = pl.program_id(2)
    qT = (qT_ref[0, 0, 0] * (NSA_HEAD_DIM ** -0.5)).astype(bf16)
    zq = jnp.zeros_like(qT)
    q_sel = jnp.concatenate([qT, zq], axis=0)
    q_win = jnp.concatenate([zq, qT], axis=0)

    r0 = pl.multiple_of(8 * i + 8, 8)
    s = jnp.dot(kc_ref[0, 0], qT, preferred_element_type=f32)
    rio = lax.broadcasted_iota(jnp.int32, (CMP_ROWS, Q_BLOCK), 0)
    rowmask = jnp.where((rio >= CMP_FRONT) & (rio < r0 + CMP_WIN), 0.0, NEG_INF)
    s_scr[...] = s + jnp.concatenate([rowmask] * NSA_GROUP, axis=1)
    s_scr[pl.ds(r0, CMP_WIN), :] = s_scr[pl.ds(r0, CMP_WIN), :] + bcT_ref[0]
    s = s_scr[...]
    m = jnp.max(s, axis=0, keepdims=True)
    p = jnp.exp(s - m)
    l = jnp.sum(p, axis=0, keepdims=True)
    inv = jnp.where(m > 0.1 * NEG_INF, 1.0 / l, 0.0)
    pn = p * inv
    o_c = jnp.dot(vcT_ref[0, 0], pn.astype(bf16), preferred_element_type=f32)

    psum = pn[:, 0:Q_BLOCK]
    for r in range(1, NSA_GROUP):
        psum = psum + pn[:, r * Q_BLOCK:(r + 1) * Q_BLOCK]
    hi = psum.astype(bf16)
    lo = (psum - hi.astype(f32)).astype(bf16)
    imp = (jnp.dot(at_ref[...], hi, preferred_element_type=f32)
           + jnp.dot(at_ref[...], lo, preferred_element_type=f32))
    blk = lax.broadcasted_iota(jnp.int32, (N_SEL, Q_BLOCK), 0)
    qi = lax.broadcasted_iota(jnp.int32, (N_SEL, Q_BLOCK), 1)
    cur = 2 * i + (qi >= SLC_BLOCK).astype(jnp.int32)
    forced = (blk == 0) | (blk == cur) | (blk == cur - 1)
    valid = blk <= cur
    score = jnp.where(forced, SLC_FORCE, jnp.where(valid, imp, -SLC_FORCE))
    blkf = blk.astype(f32)
    chosen = jnp.zeros((N_SEL, Q_BLOCK), f32)
    for _ in range(SLC_TOPN):
        mx = jnp.max(score, axis=0, keepdims=True)
        first = jnp.min(jnp.where(score == mx, blkf, float(N_SEL)), axis=0, keepdims=True)
        hit = blkf == first
        chosen = jnp.where(hit, 1.0, chosen)
        score = jnp.where(hit, -jnp.inf, score)
    sel_scr[...] = jnp.where((chosen > 0.5) & valid, 0.0, NEG_INF)

    def attend(kt, carry, q_aug, vT_ref, bias):
        m_run, l_run, acc = carry
        scs = [jnp.dot(kk_ref[0, 0, k], q_aug, preferred_element_type=f32) + b for k, b in zip(kt, bias)]
        m_new = m_run
        for sc in scs:
            m_new = jnp.maximum(m_new, jnp.max(sc, axis=0, keepdims=True))
        alpha = jnp.exp(m_run - m_new)
        l_new = alpha * l_run
        acc_new = alpha * acc
        for k, sc in zip(kt, scs):
            pt = jnp.exp(sc - m_new)
            l_new = l_new + jnp.sum(pt, axis=0, keepdims=True)
            acc_new = acc_new + jnp.dot(vT_ref[0, 0, k], pt.astype(bf16), preferred_element_type=f32)
        return m_new, l_new, acc_new

    def sel_mask(kt):
        m0 = jnp.broadcast_to(sel_scr[pl.ds(2 * kt, 1), :], (SLC_BLOCK, Q_BLOCK))
        m1 = jnp.broadcast_to(sel_scr[pl.ds(2 * kt + 1, 1), :], (SLC_BLOCK, Q_BLOCK))
        mk = jnp.concatenate([m0, m1], axis=0)
        return jnp.concatenate([mk] * NSA_GROUP, axis=1)

    init = (jnp.full((1, QL), NEG_INF, f32), jnp.zeros((1, QL), f32), jnp.zeros((NSA_HEAD_DIM, QL), f32))
    U = SEL_GROUP

    def far_group(g, c):
        kts = [g * U + u for u in range(U)]
        return attend(kts, c, q_sel, vsT_ref, [sel_mask(k) for k in kts])

    def near_group(g, c):
        kts, biases = [], []
        for u in range(U):
            delta = i - (g * U + u)
            k = jnp.minimum(g * U + u, i)
            tile = jnp.where(delta < 0, NEAR_TILES + 1, jnp.minimum(delta, NEAR_TILES))
            kts.append(k)
            biases.append(sel_mask(k) + bT_ref[0, tile])
        return attend(kts, c, q_sel, vsT_ref, biases)

    n_far = jnp.maximum((i - (NEAR_TILES - 1)) // U, 0)
    carry = lax.fori_loop(0, n_far, far_group, init)
    _, l_s, acc_s = lax.fori_loop(n_far, i // U + 1, near_group, carry)
    o_s = acc_s / l_s

    kts, biases = [], []
    for u in range(WIN_TILES):
        k = i - (WIN_TILES - 1) + u
        kts.append(jnp.maximum(k, 0))
        biases.append(bTw_ref[0, jnp.where(k >= 0, WIN_TILES - 1 - u, WIN_TILES)])
    _, l_w, acc_w = attend(kts, init, q_win, vwT_ref, biases)
    o_w = acc_w / l_w

    gate = jax.nn.sigmoid(gT_ref[0, 0, 0])
    o_ref[0, 0, 0] = gate[0:1] * o_c + gate[1:2] * o_s + gate[2:3] * o_w


def _bias_tables(rel_bias):
    G, R = NSA_KV_HEADS, NSA_GROUP
    bd = rel_bias[rel_bucket(jnp.arange(REL_TABLE))]
    bd = (bd - bd[REL_TABLE - 1]).T

    def tiles(dist, ok):
        vals = jnp.take(bd, jnp.asarray(np.clip(dist, 0, REL_TABLE - 1), jnp.int32), axis=1)
        return jnp.where(jnp.asarray(ok), vals, NEG_INF)

    kj = np.arange(Q_BLOCK)[None, :, None]
    qi = np.arange(Q_BLOCK)[None, None, :]
    d_near = Q_BLOCK * np.arange(NEAR_TILES)[:, None, None] + qi - kj
    b_near = tiles(d_near, d_near >= 0)
    d_win = d_near[:WIN_TILES]
    b_win = tiles(d_win, (d_win >= 0) & (d_win < WINDOW))
    one = jnp.zeros((NSA_HEADS, 1, Q_BLOCK, Q_BLOCK), jnp.float32)
    b_near = jnp.concatenate([b_near, one, one + NEG_INF], axis=1)
    b_win = jnp.concatenate([b_win, one + NEG_INF], axis=1)
    row = np.arange(CMP_WIN)[:, None]
    d_cmp = np.arange(Q_BLOCK)[None, :] - CMP_STRIDE * (row - (CMP_WIN - 8)) - (CMP_BLOCK - 1)
    b_cmp = tiles(d_cmp, d_cmp >= 0)

    def lanes(x):
        x = x.reshape((G, R) + x.shape[1:])
        x = jnp.moveaxis(x, 1, -2)
        return x.reshape(x.shape[:-2] + (QL,))

    return lanes(b_cmp), lanes(b_near), lanes(b_win)


def _pool_matrix():
    per = SLC_BLOCK // CMP_STRIDE
    n_span = CMP_BLOCK // CMP_STRIDE
    k = np.arange(CMP_ROWS)[None, :] - CMP_FRONT
    j = np.arange(N_SEL)[:, None]
    return jnp.asarray((k >= per * j - (n_span - 1)) & (k <= per * j + per - 1), jnp.bfloat16)


def nsa_shared_kv(h, kv_norm_g, w_kv, cmp_pos_k, cmp_pos_v, cmp_w1_k, cmp_w2_k, cmp_w1_v, cmp_w2_v):
    b, s, _ = h.shape
    G, DH = NSA_KV_HEADS, NSA_HEAD_DIM
    bf16 = jnp.bfloat16
    kv = _mm(h, kv_norm_g, w_kv, True).reshape(b, s, 6, G, DH)
    k_cmp = compress_blocks(kv[:, :, 0], cmp_pos_k, cmp_w1_k, cmp_w2_k)
    v_cmp = compress_blocks(kv[:, :, 1], cmp_pos_v, cmp_w1_v, cmp_w2_v)
    front = jnp.zeros((b, G, CMP_FRONT, DH), bf16)
    kc = jnp.concatenate([front, k_cmp.astype(bf16)], axis=2)
    vcT = jnp.concatenate([front, v_cmp.astype(bf16)], axis=2).transpose(0, 1, 3, 2)
    kk = jnp.concatenate([kv[:, :, 2], kv[:, :, 4]], axis=-1).astype(bf16)
    kk = kk.transpose(0, 2, 1, 3).reshape(b, G, N_KT, Q_BLOCK, 2 * DH)

    def vt(v):
        return v.astype(bf16).transpose(0, 2, 1, 3).reshape(b, G, N_KT, Q_BLOCK, DH).transpose(0, 1, 2, 4, 3)

    return kc, vcT, kk, vt(kv[:, :, 3]), vt(kv[:, :, 5])


def nsa_mixer(h, norm_g, w_in, w_out, rel_bias, kc, vcT, kk, vsT, vwT):
    b, s, _ = h.shape
    G, R, DH = NSA_KV_HEADS, NSA_GROUP, NSA_HEAD_DIM
    nb = s // Q_BLOCK
    proj = _mm(h, norm_g, w_in, True)
    qT = proj[..., :NSA_HEADS * DH].reshape(b, nb, Q_BLOCK, G, R, DH).transpose(0, 3, 1, 5, 4, 2)
    qT = qT.reshape(b, G, nb, DH, QL)
    gT = proj[..., NSA_HEADS * DH:].reshape(b, nb, Q_BLOCK, G, R, 3).transpose(0, 3, 1, 5, 4, 2)
    gT = gT.reshape(b, G, nb, 3, QL)
    b_cmp, b_near, b_win = _bias_tables(rel_bias)
    at = _pool_matrix()

    per_bg = lambda *blk: pl.BlockSpec((1, 1) + blk, lambda bi, gi, i: (bi, gi) + (0,) * len(blk))
    per_g = lambda *blk: pl.BlockSpec((1,) + blk, lambda bi, gi, i: (gi,) + (0,) * len(blk))
    oT = pl.pallas_call(
        _nsa_kernel,
        grid=(b, G, nb),
        in_specs=[
            pl.BlockSpec((1, 1, 1, DH, QL), lambda bi, gi, i: (bi, gi, i, 0, 0)),
            pl.BlockSpec((1, 1, 1, 3, QL), lambda bi, gi, i: (bi, gi, i, 0, 0)),
            per_bg(CMP_ROWS, DH),
            per_bg(DH, CMP_ROWS),
            per_bg(N_KT, Q_BLOCK, 2 * DH),
            per_bg(N_KT, DH, Q_BLOCK),
            per_bg(N_KT, DH, Q_BLOCK),
            pl.BlockSpec((N_SEL, CMP_ROWS), lambda bi, gi, i: (0, 0)),
            per_g(CMP_WIN, QL),
            per_g(NEAR_TILES + 2, Q_BLOCK, QL),
            per_g(WIN_TILES + 1, Q_BLOCK, QL),
        ],
        out_specs=pl.BlockSpec((1, 1, 1, DH, QL), lambda bi, gi, i: (bi, gi, i, 0, 0)),
        out_shape=jax.ShapeDtypeStruct((b, G, nb, DH, QL), jnp.float32),
        scratch_shapes=[pltpu.VMEM((CMP_ROWS, QL), jnp.float32), pltpu.VMEM((N_SEL, Q_BLOCK), jnp.float32)],
        compiler_params=pltpu.CompilerParams(
            dimension_semantics=("arbitrary", "arbitrary", "arbitrary"),
            vmem_limit_bytes=V7X_VMEM_LIMIT_BYTES),
        name="nsa_attention",
    )(qT, gT, kc, vcT, kk, vsT, vwT, at, b_cmp, b_near, b_win)
    o = oT.reshape(b, G, nb, DH, R, Q_BLOCK).transpose(0, 2, 5, 1, 4, 3).reshape(b, s, NSA_HEADS * DH)
    return _mm(o, None, w_out, False)


PEER_SLOTS = PEER_HEADS * PEER_TOPK
PEER_ROUTE_TM = 256
PEER_TT = 8
PEER_CHUNKS = D_MODEL // V7X_LANES
PEER_SLAB = 2 * PEER_CHUNKS
PEER_SLAB_STRIDE = 24


def _top_rows(s, rowid, n, payload=None):
    vals, picks = [], []
    for _ in range(n):
        mx = jnp.max(s, axis=0, keepdims=True)
        first = jnp.min(jnp.where(s == mx, rowid, float(s.shape[0])), axis=0, keepdims=True)
        hit = rowid == first
        vals.append(mx)
        picks.append(first if payload is None else jnp.max(jnp.where(hit, payload, -1.0), axis=0, keepdims=True))
        s = jnp.where(hit, -jnp.inf, s)
    return vals, picks


def _peer_route_kernel(h_ref, g_ref, wqT_ref, k1_ref, k2_ref, xn_ref, ids_ref, gate_ref):
    f32, bf16 = jnp.float32, jnp.bfloat16
    x = h_ref[...]
    xn = x * lax.rsqrt(jnp.mean(x * x, axis=-1, keepdims=True) + NORM_EPS) * g_ref[...]
    xn_ref[...] = xn
    qT = lax.dot_general(wqT_ref[...], xn.astype(bf16), (((1,), (1,)), ((), ())), preferred_element_type=f32)
    half = PEER_QDIM // 2
    tm = x.shape[0]
    key_id = lax.broadcasted_iota(jnp.int32, (PEER_N_KEYS, tm), 0).astype(f32)
    cand_pos = lax.broadcasted_iota(jnp.int32, (PEER_TOPK * PEER_TOPK, tm), 0).astype(f32)
    for hd in range(PEER_HEADS):
        q1 = qT[hd * PEER_QDIM: hd * PEER_QDIM + half].astype(bf16)
        q2 = qT[hd * PEER_QDIM + half: (hd + 1) * PEER_QDIM].astype(bf16)
        s1 = jnp.dot(k1_ref[hd], q1, preferred_element_type=f32)
        s2 = jnp.dot(k2_ref[hd], q2, preferred_element_type=f32)
        v1, i1 = _top_rows(s1, key_id, PEER_TOPK)
        v2, i2 = _top_rows(s2, key_id, PEER_TOPK)
        v2m = jnp.concatenate(v2, axis=0)
        i2m = jnp.concatenate(i2, axis=0)
        cand = jnp.concatenate([v1[a] + v2m for a in range(PEER_TOPK)], axis=0)
        cand_id = jnp.concatenate([i1[a] * float(PEER_N_KEYS) + i2m for a in range(PEER_TOPK)], axis=0)
        top_s, ids = _top_rows(cand, cand_pos, PEER_TOPK, payload=cand_id)
        top_s = jnp.concatenate(top_s, axis=0)
        e = jnp.exp(top_s - top_s[0:1])
        gate_ref[hd * PEER_TOPK:(hd + 1) * PEER_TOPK, :] = e / jnp.sum(e, axis=0, keepdims=True)
        ids_ref[hd * PEER_TOPK:(hd + 1) * PEER_TOPK, :] = jnp.concatenate(ids, axis=0).astype(jnp.int32)


def peer_route(h2, norm_g, w_q, keys1, keys2):
    T, d = h2.shape
    tm = PEER_ROUTE_TM
    bf16 = jnp.bfloat16
    nq = PEER_HEADS * PEER_QDIM
    half = PEER_QDIM // 2
    return pl.pallas_call(
        _peer_route_kernel,
        grid=(T // tm,),
        in_specs=[pl.BlockSpec((tm, d), lambda i: (i, 0)),
                  pl.BlockSpec((1, d), lambda i: (0, 0)),
                  pl.BlockSpec((nq, d), lambda i: (0, 0)),
                  pl.BlockSpec((PEER_HEADS, PEER_N_KEYS, half), lambda i: (0, 0, 0)),
                  pl.BlockSpec((PEER_HEADS, PEER_N_KEYS, half), lambda i: (0, 0, 0))],
        out_specs=[pl.BlockSpec((tm, d), lambda i: (i, 0)),
                   pl.BlockSpec((PEER_SLOTS, tm), lambda i: (0, i)),
                   pl.BlockSpec((PEER_SLOTS, tm), lambda i: (0, i))],
        out_shape=[jax.ShapeDtypeStruct((T, d), jnp.float32),
                   jax.ShapeDtypeStruct((PEER_SLOTS, T), jnp.int32),
                   jax.ShapeDtypeStruct((PEER_SLOTS, T), jnp.float32)],
        compiler_params=pltpu.CompilerParams(dimension_semantics=("arbitrary",),
                                             vmem_limit_bytes=V7X_VMEM_LIMIT_BYTES),
        name="peer_route",
    )(h2, norm_g.reshape(1, d), w_q.T.astype(bf16), keys1.astype(bf16), keys2.astype(bf16))


def _peer_slab(buf, slot, r):
    return buf.at[slot, pl.ds(pl.multiple_of(r * PEER_SLAB_STRIDE, 8), PEER_SLAB)]


def _peer_expert_kernel(ids0_ref, idsn_ref, xn_ref, gt_ref, uv_hbm, o_ref, buf, sem):
    i = pl.program_id(0)
    n = pl.num_programs(0)
    slot = i % 2
    groups = PEER_TT * PEER_SLOTS // 8

    def issue(ids_ref, dst_slot):
        def group(gidx, carry):
            for u in range(8):
                r = gidx * 8 + u
                pltpu.make_async_copy(uv_hbm.at[ids_ref[r]], _peer_slab(buf, dst_slot, r), sem.at[dst_slot]).start()
            return carry
        lax.fori_loop(0, groups, group, 0)

    @pl.when(i == 0)
    def _():
        issue(ids0_ref, 0)

    @pl.when(i + 1 < n)
    def _():
        issue(idsn_ref, 1 - slot)

    def wait_group(gidx, carry):
        for u in range(8):
            pltpu.make_async_copy(uv_hbm.at[0], _peer_slab(buf, slot, gidx * 8 + u), sem.at[slot]).wait()
        return carry
    lax.fori_loop(0, groups, wait_group, 0)

    for t in range(PEER_TT):
        base = t * PEER_SLOTS * PEER_SLAB_STRIDE
        xrow = xn_ref[t:t + 1, :]
        acc = jnp.zeros((PEER_SLOTS, V7X_LANES), jnp.float32)
        for j in range(PEER_CHUNKS):
            tile = buf[slot, pl.ds(base + j, PEER_SLOTS, stride=PEER_SLAB_STRIDE), :]
            acc = acc + tile * xrow[:, j * V7X_LANES:(j + 1) * V7X_LANES]
        act = jnp.sum(acc, axis=1, keepdims=True)
        w = gt_ref[0, :, t:t + 1] * jax.nn.gelu(act)
        for j in range(PEER_CHUNKS):
            tile = buf[slot, pl.ds(base + PEER_CHUNKS + j, PEER_SLOTS, stride=PEER_SLAB_STRIDE), :]
            o_ref[t:t + 1, j * V7X_LANES:(j + 1) * V7X_LANES] = jnp.sum(tile * w, axis=0, keepdims=True)


def peer_experts(xn, idsT, gateT, u_tab, v_tab):
    T, d = xn.shape
    tt = PEER_TT
    nt = T // tt
    uv = jnp.concatenate([u_tab.reshape(PEER_EXPERTS, PEER_CHUNKS, V7X_LANES),
                          v_tab.reshape(PEER_EXPERTS, PEER_CHUNKS, V7X_LANES)], axis=1)
    ids = idsT.T.reshape(T * PEER_SLOTS)
    gt = gateT.reshape(PEER_SLOTS, nt, tt).transpose(1, 0, 2)
    return pl.pallas_call(
        _peer_expert_kernel,
        grid=(nt,),
        in_specs=[pl.BlockSpec((tt * PEER_SLOTS,), lambda i: (0,), memory_space=pltpu.SMEM),
                  pl.BlockSpec((tt * PEER_SLOTS,), lambda i: (jnp.minimum(i + 1, nt - 1),), memory_space=pltpu.SMEM),
                  pl.BlockSpec((tt, d), lambda i: (i, 0)),
                  pl.BlockSpec((1, PEER_SLOTS, tt), lambda i: (i, 0, 0)),
                  pl.BlockSpec(memory_space=pl.ANY)],
        out_specs=pl.BlockSpec((tt, d), lambda i: (i, 0)),
        out_shape=jax.ShapeDtypeStruct((T, d), jnp.float32),
        scratch_shapes=[pltpu.VMEM((2, tt * PEER_SLOTS * PEER_SLAB_STRIDE, V7X_LANES), jnp.float32),
                        pltpu.SemaphoreType.DMA((2,))],
        compiler_params=pltpu.CompilerParams(dimension_semantics=("arbitrary",),
                                             vmem_limit_bytes=V7X_VMEM_LIMIT_BYTES),
        name="peer_experts",
    )(ids, ids, xn, gt, uv)


def peer_ffn(h, norm_g, w_q, keys1, keys2, u_tab, v_tab):
    b, s, d = h.shape
    xn, idsT, gateT = peer_route(h.reshape(b * s, d), norm_g, w_q, keys1, keys2)
    return peer_experts(xn, idsT, gateT, u_tab, v_tab).reshape(b, s, d)


def per_layer_embed(h, p_i, norm_g, w_up, w_gate):
    return _mm(p_i, None, w_up, False) * jax.nn.sigmoid(_mm(h, norm_g, w_gate, True))


def kernel(x, p, a_norm_g, a_w_in, a_conv_w, a_conv_b, a_dt_bias, a_log, a_d_skip, a_gnorm_g, a_w_out,
           kv_norm_g, w_kv, cmp_pos_k, cmp_pos_v, cmp_w1_k, cmp_w2_k, cmp_w1_v, cmp_w2_v, rel_bias,
           b_norm_g, b_w_in, b_w_out, c_norm_g, c_w_q, c_keys1, c_keys2, c_u, c_v,
           e_norm_g, e_w_up, e_w_gate, final_g):
    h = x
    shared = None
    for i in range(DEPTH):
        if i < N_A_LAYERS:
            h = h + ssd_mixer(h, a_norm_g[i], a_w_in[i], a_conv_w[i], a_conv_b[i], a_dt_bias[i],
                              a_log[i], a_d_skip[i], a_gnorm_g[i], a_w_out[i])
        else:
            if shared is None:
                shared = nsa_shared_kv(h, kv_norm_g, w_kv, cmp_pos_k, cmp_pos_v,
                                       cmp_w1_k, cmp_w2_k, cmp_w1_v, cmp_w2_v)
            j = i - N_A_LAYERS
            h = h + nsa_mixer(h, b_norm_g[j], b_w_in[j], b_w_out[j], rel_bias, *shared)
        h = h + peer_ffn(h, c_norm_g[i], c_w_q[i], c_keys1[i], c_keys2[i], c_u[i], c_v[i])
        h = h + per_layer_embed(h, p[i], e_norm_g[i], e_w_up[i], e_w_gate[i])
    return rms_norm(h, final_g)
```

```python
import functools
import math

import jax
import jax.numpy as jnp
import numpy as np
from jax import lax
from jax.experimental import pallas as pl
from jax.experimental.pallas import tpu as pltpu

D_MODEL = 1024
BATCH = 2
SEQ = 16384
DEPTH = 2
N_A_LAYERS = DEPTH // 2
N_B_LAYERS = DEPTH - N_A_LAYERS

SSD_D_INNER = 2 * D_MODEL
SSD_HEADDIM = 64
SSD_HEADS = SSD_D_INNER // SSD_HEADDIM
SSD_GROUPS = 8
SSD_HEADS_PER_GROUP = SSD_HEADS // SSD_GROUPS
SSD_STATE = 128
SSD_CONV = 4
SSD_CHUNK = 128
SSD_CONV_DIM = SSD_D_INNER + 2 * SSD_GROUPS * SSD_STATE
SSD_IN_DIM = SSD_D_INNER + SSD_CONV_DIM + SSD_HEADS

NSA_HEADS = 16
NSA_KV_HEADS = 4
NSA_GROUP = NSA_HEADS // NSA_KV_HEADS
NSA_HEAD_DIM = 64
CMP_BLOCK = 32
CMP_STRIDE = 16
CMP_HIDDEN = 256
SLC_BLOCK = 64
SLC_TOPN = 16
SLC_FORCE = 1e4
WINDOW = 512
Q_BLOCK = 128
NSA_IN_DIM = NSA_HEADS * NSA_HEAD_DIM + 3 * NSA_HEADS
KV_DIM = 6 * NSA_KV_HEADS * NSA_HEAD_DIM

REL_BUCKETS = 32
REL_MAX_DISTANCE = 4096

PEER_HEADS = 8
PEER_N_KEYS = 128
PEER_EXPERTS = PEER_N_KEYS * PEER_N_KEYS
PEER_QDIM = 256
PEER_TOPK = 16
PEER_TOKEN_CHUNK = 128
PEER_V_SCALE = PEER_HEADS ** -0.5

PLE_DIM = 256
NORM_EPS = 1e-6
NEG_INF = -1e30

V7X_LANES = 128
V7X_VMEM_LIMIT_BYTES = 56 * 1024 * 1024


def _norm_matmul_kernel(x_ref, g_ref, w_ref, o_ref, *, normalize):
    x = x_ref[...]
    if normalize:
        x = x * lax.rsqrt(jnp.mean(x * x, axis=-1, keepdims=True) + NORM_EPS) * g_ref[...]
    o_ref[...] = jnp.dot(x.astype(jnp.bfloat16), w_ref[...], preferred_element_type=jnp.float32)


def norm_matmul(x, g, w, *, normalize, tm=512, tn=None):
    m, k = x.shape
    n = w.shape[1]
    n_pad = -(-n // V7X_LANES) * V7X_LANES
    wb = w.astype(jnp.bfloat16)
    if n_pad != n:
        wb = jnp.pad(wb, ((0, 0), (0, n_pad - n)))
    if tn is None:
        tn = n_pad
    assert m % tm == 0 and n_pad % tn == 0
    out = pl.pallas_call(
        functools.partial(_norm_matmul_kernel, normalize=normalize),
        grid=(m // tm, n_pad // tn),
        in_specs=[
            pl.BlockSpec((tm, k), lambda i, j: (i, 0)),
            pl.BlockSpec((1, k), lambda i, j: (0, 0)),
            pl.BlockSpec((k, tn), lambda i, j: (0, j)),
        ],
        out_specs=pl.BlockSpec((tm, tn), lambda i, j: (i, j)),
        out_shape=jax.ShapeDtypeStruct((m, n_pad), jnp.float32),
        compiler_params=pltpu.CompilerParams(
            dimension_semantics=("arbitrary", "arbitrary"),
            vmem_limit_bytes=V7X_VMEM_LIMIT_BYTES),
        name="norm_matmul" if normalize else "matmul",
    )(x, g.reshape(1, k), wb)
    return out[:, :n] if n_pad != n else out


def _mm(x3, g, w, normalize, **kw):
    b, s, k = x3.shape
    if g is None:
        g = jnp.ones((k,), jnp.float32)
    return norm_matmul(x3.reshape(b * s, k), g, w, normalize=normalize, **kw).reshape(b, s, w.shape[1])


def rms_norm(x, g):
    xf = x.astype(jnp.float32)
    y = xf * lax.rsqrt(jnp.mean(xf * xf, axis=-1, keepdims=True) + NORM_EPS)
    return (y * g.astype(jnp.float32)).astype(x.dtype)


def rel_bucket(dist):
    dist = jnp.maximum(dist, 0)
    max_exact = REL_BUCKETS // 2
    d = jnp.maximum(dist, 1).astype(jnp.float32)
    large = max_exact + (jnp.log(d / max_exact) / math.log(REL_MAX_DISTANCE / max_exact)
                         * (REL_BUCKETS - max_exact)).astype(jnp.int32)
    large = jnp.minimum(large, REL_BUCKETS - 1)
    return jnp.where(dist < max_exact, dist, large)


def ssd_mixer(h, norm_g, w_in, conv_w, conv_b, dt_bias, a_log, d_skip, gnorm_g, w_out):
    b, s, _ = h.shape
    G, R, P, N, L = SSD_GROUPS, SSD_HEADS_PER_GROUP, SSD_HEADDIM, SSD_STATE, SSD_CHUNK
    nc = s // L
    zxbcdt = _mm(h, norm_g, w_in, True, tm=256)
    z = zxbcdt[..., :SSD_D_INNER]
    xbc = zxbcdt[..., SSD_D_INNER:SSD_D_INNER + SSD_CONV_DIM]
    dt_raw = zxbcdt[..., SSD_D_INNER + SSD_CONV_DIM:]
    xbc = lax.conv_general_dilated(xbc, conv_w[:, None, :], window_strides=(1,),
                                   padding=[(SSD_CONV - 1, 0)],
                                   dimension_numbers=('NWC', 'WIO', 'NWC'),
                                   feature_group_count=SSD_CONV_DIM) + conv_b
    xbc = jax.nn.silu(xbc)
    x = xbc[..., :SSD_D_INNER].reshape(b, nc, L, G, R, P)
    bm = xbc[..., SSD_D_INNER:SSD_D_INNER + G * N].reshape(b, nc, L, G, N)
    cm = xbc[..., SSD_D_INNER + G * N:].reshape(b, nc, L, G, N)
    dt = jax.nn.softplus((dt_raw + dt_bias).astype(jnp.float32)).reshape(b, nc, L, G, R)
    a = -jnp.exp(a_log.astype(jnp.float32)).reshape(G, R)
    xd = x * dt[..., None].astype(x.dtype)
    a_dt = jnp.transpose(dt * a, (0, 3, 4, 1, 2))
    a_cs = jnp.cumsum(a_dt, axis=-1)
    causal = jnp.tril(jnp.ones((L, L), dtype=bool))
    seg = a_cs[..., :, None] - a_cs[..., None, :]
    decay_in = jnp.exp(jnp.where(causal, seg, -jnp.inf)).astype(x.dtype)
    cb = jnp.einsum('bclgn,bcsgn->bgcls', cm, bm)
    y_diag = jnp.einsum('bgrcls,bcsgrp->bclgrp', cb[:, :, None] * decay_in, xd)
    decay_to_end = jnp.exp(a_cs[..., -1:] - a_cs).astype(x.dtype)
    states = jnp.einsum('bcsgn,bgrcs,bcsgrp->bcgrpn', bm, decay_to_end, xd)
    chunk_decay = jnp.exp(a_cs[..., -1]).astype(x.dtype)

    def step(hh, inp):
        st, dec = inp
        return hh * dec[..., None, None] + st, hh

    h0 = jnp.zeros((b, G, R, P, N), x.dtype)
    _, prev = lax.scan(step, h0, (jnp.moveaxis(states, 1, 0), jnp.moveaxis(chunk_decay, -1, 0)))
    prev = jnp.moveaxis(prev, 0, 1)
    y_off = jnp.einsum('bclgn,bcgrpn,bgrcl->bclgrp', cm, prev, jnp.exp(a_cs).astype(x.dtype))
    y = (y_diag + y_off + x * d_skip.reshape(G, R, 1)).reshape(b, s, SSD_D_INNER)
    yg = (y * jax.nn.silu(z)).reshape(b, s, G, SSD_D_INNER // G)
    yg = rms_norm(yg, gnorm_g.reshape(G, -1)).reshape(b, s, SSD_D_INNER)
    return _mm(yg, None, w_out, False)


def _mlp_kernel(x_ref, w1_ref, w2_ref, o_ref):
    hid = jnp.dot(x_ref[...].astype(jnp.bfloat16), w1_ref[...], preferred_element_type=jnp.float32)
    hid = jax.nn.gelu(hid)
    o_ref[...] = jnp.dot(hid.astype(jnp.bfloat16), w2_ref[...], preferred_element_type=jnp.float32)


def mlp2(x, w1, w2, *, tm=512):
    m, k = x.shape
    hdim, n = w2.shape
    assert m % tm == 0
    return pl.pallas_call(
        _mlp_kernel,
        grid=(m // tm,),
        in_specs=[pl.BlockSpec((tm, k), lambda i: (i, 0)),
                  pl.BlockSpec((k, hdim), lambda i: (0, 0)),
                  pl.BlockSpec((hdim, n), lambda i: (0, 0))],
        out_specs=pl.BlockSpec((tm, n), lambda i: (i, 0)),
        out_shape=jax.ShapeDtypeStruct((m, n), jnp.float32),
        compiler_params=pltpu.CompilerParams(dimension_semantics=("arbitrary",),
                                             vmem_limit_bytes=V7X_VMEM_LIMIT_BYTES),
        name="cmp_mlp",
    )(x, w1.astype(jnp.bfloat16), w2.astype(jnp.bfloat16))


def compress_blocks(t, pos, w1, w2):
    b, s, g, d = t.shape
    halves = t.transpose(0, 2, 1, 3).reshape(b, g, s // CMP_STRIDE, CMP_STRIDE, d)
    nxt = jnp.concatenate([halves[:, :, 1:], jnp.zeros_like(halves[:, :, :1])], axis=2)
    blocks = jnp.concatenate([halves, nxt], axis=3) + pos
    flat = blocks.reshape(b * g * (s // CMP_STRIDE), CMP_BLOCK * d)
    return mlp2(flat, w1, w2).reshape(b, g, s // CMP_STRIDE, d)


N_SEL = SEQ // SLC_BLOCK
N_KT = SEQ // Q_BLOCK
CMP_FRONT = 256
CMP_ROWS = CMP_FRONT + SEQ // CMP_STRIDE
CMP_WIN = 256
REL_TABLE = 4096
NEAR_TILES = 24
WIN_TILES = WINDOW // Q_BLOCK + 1
SEL_GROUP = 4
QL = NSA_GROUP * Q_BLOCK


def _nsa_kernel(qT_ref, gT_ref, kc_ref, vcT_ref, kk_ref, vsT_ref, vwT_ref, at_ref, bcT_ref, bT_ref, bTw_ref,
                o_ref, s_scr, sel_scr):
    f32, bf16 = jnp.float32, jnp.bfloat16
    i = pl.program_id(2)
    qT = (qT_ref[0, 0, 0] * (NSA_HEAD_DIM ** -0.5)).astype(bf16)
    zq = jnp.zeros_like(qT)
    q_sel = jnp.concatenate([qT, zq], axis=0)
    q_win = jnp.concatenate([zq, qT], axis=0)

    r0 = pl.multiple_of(8 * i + 8, 8)
    s = jnp.dot(kc_ref[0, 0], qT, preferred_element_type=f32)
    rio = lax.broadcasted_iota(jnp.int32, (CMP_ROWS, Q_BLOCK), 0)
    rowmask = jnp.where((rio >= CMP_FRONT) & (rio < r0 + CMP_WIN), 0.0, NEG_INF)
    s_scr[...] = s + jnp.concatenate([rowmask] * NSA_GROUP, axis=1)
    s_scr[pl.ds(r0, CMP_WIN), :] = s_scr[pl.ds(r0, CMP_WIN), :] + bcT_ref[0]
    s = s_scr[...]
    m = jnp.max(s, axis=0, keepdims=True)
    p = jnp.exp(s - m)
    l = jnp.sum(p, axis=0, keepdims=True)
    inv = jnp.where(m > 0.1 * NEG_INF, 1.0 / l, 0.0)
    pn = p * inv
    o_c = jnp.dot(vcT_ref[0, 0], pn.astype(bf16), preferred_element_type=f32)

    psum = pn[:, 0:Q_BLOCK]
    for r in range(1, NSA_GROUP):
        psum = psum + pn[:, r * Q_BLOCK:(r + 1) * Q_BLOCK]
    hi = psum.astype(bf16)
    lo = (psum - hi.astype(f32)).astype(bf16)
    imp = (jnp.dot(at_ref[...], hi, preferred_element_type=f32)
           + jnp.dot(at_ref[...], lo, preferred_element_type=f32))
    blk = lax.broadcasted_iota(jnp.int32, (N_SEL, Q_BLOCK), 0)
    qi = lax.broadcasted_iota(jnp.int32, (N_SEL, Q_BLOCK), 1)
    cur = 2 * i + (qi >= SLC_BLOCK).astype(jnp.int32)
    forced = (blk == 0) | (blk == cur) | (blk == cur - 1)
    valid = blk <= cur
    score = jnp.where(forced, SLC_FORCE, jnp.where(valid, imp, -SLC_FORCE))
    blkf = blk.astype(f32)
    chosen = jnp.zeros((N_SEL, Q_BLOCK), f32)
    for _ in range(SLC_TOPN):
        mx = jnp.max(score, axis=0, keepdims=True)
        first = jnp.min(jnp.where(score == mx, blkf, float(N_SEL)), axis=0, keepdims=True)
        hit = blkf == first
        chosen = jnp.where(hit, 1.0, chosen)
        score = jnp.where(hit, -jnp.inf, score)
    sel_scr[...] = jnp.where((chosen > 0.5) & valid, 0.0, NEG_INF)

    def attend(kt, carry, q_aug, vT_ref, bias):
        m_run, l_run, acc = carry
        scs = [jnp.dot(kk_ref[0, 0, k], q_aug, preferred_element_type=f32) + b for k, b in zip(kt, bias)]
        m_new = m_run
        for sc in scs:
            m_new = jnp.maximum(m_new, jnp.max(sc, axis=0, keepdims=True))
        alpha = jnp.exp(m_run - m_new)
        l_new = alpha * l_run
        acc_new = alpha * acc
        for k, sc in zip(kt, scs):
            pt = jnp.exp(sc - m_new)
            l_new = l_new + jnp.sum(pt, axis=0, keepdims=True)
            acc_new = acc_new + jnp.dot(vT_ref[0, 0, k], pt.astype(bf16), preferred_element_type=f32)
        return m_new, l_new, acc_new

    def sel_mask(kt):
        m0 = jnp.broadcast_to(sel_scr[pl.ds(2 * kt, 1), :], (SLC_BLOCK, Q_BLOCK))
        m1 = jnp.broadcast_to(sel_scr[pl.ds(2 * kt + 1, 1), :], (SLC_BLOCK, Q_BLOCK))
        mk = jnp.concatenate([m0, m1], axis=0)
        return jnp.concatenate([mk] * NSA_GROUP, axis=1)

    init = (jnp.full((1, QL), NEG_INF, f32), jnp.zeros((1, QL), f32), jnp.zeros((NSA_HEAD_DIM, QL), f32))
    U = SEL_GROUP

    def far_group(g, c):
        kts = [g * U + u for u in range(U)]
        return attend(kts, c, q_sel, vsT_ref, [sel_mask(k) for k in kts])

    def near_group(g, c):
        kts, biases = [], []
        for u in range(U):
            delta = i - (g * U + u)
            k = jnp.minimum(g * U + u, i)
            tile = jnp.where(delta < 0, NEAR_TILES + 1, jnp.minimum(delta, NEAR_TILES))
            kts.append(k)
            biases.append(sel_mask(k) + bT_ref[0, tile])
        return attend(kts, c, q_sel, vsT_ref, biases)

    n_far = jnp.maximum((i - (NEAR_TILES - 1)) // U, 0)
    carry = lax.fori_loop(0, n_far, far_group, init)
    _, l_s, acc_s = lax.fori_loop(n_far, i // U + 1, near_group, carry)
    o_s = acc_s / l_s

    kts, biases = [], []
    for u in range(WIN_TILES):
        k = i - (WIN_TILES - 1) + u
        kts.append(jnp.maximum(k, 0))
        biases.append(bTw_ref[0, jnp.where(k >= 0, WIN_TILES - 1 - u, WIN_TILES)])
    _, l_w, acc_w = attend(kts, init, q_win, vwT_ref, biases)
    o_w = acc_w / l_w

    gate = jax.nn.sigmoid(gT_ref[0, 0, 0])
    o_ref[0, 0, 0] = gate[0:1] * o_c + gate[1:2] * o_s + gate[2:3] * o_w


def _bias_tables(rel_bias):
    G, R = NSA_KV_HEADS, NSA_GROUP
    bd = rel_bias[rel_bucket(jnp.arange(REL_TABLE))]
    bd = (bd - bd[REL_TABLE - 1]).T

    def tiles(dist, ok):
        vals = jnp.take(bd, jnp.asarray(np.clip(dist, 0, REL_TABLE - 1), jnp.int32), axis=1)
        return jnp.where(jnp.asarray(ok), vals, NEG_INF)

    kj = np.arange(Q_BLOCK)[None, :, None]
    qi = np.arange(Q_BLOCK)[None, None, :]
    d_near = Q_BLOCK * np.arange(NEAR_TILES)[:, None, None] + qi - kj
    b_near = tiles(d_near, d_near >= 0)
    d_win = d_near[:WIN_TILES]
    b_win = tiles(d_win, (d_win >= 0) & (d_win < WINDOW))
    one = jnp.zeros((NSA_HEADS, 1, Q_BLOCK, Q_BLOCK), jnp.float32)
    b_near = jnp.concatenate([b_near, one, one + NEG_INF], axis=1)
    b_win = jnp.concatenate([b_win, one + NEG_INF], axis=1)
    row = np.arange(CMP_WIN)[:, None]
    d_cmp = np.arange(Q_BLOCK)[None, :] - CMP_STRIDE * (row - (CMP_WIN - 8)) - (CMP_BLOCK - 1)
    b_cmp = tiles(d_cmp, d_cmp >= 0)

    def lanes(x):
        x = x.reshape((G, R) + x.shape[1:])
        x = jnp.moveaxis(x, 1, -2)
        return x.reshape(x.shape[:-2] + (QL,))

    return lanes(b_cmp), lanes(b_near), lanes(b_win)


def _pool_matrix():
    per = SLC_BLOCK // CMP_STRIDE
    n_span = CMP_BLOCK // CMP_STRIDE
    k = np.arange(CMP_ROWS)[None, :] - CMP_FRONT
    j = np.arange(N_SEL)[:, None]
    return jnp.asarray((k >= per * j - (n_span - 1)) & (k <= per * j + per - 1), jnp.bfloat16)


def nsa_shared_kv(h, kv_norm_g, w_kv, cmp_pos_k, cmp_pos_v, cmp_w1_k, cmp_w2_k, cmp_w1_v, cmp_w2_v):
    b, s, _ = h.shape
    G, DH = NSA_KV_HEADS, NSA_HEAD_DIM
    bf16 = jnp.bfloat16
    kv = _mm(h, kv_norm_g, w_kv, True).reshape(b, s, 6, G, DH)
    k_cmp = compress_blocks(kv[:, :, 0], cmp_pos_k, cmp_w1_k, cmp_w2_k)
    v_cmp = compress_blocks(kv[:, :, 1], cmp_pos_v, cmp_w1_v, cmp_w2_v)
    front = jnp.zeros((b, G, CMP_FRONT, DH), bf16)
    kc = jnp.concatenate([front, k_cmp.astype(bf16)], axis=2)
    vcT = jnp.concatenate([front, v_cmp.astype(bf16)], axis=2).transpose(0, 1, 3, 2)
    kk = jnp.concatenate([kv[:, :, 2], kv[:, :, 4]], axis=-1).astype(bf16)
    kk = kk.transpose(0, 2, 1, 3).reshape(b, G, N_KT, Q_BLOCK, 2 * DH)

    def vt(v):
        return v.astype(bf16).transpose(0, 2, 1, 3).reshape(b, G, N_KT, Q_BLOCK, DH).transpose(0, 1, 2, 4, 3)

    return kc, vcT, kk, vt(kv[:, :, 3]), vt(kv[:, :, 5])


def nsa_mixer(h, norm_g, w_in, w_out, rel_bias, kc, vcT, kk, vsT, vwT):
    b, s, _ = h.shape
    G, R, DH = NSA_KV_HEADS, NSA_GROUP, NSA_HEAD_DIM
    nb = s // Q_BLOCK
    proj = _mm(h, norm_g, w_in, True)
    qT = proj[..., :NSA_HEADS * DH].reshape(b, nb, Q_BLOCK, G, R, DH).transpose(0, 3, 1, 5, 4, 2)
    qT = qT.reshape(b, G, nb, DH, QL)
    gT = proj[..., NSA_HEADS * DH:].reshape(b, nb, Q_BLOCK, G, R, 3).transpose(0, 3, 1, 5, 4, 2)
    gT = gT.reshape(b, G, nb, 3, QL)
    b_cmp, b_near, b_win = _bias_tables(rel_bias)
    at = _pool_matrix()

    per_bg = lambda *blk: pl.BlockSpec((1, 1) + blk, lambda bi, gi, i: (bi, gi) + (0,) * len(blk))
    per_g = lambda *blk: pl.BlockSpec((1,) + blk, lambda bi, gi, i: (gi,) + (0,) * len(blk))
    oT = pl.pallas_call(
        _nsa_kernel,
        grid=(b, G, nb),
        in_specs=[
            pl.BlockSpec((1, 1, 1, DH, QL), lambda bi, gi, i: (bi, gi, i, 0, 0)),
            pl.BlockSpec((1, 1, 1, 3, QL), lambda bi, gi, i: (bi, gi, i, 0, 0)),
            per_bg(CMP_ROWS, DH),
            per_bg(DH, CMP_ROWS),
            per_bg(N_KT, Q_BLOCK, 2 * DH),
            per_bg(N_KT, DH, Q_BLOCK),
            per_bg(N_KT, DH, Q_BLOCK),
            pl.BlockSpec((N_SEL, CMP_ROWS), lambda bi, gi, i: (0, 0)),
            per_g(CMP_WIN, QL),
            per_g(NEAR_TILES + 2, Q_BLOCK, QL),
            per_g(WIN_TILES + 1, Q_BLOCK, QL),
        ],
        out_specs=pl.BlockSpec((1, 1, 1, DH, QL), lambda bi, gi, i: (bi, gi, i, 0, 0)),
        out_shape=jax.ShapeDtypeStruct((b, G, nb, DH, QL), jnp.float32),
        scratch_shapes=[pltpu.VMEM((CMP_ROWS, QL), jnp.float32), pltpu.VMEM((N_SEL, Q_BLOCK), jnp.float32)],
        compiler_params=pltpu.CompilerParams(
            dimension_semantics=("arbitrary", "arbitrary", "arbitrary"),
            vmem_limit_bytes=V7X_VMEM_LIMIT_BYTES),
        name="nsa_attention",
    )(qT, gT, kc, vcT, kk, vsT, vwT, at, b_cmp, b_near, b_win)
    o = oT.reshape(b, G, nb, DH, R, Q_BLOCK).transpose(0, 2, 5, 1, 4, 3).reshape(b, s, NSA_HEADS * DH)
    return _mm(o, None, w_out, False)


PEER_SLOTS = PEER_HEADS * PEER_TOPK
PEER_ROUTE_TM = 256
PEER_TT = 8
PEER_CHUNKS = D_MODEL // V7X_LANES
PEER_SLAB = 2 * PEER_CHUNKS
PEER_SLAB_STRIDE = 24


def _top_rows(s, rowid, n, payload=None):
    vals, picks = [], []
    for _ in range(n):
        mx = jnp.max(s, axis=0, keepdims=True)
        first = jnp.min(jnp.where(s == mx, rowid, float(s.shape[0])), axis=0, keepdims=True)
        hit = rowid == first
        vals.append(mx)
        picks.append(first if payload is None else jnp.max(jnp.where(hit, payload, -1.0), axis=0, keepdims=True))
        s = jnp.where(hit, -jnp.inf, s)
    return vals, picks


def _peer_route_kernel(h_ref, g_ref, wqT_ref, k1_ref, k2_ref, xn_ref, ids_ref, gate_ref):
    f32, bf16 = jnp.float32, jnp.bfloat16
    x = h_ref[...]
    xn = x * lax.rsqrt(jnp.mean(x * x, axis=-1, keepdims=True) + NORM_EPS) * g_ref[...]
    xn_ref[...] = xn
    qT = lax.dot_general(wqT_ref[...], xn.astype(bf16), (((1,), (1,)), ((), ())), preferred_element_type=f32)
    half = PEER_QDIM // 2
    tm = x.shape[0]
    key_id = lax.broadcasted_iota(jnp.int32, (PEER_N_KEYS, tm), 0).astype(f32)
    cand_pos = lax.broadcasted_iota(jnp.int32, (PEER_TOPK * PEER_TOPK, tm), 0).astype(f32)
    for hd in range(PEER_HEADS):
        q1 = qT[hd * PEER_QDIM: hd * PEER_QDIM + half].astype(bf16)
        q2 = qT[hd * PEER_QDIM + half: (hd + 1) * PEER_QDIM].astype(bf16)
        s1 = jnp.dot(k1_ref[hd], q1, preferred_element_type=f32)
        s2 = jnp.dot(k2_ref[hd], q2, preferred_element_type=f32)
        v1, i1 = _top_rows(s1, key_id, PEER_TOPK)
        v2, i2 = _top_rows(s2, key_id, PEER_TOPK)
        v2m = jnp.concatenate(v2, axis=0)
        i2m = jnp.concatenate(i2, axis=0)
        cand = jnp.concatenate([v1[a] + v2m for a in range(PEER_TOPK)], axis=0)
        cand_id = jnp.concatenate([i1[a] * float(PEER_N_KEYS) + i2m for a in range(PEER_TOPK)], axis=0)
        top_s, ids = _top_rows(cand, cand_pos, PEER_TOPK, payload=cand_id)
        top_s = jnp.concatenate(top_s, axis=0)
        e = jnp.exp(top_s - top_s[0:1])
        gate_ref[hd * PEER_TOPK:(hd + 1) * PEER_TOPK, :] = e / jnp.sum(e, axis=0, keepdims=True)
        ids_ref[hd * PEER_TOPK:(hd + 1) * PEER_TOPK, :] = jnp.concatenate(ids, axis=0).astype(jnp.int32)


def peer_route(h2, norm_g, w_q, keys1, keys2):
    T, d = h2.shape
    tm = PEER_ROUTE_TM
    bf16 = jnp.bfloat16
    nq = PEER_HEADS * PEER_QDIM
    half = PEER_QDIM // 2
    return pl.pallas_call(
        _peer_route_kernel,
        grid=(T // tm,),
        in_specs=[pl.BlockSpec((tm, d), lambda i: (i, 0)),
                  pl.BlockSpec((1, d), lambda i: (0, 0)),
                  pl.BlockSpec((nq, d), lambda i: (0, 0)),
                  pl.BlockSpec((PEER_HEADS, PEER_N_KEYS, half), lambda i: (0, 0, 0)),
                  pl.BlockSpec((PEER_HEADS, PEER_N_KEYS, half), lambda i: (0, 0, 0))],
        out_specs=[pl.BlockSpec((tm, d), lambda i: (i, 0)),
                   pl.BlockSpec((PEER_SLOTS, tm), lambda i: (0, i)),
                   pl.BlockSpec((PEER_SLOTS, tm), lambda i: (0, i))],
        out_shape=[jax.ShapeDtypeStruct((T, d), jnp.float32),
                   jax.ShapeDtypeStruct((PEER_SLOTS, T), jnp.int32),
                   jax.ShapeDtypeStruct((PEER_SLOTS, T), jnp.float32)],
        compiler_params=pltpu.CompilerParams(dimension_semantics=("arbitrary",),
                                             vmem_limit_bytes=V7X_VMEM_LIMIT_BYTES),
        name="peer_route",
    )(h2, norm_g.reshape(1, d), w_q.T.astype(bf16), keys1.astype(bf16), keys2.astype(bf16))


def _peer_slab(buf, r):
    return buf.at[pl.ds(pl.multiple_of(r * PEER_SLAB_STRIDE, 8), PEER_SLAB)]


def _peer_slab_copy(uv_hbm, expert, buf, r, sem):
    return pltpu.make_async_copy(uv_hbm.at[expert], _peer_slab(buf, r), sem)


def _peer_expert_kernel(ids0_ref, idsb_ref, idsn_ref, xn_ref, gt_ref, uv_hbm, o_ref, buf_a, buf_b, sem):
    i = pl.program_id(0)
    n = pl.num_programs(0)
    rows = PEER_TT * PEER_SLOTS

    def loop_rows(fn):
        def group(gidx, carry):
            for u in range(8):
                fn(gidx * 8 + u)
            return carry
        lax.fori_loop(0, rows // 8, group, 0)

    def wait_tile(buf, sem_k):
        loop_rows(lambda r: _peer_slab_copy(uv_hbm, 0, buf, r, sem_k).wait())

    def token(t_out, src, t, ids_ref, dst, sem_k):
        for k in range(PEER_SLOTS):
            r = t * PEER_SLOTS + k
            _peer_slab_copy(uv_hbm, ids_ref[r], dst, r, sem_k).start()
        base = t * PEER_SLOTS * PEER_SLAB_STRIDE
        xrow = xn_ref[t_out:t_out + 1, :]
        acc = jnp.zeros((PEER_SLOTS, V7X_LANES), jnp.float32)
        for j in range(PEER_CHUNKS):
            tile = src[pl.ds(base + j, PEER_SLOTS, stride=PEER_SLAB_STRIDE), :]
            acc = acc + tile * xrow[:, j * V7X_LANES:(j + 1) * V7X_LANES]
        act = jnp.sum(acc, axis=1, keepdims=True)
        w = gt_ref[0, :, t_out:t_out + 1] * jax.nn.gelu(act)
        for j in range(PEER_CHUNKS):
            tile = src[pl.ds(base + PEER_CHUNKS + j, PEER_SLOTS, stride=PEER_SLAB_STRIDE), :]
            o_ref[t_out:t_out + 1, j * V7X_LANES:(j + 1) * V7X_LANES] = jnp.sum(tile * w, axis=0, keepdims=True)

    @pl.when(i == 0)
    def _():
        loop_rows(lambda r: _peer_slab_copy(uv_hbm, ids0_ref[r], buf_a, r, sem.at[0]).start())

    wait_tile(buf_a, sem.at[0])
    for t in range(PEER_TT):
        token(t, buf_a, t, idsb_ref, buf_b, sem.at[1])
    wait_tile(buf_b, sem.at[1])
    for t in range(PEER_TT):
        token(PEER_TT + t, buf_b, t, idsn_ref, buf_a, sem.at[0])

    @pl.when(i == n - 1)
    def _():
        wait_tile(buf_a, sem.at[0])


def peer_experts(xn, idsT, gateT, u_tab, v_tab):
    T, d = xn.shape
    tt2 = 2 * PEER_TT
    rows = PEER_TT * PEER_SLOTS
    n = T // tt2
    uv = jnp.concatenate([u_tab.reshape(PEER_EXPERTS, PEER_CHUNKS, V7X_LANES),
                          v_tab.reshape(PEER_EXPERTS, PEER_CHUNKS, V7X_LANES)], axis=1)
    ids = jnp.pad(idsT.T.reshape(T * PEER_SLOTS), (0, rows))
    gt = gateT.reshape(PEER_SLOTS, n, tt2).transpose(1, 0, 2)
    smem_tile = lambda index_map: pl.BlockSpec((rows,), index_map, memory_space=pltpu.SMEM)
    buf = pltpu.VMEM((rows * PEER_SLAB_STRIDE, V7X_LANES), jnp.float32)
    return pl.pallas_call(
        _peer_expert_kernel,
        grid=(n,),
        in_specs=[smem_tile(lambda i: (0,)),
                  smem_tile(lambda i: (2 * i + 1,)),
                  smem_tile(lambda i: (2 * i + 2,)),
                  pl.BlockSpec((tt2, d), lambda i: (i, 0)),
                  pl.BlockSpec((1, PEER_SLOTS, tt2), lambda i: (i, 0, 0)),
                  pl.BlockSpec(memory_space=pl.ANY)],
        out_specs=pl.BlockSpec((tt2, d), lambda i: (i, 0)),
        out_shape=jax.ShapeDtypeStruct((T, d), jnp.float32),
        scratch_shapes=[buf, buf, pltpu.SemaphoreType.DMA((2,))],
        compiler_params=pltpu.CompilerParams(dimension_semantics=("arbitrary",),
                                             vmem_limit_bytes=V7X_VMEM_LIMIT_BYTES),
        name="peer_experts",
    )(ids, ids, ids, xn, gt, uv)


def peer_ffn(h, norm_g, w_q, keys1, keys2, u_tab, v_tab):
    b, s, d = h.shape
    xn, idsT, gateT = peer_route(h.reshape(b * s, d), norm_g, w_q, keys1, keys2)
    return peer_experts(xn, idsT, gateT, u_tab, v_tab).reshape(b, s, d)


def per_layer_embed(h, p_i, norm_g, w_up, w_gate):
    return _mm(p_i, None, w_up, False) * jax.nn.sigmoid(_mm(h, norm_g, w_gate, True))


def kernel(x, p, a_norm_g, a_w_in, a_conv_w, a_conv_b, a_dt_bias, a_log, a_d_skip, a_gnorm_g, a_w_out,
           kv_norm_g, w_kv, cmp_pos_k, cmp_pos_v, cmp_w1_k, cmp_w2_k, cmp_w1_v, cmp_w2_v, rel_bias,
           b_norm_g, b_w_in, b_w_out, c_norm_g, c_w_q, c_keys1, c_keys2, c_u, c_v,
           e_norm_g, e_w_up, e_w_gate, final_g):
    h = x
    shared = None
    for i in range(DEPTH):
        if i < N_A_LAYERS:
            h = h + ssd_mixer(h, a_norm_g[i], a_w_in[i], a_conv_w[i], a_conv_b[i], a_dt_bias[i],
                              a_log[i], a_d_skip[i], a_gnorm_g[i], a_w_out[i])
        else:
            if shared is None:
                shared = nsa_shared_kv(h, kv_norm_g, w_kv, cmp_pos_k, cmp_pos_v,
                                       cmp_w1_k, cmp_w2_k, cmp_w1_v, cmp_w2_v)
            j = i - N_A_LAYERS
            h = h + nsa_mixer(h, b_norm_g[j], b_w_in[j], b_w_out[j], rel_bias, *shared)
        h = h + peer_ffn(h, c_norm_g[i], c_w_q[i], c_keys1[i], c_keys2[i], c_u[i], c_v[i])
        h = h + per_layer_embed(h, p[i], e_norm_g[i], e_w_up[i], e_w_gate[i])
    return rms_norm(h, final_g)
```

```python
import functools
import math

import jax
import jax.numpy as jnp
import numpy as np
from jax import lax
from jax.experimental import pallas as pl
from jax.experimental.pallas import tpu as pltpu

D_MODEL = 1024
BATCH = 2
SEQ = 16384
DEPTH = 2
N_A_LAYERS = DEPTH // 2
N_B_LAYERS = DEPTH - N_A_LAYERS

SSD_D_INNER = 2 * D_MODEL
SSD_HEADDIM = 64
SSD_HEADS = SSD_D_INNER // SSD_HEADDIM
SSD_GROUPS = 8
SSD_HEADS_PER_GROUP = SSD_HEADS // SSD_GROUPS
SSD_STATE = 128
SSD_CONV = 4
SSD_CHUNK = 128
SSD_CONV_DIM = SSD_D_INNER + 2 * SSD_GROUPS * SSD_STATE
SSD_IN_DIM = SSD_D_INNER + SSD_CONV_DIM + SSD_HEADS

NSA_HEADS = 16
NSA_KV_HEADS = 4
NSA_GROUP = NSA_HEADS // NSA_KV_HEADS
NSA_HEAD_DIM = 64
CMP_BLOCK = 32
CMP_STRIDE = 16
CMP_HIDDEN = 256
SLC_BLOCK = 64
SLC_TOPN = 16
SLC_FORCE = 1e4
WINDOW = 512
Q_BLOCK = 128
NSA_IN_DIM = NSA_HEADS * NSA_HEAD_DIM + 3 * NSA_HEADS
KV_DIM = 6 * NSA_KV_HEADS * NSA_HEAD_DIM

REL_BUCKETS = 32
REL_MAX_DISTANCE = 4096

PEER_HEADS = 8
PEER_N_KEYS = 128
PEER_EXPERTS = PEER_N_KEYS * PEER_N_KEYS
PEER_QDIM = 256
PEER_TOPK = 16
PEER_TOKEN_CHUNK = 128
PEER_V_SCALE = PEER_HEADS ** -0.5

PLE_DIM = 256
NORM_EPS = 1e-6
NEG_INF = -1e30

V7X_LANES = 128
V7X_VMEM_LIMIT_BYTES = 56 * 1024 * 1024


def _norm_matmul_kernel(x_ref, g_ref, w_ref, o_ref, *, normalize):
    x = x_ref[...]
    if normalize:
        x = x * lax.rsqrt(jnp.mean(x * x, axis=-1, keepdims=True) + NORM_EPS) * g_ref[...]
    o_ref[...] = jnp.dot(x.astype(jnp.bfloat16), w_ref[...], preferred_element_type=jnp.float32)


def norm_matmul(x, g, w, *, normalize, tm=512, tn=None):
    m, k = x.shape
    n = w.shape[1]
    n_pad = -(-n // V7X_LANES) * V7X_LANES
    wb = w.astype(jnp.bfloat16)
    if n_pad != n:
        wb = jnp.pad(wb, ((0, 0), (0, n_pad - n)))
    if tn is None:
        tn = n_pad
    assert m % tm == 0 and n_pad % tn == 0
    out = pl.pallas_call(
        functools.partial(_norm_matmul_kernel, normalize=normalize),
        grid=(m // tm, n_pad // tn),
        in_specs=[
            pl.BlockSpec((tm, k), lambda i, j: (i, 0)),
            pl.BlockSpec((1, k), lambda i, j: (0, 0)),
            pl.BlockSpec((k, tn), lambda i, j: (0, j)),
        ],
        out_specs=pl.BlockSpec((tm, tn), lambda i, j: (i, j)),
        out_shape=jax.ShapeDtypeStruct((m, n_pad), jnp.float32),
        compiler_params=pltpu.CompilerParams(
            dimension_semantics=("arbitrary", "arbitrary"),
            vmem_limit_bytes=V7X_VMEM_LIMIT_BYTES),
        name="norm_matmul" if normalize else "matmul",
    )(x, g.reshape(1, k), wb)
    return out[:, :n] if n_pad != n else out


def _mm(x3, g, w, normalize, **kw):
    b, s, k = x3.shape
    if g is None:
        g = jnp.ones((k,), jnp.float32)
    return norm_matmul(x3.reshape(b * s, k), g, w, normalize=normalize, **kw).reshape(b, s, w.shape[1])


def rms_norm(x, g):
    xf = x.astype(jnp.float32)
    y = xf * lax.rsqrt(jnp.mean(xf * xf, axis=-1, keepdims=True) + NORM_EPS)
    return (y * g.astype(jnp.float32)).astype(x.dtype)


def rel_bucket(dist):
    dist = jnp.maximum(dist, 0)
    max_exact = REL_BUCKETS // 2
    d = jnp.maximum(dist, 1).astype(jnp.float32)
    large = max_exact + (jnp.log(d / max_exact) / math.log(REL_MAX_DISTANCE / max_exact)
                         * (REL_BUCKETS - max_exact)).astype(jnp.int32)
    large = jnp.minimum(large, REL_BUCKETS - 1)
    return jnp.where(dist < max_exact, dist, large)


def ssd_mixer(h, norm_g, w_in, conv_w, conv_b, dt_bias, a_log, d_skip, gnorm_g, w_out):
    b, s, _ = h.shape
    G, R, P, N, L = SSD_GROUPS, SSD_HEADS_PER_GROUP, SSD_HEADDIM, SSD_STATE, SSD_CHUNK
    nc = s // L
    zxbcdt = _mm(h, norm_g, w_in, True, tm=256)
    z = zxbcdt[..., :SSD_D_INNER]
    xbc = zxbcdt[..., SSD_D_INNER:SSD_D_INNER + SSD_CONV_DIM]
    dt_raw = zxbcdt[..., SSD_D_INNER + SSD_CONV_DIM:]
    xbc = lax.conv_general_dilated(xbc, conv_w[:, None, :], window_strides=(1,),
                                   padding=[(SSD_CONV - 1, 0)],
                                   dimension_numbers=('NWC', 'WIO', 'NWC'),
                                   feature_group_count=SSD_CONV_DIM) + conv_b
    xbc = jax.nn.silu(xbc)
    x = xbc[..., :SSD_D_INNER].reshape(b, nc, L, G, R, P)
    bm = xbc[..., SSD_D_INNER:SSD_D_INNER + G * N].reshape(b, nc, L, G, N)
    cm = xbc[..., SSD_D_INNER + G * N:].reshape(b, nc, L, G, N)
    dt = jax.nn.softplus((dt_raw + dt_bias).astype(jnp.float32)).reshape(b, nc, L, G, R)
    a = -jnp.exp(a_log.astype(jnp.float32)).reshape(G, R)
    xd = x * dt[..., None].astype(x.dtype)
    a_dt = jnp.transpose(dt * a, (0, 3, 4, 1, 2))
    a_cs = jnp.cumsum(a_dt, axis=-1)
    causal = jnp.tril(jnp.ones((L, L), dtype=bool))
    seg = a_cs[..., :, None] - a_cs[..., None, :]
    decay_in = jnp.exp(jnp.where(causal, seg, -jnp.inf)).astype(x.dtype)
    cb = jnp.einsum('bclgn,bcsgn->bgcls', cm, bm)
    y_diag = jnp.einsum('bgrcls,bcsgrp->bclgrp', cb[:, :, None] * decay_in, xd)
    decay_to_end = jnp.exp(a_cs[..., -1:] - a_cs).astype(x.dtype)
    states = jnp.einsum('bcsgn,bgrcs,bcsgrp->bcgrpn', bm, decay_to_end, xd)
    chunk_decay = jnp.exp(a_cs[..., -1]).astype(x.dtype)

    def step(hh, inp):
        st, dec = inp
        return hh * dec[..., None, None] + st, hh

    h0 = jnp.zeros((b, G, R, P, N), x.dtype)
    _, prev = lax.scan(step, h0, (jnp.moveaxis(states, 1, 0), jnp.moveaxis(chunk_decay, -1, 0)))
    prev = jnp.moveaxis(prev, 0, 1)
    y_off = jnp.einsum('bclgn,bcgrpn,bgrcl->bclgrp', cm, prev, jnp.exp(a_cs).astype(x.dtype))
    y = (y_diag + y_off + x * d_skip.reshape(G, R, 1)).reshape(b, s, SSD_D_INNER)
    yg = (y * jax.nn.silu(z)).reshape(b, s, G, SSD_D_INNER // G)
    yg = rms_norm(yg, gnorm_g.reshape(G, -1)).reshape(b, s, SSD_D_INNER)
    return _mm(yg, None, w_out, False)


def _mlp_kernel(x_ref, w1_ref, w2_ref, o_ref):
    hid = jnp.dot(x_ref[...].astype(jnp.bfloat16), w1_ref[...], preferred_element_type=jnp.float32)
    hid = jax.nn.gelu(hid)
    o_ref[...] = jnp.dot(hid.astype(jnp.bfloat16), w2_ref[...], preferred_element_type=jnp.float32)


def mlp2(x, w1, w2, *, tm=512):
    m, k = x.shape
    hdim, n = w2.shape
    assert m % tm == 0
    return pl.pallas_call(
        _mlp_kernel,
        grid=(m // tm,),
        in_specs=[pl.BlockSpec((tm, k), lambda i: (i, 0)),
                  pl.BlockSpec((k, hdim), lambda i: (0, 0)),
                  pl.BlockSpec((hdim, n), lambda i: (0, 0))],
        out_specs=pl.BlockSpec((tm, n), lambda i: (i, 0)),
        out_shape=jax.ShapeDtypeStruct((m, n), jnp.float32),
        compiler_params=pltpu.CompilerParams(dimension_semantics=("arbitrary",),
                                             vmem_limit_bytes=V7X_VMEM_LIMIT_BYTES),
        name="cmp_mlp",
    )(x, w1.astype(jnp.bfloat16), w2.astype(jnp.bfloat16))


def compress_blocks(t, pos, w1, w2):
    b, s, g, d = t.shape
    halves = t.transpose(0, 2, 1, 3).reshape(b, g, s // CMP_STRIDE, CMP_STRIDE, d)
    nxt = jnp.concatenate([halves[:, :, 1:], jnp.zeros_like(halves[:, :, :1])], axis=2)
    blocks = jnp.concatenate([halves, nxt], axis=3) + pos
    flat = blocks.reshape(b * g * (s // CMP_STRIDE), CMP_BLOCK * d)
    return mlp2(flat, w1, w2).reshape(b, g, s // CMP_STRIDE, d)


N_SEL = SEQ // SLC_BLOCK
N_KT = SEQ // Q_BLOCK
CMP_FRONT = 256
CMP_ROWS = CMP_FRONT + SEQ // CMP_STRIDE
CMP_WIN = 256
REL_TABLE = 4096
NEAR_TILES = 24
WIN_TILES = WINDOW // Q_BLOCK + 1
SEL_GROUP = 4
QL = NSA_GROUP * Q_BLOCK


def _nsa_kernel(qT_ref, gT_ref, kc_ref, vcT_ref, kk_ref, vsT_ref, vwT_ref, at_ref, bcT_ref, bT_ref, bTw_ref,
                o_ref, s_scr, sel_scr):
    f32, bf16 = jnp.float32, jnp.bfloat16
    i = pl.program_id(2)
    qT = (qT_ref[0, 0, 0] * (NSA_HEAD_DIM ** -0.5)).astype(bf16)
    zq = jnp.zeros_like(qT)
    q_sel = jnp.concatenate([qT, zq], axis=0)
    q_win = jnp.concatenate([zq, qT], axis=0)

    r0 = pl.multiple_of(8 * i + 8, 8)
    s = jnp.dot(kc_ref[0, 0], qT, preferred_element_type=f32)
    rio = lax.broadcasted_iota(jnp.int32, (CMP_ROWS, Q_BLOCK), 0)
    rowmask = jnp.where((rio >= CMP_FRONT) & (rio < r0 + CMP_WIN), 0.0, NEG_INF)
    s_scr[...] = s + jnp.concatenate([rowmask] * NSA_GROUP, axis=1)
    s_scr[pl.ds(r0, CMP_WIN), :] = s_scr[pl.ds(r0, CMP_WIN), :] + bcT_ref[0]
    s = s_scr[...]
    m = jnp.max(s, axis=0, keepdims=True)
    p = jnp.exp(s - m)
    l = jnp.sum(p, axis=0, keepdims=True)
    inv = jnp.where(m > 0.1 * NEG_INF, 1.0 / l, 0.0)
    pn = p * inv
    o_c = jnp.dot(vcT_ref[0, 0], pn.astype(bf16), preferred_element_type=f32)

    psum = pn[:, 0:Q_BLOCK]
    for r in range(1, NSA_GROUP):
        psum = psum + pn[:, r * Q_BLOCK:(r + 1) * Q_BLOCK]
    hi = psum.astype(bf16)
    lo = (psum - hi.astype(f32)).astype(bf16)
    imp = (jnp.dot(at_ref[...], hi, preferred_element_type=f32)
           + jnp.dot(at_ref[...], lo, preferred_element_type=f32))
    blk = lax.broadcasted_iota(jnp.int32, (N_SEL, Q_BLOCK), 0)
    qi = lax.broadcasted_iota(jnp.int32, (N_SEL, Q_BLOCK), 1)
    cur = 2 * i + (qi >= SLC_BLOCK).astype(jnp.int32)
    forced = (blk == 0) | (blk == cur) | (blk == cur - 1)
    valid = blk <= cur
    score = jnp.where(forced, SLC_FORCE, jnp.where(valid, imp, -SLC_FORCE))
    blkf = blk.astype(f32)
    chosen = jnp.zeros((N_SEL, Q_BLOCK), f32)
    for _ in range(SLC_TOPN):
        mx = jnp.max(score, axis=0, keepdims=True)
        first = jnp.min(jnp.where(score == mx, blkf, float(N_SEL)), axis=0, keepdims=True)
        hit = blkf == first
        chosen = jnp.where(hit, 1.0, chosen)
        score = jnp.where(hit, -jnp.inf, score)
    sel_scr[...] = jnp.where((chosen > 0.5) & valid, 0.0, NEG_INF)

    def attend(kt, carry, q_aug, vT_ref, bias):
        m_run, l_run, acc = carry
        scs = [jnp.dot(kk_ref[0, 0, k], q_aug, preferred_element_type=f32) + b for k, b in zip(kt, bias)]
        m_new = m_run
        for sc in scs:
            m_new = jnp.maximum(m_new, jnp.max(sc, axis=0, keepdims=True))
        alpha = jnp.exp(m_run - m_new)
        l_new = alpha * l_run
        acc_new = alpha * acc
        for k, sc in zip(kt, scs):
            pt = jnp.exp(sc - m_new)
            l_new = l_new + jnp.sum(pt, axis=0, keepdims=True)
            acc_new = acc_new + jnp.dot(vT_ref[0, 0, k], pt.astype(bf16), preferred_element_type=f32)
        return m_new, l_new, acc_new

    def sel_mask(kt):
        m0 = jnp.broadcast_to(sel_scr[pl.ds(2 * kt, 1), :], (SLC_BLOCK, Q_BLOCK))
        m1 = jnp.broadcast_to(sel_scr[pl.ds(2 * kt + 1, 1), :], (SLC_BLOCK, Q_BLOCK))
        mk = jnp.concatenate([m0, m1], axis=0)
        return jnp.concatenate([mk] * NSA_GROUP, axis=1)

    init = (jnp.full((1, QL), NEG_INF, f32), jnp.zeros((1, QL), f32), jnp.zeros((NSA_HEAD_DIM, QL), f32))
    U = SEL_GROUP

    def far_group(g, c):
        kts = [g * U + u for u in range(U)]
        return attend(kts, c, q_sel, vsT_ref, [sel_mask(k) for k in kts])

    def near_group(g, c):
        kts, biases = [], []
        for u in range(U):
            delta = i - (g * U + u)
            k = jnp.minimum(g * U + u, i)
            tile = jnp.where(delta < 0, NEAR_TILES + 1, jnp.minimum(delta, NEAR_TILES))
            kts.append(k)
            biases.append(sel_mask(k) + bT_ref[0, tile])
        return attend(kts, c, q_sel, vsT_ref, biases)

    n_far = jnp.maximum((i - (NEAR_TILES - 1)) // U, 0)
    carry = lax.fori_loop(0, n_far, far_group, init)
    _, l_s, acc_s = lax.fori_loop(n_far, i // U + 1, near_group, carry)
    o_s = acc_s / l_s

    kts, biases = [], []
    for u in range(WIN_TILES):
        k = i - (WIN_TILES - 1) + u
        kts.append(jnp.maximum(k, 0))
        biases.append(bTw_ref[0, jnp.where(k >= 0, WIN_TILES - 1 - u, WIN_TILES)])
    _, l_w, acc_w = attend(kts, init, q_win, vwT_ref, biases)
    o_w = acc_w / l_w

    gate = jax.nn.sigmoid(gT_ref[0, 0, 0])
    o_ref[0, 0, 0] = gate[0:1] * o_c + gate[1:2] * o_s + gate[2:3] * o_w


def _bias_tables(rel_bias):
    G, R = NSA_KV_HEADS, NSA_GROUP
    bd = rel_bias[rel_bucket(jnp.arange(REL_TABLE))]
    bd = (bd - bd[REL_TABLE - 1]).T

    def skewed(rows, step, pad):
        v = jnp.concatenate([bd, jnp.full((NSA_HEADS, pad), NEG_INF, jnp.float32)], axis=1)
        width = REL_TABLE + pad
        flat = jnp.broadcast_to(v[:, None, :], (NSA_HEADS, rows, width)).reshape(NSA_HEADS, rows * width)
        return flat[:, :rows * (width - step)].reshape(NSA_HEADS, rows, width - step)

    near = skewed(Q_BLOCK, 1, Q_BLOCK)[:, :, :NEAR_TILES * Q_BLOCK]
    b_near = near.reshape(NSA_HEADS, Q_BLOCK, NEAR_TILES, Q_BLOCK).transpose(0, 2, 1, 3)
    d_win = (Q_BLOCK * np.arange(WIN_TILES)[:, None, None] + np.arange(Q_BLOCK)[None, None, :]
             - np.arange(Q_BLOCK)[None, :, None])
    b_win = jnp.where(jnp.asarray(d_win < WINDOW), b_near[:, :WIN_TILES], NEG_INF)
    one = jnp.zeros((NSA_HEADS, 1, Q_BLOCK, Q_BLOCK), jnp.float32)
    b_near = jnp.concatenate([b_near, one, one + NEG_INF], axis=1)
    b_win = jnp.concatenate([b_win, one + NEG_INF], axis=1)
    off = CMP_STRIDE * (CMP_WIN - 8) - (CMP_BLOCK - 1)
    b_cmp = skewed(CMP_WIN, CMP_STRIDE, CMP_WIN)[:, :, off:off + Q_BLOCK]

    def lanes(x):
        x = x.reshape((G, R) + x.shape[1:])
        x = jnp.moveaxis(x, 1, -2)
        return x.reshape(x.shape[:-2] + (QL,))

    return lanes(b_cmp), lanes(b_near), lanes(b_win)


def _pool_matrix():
    per = SLC_BLOCK // CMP_STRIDE
    n_span = CMP_BLOCK // CMP_STRIDE
    k = np.arange(CMP_ROWS)[None, :] - CMP_FRONT
    j = np.arange(N_SEL)[:, None]
    return jnp.asarray((k >= per * j - (n_span - 1)) & (k <= per * j + per - 1), jnp.bfloat16)


def nsa_shared_kv(h, kv_norm_g, w_kv, cmp_pos_k, cmp_pos_v, cmp_w1_k, cmp_w2_k, cmp_w1_v, cmp_w2_v):
    b, s, _ = h.shape
    G, DH = NSA_KV_HEADS, NSA_HEAD_DIM
    bf16 = jnp.bfloat16
    kv = _mm(h, kv_norm_g, w_kv, True).reshape(b, s, 6, G, DH)
    k_cmp = compress_blocks(kv[:, :, 0], cmp_pos_k, cmp_w1_k, cmp_w2_k)
    v_cmp = compress_blocks(kv[:, :, 1], cmp_pos_v, cmp_w1_v, cmp_w2_v)
    front = jnp.zeros((b, G, CMP_FRONT, DH), bf16)
    kc = jnp.concatenate([front, k_cmp.astype(bf16)], axis=2)
    vcT = jnp.concatenate([front, v_cmp.astype(bf16)], axis=2).transpose(0, 1, 3, 2)
    kk = jnp.concatenate([kv[:, :, 2], kv[:, :, 4]], axis=-1).astype(bf16)
    kk = kk.transpose(0, 2, 1, 3).reshape(b, G, N_KT, Q_BLOCK, 2 * DH)

    def vt(v):
        return v.astype(bf16).transpose(0, 2, 1, 3).reshape(b, G, N_KT, Q_BLOCK, DH).transpose(0, 1, 2, 4, 3)

    return kc, vcT, kk, vt(kv[:, :, 3]), vt(kv[:, :, 5])


def nsa_mixer(h, norm_g, w_in, w_out, rel_bias, kc, vcT, kk, vsT, vwT):
    b, s, _ = h.shape
    G, R, DH = NSA_KV_HEADS, NSA_GROUP, NSA_HEAD_DIM
    nb = s // Q_BLOCK
    proj = _mm(h, norm_g, w_in, True)
    qT = proj[..., :NSA_HEADS * DH].reshape(b, nb, Q_BLOCK, G, R, DH).transpose(0, 3, 1, 5, 4, 2)
    qT = qT.reshape(b, G, nb, DH, QL)
    gT = proj[..., NSA_HEADS * DH:].reshape(b, nb, Q_BLOCK, G, R, 3).transpose(0, 3, 1, 5, 4, 2)
    gT = gT.reshape(b, G, nb, 3, QL)
    b_cmp, b_near, b_win = _bias_tables(rel_bias)
    at = _pool_matrix()

    per_bg = lambda *blk: pl.BlockSpec((1, 1) + blk, lambda bi, gi, i: (bi, gi) + (0,) * len(blk))
    per_g = lambda *blk: pl.BlockSpec((1,) + blk, lambda bi, gi, i: (gi,) + (0,) * len(blk))
    oT = pl.pallas_call(
        _nsa_kernel,
        grid=(b, G, nb),
        in_specs=[
            pl.BlockSpec((1, 1, 1, DH, QL), lambda bi, gi, i: (bi, gi, i, 0, 0)),
            pl.BlockSpec((1, 1, 1, 3, QL), lambda bi, gi, i: (bi, gi, i, 0, 0)),
            per_bg(CMP_ROWS, DH),
            per_bg(DH, CMP_ROWS),
            per_bg(N_KT, Q_BLOCK, 2 * DH),
            per_bg(N_KT, DH, Q_BLOCK),
            per_bg(N_KT, DH, Q_BLOCK),
            pl.BlockSpec((N_SEL, CMP_ROWS), lambda bi, gi, i: (0, 0)),
            per_g(CMP_WIN, QL),
            per_g(NEAR_TILES + 2, Q_BLOCK, QL),
            per_g(WIN_TILES + 1, Q_BLOCK, QL),
        ],
        out_specs=pl.BlockSpec((1, 1, 1, DH, QL), lambda bi, gi, i: (bi, gi, i, 0, 0)),
        out_shape=jax.ShapeDtypeStruct((b, G, nb, DH, QL), jnp.float32),
        scratch_shapes=[pltpu.VMEM((CMP_ROWS, QL), jnp.float32), pltpu.VMEM((N_SEL, Q_BLOCK), jnp.float32)],
        compiler_params=pltpu.CompilerParams(
            dimension_semantics=("arbitrary", "arbitrary", "arbitrary"),
            vmem_limit_bytes=V7X_VMEM_LIMIT_BYTES),
        name="nsa_attention",
    )(qT, gT, kc, vcT, kk, vsT, vwT, at, b_cmp, b_near, b_win)
    o = oT.reshape(b, G, nb, DH, R, Q_BLOCK).transpose(0, 2, 5, 1, 4, 3).reshape(b, s, NSA_HEADS * DH)
    return _mm(o, None, w_out, False)


PEER_SLOTS = PEER_HEADS * PEER_TOPK
PEER_ROUTE_TM = 256
PEER_TT = 8
PEER_CHUNKS = D_MODEL // V7X_LANES
PEER_SLAB = 2 * PEER_CHUNKS
PEER_SLAB_STRIDE = 24


def _top_rows(s, rowid, n, payload=None):
    vals, picks = [], []
    for _ in range(n):
        mx = jnp.max(s, axis=0, keepdims=True)
        first = jnp.min(jnp.where(s == mx, rowid, float(s.shape[0])), axis=0, keepdims=True)
        hit = rowid == first
        vals.append(mx)
        picks.append(first if payload is None else jnp.max(jnp.where(hit, payload, -1.0), axis=0, keepdims=True))
        s = jnp.where(hit, -jnp.inf, s)
    return vals, picks


def _peer_route_kernel(h_ref, g_ref, wqT_ref, k1_ref, k2_ref, xn_ref, ids_ref, gate_ref):
    f32, bf16 = jnp.float32, jnp.bfloat16
    x = h_ref[...]
    xn = x * lax.rsqrt(jnp.mean(x * x, axis=-1, keepdims=True) + NORM_EPS) * g_ref[...]
    xn_ref[...] = xn
    qT = lax.dot_general(wqT_ref[...], xn.astype(bf16), (((1,), (1,)), ((), ())), preferred_element_type=f32)
    half = PEER_QDIM // 2
    tm = x.shape[0]
    key_id = lax.broadcasted_iota(jnp.int32, (PEER_N_KEYS, tm), 0).astype(f32)
    cand_pos = lax.broadcasted_iota(jnp.int32, (PEER_TOPK * PEER_TOPK, tm), 0).astype(f32)
    for hd in range(PEER_HEADS):
        q1 = qT[hd * PEER_QDIM: hd * PEER_QDIM + half].astype(bf16)
        q2 = qT[hd * PEER_QDIM + half: (hd + 1) * PEER_QDIM].astype(bf16)
        s1 = jnp.dot(k1_ref[hd], q1, preferred_element_type=f32)
        s2 = jnp.dot(k2_ref[hd], q2, preferred_element_type=f32)
        v1, i1 = _top_rows(s1, key_id, PEER_TOPK)
        v2, i2 = _top_rows(s2, key_id, PEER_TOPK)
        v2m = jnp.concatenate(v2, axis=0)
        i2m = jnp.concatenate(i2, axis=0)
        cand = jnp.concatenate([v1[a] + v2m for a in range(PEER_TOPK)], axis=0)
        cand_id = jnp.concatenate([i1[a] * float(PEER_N_KEYS) + i2m for a in range(PEER_TOPK)], axis=0)
        top_s, ids = _top_rows(cand, cand_pos, PEER_TOPK, payload=cand_id)
        top_s = jnp.concatenate(top_s, axis=0)
        e = jnp.exp(top_s - top_s[0:1])
        gate_ref[hd * PEER_TOPK:(hd + 1) * PEER_TOPK, :] = e / jnp.sum(e, axis=0, keepdims=True)
        ids_ref[hd * PEER_TOPK:(hd + 1) * PEER_TOPK, :] = jnp.concatenate(ids, axis=0).astype(jnp.int32)


def peer_route(h2, norm_g, w_q, keys1, keys2):
    T, d = h2.shape
    tm = PEER_ROUTE_TM
    bf16 = jnp.bfloat16
    nq = PEER_HEADS * PEER_QDIM
    half = PEER_QDIM // 2
    return pl.pallas_call(
        _peer_route_kernel,
        grid=(T // tm,),
        in_specs=[pl.BlockSpec((tm, d), lambda i: (i, 0)),
                  pl.BlockSpec((1, d), lambda i: (0, 0)),
                  pl.BlockSpec((nq, d), lambda i: (0, 0)),
                  pl.BlockSpec((PEER_HEADS, PEER_N_KEYS, half), lambda i: (0, 0, 0)),
                  pl.BlockSpec((PEER_HEADS, PEER_N_KEYS, half), lambda i: (0, 0, 0))],
        out_specs=[pl.BlockSpec((tm, d), lambda i: (i, 0)),
                   pl.BlockSpec((PEER_SLOTS, tm), lambda i: (0, i)),
                   pl.BlockSpec((PEER_SLOTS, tm), lambda i: (0, i))],
        out_shape=[jax.ShapeDtypeStruct((T, d), jnp.float32),
                   jax.ShapeDtypeStruct((PEER_SLOTS, T), jnp.int32),
                   jax.ShapeDtypeStruct((PEER_SLOTS, T), jnp.float32)],
        compiler_params=pltpu.CompilerParams(dimension_semantics=("arbitrary",),
                                             vmem_limit_bytes=V7X_VMEM_LIMIT_BYTES),
        name="peer_route",
    )(h2, norm_g.reshape(1, d), w_q.T.astype(bf16), keys1.astype(bf16), keys2.astype(bf16))


def _peer_slab(buf, slot, r):
    return buf.at[slot, pl.ds(pl.multiple_of(r * PEER_SLAB_STRIDE, 8), PEER_SLAB)]


def _peer_expert_kernel(ids0_ref, idsn_ref, xn_ref, gt_ref, uv_hbm, o_ref, buf, sem):
    i = pl.program_id(0)
    n = pl.num_programs(0)
    slot = i % 2
    groups = PEER_TT * PEER_SLOTS // 8

    def issue(ids_ref, dst_slot):
        def group(gidx, carry):
            for u in range(8):
                r = gidx * 8 + u
                pltpu.make_async_copy(uv_hbm.at[ids_ref[r]], _peer_slab(buf, dst_slot, r),
                                      sem.at[dst_slot]).start(priority=u % 2)
            return carry
        lax.fori_loop(0, groups, group, 0)

    @pl.when(i == 0)
    def _():
        issue(ids0_ref, 0)

    @pl.when(i + 1 < n)
    def _():
        issue(idsn_ref, 1 - slot)

    def wait_group(gidx, carry):
        for u in range(8):
            pltpu.make_async_copy(uv_hbm.at[0], _peer_slab(buf, slot, gidx * 8 + u), sem.at[slot]).wait()
        return carry
    lax.fori_loop(0, groups, wait_group, 0)

    for t in range(PEER_TT):
        base = t * PEER_SLOTS * PEER_SLAB_STRIDE
        xrow = xn_ref[t:t + 1, :]
        acc = jnp.zeros((PEER_SLOTS, V7X_LANES), jnp.float32)
        for j in range(PEER_CHUNKS):
            tile = buf[slot, pl.ds(base + j, PEER_SLOTS, stride=PEER_SLAB_STRIDE), :]
            acc = acc + tile * xrow[:, j * V7X_LANES:(j + 1) * V7X_LANES]
        act = jnp.sum(acc, axis=1, keepdims=True)
        w = gt_ref[0, :, t:t + 1] * jax.nn.gelu(act)
        for j in range(PEER_CHUNKS):
            tile = buf[slot, pl.ds(base + PEER_CHUNKS + j, PEER_SLOTS, stride=PEER_SLAB_STRIDE), :]
            o_ref[t:t + 1, j * V7X_LANES:(j + 1) * V7X_LANES] = jnp.sum(tile * w, axis=0, keepdims=True)


def peer_experts(xn, idsT, gateT, u_tab, v_tab):
    T, d = xn.shape
    tt = PEER_TT
    nt = T // tt
    uv = jnp.concatenate([u_tab.reshape(PEER_EXPERTS, PEER_CHUNKS, V7X_LANES),
                          v_tab.reshape(PEER_EXPERTS, PEER_CHUNKS, V7X_LANES)], axis=1)
    ids = idsT.T.reshape(T * PEER_SLOTS)
    gt = gateT.reshape(PEER_SLOTS, nt, tt).transpose(1, 0, 2)
    return pl.pallas_call(
        _peer_expert_kernel,
        grid=(nt,),
        in_specs=[pl.BlockSpec((tt * PEER_SLOTS,), lambda i: (0,), memory_space=pltpu.SMEM),
                  pl.BlockSpec((tt * PEER_SLOTS,), lambda i: (jnp.minimum(i + 1, nt - 1),), memory_space=pltpu.SMEM),
                  pl.BlockSpec((tt, d), lambda i: (i, 0)),
                  pl.BlockSpec((1, PEER_SLOTS, tt), lambda i: (i, 0, 0)),
                  pl.BlockSpec(memory_space=pl.ANY)],
        out_specs=pl.BlockSpec((tt, d), lambda i: (i, 0)),
        out_shape=jax.ShapeDtypeStruct((T, d), jnp.float32),
        scratch_shapes=[pltpu.VMEM((2, tt * PEER_SLOTS * PEER_SLAB_STRIDE, V7X_LANES), jnp.float32),
                        pltpu.SemaphoreType.DMA((2,))],
        compiler_params=pltpu.CompilerParams(dimension_semantics=("arbitrary",),
                                             vmem_limit_bytes=V7X_VMEM_LIMIT_BYTES),
        name="peer_experts",
    )(ids, ids, xn, gt, uv)


def peer_ffn(h, norm_g, w_q, keys1, keys2, u_tab, v_tab):
    b, s, d = h.shape
    xn, idsT, gateT = peer_route(h.reshape(b * s, d), norm_g, w_q, keys1, keys2)
    return peer_experts(xn, idsT, gateT, u_tab, v_tab).reshape(b, s, d)


def per_layer_embed(h, p_i, norm_g, w_up, w_gate):
    return _mm(p_i, None, w_up, False) * jax.nn.sigmoid(_mm(h, norm_g, w_gate, True))


def kernel(x, p, a_norm_g, a_w_in, a_conv_w, a_conv_b, a_dt_bias, a_log, a_d_skip, a_gnorm_g, a_w_out,
           kv_norm_g, w_kv, cmp_pos_k, cmp_pos_v, cmp_w1_k, cmp_w2_k, cmp_w1_v, cmp_w2_v, rel_bias,
           b_norm_g, b_w_in, b_w_out, c_norm_g, c_w_q, c_keys1, c_keys2, c_u, c_v,
           e_norm_g, e_w_up, e_w_gate, final_g):
    h = x
    shared = None
    for i in range(DEPTH):
        if i < N_A_LAYERS:
            h = h + ssd_mixer(h, a_norm_g[i], a_w_in[i], a_conv_w[i], a_conv_b[i], a_dt_bias[i],
                              a_log[i], a_d_skip[i], a_gnorm_g[i], a_w_out[i])
        else:
            if shared is None:
                shared = nsa_shared_kv(h, kv_norm_g, w_kv, cmp_pos_k, cmp_pos_v,
                                       cmp_w1_k, cmp_w2_k, cmp_w1_v, cmp_w2_v)
            j = i - N_A_LAYERS
            h = h + nsa_mixer(h, b_norm_g[j], b_w_in[j], b_w_out[j], rel_bias, *shared)
        h = h + peer_ffn(h, c_norm_g[i], c_w_q[i], c_keys1[i], c_keys2[i], c_u[i], c_v[i])
        h = h + per_layer_embed(h, p[i], e_norm_g[i], e_w_up[i], e_w_gate[i])
    return rms_norm(h, final_g)
```

```python
import functools
import math

import jax
import jax.numpy as jnp
import numpy as np
from jax import lax
from jax.experimental import pallas as pl
from jax.experimental.pallas import tpu as pltpu

D_MODEL = 1024
BATCH = 2
SEQ = 16384
DEPTH = 2
N_A_LAYERS = DEPTH // 2
N_B_LAYERS = DEPTH - N_A_LAYERS

SSD_D_INNER = 2 * D_MODEL
SSD_HEADDIM = 64
SSD_HEADS = SSD_D_INNER // SSD_HEADDIM
SSD_GROUPS = 8
SSD_HEADS_PER_GROUP = SSD_HEADS // SSD_GROUPS
SSD_STATE = 128
SSD_CONV = 4
SSD_CHUNK = 128
SSD_CONV_DIM = SSD_D_INNER + 2 * SSD_GROUPS * SSD_STATE
SSD_IN_DIM = SSD_D_INNER + SSD_CONV_DIM + SSD_HEADS

NSA_HEADS = 16
NSA_KV_HEADS = 4
NSA_GROUP = NSA_HEADS // NSA_KV_HEADS
NSA_HEAD_DIM = 64
CMP_BLOCK = 32
CMP_STRIDE = 16
CMP_HIDDEN = 256
SLC_BLOCK = 64
SLC_TOPN = 16
SLC_FORCE = 1e4
WINDOW = 512
Q_BLOCK = 128
NSA_IN_DIM = NSA_HEADS * NSA_HEAD_DIM + 3 * NSA_HEADS
KV_DIM = 6 * NSA_KV_HEADS * NSA_HEAD_DIM

REL_BUCKETS = 32
REL_MAX_DISTANCE = 4096

PEER_HEADS = 8
PEER_N_KEYS = 128
PEER_EXPERTS = PEER_N_KEYS * PEER_N_KEYS
PEER_QDIM = 256
PEER_TOPK = 16
PEER_TOKEN_CHUNK = 128
PEER_V_SCALE = PEER_HEADS ** -0.5

PLE_DIM = 256
NORM_EPS = 1e-6
NEG_INF = -1e30

V7X_LANES = 128
V7X_VMEM_LIMIT_BYTES = 56 * 1024 * 1024


def _norm_matmul_kernel(x_ref, g_ref, w_ref, o_ref, *, normalize):
    x = x_ref[...]
    if normalize:
        x = x * lax.rsqrt(jnp.mean(x * x, axis=-1, keepdims=True) + NORM_EPS) * g_ref[...]
    o_ref[...] = jnp.dot(x.astype(jnp.bfloat16), w_ref[...], preferred_element_type=jnp.float32)


def norm_matmul(x, g, w, *, normalize, tm=512, tn=None):
    m, k = x.shape
    n = w.shape[1]
    n_pad = -(-n // V7X_LANES) * V7X_LANES
    wb = w.astype(jnp.bfloat16)
    if n_pad != n:
        wb = jnp.pad(wb, ((0, 0), (0, n_pad - n)))
    if tn is None:
        tn = n_pad
    assert m % tm == 0 and n_pad % tn == 0
    out = pl.pallas_call(
        functools.partial(_norm_matmul_kernel, normalize=normalize),
        grid=(m // tm, n_pad // tn),
        in_specs=[
            pl.BlockSpec((tm, k), lambda i, j: (i, 0)),
            pl.BlockSpec((1, k), lambda i, j: (0, 0)),
            pl.BlockSpec((k, tn), lambda i, j: (0, j)),
        ],
        out_specs=pl.BlockSpec((tm, tn), lambda i, j: (i, j)),
        out_shape=jax.ShapeDtypeStruct((m, n_pad), jnp.float32),
        compiler_params=pltpu.CompilerParams(
            dimension_semantics=("arbitrary", "arbitrary"),
            vmem_limit_bytes=V7X_VMEM_LIMIT_BYTES),
        name="norm_matmul" if normalize else "matmul",
    )(x, g.reshape(1, k), wb)
    return out[:, :n] if n_pad != n else out


def _mm(x3, g, w, normalize, **kw):
    b, s, k = x3.shape
    if g is None:
        g = jnp.ones((k,), jnp.float32)
    return norm_matmul(x3.reshape(b * s, k), g, w, normalize=normalize, **kw).reshape(b, s, w.shape[1])


def rms_norm(x, g):
    xf = x.astype(jnp.float32)
    y = xf * lax.rsqrt(jnp.mean(xf * xf, axis=-1, keepdims=True) + NORM_EPS)
    return (y * g.astype(jnp.float32)).astype(x.dtype)


def rel_bucket(dist):
    dist = jnp.maximum(dist, 0)
    max_exact = REL_BUCKETS // 2
    d = jnp.maximum(dist, 1).astype(jnp.float32)
    large = max_exact + (jnp.log(d / max_exact) / math.log(REL_MAX_DISTANCE / max_exact)
                         * (REL_BUCKETS - max_exact)).astype(jnp.int32)
    large = jnp.minimum(large, REL_BUCKETS - 1)
    return jnp.where(dist < max_exact, dist, large)


SSD_GROUP_W = SSD_HEADS_PER_GROUP * SSD_HEADDIM
SSD_IN_PAD = -(-SSD_IN_DIM // V7X_LANES) * V7X_LANES
SSD_TAIL = 8


def _split3(x):
    bf16, f32 = jnp.bfloat16, jnp.float32
    hi = x.astype(bf16)
    r1 = x - hi.astype(f32)
    mid = r1.astype(bf16)
    lo = (r1 - mid.astype(f32)).astype(bf16)
    return hi, mid, lo


def _ssd_kernel(zx_ref, h_ref, cw_ref, cb_ref, dtb_ref, alog_ref, dskip_ref, gn_ref, wout_ref, tri_ref, exp_ref,
                o_ref, xs_scr, state_scr):
    f32, bf16 = jnp.float32, jnp.bfloat16
    L, N, GW = SSD_CHUNK, SSD_STATE, SSD_GROUP_W
    c = pl.program_id(1)

    @pl.when(c == 0)
    def _():
        xs_scr[0:SSD_TAIL, :] = jnp.zeros((SSD_TAIL, SSD_CONV_DIM), f32)
        state_scr[...] = jnp.zeros_like(state_scr)

    xs_scr[SSD_TAIL:SSD_TAIL + L, :] = zx_ref[:, SSD_D_INNER:SSD_D_INNER + SSD_CONV_DIM]
    conv = cb_ref[...]
    for w in range(SSD_CONV):
        conv = conv + cw_ref[w:w + 1, :] * xs_scr[pl.ds(SSD_TAIL - (SSD_CONV - 1) + w, L), :]
    xs_scr[0:SSD_TAIL, :] = xs_scr[L:L + SSD_TAIL, :]
    xbc = conv * jax.nn.sigmoid(conv)

    dt_in = zx_ref[:, SSD_D_INNER + SSD_CONV_DIM:SSD_IN_PAD] + dtb_ref[...]
    dt = jnp.maximum(dt_in, 0.0) + jnp.log1p(jnp.exp(-jnp.abs(dt_in)))
    a_dt = dt * -jnp.exp(alog_ref[...])
    a_cs = sum(jnp.dot(tri_ref[...], t, preferred_element_type=f32) for t in _split3(a_dt))
    a_csT = a_cs.T
    a_end = a_cs[L - 1:L, :]
    grow = jnp.exp(a_cs)
    to_end = jnp.exp(a_end - a_cs)

    def per_channel(q):
        return sum(jnp.dot(t, exp_ref[...], preferred_element_type=f32) for t in _split3(q)[:2])

    dt_c, grow_c, to_end_c = per_channel(dt), per_channel(grow), per_channel(to_end)
    end_c = per_channel(jnp.broadcast_to(jnp.exp(a_end), (8, V7X_LANES)))[0:1]
    dskip_c = per_channel(jnp.broadcast_to(dskip_ref[...], (8, V7X_LANES)))[0:1]

    x = xbc[:, :SSD_D_INNER]
    xd = x * dt_c
    xd_bf = xd.astype(bf16)
    xe_bf = (xd * to_end_c).astype(bf16)
    row_i = lax.broadcasted_iota(jnp.int32, (L, L), 0)
    col_i = lax.broadcasted_iota(jnp.int32, (L, L), 1)
    lane_head = lax.broadcasted_iota(jnp.int32, (L, GW), 1) // SSD_HEADDIM
    ys = []
    for g in range(SSD_GROUPS):
        bm = xbc[:, SSD_D_INNER + g * N:SSD_D_INNER + (g + 1) * N]
        cm = xbc[:, SSD_D_INNER + SSD_GROUPS * N + g * N:SSD_D_INNER + SSD_GROUPS * N + (g + 1) * N].astype(bf16)
        bmT = bm.T.astype(bf16)
        cb = jnp.dot(cm, bmT, preferred_element_type=f32)
        xd_g = xd_bf[:, g * GW:(g + 1) * GW]
        y_g = jnp.zeros((L, GW), f32)
        for r in range(SSD_HEADS_PER_GROUP):
            hd = g * SSD_HEADS_PER_GROUP + r
            seg = a_cs[:, hd:hd + 1] - a_csT[hd:hd + 1, :]
            decay = jnp.where(row_i >= col_i, jnp.exp(seg), 0.0)
            xr = jnp.where(lane_head == r, xd_g, jnp.zeros_like(xd_g))
            y_g = y_g + jnp.dot((cb * decay).astype(bf16), xr, preferred_element_type=f32)
        hT = state_scr[g]
        y_off = jnp.dot(cm, hT.astype(bf16), preferred_element_type=f32) * grow_c[:, g * GW:(g + 1) * GW]
        new_states = jnp.dot(bmT, xe_bf[:, g * GW:(g + 1) * GW], preferred_element_type=f32)
        state_scr[g] = hT * end_c[:, g * GW:(g + 1) * GW] + new_states
        ys.append(y_g + y_off)
    y = jnp.concatenate(ys, axis=1) + x * dskip_c

    z = zx_ref[:, :SSD_D_INNER]
    yz = y * (z * jax.nn.sigmoid(z))
    parts = []
    for g in range(SSD_GROUPS):
        t = yz[:, g * GW:(g + 1) * GW]
        parts.append(t * lax.rsqrt(jnp.mean(t * t, axis=-1, keepdims=True) + NORM_EPS))
    yn = jnp.concatenate(parts, axis=1) * gn_ref[...]
    o_ref[...] = h_ref[...] + jnp.dot(yn.astype(bf16), wout_ref[...], preferred_element_type=f32)


def ssd_mixer_residual(h, norm_g, w_in, conv_w, conv_b, dt_bias, a_log, d_skip, gnorm_g, w_out):
    b, s, d = h.shape
    L = SSD_CHUNK
    nc = s // L
    w_in_p = jnp.pad(w_in, ((0, 0), (0, SSD_IN_PAD - SSD_IN_DIM)))
    zx = norm_matmul(h.reshape(b * s, d), norm_g, w_in_p, normalize=True, tm=256)
    lane_pad = lambda v: jnp.pad(v.reshape(1, -1), ((0, 0), (0, V7X_LANES - v.shape[-1])))
    tri = jnp.asarray(np.tril(np.ones((L, L))), jnp.bfloat16)
    head_of = np.arange(SSD_D_INNER) // SSD_HEADDIM
    expand = jnp.asarray(np.arange(V7X_LANES)[:, None] == head_of[None, :], jnp.bfloat16)
    const = lambda *shape: pl.BlockSpec(shape, lambda bi, ci: (0,) * len(shape))
    out = pl.pallas_call(
        _ssd_kernel,
        grid=(b, nc),
        in_specs=[pl.BlockSpec((L, SSD_IN_PAD), lambda bi, ci: (bi * nc + ci, 0)),
                  pl.BlockSpec((L, d), lambda bi, ci: (bi * nc + ci, 0)),
                  const(SSD_CONV, SSD_CONV_DIM), const(1, SSD_CONV_DIM),
                  const(1, V7X_LANES), const(1, V7X_LANES), const(1, V7X_LANES),
                  const(1, SSD_D_INNER), const(SSD_D_INNER, d), const(L, L), const(V7X_LANES, SSD_D_INNER)],
        out_specs=pl.BlockSpec((L, d), lambda bi, ci: (bi * nc + ci, 0)),
        out_shape=jax.ShapeDtypeStruct((b * s, d), jnp.float32),
        scratch_shapes=[pltpu.VMEM((L + SSD_TAIL, SSD_CONV_DIM), jnp.float32),
                        pltpu.VMEM((SSD_GROUPS, SSD_STATE, SSD_GROUP_W), jnp.float32)],
        compiler_params=pltpu.CompilerParams(dimension_semantics=("arbitrary", "arbitrary"),
                                             vmem_limit_bytes=V7X_VMEM_LIMIT_BYTES),
        name="ssd_mixer",
    )(zx, h.reshape(b * s, d), conv_w, conv_b.reshape(1, -1), lane_pad(dt_bias), lane_pad(a_log), lane_pad(d_skip),
      gnorm_g.reshape(1, -1), w_out.astype(jnp.bfloat16), tri, expand)
    return out.reshape(b, s, d)


def _mlp_kernel(x_ref, w1_ref, w2_ref, o_ref):
    hid = jnp.dot(x_ref[...].astype(jnp.bfloat16), w1_ref[...], preferred_element_type=jnp.float32)
    hid = jax.nn.gelu(hid)
    o_ref[...] = jnp.dot(hid.astype(jnp.bfloat16), w2_ref[...], preferred_element_type=jnp.float32)


def mlp2(x, w1, w2, *, tm=512):
    m, k = x.shape
    hdim, n = w2.shape
    assert m % tm == 0
    return pl.pallas_call(
        _mlp_kernel,
        grid=(m // tm,),
        in_specs=[pl.BlockSpec((tm, k), lambda i: (i, 0)),
                  pl.BlockSpec((k, hdim), lambda i: (0, 0)),
                  pl.BlockSpec((hdim, n), lambda i: (0, 0))],
        out_specs=pl.BlockSpec((tm, n), lambda i: (i, 0)),
        out_shape=jax.ShapeDtypeStruct((m, n), jnp.float32),
        compiler_params=pltpu.CompilerParams(dimension_semantics=("arbitrary",),
                                             vmem_limit_bytes=V7X_VMEM_LIMIT_BYTES),
        name="cmp_mlp",
    )(x, w1.astype(jnp.bfloat16), w2.astype(jnp.bfloat16))


def compress_blocks(t, pos, w1, w2):
    b, s, g, d = t.shape
    halves = t.transpose(0, 2, 1, 3).reshape(b, g, s // CMP_STRIDE, CMP_STRIDE, d)
    nxt = jnp.concatenate([halves[:, :, 1:], jnp.zeros_like(halves[:, :, :1])], axis=2)
    blocks = jnp.concatenate([halves, nxt], axis=3) + pos
    flat = blocks.reshape(b * g * (s // CMP_STRIDE), CMP_BLOCK * d)
    return mlp2(flat, w1, w2).reshape(b, g, s // CMP_STRIDE, d)


N_SEL = SEQ // SLC_BLOCK
N_KT = SEQ // Q_BLOCK
CMP_FRONT = 256
CMP_ROWS = CMP_FRONT + SEQ // CMP_STRIDE
CMP_WIN = 256
REL_TABLE = 4096
NEAR_TILES = 24
WIN_TILES = WINDOW // Q_BLOCK + 1
SEL_GROUP = 4
QL = NSA_GROUP * Q_BLOCK


def _nsa_kernel(qT_ref, gT_ref, kc_ref, vcT_ref, kk_ref, vsT_ref, vwT_ref, at_ref, bcT_ref, bT_ref, bTw_ref,
                o_ref, s_scr, sel_scr):
    f32, bf16 = jnp.float32, jnp.bfloat16
    i = pl.program_id(2)
    qT = (qT_ref[0, 0, 0] * (NSA_HEAD_DIM ** -0.5)).astype(bf16)
    zq = jnp.zeros_like(qT)
    q_sel = jnp.concatenate([qT, zq], axis=0)
    q_win = jnp.concatenate([zq, qT], axis=0)

    r0 = pl.multiple_of(8 * i + 8, 8)
    s = jnp.dot(kc_ref[0, 0], qT, preferred_element_type=f32)
    rio = lax.broadcasted_iota(jnp.int32, (CMP_ROWS, Q_BLOCK), 0)
    rowmask = jnp.where((rio >= CMP_FRONT) & (rio < r0 + CMP_WIN), 0.0, NEG_INF)
    s_scr[...] = s + jnp.concatenate([rowmask] * NSA_GROUP, axis=1)
    s_scr[pl.ds(r0, CMP_WIN), :] = s_scr[pl.ds(r0, CMP_WIN), :] + bcT_ref[0]
    s = s_scr[...]
    m = jnp.max(s, axis=0, keepdims=True)
    p = jnp.exp(s - m)
    l = jnp.sum(p, axis=0, keepdims=True)
    inv = jnp.where(m > 0.1 * NEG_INF, 1.0 / l, 0.0)
    pn = p * inv
    o_c = jnp.dot(vcT_ref[0, 0], pn.astype(bf16), preferred_element_type=f32)

    psum = pn[:, 0:Q_BLOCK]
    for r in range(1, NSA_GROUP):
        psum = psum + pn[:, r * Q_BLOCK:(r + 1) * Q_BLOCK]
    hi = psum.astype(bf16)
    lo = (psum - hi.astype(f32)).astype(bf16)
    imp = (jnp.dot(at_ref[...], hi, preferred_element_type=f32)
           + jnp.dot(at_ref[...], lo, preferred_element_type=f32))
    blk = lax.broadcasted_iota(jnp.int32, (N_SEL, Q_BLOCK), 0)
    qi = lax.broadcasted_iota(jnp.int32, (N_SEL, Q_BLOCK), 1)
    cur = 2 * i + (qi >= SLC_BLOCK).astype(jnp.int32)
    forced = (blk == 0) | (blk == cur) | (blk == cur - 1)
    valid = blk <= cur
    score = jnp.where(forced, SLC_FORCE, jnp.where(valid, imp, -SLC_FORCE))
    blkf = blk.astype(f32)
    chosen = jnp.zeros((N_SEL, Q_BLOCK), f32)
    for _ in range(SLC_TOPN):
        mx = jnp.max(score, axis=0, keepdims=True)
        first = jnp.min(jnp.where(score == mx, blkf, float(N_SEL)), axis=0, keepdims=True)
        hit = blkf == first
        chosen = jnp.where(hit, 1.0, chosen)
        score = jnp.where(hit, -jnp.inf, score)
    sel_scr[...] = jnp.where((chosen > 0.5) & valid, 0.0, NEG_INF)

    def attend(kt, carry, q_aug, vT_ref, bias):
        m_run, l_run, acc = carry
        scs = [jnp.dot(kk_ref[0, 0, k], q_aug, preferred_element_type=f32) + b for k, b in zip(kt, bias)]
        m_new = m_run
        for sc in scs:
            m_new = jnp.maximum(m_new, jnp.max(sc, axis=0, keepdims=True))
        alpha = jnp.exp(m_run - m_new)
        l_new = alpha * l_run
        acc_new = alpha * acc
        for k, sc in zip(kt, scs):
            pt = jnp.exp(sc - m_new)
            l_new = l_new + jnp.sum(pt, axis=0, keepdims=True)
            acc_new = acc_new + jnp.dot(vT_ref[0, 0, k], pt.astype(bf16), preferred_element_type=f32)
        return m_new, l_new, acc_new

    def sel_mask(kt):
        m0 = jnp.broadcast_to(sel_scr[pl.ds(2 * kt, 1), :], (SLC_BLOCK, Q_BLOCK))
        m1 = jnp.broadcast_to(sel_scr[pl.ds(2 * kt + 1, 1), :], (SLC_BLOCK, Q_BLOCK))
        mk = jnp.concatenate([m0, m1], axis=0)
        return jnp.concatenate([mk] * NSA_GROUP, axis=1)

    init = (jnp.full((1, QL), NEG_INF, f32), jnp.zeros((1, QL), f32), jnp.zeros((NSA_HEAD_DIM, QL), f32))
    U = SEL_GROUP

    def far_group(g, c):
        kts = [g * U + u for u in range(U)]
        return attend(kts, c, q_sel, vsT_ref, [sel_mask(k) for k in kts])

    def near_group(g, c):
        kts, biases = [], []
        for u in range(U):
            delta = i - (g * U + u)
            k = jnp.minimum(g * U + u, i)
            tile = jnp.where(delta < 0, NEAR_TILES + 1, jnp.minimum(delta, NEAR_TILES))
            kts.append(k)
            biases.append(sel_mask(k) + bT_ref[0, tile])
        return attend(kts, c, q_sel, vsT_ref, biases)

    n_far = jnp.maximum((i - (NEAR_TILES - 1)) // U, 0)
    carry = lax.fori_loop(0, n_far, far_group, init)
    _, l_s, acc_s = lax.fori_loop(n_far, i // U + 1, near_group, carry)
    o_s = acc_s / l_s

    kts, biases = [], []
    for u in range(WIN_TILES):
        k = i - (WIN_TILES - 1) + u
        kts.append(jnp.maximum(k, 0))
        biases.append(bTw_ref[0, jnp.where(k >= 0, WIN_TILES - 1 - u, WIN_TILES)])
    _, l_w, acc_w = attend(kts, init, q_win, vwT_ref, biases)
    o_w = acc_w / l_w

    gate = jax.nn.sigmoid(gT_ref[0, 0, 0])
    o_ref[0, 0, 0] = gate[0:1] * o_c + gate[1:2] * o_s + gate[2:3] * o_w


def _bias_tables(rel_bias):
    G, R = NSA_KV_HEADS, NSA_GROUP
    bd = rel_bias[rel_bucket(jnp.arange(REL_TABLE))]
    bd = (bd - bd[REL_TABLE - 1]).T

    def skewed(rows, step, pad):
        v = jnp.concatenate([bd, jnp.full((NSA_HEADS, pad), NEG_INF, jnp.float32)], axis=1)
        width = REL_TABLE + pad
        flat = jnp.broadcast_to(v[:, None, :], (NSA_HEADS, rows, width)).reshape(NSA_HEADS, rows * width)
        return flat[:, :rows * (width - step)].reshape(NSA_HEADS, rows, width - step)

    near = skewed(Q_BLOCK, 1, Q_BLOCK)[:, :, :NEAR_TILES * Q_BLOCK]
    b_near = near.reshape(NSA_HEADS, Q_BLOCK, NEAR_TILES, Q_BLOCK).transpose(0, 2, 1, 3)
    d_win = (Q_BLOCK * np.arange(WIN_TILES)[:, None, None] + np.arange(Q_BLOCK)[None, None, :]
             - np.arange(Q_BLOCK)[None, :, None])
    b_win = jnp.where(jnp.asarray(d_win < WINDOW), b_near[:, :WIN_TILES], NEG_INF)
    one = jnp.zeros((NSA_HEADS, 1, Q_BLOCK, Q_BLOCK), jnp.float32)
    b_near = jnp.concatenate([b_near, one, one + NEG_INF], axis=1)
    b_win = jnp.concatenate([b_win, one + NEG_INF], axis=1)
    off = CMP_STRIDE * (CMP_WIN - 8) - (CMP_BLOCK - 1)
    b_cmp = skewed(CMP_WIN, CMP_STRIDE, CMP_WIN)[:, :, off:off + Q_BLOCK]

    def lanes(x):
        x = x.reshape((G, R) + x.shape[1:])
        x = jnp.moveaxis(x, 1, -2)
        return x.reshape(x.shape[:-2] + (QL,))

    return lanes(b_cmp), lanes(b_near), lanes(b_win)


def _pool_matrix():
    per = SLC_BLOCK // CMP_STRIDE
    n_span = CMP_BLOCK // CMP_STRIDE
    k = np.arange(CMP_ROWS)[None, :] - CMP_FRONT
    j = np.arange(N_SEL)[:, None]
    return jnp.asarray((k >= per * j - (n_span - 1)) & (k <= per * j + per - 1), jnp.bfloat16)


def nsa_shared_kv(h, kv_norm_g, w_kv, cmp_pos_k, cmp_pos_v, cmp_w1_k, cmp_w2_k, cmp_w1_v, cmp_w2_v):
    b, s, _ = h.shape
    G, DH = NSA_KV_HEADS, NSA_HEAD_DIM
    bf16 = jnp.bfloat16
    kv = _mm(h, kv_norm_g, w_kv, True).reshape(b, s, 6, G, DH)
    k_cmp = compress_blocks(kv[:, :, 0], cmp_pos_k, cmp_w1_k, cmp_w2_k)
    v_cmp = compress_blocks(kv[:, :, 1], cmp_pos_v, cmp_w1_v, cmp_w2_v)
    front = jnp.zeros((b, G, CMP_FRONT, DH), bf16)
    kc = jnp.concatenate([front, k_cmp.astype(bf16)], axis=2)
    vcT = jnp.concatenate([front, v_cmp.astype(bf16)], axis=2).transpose(0, 1, 3, 2)
    kk = jnp.concatenate([kv[:, :, 2], kv[:, :, 4]], axis=-1).astype(bf16)
    kk = kk.transpose(0, 2, 1, 3).reshape(b, G, N_KT, Q_BLOCK, 2 * DH)

    def vt(v):
        return v.astype(bf16).transpose(0, 2, 1, 3).reshape(b, G, N_KT, Q_BLOCK, DH).transpose(0, 1, 2, 4, 3)

    return kc, vcT, kk, vt(kv[:, :, 3]), vt(kv[:, :, 5])


def nsa_mixer(h, norm_g, w_in, w_out, rel_bias, kc, vcT, kk, vsT, vwT):
    b, s, _ = h.shape
    G, R, DH = NSA_KV_HEADS, NSA_GROUP, NSA_HEAD_DIM
    nb = s // Q_BLOCK
    proj = _mm(h, norm_g, w_in, True)
    qT = proj[..., :NSA_HEADS * DH].reshape(b, nb, Q_BLOCK, G, R, DH).transpose(0, 3, 1, 5, 4, 2)
    qT = qT.reshape(b, G, nb, DH, QL)
    gT = proj[..., NSA_HEADS * DH:].reshape(b, nb, Q_BLOCK, G, R, 3).transpose(0, 3, 1, 5, 4, 2)
    gT = gT.reshape(b, G, nb, 3, QL)
    b_cmp, b_near, b_win = _bias_tables(rel_bias)
    at = _pool_matrix()

    per_bg = lambda *blk: pl.BlockSpec((1, 1) + blk, lambda bi, gi, i: (bi, gi) + (0,) * len(blk))
    per_g = lambda *blk: pl.BlockSpec((1,) + blk, lambda bi, gi, i: (gi,) + (0,) * len(blk))
    oT = pl.pallas_call(
        _nsa_kernel,
        grid=(b, G, nb),
        in_specs=[
            pl.BlockSpec((1, 1, 1, DH, QL), lambda bi, gi, i: (bi, gi, i, 0, 0)),
            pl.BlockSpec((1, 1, 1, 3, QL), lambda bi, gi, i: (bi, gi, i, 0, 0)),
            per_bg(CMP_ROWS, DH),
            per_bg(DH, CMP_ROWS),
            per_bg(N_KT, Q_BLOCK, 2 * DH),
            per_bg(N_KT, DH, Q_BLOCK),
            per_bg(N_KT, DH, Q_BLOCK),
            pl.BlockSpec((N_SEL, CMP_ROWS), lambda bi, gi, i: (0, 0)),
            per_g(CMP_WIN, QL),
            per_g(NEAR_TILES + 2, Q_BLOCK, QL),
            per_g(WIN_TILES + 1, Q_BLOCK, QL),
        ],
        out_specs=pl.BlockSpec((1, 1, 1, DH, QL), lambda bi, gi, i: (bi, gi, i, 0, 0)),
        out_shape=jax.ShapeDtypeStruct((b, G, nb, DH, QL), jnp.float32),
        scratch_shapes=[pltpu.VMEM((CMP_ROWS, QL), jnp.float32), pltpu.VMEM((N_SEL, Q_BLOCK), jnp.float32)],
        compiler_params=pltpu.CompilerParams(
            dimension_semantics=("arbitrary", "arbitrary", "arbitrary"),
            vmem_limit_bytes=V7X_VMEM_LIMIT_BYTES),
        name="nsa_attention",
    )(qT, gT, kc, vcT, kk, vsT, vwT, at, b_cmp, b_near, b_win)
    o = oT.reshape(b, G, nb, DH, R, Q_BLOCK).transpose(0, 2, 5, 1, 4, 3).reshape(b, s, NSA_HEADS * DH)
    return _mm(o, None, w_out, False)


PEER_SLOTS = PEER_HEADS * PEER_TOPK
PEER_ROUTE_TM = 256
PEER_TT = 8
PEER_CHUNKS = D_MODEL // V7X_LANES
PEER_SLAB = 2 * PEER_CHUNKS
PEER_SLAB_STRIDE = 24


def _top_rows(s, rowid, n, payload=None):
    vals, picks = [], []
    for _ in range(n):
        mx = jnp.max(s, axis=0, keepdims=True)
        first = jnp.min(jnp.where(s == mx, rowid, float(s.shape[0])), axis=0, keepdims=True)
        hit = rowid == first
        vals.append(mx)
        picks.append(first if payload is None else jnp.max(jnp.where(hit, payload, -1.0), axis=0, keepdims=True))
        s = jnp.where(hit, -jnp.inf, s)
    return vals, picks


def _peer_route_kernel(h_ref, g_ref, wqT_ref, k1_ref, k2_ref, xn_ref, ids_ref, gate_ref):
    f32, bf16 = jnp.float32, jnp.bfloat16
    x = h_ref[...]
    xn = x * lax.rsqrt(jnp.mean(x * x, axis=-1, keepdims=True) + NORM_EPS) * g_ref[...]
    xn_ref[...] = xn
    qT = lax.dot_general(wqT_ref[...], xn.astype(bf16), (((1,), (1,)), ((), ())), preferred_element_type=f32)
    half = PEER_QDIM // 2
    tm = x.shape[0]
    key_id = lax.broadcasted_iota(jnp.int32, (PEER_N_KEYS, tm), 0).astype(f32)
    cand_pos = lax.broadcasted_iota(jnp.int32, (PEER_TOPK * PEER_TOPK, tm), 0).astype(f32)
    for hd in range(PEER_HEADS):
        q1 = qT[hd * PEER_QDIM: hd * PEER_QDIM + half].astype(bf16)
        q2 = qT[hd * PEER_QDIM + half: (hd + 1) * PEER_QDIM].astype(bf16)
        s1 = jnp.dot(k1_ref[hd], q1, preferred_element_type=f32)
        s2 = jnp.dot(k2_ref[hd], q2, preferred_element_type=f32)
        v1, i1 = _top_rows(s1, key_id, PEER_TOPK)
        v2, i2 = _top_rows(s2, key_id, PEER_TOPK)
        v2m = jnp.concatenate(v2, axis=0)
        i2m = jnp.concatenate(i2, axis=0)
        cand = jnp.concatenate([v1[a] + v2m for a in range(PEER_TOPK)], axis=0)
        cand_id = jnp.concatenate([i1[a] * float(PEER_N_KEYS) + i2m for a in range(PEER_TOPK)], axis=0)
        top_s, ids = _top_rows(cand, cand_pos, PEER_TOPK, payload=cand_id)
        top_s = jnp.concatenate(top_s, axis=0)
        e = jnp.exp(top_s - top_s[0:1])
        gate_ref[hd * PEER_TOPK:(hd + 1) * PEER_TOPK, :] = e / jnp.sum(e, axis=0, keepdims=True)
        ids_ref[hd * PEER_TOPK:(hd + 1) * PEER_TOPK, :] = jnp.concatenate(ids, axis=0).astype(jnp.int32)


def peer_route(h2, norm_g, w_q, keys1, keys2):
    T, d = h2.shape
    tm = PEER_ROUTE_TM
    bf16 = jnp.bfloat16
    nq = PEER_HEADS * PEER_QDIM
    half = PEER_QDIM // 2
    return pl.pallas_call(
        _peer_route_kernel,
        grid=(T // tm,),
        in_specs=[pl.BlockSpec((tm, d), lambda i: (i, 0)),
                  pl.BlockSpec((1, d), lambda i: (0, 0)),
                  pl.BlockSpec((nq, d), lambda i: (0, 0)),
                  pl.BlockSpec((PEER_HEADS, PEER_N_KEYS, half), lambda i: (0, 0, 0)),
                  pl.BlockSpec((PEER_HEADS, PEER_N_KEYS, half), lambda i: (0, 0, 0))],
        out_specs=[pl.BlockSpec((tm, d), lambda i: (i, 0)),
                   pl.BlockSpec((PEER_SLOTS, tm), lambda i: (0, i)),
                   pl.BlockSpec((PEER_SLOTS, tm), lambda i: (0, i))],
        out_shape=[jax.ShapeDtypeStruct((T, d), jnp.float32),
                   jax.ShapeDtypeStruct((PEER_SLOTS, T), jnp.int32),
                   jax.ShapeDtypeStruct((PEER_SLOTS, T), jnp.float32)],
        compiler_params=pltpu.CompilerParams(dimension_semantics=("arbitrary",),
                                             vmem_limit_bytes=V7X_VMEM_LIMIT_BYTES),
        name="peer_route",
    )(h2, norm_g.reshape(1, d), w_q.T.astype(bf16), keys1.astype(bf16), keys2.astype(bf16))


def _peer_slab(buf, slot, r):
    return buf.at[slot, pl.ds(pl.multiple_of(r * PEER_SLAB_STRIDE, 8), PEER_SLAB)]


def _peer_expert_kernel(ids0_ref, idsn_ref, xn_ref, gt_ref, uv_hbm, o_ref, buf, sem):
    i = pl.program_id(0)
    n = pl.num_programs(0)
    slot = i % 2
    groups = PEER_TT * PEER_SLOTS // 8

    def issue(ids_ref, dst_slot):
        def group(gidx, carry):
            for u in range(8):
                r = gidx * 8 + u
                pltpu.make_async_copy(uv_hbm.at[ids_ref[r]], _peer_slab(buf, dst_slot, r),
                                      sem.at[dst_slot]).start(priority=u % 2)
            return carry
        lax.fori_loop(0, groups, group, 0)

    @pl.when(i == 0)
    def _():
        issue(ids0_ref, 0)

    @pl.when(i + 1 < n)
    def _():
        issue(idsn_ref, 1 - slot)

    def wait_group(gidx, carry):
        for u in range(8):
            pltpu.make_async_copy(uv_hbm.at[0], _peer_slab(buf, slot, gidx * 8 + u), sem.at[slot]).wait()
        return carry
    lax.fori_loop(0, groups, wait_group, 0)

    for t in range(PEER_TT):
        base = t * PEER_SLOTS * PEER_SLAB_STRIDE
        xrow = xn_ref[t:t + 1, :]
        acc = jnp.zeros((PEER_SLOTS, V7X_LANES), jnp.float32)
        for j in range(PEER_CHUNKS):
            tile = buf[slot, pl.ds(base + j, PEER_SLOTS, stride=PEER_SLAB_STRIDE), :]
            acc = acc + tile * xrow[:, j * V7X_LANES:(j + 1) * V7X_LANES]
        act = jnp.sum(acc, axis=1, keepdims=True)
        w = gt_ref[0, :, t:t + 1] * jax.nn.gelu(act)
        for j in range(PEER_CHUNKS):
            tile = buf[slot, pl.ds(base + PEER_CHUNKS + j, PEER_SLOTS, stride=PEER_SLAB_STRIDE), :]
            o_ref[t:t + 1, j * V7X_LANES:(j + 1) * V7X_LANES] = jnp.sum(tile * w, axis=0, keepdims=True)


def peer_experts(xn, idsT, gateT, u_tab, v_tab):
    T, d = xn.shape
    tt = PEER_TT
    nt = T // tt
    uv = jnp.concatenate([u_tab.reshape(PEER_EXPERTS, PEER_CHUNKS, V7X_LANES),
                          v_tab.reshape(PEER_EXPERTS, PEER_CHUNKS, V7X_LANES)], axis=1)
    ids = idsT.T.reshape(T * PEER_SLOTS)
    gt = gateT.reshape(PEER_SLOTS, nt, tt).transpose(1, 0, 2)
    return pl.pallas_call(
        _peer_expert_kernel,
        grid=(nt,),
        in_specs=[pl.BlockSpec((tt * PEER_SLOTS,), lambda i: (0,), memory_space=pltpu.SMEM),
                  pl.BlockSpec((tt * PEER_SLOTS,), lambda i: (jnp.minimum(i + 1, nt - 1),), memory_space=pltpu.SMEM),
                  pl.BlockSpec((tt, d), lambda i: (i, 0)),
                  pl.BlockSpec((1, PEER_SLOTS, tt), lambda i: (i, 0, 0)),
                  pl.BlockSpec(memory_space=pl.ANY)],
        out_specs=pl.BlockSpec((tt, d), lambda i: (i, 0)),
        out_shape=jax.ShapeDtypeStruct((T, d), jnp.float32),
        scratch_shapes=[pltpu.VMEM((2, tt * PEER_SLOTS * PEER_SLAB_STRIDE, V7X_LANES), jnp.float32),
                        pltpu.SemaphoreType.DMA((2,))],
        compiler_params=pltpu.CompilerParams(dimension_semantics=("arbitrary",),
                                             vmem_limit_bytes=V7X_VMEM_LIMIT_BYTES),
        name="peer_experts",
    )(ids, ids, xn, gt, uv)


def peer_ffn(h, norm_g, w_q, keys1, keys2, u_tab, v_tab):
    b, s, d = h.shape
    xn, idsT, gateT = peer_route(h.reshape(b * s, d), norm_g, w_q, keys1, keys2)
    return peer_experts(xn, idsT, gateT, u_tab, v_tab).reshape(b, s, d)


def per_layer_embed(h, p_i, norm_g, w_up, w_gate):
    return _mm(p_i, None, w_up, False) * jax.nn.sigmoid(_mm(h, norm_g, w_gate, True))


def kernel(x, p, a_norm_g, a_w_in, a_conv_w, a_conv_b, a_dt_bias, a_log, a_d_skip, a_gnorm_g, a_w_out,
           kv_norm_g, w_kv, cmp_pos_k, cmp_pos_v, cmp_w1_k, cmp_w2_k, cmp_w1_v, cmp_w2_v, rel_bias,
           b_norm_g, b_w_in, b_w_out, c_norm_g, c_w_q, c_keys1, c_keys2, c_u, c_v,
           e_norm_g, e_w_up, e_w_gate, final_g):
    h = x
    shared = None
    for i in range(DEPTH):
        if i < N_A_LAYERS:
            h = ssd_mixer_residual(h, a_norm_g[i], a_w_in[i], a_conv_w[i], a_conv_b[i], a_dt_bias[i],
                                   a_log[i], a_d_skip[i], a_gnorm_g[i], a_w_out[i])
        else:
            if shared is None:
                shared = nsa_shared_kv(h, kv_norm_g, w_kv, cmp_pos_k, cmp_pos_v,
                                       cmp_w1_k, cmp_w2_k, cmp_w1_v, cmp_w2_v)
            j = i - N_A_LAYERS
            h = h + nsa_mixer(h, b_norm_g[j], b_w_in[j], b_w_out[j], rel_bias, *shared)
        h = h + peer_ffn(h, c_norm_g[i], c_w_q[i], c_keys1[i], c_keys2[i], c_u[i], c_v[i])
        h = h + per_layer_embed(h, p[i], e_norm_g[i], e_w_up[i], e_w_gate[i])
    return rms_norm(h, final_g)
```

```python
import functools
import math

import jax
import jax.numpy as jnp
import numpy as np
from jax import lax
from jax.experimental import pallas as pl
from jax.experimental.pallas import tpu as pltpu

D_MODEL = 1024
BATCH = 2
SEQ = 16384
DEPTH = 2
N_A_LAYERS = DEPTH // 2
N_B_LAYERS = DEPTH - N_A_LAYERS

SSD_D_INNER = 2 * D_MODEL
SSD_HEADDIM = 64
SSD_HEADS = SSD_D_INNER // SSD_HEADDIM
SSD_GROUPS = 8
SSD_HEADS_PER_GROUP = SSD_HEADS // SSD_GROUPS
SSD_STATE = 128
SSD_CONV = 4
SSD_CHUNK = 128
SSD_CONV_DIM = SSD_D_INNER + 2 * SSD_GROUPS * SSD_STATE
SSD_IN_DIM = SSD_D_INNER + SSD_CONV_DIM + SSD_HEADS

NSA_HEADS = 16
NSA_KV_HEADS = 4
NSA_GROUP = NSA_HEADS // NSA_KV_HEADS
NSA_HEAD_DIM = 64
CMP_BLOCK = 32
CMP_STRIDE = 16
CMP_HIDDEN = 256
SLC_BLOCK = 64
SLC_TOPN = 16
SLC_FORCE = 1e4
WINDOW = 512
Q_BLOCK = 128
NSA_IN_DIM = NSA_HEADS * NSA_HEAD_DIM + 3 * NSA_HEADS
KV_DIM = 6 * NSA_KV_HEADS * NSA_HEAD_DIM

REL_BUCKETS = 32
REL_MAX_DISTANCE = 4096

PEER_HEADS = 8
PEER_N_KEYS = 128
PEER_EXPERTS = PEER_N_KEYS * PEER_N_KEYS
PEER_QDIM = 256
PEER_TOPK = 16
PEER_TOKEN_CHUNK = 128
PEER_V_SCALE = PEER_HEADS ** -0.5

PLE_DIM = 256
NORM_EPS = 1e-6
NEG_INF = -1e30

V7X_LANES = 128
V7X_VMEM_LIMIT_BYTES = 56 * 1024 * 1024


def _norm_matmul_kernel(x_ref, g_ref, w_ref, o_ref, *, normalize):
    x = x_ref[...]
    if normalize:
        x = x * lax.rsqrt(jnp.mean(x * x, axis=-1, keepdims=True) + NORM_EPS) * g_ref[...]
    o_ref[...] = jnp.dot(x.astype(jnp.bfloat16), w_ref[...], preferred_element_type=jnp.float32)


def norm_matmul(x, g, w, *, normalize, tm=512, tn=None):
    m, k = x.shape
    n = w.shape[1]
    n_pad = -(-n // V7X_LANES) * V7X_LANES
    wb = w.astype(jnp.bfloat16)
    if n_pad != n:
        wb = jnp.pad(wb, ((0, 0), (0, n_pad - n)))
    if tn is None:
        tn = n_pad
    assert m % tm == 0 and n_pad % tn == 0
    out = pl.pallas_call(
        functools.partial(_norm_matmul_kernel, normalize=normalize),
        grid=(m // tm, n_pad // tn),
        in_specs=[
            pl.BlockSpec((tm, k), lambda i, j: (i, 0)),
            pl.BlockSpec((1, k), lambda i, j: (0, 0)),
            pl.BlockSpec((k, tn), lambda i, j: (0, j)),
        ],
        out_specs=pl.BlockSpec((tm, tn), lambda i, j: (i, j)),
        out_shape=jax.ShapeDtypeStruct((m, n_pad), jnp.float32),
        compiler_params=pltpu.CompilerParams(
            dimension_semantics=("arbitrary", "arbitrary"),
            vmem_limit_bytes=V7X_VMEM_LIMIT_BYTES),
        name="norm_matmul" if normalize else "matmul",
    )(x, g.reshape(1, k), wb)
    return out[:, :n] if n_pad != n else out


def _mm(x3, g, w, normalize, **kw):
    b, s, k = x3.shape
    if g is None:
        g = jnp.ones((k,), jnp.float32)
    return norm_matmul(x3.reshape(b * s, k), g, w, normalize=normalize, **kw).reshape(b, s, w.shape[1])


def rms_norm(x, g):
    xf = x.astype(jnp.float32)
    y = xf * lax.rsqrt(jnp.mean(xf * xf, axis=-1, keepdims=True) + NORM_EPS)
    return (y * g.astype(jnp.float32)).astype(x.dtype)


def rel_bucket(dist):
    dist = jnp.maximum(dist, 0)
    max_exact = REL_BUCKETS // 2
    d = jnp.maximum(dist, 1).astype(jnp.float32)
    large = max_exact + (jnp.log(d / max_exact) / math.log(REL_MAX_DISTANCE / max_exact)
                         * (REL_BUCKETS - max_exact)).astype(jnp.int32)
    large = jnp.minimum(large, REL_BUCKETS - 1)
    return jnp.where(dist < max_exact, dist, large)


SSD_GROUP_W = SSD_HEADS_PER_GROUP * SSD_HEADDIM
SSD_IN_PAD = -(-SSD_IN_DIM // V7X_LANES) * V7X_LANES
SSD_TAIL = 8


def _split3(x):
    bf16, f32 = jnp.bfloat16, jnp.float32
    hi = x.astype(bf16)
    r1 = x - hi.astype(f32)
    mid = r1.astype(bf16)
    lo = (r1 - mid.astype(f32)).astype(bf16)
    return hi, mid, lo


def _ssd_kernel(zx_ref, h_ref, cw_ref, cb_ref, dtb_ref, alog_ref, dskip_ref, gn_ref, wout_ref, tri_ref, exp_ref,
                o_ref, xs_scr, state_scr):
    f32, bf16 = jnp.float32, jnp.bfloat16
    L, N, GW = SSD_CHUNK, SSD_STATE, SSD_GROUP_W
    c = pl.program_id(1)

    @pl.when(c == 0)
    def _():
        xs_scr[0:SSD_TAIL, :] = jnp.zeros((SSD_TAIL, SSD_CONV_DIM), f32)
        state_scr[...] = jnp.zeros_like(state_scr)

    xs_scr[SSD_TAIL:SSD_TAIL + L, :] = zx_ref[:, SSD_D_INNER:SSD_D_INNER + SSD_CONV_DIM]
    conv = cb_ref[...]
    for w in range(SSD_CONV):
        conv = conv + cw_ref[w:w + 1, :] * xs_scr[pl.ds(SSD_TAIL - (SSD_CONV - 1) + w, L), :]
    xs_scr[0:SSD_TAIL, :] = xs_scr[L:L + SSD_TAIL, :]
    xbc = conv * jax.nn.sigmoid(conv)

    dt_in = zx_ref[:, SSD_D_INNER + SSD_CONV_DIM:SSD_IN_PAD] + dtb_ref[...]
    dt = jnp.maximum(dt_in, 0.0) + jnp.log1p(jnp.exp(-jnp.abs(dt_in)))
    a_dt = dt * -jnp.exp(alog_ref[...])
    a_cs = sum(jnp.dot(tri_ref[...], t, preferred_element_type=f32) for t in _split3(a_dt))
    a_csT = a_cs.T
    a_end = a_cs[L - 1:L, :]
    grow = jnp.exp(a_cs)
    to_end = jnp.exp(a_end - a_cs)

    def per_channel(q):
        return sum(jnp.dot(t, exp_ref[...], preferred_element_type=f32) for t in _split3(q)[:2])

    dt_c, grow_c, to_end_c = per_channel(dt), per_channel(grow), per_channel(to_end)
    end_c = per_channel(jnp.broadcast_to(jnp.exp(a_end), (8, V7X_LANES)))[0:1]
    dskip_c = per_channel(jnp.broadcast_to(dskip_ref[...], (8, V7X_LANES)))[0:1]

    x = xbc[:, :SSD_D_INNER]
    xd = x * dt_c
    xd_bf = xd.astype(bf16)
    xe_bf = (xd * to_end_c).astype(bf16)
    row_i = lax.broadcasted_iota(jnp.int32, (L, L), 0)
    col_i = lax.broadcasted_iota(jnp.int32, (L, L), 1)
    lane_head = lax.broadcasted_iota(jnp.int32, (L, GW), 1) // SSD_HEADDIM
    ys = []
    for g in range(SSD_GROUPS):
        bm = xbc[:, SSD_D_INNER + g * N:SSD_D_INNER + (g + 1) * N]
        cm = xbc[:, SSD_D_INNER + SSD_GROUPS * N + g * N:SSD_D_INNER + SSD_GROUPS * N + (g + 1) * N].astype(bf16)
        bmT = bm.T.astype(bf16)
        cb = jnp.dot(cm, bmT, preferred_element_type=f32)
        xd_g = xd_bf[:, g * GW:(g + 1) * GW]
        y_g = jnp.zeros((L, GW), f32)
        for r in range(SSD_HEADS_PER_GROUP):
            hd = g * SSD_HEADS_PER_GROUP + r
            seg = a_cs[:, hd:hd + 1] - a_csT[hd:hd + 1, :]
            decay = jnp.where(row_i >= col_i, jnp.exp(seg), 0.0)
            xr = jnp.where(lane_head == r, xd_g, jnp.zeros_like(xd_g))
            y_g = y_g + jnp.dot((cb * decay).astype(bf16), xr, preferred_element_type=f32)
        hT = state_scr[g]
        y_off = jnp.dot(cm, hT.astype(bf16), preferred_element_type=f32) * grow_c[:, g * GW:(g + 1) * GW]
        new_states = jnp.dot(bmT, xe_bf[:, g * GW:(g + 1) * GW], preferred_element_type=f32)
        state_scr[g] = hT * end_c[:, g * GW:(g + 1) * GW] + new_states
        ys.append(y_g + y_off)
    y = jnp.concatenate(ys, axis=1) + x * dskip_c

    z = zx_ref[:, :SSD_D_INNER]
    yz = y * (z * jax.nn.sigmoid(z))
    parts = []
    for g in range(SSD_GROUPS):
        t = yz[:, g * GW:(g + 1) * GW]
        parts.append(t * lax.rsqrt(jnp.mean(t * t, axis=-1, keepdims=True) + NORM_EPS))
    yn = jnp.concatenate(parts, axis=1) * gn_ref[...]
    o_ref[...] = h_ref[...] + jnp.dot(yn.astype(bf16), wout_ref[...], preferred_element_type=f32)


def ssd_mixer_residual(h, norm_g, w_in, conv_w, conv_b, dt_bias, a_log, d_skip, gnorm_g, w_out):
    b, s, d = h.shape
    L = SSD_CHUNK
    nc = s // L
    w_in_p = jnp.pad(w_in, ((0, 0), (0, SSD_IN_PAD - SSD_IN_DIM)))
    zx = norm_matmul(h.reshape(b * s, d), norm_g, w_in_p, normalize=True, tm=256)
    lane_pad = lambda v: jnp.pad(v.reshape(1, -1), ((0, 0), (0, V7X_LANES - v.shape[-1])))
    tri = jnp.asarray(np.tril(np.ones((L, L))), jnp.bfloat16)
    head_of = np.arange(SSD_D_INNER) // SSD_HEADDIM
    expand = jnp.asarray(np.arange(V7X_LANES)[:, None] == head_of[None, :], jnp.bfloat16)
    const = lambda *shape: pl.BlockSpec(shape, lambda bi, ci: (0,) * len(shape))
    out = pl.pallas_call(
        _ssd_kernel,
        grid=(b, nc),
        in_specs=[pl.BlockSpec((L, SSD_IN_PAD), lambda bi, ci: (bi * nc + ci, 0)),
                  pl.BlockSpec((L, d), lambda bi, ci: (bi * nc + ci, 0)),
                  const(SSD_CONV, SSD_CONV_DIM), const(1, SSD_CONV_DIM),
                  const(1, V7X_LANES), const(1, V7X_LANES), const(1, V7X_LANES),
                  const(1, SSD_D_INNER), const(SSD_D_INNER, d), const(L, L), const(V7X_LANES, SSD_D_INNER)],
        out_specs=pl.BlockSpec((L, d), lambda bi, ci: (bi * nc + ci, 0)),
        out_shape=jax.ShapeDtypeStruct((b * s, d), jnp.float32),
        scratch_shapes=[pltpu.VMEM((L + SSD_TAIL, SSD_CONV_DIM), jnp.float32),
                        pltpu.VMEM((SSD_GROUPS, SSD_STATE, SSD_GROUP_W), jnp.float32)],
        compiler_params=pltpu.CompilerParams(dimension_semantics=("arbitrary", "arbitrary"),
                                             vmem_limit_bytes=V7X_VMEM_LIMIT_BYTES),
        name="ssd_mixer",
    )(zx, h.reshape(b * s, d), conv_w, conv_b.reshape(1, -1), lane_pad(dt_bias), lane_pad(a_log), lane_pad(d_skip),
      gnorm_g.reshape(1, -1), w_out.astype(jnp.bfloat16), tri, expand)
    return out.reshape(b, s, d)


def _mlp_kernel(x_ref, w1_ref, w2_ref, o_ref):
    hid = jnp.dot(x_ref[...].astype(jnp.bfloat16), w1_ref[...], preferred_element_type=jnp.float32)
    hid = jax.nn.gelu(hid)
    o_ref[...] = jnp.dot(hid.astype(jnp.bfloat16), w2_ref[...], preferred_element_type=jnp.float32)


def mlp2(x, w1, w2, *, tm=512):
    m, k = x.shape
    hdim, n = w2.shape
    assert m % tm == 0
    return pl.pallas_call(
        _mlp_kernel,
        grid=(m // tm,),
        in_specs=[pl.BlockSpec((tm, k), lambda i: (i, 0)),
                  pl.BlockSpec((k, hdim), lambda i: (0, 0)),
                  pl.BlockSpec((hdim, n), lambda i: (0, 0))],
        out_specs=pl.BlockSpec((tm, n), lambda i: (i, 0)),
        out_shape=jax.ShapeDtypeStruct((m, n), jnp.float32),
        compiler_params=pltpu.CompilerParams(dimension_semantics=("arbitrary",),
                                             vmem_limit_bytes=V7X_VMEM_LIMIT_BYTES),
        name="cmp_mlp",
    )(x, w1.astype(jnp.bfloat16), w2.astype(jnp.bfloat16))


def compress_blocks(t, pos, w1, w2):
    b, s, g, d = t.shape
    halves = t.transpose(0, 2, 1, 3).reshape(b, g, s // CMP_STRIDE, CMP_STRIDE, d)
    nxt = jnp.concatenate([halves[:, :, 1:], jnp.zeros_like(halves[:, :, :1])], axis=2)
    blocks = jnp.concatenate([halves, nxt], axis=3) + pos
    flat = blocks.reshape(b * g * (s // CMP_STRIDE), CMP_BLOCK * d)
    return mlp2(flat, w1, w2).reshape(b, g, s // CMP_STRIDE, d)


N_SEL = SEQ // SLC_BLOCK
N_KT = SEQ // Q_BLOCK
CMP_FRONT = 256
CMP_ROWS = CMP_FRONT + SEQ // CMP_STRIDE
CMP_WIN = 256
REL_TABLE = 4096
NEAR_TILES = 24
WIN_TILES = WINDOW // Q_BLOCK + 1
SEL_GROUP = 4
QL = NSA_GROUP * Q_BLOCK


def _nsa_kernel(qT_ref, gT_ref, kc_ref, vcT_ref, kk_ref, vsT_ref, vwT_ref, at_ref, bcT_ref, bT_ref, bTw_ref,
                o_ref, s_scr, sel_scr):
    f32, bf16 = jnp.float32, jnp.bfloat16
    i = pl.program_id(2)
    qT = (qT_ref[0, 0, 0] * (NSA_HEAD_DIM ** -0.5)).astype(bf16)
    zq = jnp.zeros_like(qT)
    q_sel = jnp.concatenate([qT, zq], axis=0)
    q_win = jnp.concatenate([zq, qT], axis=0)

    r0 = pl.multiple_of(8 * i + 8, 8)
    s = jnp.dot(kc_ref[0, 0], qT, preferred_element_type=f32)
    rio = lax.broadcasted_iota(jnp.int32, (CMP_ROWS, Q_BLOCK), 0)
    rowmask = jnp.where((rio >= CMP_FRONT) & (rio < r0 + CMP_WIN), 0.0, NEG_INF)
    s_scr[...] = s + jnp.concatenate([rowmask] * NSA_GROUP, axis=1)
    s_scr[pl.ds(r0, CMP_WIN), :] = s_scr[pl.ds(r0, CMP_WIN), :] + bcT_ref[0]
    s = s_scr[...]
    m = jnp.max(s, axis=0, keepdims=True)
    p = jnp.exp(s - m)
    l = jnp.sum(p, axis=0, keepdims=True)
    inv = jnp.where(m > 0.1 * NEG_INF, 1.0 / l, 0.0)
    pn = p * inv
    o_c = jnp.dot(vcT_ref[0, 0], pn.astype(bf16), preferred_element_type=f32)

    psum = pn[:, 0:Q_BLOCK]
    for r in range(1, NSA_GROUP):
        psum = psum + pn[:, r * Q_BLOCK:(r + 1) * Q_BLOCK]
    hi = psum.astype(bf16)
    lo = (psum - hi.astype(f32)).astype(bf16)
    imp = (jnp.dot(at_ref[...], hi, preferred_element_type=f32)
           + jnp.dot(at_ref[...], lo, preferred_element_type=f32))
    blk = lax.broadcasted_iota(jnp.int32, (N_SEL, Q_BLOCK), 0)
    qi = lax.broadcasted_iota(jnp.int32, (N_SEL, Q_BLOCK), 1)
    cur = 2 * i + (qi >= SLC_BLOCK).astype(jnp.int32)
    forced = (blk == 0) | (blk == cur) | (blk == cur - 1)
    valid = blk <= cur
    score = jnp.where(forced, SLC_FORCE, jnp.where(valid, imp, -SLC_FORCE))
    blkf = blk.astype(f32)
    chosen = jnp.zeros((N_SEL, Q_BLOCK), f32)
    for _ in range(SLC_TOPN):
        mx = jnp.max(score, axis=0, keepdims=True)
        first = jnp.min(jnp.where(score == mx, blkf, float(N_SEL)), axis=0, keepdims=True)
        hit = blkf == first
        chosen = jnp.where(hit, 1.0, chosen)
        score = jnp.where(hit, -jnp.inf, score)
    sel_scr[...] = jnp.where((chosen > 0.5) & valid, 0.0, NEG_INF)

    def attend(kt, carry, q_aug, vT_ref, bias):
        m_run, l_run, acc = carry
        scs = [jnp.dot(kk_ref[0, 0, k], q_aug, preferred_element_type=f32) + b for k, b in zip(kt, bias)]
        m_new = m_run
        for sc in scs:
            m_new = jnp.maximum(m_new, jnp.max(sc, axis=0, keepdims=True))
        alpha = jnp.exp(m_run - m_new)
        l_new = alpha * l_run
        acc_new = alpha * acc
        for k, sc in zip(kt, scs):
            pt = jnp.exp(sc - m_new)
            l_new = l_new + jnp.sum(pt, axis=0, keepdims=True)
            acc_new = acc_new + jnp.dot(vT_ref[0, 0, k], pt.astype(bf16), preferred_element_type=f32)
        return m_new, l_new, acc_new

    def sel_mask(kt):
        m0 = jnp.broadcast_to(sel_scr[pl.ds(2 * kt, 1), :], (SLC_BLOCK, Q_BLOCK))
        m1 = jnp.broadcast_to(sel_scr[pl.ds(2 * kt + 1, 1), :], (SLC_BLOCK, Q_BLOCK))
        mk = jnp.concatenate([m0, m1], axis=0)
        return jnp.concatenate([mk] * NSA_GROUP, axis=1)

    init = (jnp.full((1, QL), NEG_INF, f32), jnp.zeros((1, QL), f32), jnp.zeros((NSA_HEAD_DIM, QL), f32))
    U = SEL_GROUP

    def far_group(g, c):
        kts = [g * U + u for u in range(U)]
        return attend(kts, c, q_sel, vsT_ref, [sel_mask(k) for k in kts])

    def near_group(g, c):
        kts, biases = [], []
        for u in range(U):
            delta = i - (g * U + u)
            k = jnp.minimum(g * U + u, i)
            tile = jnp.where(delta < 0, NEAR_TILES + 1, jnp.minimum(delta, NEAR_TILES))
            kts.append(k)
            biases.append(sel_mask(k) + bT_ref[0, tile])
        return attend(kts, c, q_sel, vsT_ref, biases)

    n_far = jnp.maximum((i - (NEAR_TILES - 1)) // U, 0)
    carry = lax.fori_loop(0, n_far, far_group, init)
    _, l_s, acc_s = lax.fori_loop(n_far, i // U + 1, near_group, carry)
    o_s = acc_s / l_s

    kts, biases = [], []
    for u in range(WIN_TILES):
        k = i - (WIN_TILES - 1) + u
        kts.append(jnp.maximum(k, 0))
        biases.append(bTw_ref[0, jnp.where(k >= 0, WIN_TILES - 1 - u, WIN_TILES)])
    _, l_w, acc_w = attend(kts, init, q_win, vwT_ref, biases)
    o_w = acc_w / l_w

    gate = jax.nn.sigmoid(gT_ref[0, 0, 0])
    o_ref[0, 0, 0] = gate[0:1] * o_c + gate[1:2] * o_s + gate[2:3] * o_w


def _bias_tables(rel_bias):
    G, R = NSA_KV_HEADS, NSA_GROUP
    bd = rel_bias[rel_bucket(jnp.arange(REL_TABLE))]
    bd = (bd - bd[REL_TABLE - 1]).T

    def skewed(rows, step, pad):
        v = jnp.concatenate([bd, jnp.full((NSA_HEADS, pad), NEG_INF, jnp.float32)], axis=1)
        width = REL_TABLE + pad
        flat = jnp.broadcast_to(v[:, None, :], (NSA_HEADS, rows, width)).reshape(NSA_HEADS, rows * width)
        return flat[:, :rows * (width - step)].reshape(NSA_HEADS, rows, width - step)

    near = skewed(Q_BLOCK, 1, Q_BLOCK)[:, :, :NEAR_TILES * Q_BLOCK]
    b_near = near.reshape(NSA_HEADS, Q_BLOCK, NEAR_TILES, Q_BLOCK).transpose(0, 2, 1, 3)
    d_win = (Q_BLOCK * np.arange(WIN_TILES)[:, None, None] + np.arange(Q_BLOCK)[None, None, :]
             - np.arange(Q_BLOCK)[None, :, None])
    b_win = jnp.where(jnp.asarray(d_win < WINDOW), b_near[:, :WIN_TILES], NEG_INF)
    one = jnp.zeros((NSA_HEADS, 1, Q_BLOCK, Q_BLOCK), jnp.float32)
    b_near = jnp.concatenate([b_near, one, one + NEG_INF], axis=1)
    b_win = jnp.concatenate([b_win, one + NEG_INF], axis=1)
    off = CMP_STRIDE * (CMP_WIN - 8) - (CMP_BLOCK - 1)
    b_cmp = skewed(CMP_WIN, CMP_STRIDE, CMP_WIN)[:, :, off:off + Q_BLOCK]

    def lanes(x):
        x = x.reshape((G, R) + x.shape[1:])
        x = jnp.moveaxis(x, 1, -2)
        return x.reshape(x.shape[:-2] + (QL,))

    return lanes(b_cmp), lanes(b_near), lanes(b_win)


def _pool_matrix():
    per = SLC_BLOCK // CMP_STRIDE
    n_span = CMP_BLOCK // CMP_STRIDE
    k = np.arange(CMP_ROWS)[None, :] - CMP_FRONT
    j = np.arange(N_SEL)[:, None]
    return jnp.asarray((k >= per * j - (n_span - 1)) & (k <= per * j + per - 1), jnp.bfloat16)


def nsa_shared_kv(h, kv_norm_g, w_kv, cmp_pos_k, cmp_pos_v, cmp_w1_k, cmp_w2_k, cmp_w1_v, cmp_w2_v):
    b, s, _ = h.shape
    G, DH = NSA_KV_HEADS, NSA_HEAD_DIM
    bf16 = jnp.bfloat16
    kv = _mm(h, kv_norm_g, w_kv, True).reshape(b, s, 6, G, DH)
    k_cmp = compress_blocks(kv[:, :, 0], cmp_pos_k, cmp_w1_k, cmp_w2_k)
    v_cmp = compress_blocks(kv[:, :, 1], cmp_pos_v, cmp_w1_v, cmp_w2_v)
    front = jnp.zeros((b, G, CMP_FRONT, DH), bf16)
    kc = jnp.concatenate([front, k_cmp.astype(bf16)], axis=2)
    vcT = jnp.concatenate([front, v_cmp.astype(bf16)], axis=2).transpose(0, 1, 3, 2)
    kk = jnp.concatenate([kv[:, :, 2], kv[:, :, 4]], axis=-1).astype(bf16)
    kk = kk.transpose(0, 2, 1, 3).reshape(b, G, N_KT, Q_BLOCK, 2 * DH)

    def vt(v):
        return v.astype(bf16).transpose(0, 2, 1, 3).reshape(b, G, N_KT, Q_BLOCK, DH).transpose(0, 1, 2, 4, 3)

    return kc, vcT, kk, vt(kv[:, :, 3]), vt(kv[:, :, 5])


def nsa_mixer(h, norm_g, w_in, w_out, rel_bias, kc, vcT, kk, vsT, vwT):
    b, s, _ = h.shape
    G, R, DH = NSA_KV_HEADS, NSA_GROUP, NSA_HEAD_DIM
    nb = s // Q_BLOCK
    proj = _mm(h, norm_g, w_in, True)
    qT = proj[..., :NSA_HEADS * DH].reshape(b, nb, Q_BLOCK, G, R, DH).transpose(0, 3, 1, 5, 4, 2)
    qT = qT.reshape(b, G, nb, DH, QL)
    gT = proj[..., NSA_HEADS * DH:].reshape(b, nb, Q_BLOCK, G, R, 3).transpose(0, 3, 1, 5, 4, 2)
    gT = gT.reshape(b, G, nb, 3, QL)
    b_cmp, b_near, b_win = _bias_tables(rel_bias)
    at = _pool_matrix()

    per_bg = lambda *blk: pl.BlockSpec((1, 1) + blk, lambda bi, gi, i: (bi, gi) + (0,) * len(blk))
    per_g = lambda *blk: pl.BlockSpec((1,) + blk, lambda bi, gi, i: (gi,) + (0,) * len(blk))
    oT = pl.pallas_call(
        _nsa_kernel,
        grid=(b, G, nb),
        in_specs=[
            pl.BlockSpec((1, 1, 1, DH, QL), lambda bi, gi, i: (bi, gi, i, 0, 0)),
            pl.BlockSpec((1, 1, 1, 3, QL), lambda bi, gi, i: (bi, gi, i, 0, 0)),
            per_bg(CMP_ROWS, DH),
            per_bg(DH, CMP_ROWS),
            per_bg(N_KT, Q_BLOCK, 2 * DH),
            per_bg(N_KT, DH, Q_BLOCK),
            per_bg(N_KT, DH, Q_BLOCK),
            pl.BlockSpec((N_SEL, CMP_ROWS), lambda bi, gi, i: (0, 0)),
            per_g(CMP_WIN, QL),
            per_g(NEAR_TILES + 2, Q_BLOCK, QL),
            per_g(WIN_TILES + 1, Q_BLOCK, QL),
        ],
        out_specs=pl.BlockSpec((1, 1, 1, DH, QL), lambda bi, gi, i: (bi, gi, i, 0, 0)),
        out_shape=jax.ShapeDtypeStruct((b, G, nb, DH, QL), jnp.float32),
        scratch_shapes=[pltpu.VMEM((CMP_ROWS, QL), jnp.float32), pltpu.VMEM((N_SEL, Q_BLOCK), jnp.float32)],
        compiler_params=pltpu.CompilerParams(
            dimension_semantics=("arbitrary", "arbitrary", "arbitrary"),
            vmem_limit_bytes=V7X_VMEM_LIMIT_BYTES),
        name="nsa_attention",
    )(qT, gT, kc, vcT, kk, vsT, vwT, at, b_cmp, b_near, b_win)
    o = oT.reshape(b, G, nb, DH, R, Q_BLOCK).transpose(0, 2, 5, 1, 4, 3).reshape(b, s, NSA_HEADS * DH)
    return _mm(o, None, w_out, False)


PEER_SLOTS = PEER_HEADS * PEER_TOPK
PEER_ROUTE_TM = 256
PEER_TT = 16
PEER_CHUNKS = D_MODEL // V7X_LANES
PEER_WORDS = PEER_CHUNKS // 2
PEER_SLAB = 2 * PEER_WORDS


def _top_rows(s, rowid, n, payload=None):
    vals, picks = [], []
    for _ in range(n):
        mx = jnp.max(s, axis=0, keepdims=True)
        first = jnp.min(jnp.where(s == mx, rowid, float(s.shape[0])), axis=0, keepdims=True)
        hit = rowid == first
        vals.append(mx)
        picks.append(first if payload is None else jnp.max(jnp.where(hit, payload, -1.0), axis=0, keepdims=True))
        s = jnp.where(hit, -jnp.inf, s)
    return vals, picks


def _peer_route_kernel(h_ref, g_ref, wqT_ref, k1_ref, k2_ref, xn_ref, ids_ref, gate_ref):
    f32, bf16 = jnp.float32, jnp.bfloat16
    x = h_ref[...]
    xn = x * lax.rsqrt(jnp.mean(x * x, axis=-1, keepdims=True) + NORM_EPS) * g_ref[...]
    xn_ref[...] = xn
    qT = lax.dot_general(wqT_ref[...], xn.astype(bf16), (((1,), (1,)), ((), ())), preferred_element_type=f32)
    half = PEER_QDIM // 2
    tm = x.shape[0]
    key_id = lax.broadcasted_iota(jnp.int32, (PEER_N_KEYS, tm), 0).astype(f32)
    cand_pos = lax.broadcasted_iota(jnp.int32, (PEER_TOPK * PEER_TOPK, tm), 0).astype(f32)
    for hd in range(PEER_HEADS):
        q1 = qT[hd * PEER_QDIM: hd * PEER_QDIM + half].astype(bf16)
        q2 = qT[hd * PEER_QDIM + half: (hd + 1) * PEER_QDIM].astype(bf16)
        s1 = jnp.dot(k1_ref[hd], q1, preferred_element_type=f32)
        s2 = jnp.dot(k2_ref[hd], q2, preferred_element_type=f32)
        v1, i1 = _top_rows(s1, key_id, PEER_TOPK)
        v2, i2 = _top_rows(s2, key_id, PEER_TOPK)
        v2m = jnp.concatenate(v2, axis=0)
        i2m = jnp.concatenate(i2, axis=0)
        cand = jnp.concatenate([v1[a] + v2m for a in range(PEER_TOPK)], axis=0)
        cand_id = jnp.concatenate([i1[a] * float(PEER_N_KEYS) + i2m for a in range(PEER_TOPK)], axis=0)
        top_s, ids = _top_rows(cand, cand_pos, PEER_TOPK, payload=cand_id)
        top_s = jnp.concatenate(top_s, axis=0)
        e = jnp.exp(top_s - top_s[0:1])
        gate_ref[hd * PEER_TOPK:(hd + 1) * PEER_TOPK, :] = e / jnp.sum(e, axis=0, keepdims=True)
        ids_ref[hd * PEER_TOPK:(hd + 1) * PEER_TOPK, :] = jnp.concatenate(ids, axis=0).astype(jnp.int32)


def peer_route(h2, norm_g, w_q, keys1, keys2):
    T, d = h2.shape
    tm = PEER_ROUTE_TM
    bf16 = jnp.bfloat16
    nq = PEER_HEADS * PEER_QDIM
    half = PEER_QDIM // 2
    return pl.pallas_call(
        _peer_route_kernel,
        grid=(T // tm,),
        in_specs=[pl.BlockSpec((tm, d), lambda i: (i, 0)),
                  pl.BlockSpec((1, d), lambda i: (0, 0)),
                  pl.BlockSpec((nq, d), lambda i: (0, 0)),
                  pl.BlockSpec((PEER_HEADS, PEER_N_KEYS, half), lambda i: (0, 0, 0)),
                  pl.BlockSpec((PEER_HEADS, PEER_N_KEYS, half), lambda i: (0, 0, 0))],
        out_specs=[pl.BlockSpec((tm, d), lambda i: (i, 0)),
                   pl.BlockSpec((PEER_SLOTS, tm), lambda i: (0, i)),
                   pl.BlockSpec((PEER_SLOTS, tm), lambda i: (0, i))],
        out_shape=[jax.ShapeDtypeStruct((T, d), jnp.float32),
                   jax.ShapeDtypeStruct((PEER_SLOTS, T), jnp.int32),
                   jax.ShapeDtypeStruct((PEER_SLOTS, T), jnp.float32)],
        compiler_params=pltpu.CompilerParams(dimension_semantics=("arbitrary",),
                                             vmem_limit_bytes=V7X_VMEM_LIMIT_BYTES),
        name="peer_route",
    )(h2, norm_g.reshape(1, d), w_q.T.astype(bf16), keys1.astype(bf16), keys2.astype(bf16))


def _peer_slab(buf, slot, r):
    return buf.at[slot, pl.ds(pl.multiple_of(r * PEER_SLAB, PEER_SLAB), PEER_SLAB)]


def _unpack_pair(words):
    lo = lax.bitcast_convert_type(lax.shift_left(words, 16), jnp.float32)
    hi = lax.bitcast_convert_type(lax.bitwise_and(words, -65536), jnp.float32)
    return lo, hi


def _peer_expert_kernel(ids0_ref, idsn_ref, xn_ref, gt_ref, uv_hbm, o_ref, buf, sem):
    i = pl.program_id(0)
    n = pl.num_programs(0)
    slot = i % 2
    groups = PEER_TT * PEER_SLOTS // 8

    def issue(ids_ref, dst_slot):
        def group(gidx, carry):
            for u in range(8):
                r = gidx * 8 + u
                pltpu.make_async_copy(uv_hbm.at[ids_ref[r]], _peer_slab(buf, dst_slot, r), sem.at[dst_slot]).start()
            return carry
        lax.fori_loop(0, groups, group, 0)

    @pl.when(i == 0)
    def _():
        issue(ids0_ref, 0)

    @pl.when(i + 1 < n)
    def _():
        issue(idsn_ref, 1 - slot)

    def wait_group(gidx, carry):
        for u in range(8):
            pltpu.make_async_copy(uv_hbm.at[0], _peer_slab(buf, slot, gidx * 8 + u), sem.at[slot]).wait()
        return carry
    lax.fori_loop(0, groups, wait_group, 0)

    def chunk_pair(base, j):
        return _unpack_pair(buf[slot, pl.ds(base + j, PEER_SLOTS, stride=PEER_SLAB), :])

    for t in range(PEER_TT):
        base = t * PEER_SLOTS * PEER_SLAB
        xrow = xn_ref[t:t + 1, :]
        acc = jnp.zeros((PEER_SLOTS, V7X_LANES), jnp.float32)
        for j in range(PEER_WORDS):
            lo, hi = chunk_pair(base, j)
            acc = acc + lo * xrow[:, 2 * j * V7X_LANES:(2 * j + 1) * V7X_LANES]
            acc = acc + hi * xrow[:, (2 * j + 1) * V7X_LANES:(2 * j + 2) * V7X_LANES]
        act = jnp.sum(acc, axis=1, keepdims=True)
        w = gt_ref[0, :, t:t + 1] * jax.nn.gelu(act)
        for j in range(PEER_WORDS):
            lo, hi = chunk_pair(base + PEER_WORDS, j)
            o_ref[t:t + 1, 2 * j * V7X_LANES:(2 * j + 1) * V7X_LANES] = jnp.sum(lo * w, axis=0, keepdims=True)
            o_ref[t:t + 1, (2 * j + 1) * V7X_LANES:(2 * j + 2) * V7X_LANES] = jnp.sum(hi * w, axis=0, keepdims=True)


def _pack_rows(tab):
    bits = lax.bitcast_convert_type(tab.astype(jnp.bfloat16), jnp.uint16).astype(jnp.uint32)
    bits = bits.reshape(tab.shape[0], PEER_WORDS, 2, V7X_LANES)
    return lax.bitcast_convert_type(bits[:, :, 0] | (bits[:, :, 1] << 16), jnp.int32)


def peer_experts(xn, idsT, gateT, u_tab, v_tab):
    T, d = xn.shape
    tt = PEER_TT
    nt = T // tt
    uv = jnp.concatenate([_pack_rows(u_tab), _pack_rows(v_tab)], axis=1)
    ids = idsT.T.reshape(T * PEER_SLOTS)
    gt = gateT.reshape(PEER_SLOTS, nt, tt).transpose(1, 0, 2)
    return pl.pallas_call(
        _peer_expert_kernel,
        grid=(nt,),
        in_specs=[pl.BlockSpec((tt * PEER_SLOTS,), lambda i: (0,), memory_space=pltpu.SMEM),
                  pl.BlockSpec((tt * PEER_SLOTS,), lambda i: (jnp.minimum(i + 1, nt - 1),), memory_space=pltpu.SMEM),
                  pl.BlockSpec((tt, d), lambda i: (i, 0)),
                  pl.BlockSpec((1, PEER_SLOTS, tt), lambda i: (i, 0, 0)),
                  pl.BlockSpec(memory_space=pl.ANY)],
        out_specs=pl.BlockSpec((tt, d), lambda i: (i, 0)),
        out_shape=jax.ShapeDtypeStruct((T, d), jnp.float32),
        scratch_shapes=[pltpu.VMEM((2, tt * PEER_SLOTS * PEER_SLAB, V7X_LANES), jnp.int32),
                        pltpu.SemaphoreType.DMA((2,))],
        compiler_params=pltpu.CompilerParams(dimension_semantics=("arbitrary",),
                                             vmem_limit_bytes=V7X_VMEM_LIMIT_BYTES),
        name="peer_experts",
    )(ids, ids, xn, gt, uv)


def peer_ffn(h, norm_g, w_q, keys1, keys2, u_tab, v_tab):
    b, s, d = h.shape
    xn, idsT, gateT = peer_route(h.reshape(b * s, d), norm_g, w_q, keys1, keys2)
    return peer_experts(xn, idsT, gateT, u_tab, v_tab).reshape(b, s, d)


def per_layer_embed(h, p_i, norm_g, w_up, w_gate):
    return _mm(p_i, None, w_up, False) * jax.nn.sigmoid(_mm(h, norm_g, w_gate, True))


def kernel(x, p, a_norm_g, a_w_in, a_conv_w, a_conv_b, a_dt_bias, a_log, a_d_skip, a_gnorm_g, a_w_out,
           kv_norm_g, w_kv, cmp_pos_k, cmp_pos_v, cmp_w1_k, cmp_w2_k, cmp_w1_v, cmp_w2_v, rel_bias,
           b_norm_g, b_w_in, b_w_out, c_norm_g, c_w_q, c_keys1, c_keys2, c_u, c_v,
           e_norm_g, e_w_up, e_w_gate, final_g):
    h = x
    shared = None
    for i in range(DEPTH):
        if i < N_A_LAYERS:
            h = ssd_mixer_residual(h, a_norm_g[i], a_w_in[i], a_conv_w[i], a_conv_b[i], a_dt_bias[i],
                                   a_log[i], a_d_skip[i], a_gnorm_g[i], a_w_out[i])
        else:
            if shared is None:
                shared = nsa_shared_kv(h, kv_norm_g, w_kv, cmp_pos_k, cmp_pos_v,
                                       cmp_w1_k, cmp_w2_k, cmp_w1_v, cmp_w2_v)
            j = i - N_A_LAYERS
            h = h + nsa_mixer(h, b_norm_g[j], b_w_in[j], b_w_out[j], rel_bias, *shared)
        h = h + peer_ffn(h, c_norm_g[i], c_w_q[i], c_keys1[i], c_keys2[i], c_u[i], c_v[i])
        h = h + per_layer_embed(h, p[i], e_norm_g[i], e_w_up[i], e_w_gate[i])
    return rms_norm(h, final_g)
```

```python
import functools
import math

import jax
import jax.numpy as jnp
import numpy as np
from jax import lax
from jax.experimental import pallas as pl
from jax.experimental.pallas import tpu as pltpu

D_MODEL = 1024
BATCH = 2
SEQ = 16384
DEPTH = 2
N_A_LAYERS = DEPTH // 2
N_B_LAYERS = DEPTH - N_A_LAYERS

SSD_D_INNER = 2 * D_MODEL
SSD_HEADDIM = 64
SSD_HEADS = SSD_D_INNER // SSD_HEADDIM
SSD_GROUPS = 8
SSD_HEADS_PER_GROUP = SSD_HEADS // SSD_GROUPS
SSD_STATE = 128
SSD_CONV = 4
SSD_CHUNK = 128
SSD_CONV_DIM = SSD_D_INNER + 2 * SSD_GROUPS * SSD_STATE
SSD_IN_DIM = SSD_D_INNER + SSD_CONV_DIM + SSD_HEADS

NSA_HEADS = 16
NSA_KV_HEADS = 4
NSA_GROUP = NSA_HEADS // NSA_KV_HEADS
NSA_HEAD_DIM = 64
CMP_BLOCK = 32
CMP_STRIDE = 16
CMP_HIDDEN = 256
SLC_BLOCK = 64
SLC_TOPN = 16
SLC_FORCE = 1e4
WINDOW = 512
Q_BLOCK = 128
NSA_IN_DIM = NSA_HEADS * NSA_HEAD_DIM + 3 * NSA_HEADS
KV_DIM = 6 * NSA_KV_HEADS * NSA_HEAD_DIM

REL_BUCKETS = 32
REL_MAX_DISTANCE = 4096

PEER_HEADS = 8
PEER_N_KEYS = 128
PEER_EXPERTS = PEER_N_KEYS * PEER_N_KEYS
PEER_QDIM = 256
PEER_TOPK = 16
PEER_TOKEN_CHUNK = 128
PEER_V_SCALE = PEER_HEADS ** -0.5

PLE_DIM = 256
NORM_EPS = 1e-6
NEG_INF = -1e30

V7X_LANES = 128
V7X_VMEM_LIMIT_BYTES = 56 * 1024 * 1024


def _norm_matmul_kernel(x_ref, g_ref, w_ref, o_ref, *, normalize):
    x = x_ref[...]
    if normalize:
        x = x * lax.rsqrt(jnp.mean(x * x, axis=-1, keepdims=True) + NORM_EPS) * g_ref[...]
    o_ref[...] = jnp.dot(x.astype(jnp.bfloat16), w_ref[...], preferred_element_type=jnp.float32)


def norm_matmul(x, g, w, *, normalize, tm=512, tn=None):
    m, k = x.shape
    n = w.shape[1]
    n_pad = -(-n // V7X_LANES) * V7X_LANES
    wb = w.astype(jnp.bfloat16)
    if n_pad != n:
        wb = jnp.pad(wb, ((0, 0), (0, n_pad - n)))
    if tn is None:
        tn = n_pad
    assert m % tm == 0 and n_pad % tn == 0
    out = pl.pallas_call(
        functools.partial(_norm_matmul_kernel, normalize=normalize),
        grid=(m // tm, n_pad // tn),
        in_specs=[
            pl.BlockSpec((tm, k), lambda i, j: (i, 0)),
            pl.BlockSpec((1, k), lambda i, j: (0, 0)),
            pl.BlockSpec((k, tn), lambda i, j: (0, j)),
        ],
        out_specs=pl.BlockSpec((tm, tn), lambda i, j: (i, j)),
        out_shape=jax.ShapeDtypeStruct((m, n_pad), jnp.float32),
        compiler_params=pltpu.CompilerParams(
            dimension_semantics=("arbitrary", "arbitrary"),
            vmem_limit_bytes=V7X_VMEM_LIMIT_BYTES),
        name="norm_matmul" if normalize else "matmul",
    )(x, g.reshape(1, k), wb)
    return out[:, :n] if n_pad != n else out


def _mm(x3, g, w, normalize, **kw):
    b, s, k = x3.shape
    if g is None:
        g = jnp.ones((k,), jnp.float32)
    return norm_matmul(x3.reshape(b * s, k), g, w, normalize=normalize, **kw).reshape(b, s, w.shape[1])


def rms_norm(x, g):
    xf = x.astype(jnp.float32)
    y = xf * lax.rsqrt(jnp.mean(xf * xf, axis=-1, keepdims=True) + NORM_EPS)
    return (y * g.astype(jnp.float32)).astype(x.dtype)


def rel_bucket(dist):
    dist = jnp.maximum(dist, 0)
    max_exact = REL_BUCKETS // 2
    d = jnp.maximum(dist, 1).astype(jnp.float32)
    large = max_exact + (jnp.log(d / max_exact) / math.log(REL_MAX_DISTANCE / max_exact)
                         * (REL_BUCKETS - max_exact)).astype(jnp.int32)
    large = jnp.minimum(large, REL_BUCKETS - 1)
    return jnp.where(dist < max_exact, dist, large)


SSD_GROUP_W = SSD_HEADS_PER_GROUP * SSD_HEADDIM
SSD_IN_PAD = -(-SSD_IN_DIM // V7X_LANES) * V7X_LANES
SSD_TAIL = 8


def _split3(x):
    bf16, f32 = jnp.bfloat16, jnp.float32
    hi = x.astype(bf16)
    r1 = x - hi.astype(f32)
    mid = r1.astype(bf16)
    lo = (r1 - mid.astype(f32)).astype(bf16)
    return hi, mid, lo


def _ssd_kernel(zx_ref, h_ref, cw_ref, cb_ref, dtb_ref, alog_ref, dskip_ref, gn_ref, wout_ref, tri_ref, exp_ref,
                o_ref, xs_scr, state_scr):
    f32, bf16 = jnp.float32, jnp.bfloat16
    L, N, GW = SSD_CHUNK, SSD_STATE, SSD_GROUP_W
    c = pl.program_id(1)

    @pl.when(c == 0)
    def _():
        xs_scr[0:SSD_TAIL, :] = jnp.zeros((SSD_TAIL, SSD_CONV_DIM), f32)
        state_scr[...] = jnp.zeros_like(state_scr)

    xs_scr[SSD_TAIL:SSD_TAIL + L, :] = zx_ref[:, SSD_D_INNER:SSD_D_INNER + SSD_CONV_DIM]
    conv = cb_ref[...]
    for w in range(SSD_CONV):
        conv = conv + cw_ref[w:w + 1, :] * xs_scr[pl.ds(SSD_TAIL - (SSD_CONV - 1) + w, L), :]
    xs_scr[0:SSD_TAIL, :] = xs_scr[L:L + SSD_TAIL, :]
    xbc = conv * jax.nn.sigmoid(conv)

    dt_in = zx_ref[:, SSD_D_INNER + SSD_CONV_DIM:SSD_IN_PAD] + dtb_ref[...]
    dt = jnp.maximum(dt_in, 0.0) + jnp.log1p(jnp.exp(-jnp.abs(dt_in)))
    a_dt = dt * -jnp.exp(alog_ref[...])
    a_cs = sum(jnp.dot(tri_ref[...], t, preferred_element_type=f32) for t in _split3(a_dt))
    a_csT = a_cs.T
    a_end = a_cs[L - 1:L, :]
    grow = jnp.exp(a_cs)
    to_end = jnp.exp(a_end - a_cs)

    def per_channel(q):
        return sum(jnp.dot(t, exp_ref[...], preferred_element_type=f32) for t in _split3(q)[:2])

    dt_c, grow_c, to_end_c = per_channel(dt), per_channel(grow), per_channel(to_end)
    end_c = per_channel(jnp.broadcast_to(jnp.exp(a_end), (8, V7X_LANES)))[0:1]
    dskip_c = per_channel(jnp.broadcast_to(dskip_ref[...], (8, V7X_LANES)))[0:1]

    x = xbc[:, :SSD_D_INNER]
    xd = x * dt_c
    xd_bf = xd.astype(bf16)
    xe_bf = (xd * to_end_c).astype(bf16)
    row_i = lax.broadcasted_iota(jnp.int32, (L, L), 0)
    col_i = lax.broadcasted_iota(jnp.int32, (L, L), 1)
    lane_head = lax.broadcasted_iota(jnp.int32, (L, GW), 1) // SSD_HEADDIM
    ys = []
    for g in range(SSD_GROUPS):
        bm = xbc[:, SSD_D_INNER + g * N:SSD_D_INNER + (g + 1) * N]
        cm = xbc[:, SSD_D_INNER + SSD_GROUPS * N + g * N:SSD_D_INNER + SSD_GROUPS * N + (g + 1) * N].astype(bf16)
        bmT = bm.T.astype(bf16)
        cb = jnp.dot(cm, bmT, preferred_element_type=f32)
        xd_g = xd_bf[:, g * GW:(g + 1) * GW]
        y_g = jnp.zeros((L, GW), f32)
        for r in range(SSD_HEADS_PER_GROUP):
            hd = g * SSD_HEADS_PER_GROUP + r
            seg = a_cs[:, hd:hd + 1] - a_csT[hd:hd + 1, :]
            decay = jnp.where(row_i >= col_i, jnp.exp(seg), 0.0)
            xr = jnp.where(lane_head == r, xd_g, jnp.zeros_like(xd_g))
            y_g = y_g + jnp.dot((cb * decay).astype(bf16), xr, preferred_element_type=f32)
        hT = state_scr[g]
        y_off = jnp.dot(cm, hT.astype(bf16), preferred_element_type=f32) * grow_c[:, g * GW:(g + 1) * GW]
        new_states = jnp.dot(bmT, xe_bf[:, g * GW:(g + 1) * GW], preferred_element_type=f32)
        state_scr[g] = hT * end_c[:, g * GW:(g + 1) * GW] + new_states
        ys.append(y_g + y_off)
    y = jnp.concatenate(ys, axis=1) + x * dskip_c

    z = zx_ref[:, :SSD_D_INNER]
    yz = y * (z * jax.nn.sigmoid(z))
    parts = []
    for g in range(SSD_GROUPS):
        t = yz[:, g * GW:(g + 1) * GW]
        parts.append(t * lax.rsqrt(jnp.mean(t * t, axis=-1, keepdims=True) + NORM_EPS))
    yn = jnp.concatenate(parts, axis=1) * gn_ref[...]
    o_ref[...] = h_ref[...] + jnp.dot(yn.astype(bf16), wout_ref[...], preferred_element_type=f32)


def ssd_mixer_residual(h, norm_g, w_in, conv_w, conv_b, dt_bias, a_log, d_skip, gnorm_g, w_out):
    b, s, d = h.shape
    L = SSD_CHUNK
    nc = s // L
    w_in_p = jnp.pad(w_in, ((0, 0), (0, SSD_IN_PAD - SSD_IN_DIM)))
    zx = norm_matmul(h.reshape(b * s, d), norm_g, w_in_p, normalize=True, tm=256)
    lane_pad = lambda v: jnp.pad(v.reshape(1, -1), ((0, 0), (0, V7X_LANES - v.shape[-1])))
    tri = jnp.asarray(np.tril(np.ones((L, L))), jnp.bfloat16)
    head_of = np.arange(SSD_D_INNER) // SSD_HEADDIM
    expand = jnp.asarray(np.arange(V7X_LANES)[:, None] == head_of[None, :], jnp.bfloat16)
    const = lambda *shape: pl.BlockSpec(shape, lambda bi, ci: (0,) * len(shape))
    out = pl.pallas_call(
        _ssd_kernel,
        grid=(b, nc),
        in_specs=[pl.BlockSpec((L, SSD_IN_PAD), lambda bi, ci: (bi * nc + ci, 0)),
                  pl.BlockSpec((L, d), lambda bi, ci: (bi * nc + ci, 0)),
                  const(SSD_CONV, SSD_CONV_DIM), const(1, SSD_CONV_DIM),
                  const(1, V7X_LANES), const(1, V7X_LANES), const(1, V7X_LANES),
                  const(1, SSD_D_INNER), const(SSD_D_INNER, d), const(L, L), const(V7X_LANES, SSD_D_INNER)],
        out_specs=pl.BlockSpec((L, d), lambda bi, ci: (bi * nc + ci, 0)),
        out_shape=jax.ShapeDtypeStruct((b * s, d), jnp.float32),
        scratch_shapes=[pltpu.VMEM((L + SSD_TAIL, SSD_CONV_DIM), jnp.float32),
                        pltpu.VMEM((SSD_GROUPS, SSD_STATE, SSD_GROUP_W), jnp.float32)],
        compiler_params=pltpu.CompilerParams(dimension_semantics=("arbitrary", "arbitrary"),
                                             vmem_limit_bytes=V7X_VMEM_LIMIT_BYTES),
        name="ssd_mixer",
    )(zx, h.reshape(b * s, d), conv_w, conv_b.reshape(1, -1), lane_pad(dt_bias), lane_pad(a_log), lane_pad(d_skip),
      gnorm_g.reshape(1, -1), w_out.astype(jnp.bfloat16), tri, expand)
    return out.reshape(b, s, d)


def _mlp_kernel(x_ref, w1_ref, w2_ref, o_ref):
    hid = jnp.dot(x_ref[...].astype(jnp.bfloat16), w1_ref[...], preferred_element_type=jnp.float32)
    hid = jax.nn.gelu(hid)
    o_ref[...] = jnp.dot(hid.astype(jnp.bfloat16), w2_ref[...], preferred_element_type=jnp.float32)


def mlp2(x, w1, w2, *, tm=512):
    m, k = x.shape
    hdim, n = w2.shape
    assert m % tm == 0
    return pl.pallas_call(
        _mlp_kernel,
        grid=(m // tm,),
        in_specs=[pl.BlockSpec((tm, k), lambda i: (i, 0)),
                  pl.BlockSpec((k, hdim), lambda i: (0, 0)),
                  pl.BlockSpec((hdim, n), lambda i: (0, 0))],
        out_specs=pl.BlockSpec((tm, n), lambda i: (i, 0)),
        out_shape=jax.ShapeDtypeStruct((m, n), jnp.float32),
        compiler_params=pltpu.CompilerParams(dimension_semantics=("arbitrary",),
                                             vmem_limit_bytes=V7X_VMEM_LIMIT_BYTES),
        name="cmp_mlp",
    )(x, w1.astype(jnp.bfloat16), w2.astype(jnp.bfloat16))


def compress_blocks(t, pos, w1, w2):
    b, s, g, d = t.shape
    halves = t.transpose(0, 2, 1, 3).reshape(b, g, s // CMP_STRIDE, CMP_STRIDE, d)
    nxt = jnp.concatenate([halves[:, :, 1:], jnp.zeros_like(halves[:, :, :1])], axis=2)
    blocks = jnp.concatenate([halves, nxt], axis=3) + pos
    flat = blocks.reshape(b * g * (s // CMP_STRIDE), CMP_BLOCK * d)
    return mlp2(flat, w1, w2).reshape(b, g, s // CMP_STRIDE, d)


N_SEL = SEQ // SLC_BLOCK
N_KT = SEQ // Q_BLOCK
CMP_FRONT = 256
CMP_ROWS = CMP_FRONT + SEQ // CMP_STRIDE
CMP_WIN = 256
REL_TABLE = 4096
NEAR_TILES = 24
WIN_TILES = WINDOW // Q_BLOCK + 1
SEL_GROUP = 8
QL = NSA_GROUP * Q_BLOCK


def _nsa_kernel(qT_ref, gT_ref, kc_ref, vcT_ref, kk_ref, vsT_ref, vwT_ref, at_ref, bcT_ref, bT_ref, bTw_ref,
                o_ref, s_scr, sel_scr):
    f32, bf16 = jnp.float32, jnp.bfloat16
    i = pl.program_id(2)
    qT = (qT_ref[0, 0, 0] * (NSA_HEAD_DIM ** -0.5)).astype(bf16)
    zq = jnp.zeros_like(qT)
    q_sel = jnp.concatenate([qT, zq], axis=0)
    q_win = jnp.concatenate([zq, qT], axis=0)

    r0 = pl.multiple_of(8 * i + 8, 8)
    s = jnp.dot(kc_ref[0, 0], qT, preferred_element_type=f32)
    rio = lax.broadcasted_iota(jnp.int32, (CMP_ROWS, Q_BLOCK), 0)
    rowmask = jnp.where((rio >= CMP_FRONT) & (rio < r0 + CMP_WIN), 0.0, NEG_INF)
    s_scr[...] = s + jnp.concatenate([rowmask] * NSA_GROUP, axis=1)
    s_scr[pl.ds(r0, CMP_WIN), :] = s_scr[pl.ds(r0, CMP_WIN), :] + bcT_ref[0]
    s = s_scr[...]
    m = jnp.max(s, axis=0, keepdims=True)
    p = jnp.exp(s - m)
    l = jnp.sum(p, axis=0, keepdims=True)
    inv = jnp.where(m > 0.1 * NEG_INF, 1.0 / l, 0.0)
    pn = p * inv
    o_c = jnp.dot(vcT_ref[0, 0], pn.astype(bf16), preferred_element_type=f32)

    psum = pn[:, 0:Q_BLOCK]
    for r in range(1, NSA_GROUP):
        psum = psum + pn[:, r * Q_BLOCK:(r + 1) * Q_BLOCK]
    hi = psum.astype(bf16)
    lo = (psum - hi.astype(f32)).astype(bf16)
    imp = (jnp.dot(at_ref[...], hi, preferred_element_type=f32)
           + jnp.dot(at_ref[...], lo, preferred_element_type=f32))
    blk = lax.broadcasted_iota(jnp.int32, (N_SEL, Q_BLOCK), 0)
    qi = lax.broadcasted_iota(jnp.int32, (N_SEL, Q_BLOCK), 1)
    cur = 2 * i + (qi >= SLC_BLOCK).astype(jnp.int32)
    forced = (blk == 0) | (blk == cur) | (blk == cur - 1)
    valid = blk <= cur
    score = jnp.where(forced, SLC_FORCE, jnp.where(valid, imp, -SLC_FORCE))
    blkf = blk.astype(f32)
    chosen = jnp.zeros((N_SEL, Q_BLOCK), f32)
    for _ in range(SLC_TOPN):
        mx = jnp.max(score, axis=0, keepdims=True)
        first = jnp.min(jnp.where(score == mx, blkf, float(N_SEL)), axis=0, keepdims=True)
        hit = blkf == first
        chosen = jnp.where(hit, 1.0, chosen)
        score = jnp.where(hit, -jnp.inf, score)
    sel_scr[...] = jnp.where((chosen > 0.5) & valid, 0.0, NEG_INF)

    def attend(kt, carry, q_aug, vT_ref, bias):
        m_run, l_run, acc = carry
        scs = [jnp.dot(kk_ref[0, 0, k], q_aug, preferred_element_type=f32) + b for k, b in zip(kt, bias)]
        m_new = m_run
        for sc in scs:
            m_new = jnp.maximum(m_new, jnp.max(sc, axis=0, keepdims=True))
        alpha = jnp.exp(m_run - m_new)
        l_new = alpha * l_run
        acc_new = alpha * acc
        for k, sc in zip(kt, scs):
            pt = jnp.exp(sc - m_new)
            l_new = l_new + jnp.sum(pt, axis=0, keepdims=True)
            acc_new = acc_new + jnp.dot(vT_ref[0, 0, k], pt.astype(bf16), preferred_element_type=f32)
        return m_new, l_new, acc_new

    def sel_mask(kt):
        m0 = jnp.broadcast_to(sel_scr[pl.ds(2 * kt, 1), :], (SLC_BLOCK, Q_BLOCK))
        m1 = jnp.broadcast_to(sel_scr[pl.ds(2 * kt + 1, 1), :], (SLC_BLOCK, Q_BLOCK))
        mk = jnp.concatenate([m0, m1], axis=0)
        return jnp.concatenate([mk] * NSA_GROUP, axis=1)

    init = (jnp.full((1, QL), NEG_INF, f32), jnp.zeros((1, QL), f32), jnp.zeros((NSA_HEAD_DIM, QL), f32))
    U = SEL_GROUP

    def far_group(g, c):
        kts = [g * U + u for u in range(U)]
        return attend(kts, c, q_sel, vsT_ref, [sel_mask(k) for k in kts])

    def near_group(g, c):
        kts, biases = [], []
        for u in range(U):
            delta = i - (g * U + u)
            k = jnp.minimum(g * U + u, i)
            tile = jnp.where(delta < 0, NEAR_TILES + 1, jnp.minimum(delta, NEAR_TILES))
            kts.append(k)
            biases.append(sel_mask(k) + bT_ref[0, tile])
        return attend(kts, c, q_sel, vsT_ref, biases)

    n_far = jnp.maximum((i - (NEAR_TILES - 1)) // U, 0)
    carry = lax.fori_loop(0, n_far, far_group, init)
    _, l_s, acc_s = lax.fori_loop(n_far, i // U + 1, near_group, carry)
    o_s = acc_s / l_s

    kts, biases = [], []
    for u in range(WIN_TILES):
        k = i - (WIN_TILES - 1) + u
        kts.append(jnp.maximum(k, 0))
        biases.append(bTw_ref[0, jnp.where(k >= 0, WIN_TILES - 1 - u, WIN_TILES)])
    _, l_w, acc_w = attend(kts, init, q_win, vwT_ref, biases)
    o_w = acc_w / l_w

    gate = jax.nn.sigmoid(gT_ref[0, 0, 0])
    o_ref[0, 0, 0] = gate[0:1] * o_c + gate[1:2] * o_s + gate[2:3] * o_w


BIAS_EXT = REL_TABLE + 2 * Q_BLOCK


CMP_OFF = CMP_STRIDE * (CMP_WIN - 8) - (CMP_BLOCK - 1)


def _bias_tile_kernel(bd_ref, bdc_ref, cmp_ref, near_ref, win_ref):
    row = bd_ref[0]
    kj = lax.broadcasted_iota(jnp.int32, (Q_BLOCK, Q_BLOCK), 0)
    qi = lax.broadcasted_iota(jnp.int32, (Q_BLOCK, Q_BLOCK), 1)
    for t in range(NEAR_TILES):
        seg = jnp.broadcast_to(row[:, t * Q_BLOCK:(t + 2) * Q_BLOCK], (Q_BLOCK, 2 * Q_BLOCK))
        tile = pltpu.roll(seg, 0, 1, stride=1, stride_axis=0)[:, Q_BLOCK:]
        near_ref[0, t] = tile
        if t < WIN_TILES:
            win_ref[0, t] = jnp.where(t * Q_BLOCK + qi - kj < WINDOW, tile, NEG_INF)
    near_ref[0, NEAR_TILES] = jnp.zeros((Q_BLOCK, Q_BLOCK), jnp.float32)
    near_ref[0, NEAR_TILES + 1] = jnp.full((Q_BLOCK, Q_BLOCK), NEG_INF, jnp.float32)
    win_ref[0, WIN_TILES] = jnp.full((Q_BLOCK, Q_BLOCK), NEG_INF, jnp.float32)
    full = jnp.broadcast_to(bdc_ref[0], (CMP_WIN, BIAS_EXT))
    cmp_ref[0] = pltpu.roll(full, 0, 1, stride=CMP_STRIDE, stride_axis=0)[:, REL_TABLE:REL_TABLE + Q_BLOCK]


def _bias_tables(rel_bias):
    G, R = NSA_KV_HEADS, NSA_GROUP
    bd = rel_bias[rel_bucket(jnp.arange(REL_TABLE))]
    bd = (bd - bd[REL_TABLE - 1]).T
    def padded(front):
        pad = lambda n: jnp.full((NSA_HEADS, n), NEG_INF, jnp.float32)
        return jnp.concatenate([pad(front), bd, pad(BIAS_EXT - REL_TABLE - front)], axis=1).reshape(NSA_HEADS, 1, BIAS_EXT)

    bd_ext, bd_cmp = padded(Q_BLOCK), padded(REL_TABLE - CMP_OFF)
    head = lambda *blk: pl.BlockSpec((1,) + blk + (Q_BLOCK,), lambda hd: (hd // R,) + (0,) * len(blk) + (hd % R,))
    return pl.pallas_call(
        _bias_tile_kernel,
        grid=(NSA_HEADS,),
        in_specs=[pl.BlockSpec((1, 1, BIAS_EXT), lambda hd: (hd, 0, 0))] * 2,
        out_specs=[head(CMP_WIN), head(NEAR_TILES + 2, Q_BLOCK), head(WIN_TILES + 1, Q_BLOCK)],
        out_shape=[jax.ShapeDtypeStruct((G, CMP_WIN, QL), jnp.float32),
                   jax.ShapeDtypeStruct((G, NEAR_TILES + 2, Q_BLOCK, QL), jnp.float32),
                   jax.ShapeDtypeStruct((G, WIN_TILES + 1, Q_BLOCK, QL), jnp.float32)],
        compiler_params=pltpu.CompilerParams(dimension_semantics=("arbitrary",),
                                             vmem_limit_bytes=V7X_VMEM_LIMIT_BYTES),
        name="nsa_bias_tiles",
    )(bd_ext, bd_cmp)


def _pool_matrix():
    per = SLC_BLOCK // CMP_STRIDE
    n_span = CMP_BLOCK // CMP_STRIDE
    k = np.arange(CMP_ROWS)[None, :] - CMP_FRONT
    j = np.arange(N_SEL)[:, None]
    return jnp.asarray((k >= per * j - (n_span - 1)) & (k <= per * j + per - 1), jnp.bfloat16)


def nsa_shared_kv(h, kv_norm_g, w_kv, cmp_pos_k, cmp_pos_v, cmp_w1_k, cmp_w2_k, cmp_w1_v, cmp_w2_v):
    b, s, _ = h.shape
    G, DH = NSA_KV_HEADS, NSA_HEAD_DIM
    bf16 = jnp.bfloat16
    kv = _mm(h, kv_norm_g, w_kv, True).reshape(b, s, 6, G, DH)
    k_cmp = compress_blocks(kv[:, :, 0], cmp_pos_k, cmp_w1_k, cmp_w2_k)
    v_cmp = compress_blocks(kv[:, :, 1], cmp_pos_v, cmp_w1_v, cmp_w2_v)
    front = jnp.zeros((b, G, CMP_FRONT, DH), bf16)
    kc = jnp.concatenate([front, k_cmp.astype(bf16)], axis=2)
    vcT = jnp.concatenate([front, v_cmp.astype(bf16)], axis=2).transpose(0, 1, 3, 2)
    kk = jnp.concatenate([kv[:, :, 2], kv[:, :, 4]], axis=-1).astype(bf16)
    kk = kk.transpose(0, 2, 1, 3).reshape(b, G, N_KT, Q_BLOCK, 2 * DH)

    def vt(v):
        return v.astype(bf16).transpose(0, 2, 1, 3).reshape(b, G, N_KT, Q_BLOCK, DH).transpose(0, 1, 2, 4, 3)

    return kc, vcT, kk, vt(kv[:, :, 3]), vt(kv[:, :, 5])


def nsa_mixer(h, norm_g, w_in, w_out, rel_bias, kc, vcT, kk, vsT, vwT):
    b, s, _ = h.shape
    G, R, DH = NSA_KV_HEADS, NSA_GROUP, NSA_HEAD_DIM
    nb = s // Q_BLOCK
    proj = _mm(h, norm_g, w_in, True)
    qT = proj[..., :NSA_HEADS * DH].reshape(b, nb, Q_BLOCK, G, R, DH).transpose(0, 3, 1, 5, 4, 2)
    qT = qT.reshape(b, G, nb, DH, QL)
    gT = proj[..., NSA_HEADS * DH:].reshape(b, nb, Q_BLOCK, G, R, 3).transpose(0, 3, 1, 5, 4, 2)
    gT = gT.reshape(b, G, nb, 3, QL)
    b_cmp, b_near, b_win = _bias_tables(rel_bias)
    at = _pool_matrix()

    per_bg = lambda *blk: pl.BlockSpec((1, 1) + blk, lambda bi, gi, i: (bi, gi) + (0,) * len(blk))
    per_g = lambda *blk: pl.BlockSpec((1,) + blk, lambda bi, gi, i: (gi,) + (0,) * len(blk))
    oT = pl.pallas_call(
        _nsa_kernel,
        grid=(b, G, nb),
        in_specs=[
            pl.BlockSpec((1, 1, 1, DH, QL), lambda bi, gi, i: (bi, gi, i, 0, 0)),
            pl.BlockSpec((1, 1, 1, 3, QL), lambda bi, gi, i: (bi, gi, i, 0, 0)),
            per_bg(CMP_ROWS, DH),
            per_bg(DH, CMP_ROWS),
            per_bg(N_KT, Q_BLOCK, 2 * DH),
            per_bg(N_KT, DH, Q_BLOCK),
            per_bg(N_KT, DH, Q_BLOCK),
            pl.BlockSpec((N_SEL, CMP_ROWS), lambda bi, gi, i: (0, 0)),
            per_g(CMP_WIN, QL),
            per_g(NEAR_TILES + 2, Q_BLOCK, QL),
            per_g(WIN_TILES + 1, Q_BLOCK, QL),
        ],
        out_specs=pl.BlockSpec((1, 1, 1, DH, QL), lambda bi, gi, i: (bi, gi, i, 0, 0)),
        out_shape=jax.ShapeDtypeStruct((b, G, nb, DH, QL), jnp.float32),
        scratch_shapes=[pltpu.VMEM((CMP_ROWS, QL), jnp.float32), pltpu.VMEM((N_SEL, Q_BLOCK), jnp.float32)],
        compiler_params=pltpu.CompilerParams(
            dimension_semantics=("arbitrary", "arbitrary", "arbitrary"),
            vmem_limit_bytes=V7X_VMEM_LIMIT_BYTES),
        name="nsa_attention",
    )(qT, gT, kc, vcT, kk, vsT, vwT, at, b_cmp, b_near, b_win)
    o = oT.reshape(b, G, nb, DH, R, Q_BLOCK).transpose(0, 2, 5, 1, 4, 3).reshape(b, s, NSA_HEADS * DH)
    return _mm(o, None, w_out, False)


PEER_SLOTS = PEER_HEADS * PEER_TOPK
PEER_ROUTE_TM = 256
PEER_TT = 8
PEER_CHUNKS = D_MODEL // V7X_LANES
PEER_SLAB = 2 * PEER_CHUNKS
PEER_SLAB_STRIDE = 24


def _top_rows(s, rowid, n, payload=None):
    vals, picks = [], []
    for _ in range(n):
        mx = jnp.max(s, axis=0, keepdims=True)
        first = jnp.min(jnp.where(s == mx, rowid, float(s.shape[0])), axis=0, keepdims=True)
        hit = rowid == first
        vals.append(mx)
        picks.append(first if payload is None else jnp.max(jnp.where(hit, payload, -1.0), axis=0, keepdims=True))
        s = jnp.where(hit, -jnp.inf, s)
    return vals, picks


def _peer_route_kernel(h_ref, g_ref, wqT_ref, k1_ref, k2_ref, xn_ref, ids_ref, gate_ref):
    f32, bf16 = jnp.float32, jnp.bfloat16
    x = h_ref[...]
    xn = x * lax.rsqrt(jnp.mean(x * x, axis=-1, keepdims=True) + NORM_EPS) * g_ref[...]
    xn_ref[...] = xn
    qT = lax.dot_general(wqT_ref[...], xn.astype(bf16), (((1,), (1,)), ((), ())), preferred_element_type=f32)
    half = PEER_QDIM // 2
    tm = x.shape[0]
    key_id = lax.broadcasted_iota(jnp.int32, (PEER_N_KEYS, tm), 0).astype(f32)
    cand_pos = lax.broadcasted_iota(jnp.int32, (PEER_TOPK * PEER_TOPK, tm), 0).astype(f32)
    for hd in range(PEER_HEADS):
        q1 = qT[hd * PEER_QDIM: hd * PEER_QDIM + half].astype(bf16)
        q2 = qT[hd * PEER_QDIM + half: (hd + 1) * PEER_QDIM].astype(bf16)
        s1 = jnp.dot(k1_ref[hd], q1, preferred_element_type=f32)
        s2 = jnp.dot(k2_ref[hd], q2, preferred_element_type=f32)
        v1, i1 = _top_rows(s1, key_id, PEER_TOPK)
        v2, i2 = _top_rows(s2, key_id, PEER_TOPK)
        v2m = jnp.concatenate(v2, axis=0)
        i2m = jnp.concatenate(i2, axis=0)
        cand = jnp.concatenate([v1[a] + v2m for a in range(PEER_TOPK)], axis=0)
        cand_id = jnp.concatenate([i1[a] * float(PEER_N_KEYS) + i2m for a in range(PEER_TOPK)], axis=0)
        top_s, ids = _top_rows(cand, cand_pos, PEER_TOPK, payload=cand_id)
        top_s = jnp.concatenate(top_s, axis=0)
        e = jnp.exp(top_s - top_s[0:1])
        gate_ref[hd * PEER_TOPK:(hd + 1) * PEER_TOPK, :] = e / jnp.sum(e, axis=0, keepdims=True)
        ids_ref[hd * PEER_TOPK:(hd + 1) * PEER_TOPK, :] = jnp.concatenate(ids, axis=0).astype(jnp.int32)


def peer_route(h2, norm_g, w_q, keys1, keys2):
    T, d = h2.shape
    tm = PEER_ROUTE_TM
    bf16 = jnp.bfloat16
    nq = PEER_HEADS * PEER_QDIM
    half = PEER_QDIM // 2
    return pl.pallas_call(
        _peer_route_kernel,
        grid=(T // tm,),
        in_specs=[pl.BlockSpec((tm, d), lambda i: (i, 0)),
                  pl.BlockSpec((1, d), lambda i: (0, 0)),
                  pl.BlockSpec((nq, d), lambda i: (0, 0)),
                  pl.BlockSpec((PEER_HEADS, PEER_N_KEYS, half), lambda i: (0, 0, 0)),
                  pl.BlockSpec((PEER_HEADS, PEER_N_KEYS, half), lambda i: (0, 0, 0))],
        out_specs=[pl.BlockSpec((tm, d), lambda i: (i, 0)),
                   pl.BlockSpec((PEER_SLOTS, tm), lambda i: (0, i)),
                   pl.BlockSpec((PEER_SLOTS, tm), lambda i: (0, i))],
        out_shape=[jax.ShapeDtypeStruct((T, d), jnp.float32),
                   jax.ShapeDtypeStruct((PEER_SLOTS, T), jnp.int32),
                   jax.ShapeDtypeStruct((PEER_SLOTS, T), jnp.float32)],
        compiler_params=pltpu.CompilerParams(dimension_semantics=("arbitrary",),
                                             vmem_limit_bytes=V7X_VMEM_LIMIT_BYTES),
        name="peer_route",
    )(h2, norm_g.reshape(1, d), w_q.T.astype(bf16), keys1.astype(bf16), keys2.astype(bf16))


def _peer_slab(buf, slot, r):
    return buf.at[slot, pl.ds(pl.multiple_of(r * PEER_SLAB_STRIDE, 8), PEER_SLAB)]


def _peer_expert_kernel(ids0_ref, idsn_ref, xn_ref, gt_ref, uv_hbm, o_ref, buf, sem):
    i = pl.program_id(0)
    n = pl.num_programs(0)
    slot = i % 2
    groups = PEER_TT * PEER_SLOTS // 8

    def issue(ids_ref, dst_slot):
        def group(gidx, carry):
            for u in range(8):
                r = gidx * 8 + u
                pltpu.make_async_copy(uv_hbm.at[ids_ref[r]], _peer_slab(buf, dst_slot, r), sem.at[dst_slot]).start()
            return carry
        lax.fori_loop(0, groups, group, 0)

    @pl.when(i == 0)
    def _():
        issue(ids0_ref, 0)

    @pl.when(i + 1 < n)
    def _():
        issue(idsn_ref, 1 - slot)

    def wait_group(gidx, carry):
        for u in range(8):
            pltpu.make_async_copy(uv_hbm.at[0], _peer_slab(buf, slot, gidx * 8 + u), sem.at[slot]).wait()
        return carry
    lax.fori_loop(0, groups, wait_group, 0)

    for t in range(PEER_TT):
        base = t * PEER_SLOTS * PEER_SLAB_STRIDE
        xrow = xn_ref[t:t + 1, :]
        acc = jnp.zeros((PEER_SLOTS, V7X_LANES), jnp.float32)
        for j in range(PEER_CHUNKS):
            tile = buf[slot, pl.ds(base + j, PEER_SLOTS, stride=PEER_SLAB_STRIDE), :]
            acc = acc + tile * xrow[:, j * V7X_LANES:(j + 1) * V7X_LANES]
        act = jnp.sum(acc, axis=1, keepdims=True)
        w = gt_ref[0, :, t:t + 1] * jax.nn.gelu(act)
        for j in range(PEER_CHUNKS):
            tile = buf[slot, pl.ds(base + PEER_CHUNKS + j, PEER_SLOTS, stride=PEER_SLAB_STRIDE), :]
            o_ref[t:t + 1, j * V7X_LANES:(j + 1) * V7X_LANES] = jnp.sum(tile * w, axis=0, keepdims=True)


def peer_experts(xn, idsT, gateT, u_tab, v_tab):
    T, d = xn.shape
    tt = PEER_TT
    nt = T // tt
    uv = jnp.concatenate([u_tab.reshape(PEER_EXPERTS, PEER_CHUNKS, V7X_LANES),
                          v_tab.reshape(PEER_EXPERTS, PEER_CHUNKS, V7X_LANES)], axis=1)
    ids = idsT.T.reshape(T * PEER_SLOTS)
    gt = gateT.reshape(PEER_SLOTS, nt, tt).transpose(1, 0, 2)
    return pl.pallas_call(
        _peer_expert_kernel,
        grid=(nt,),
        in_specs=[pl.BlockSpec((tt * PEER_SLOTS,), lambda i: (0,), memory_space=pltpu.SMEM),
                  pl.BlockSpec((tt * PEER_SLOTS,), lambda i: (jnp.minimum(i + 1, nt - 1),), memory_space=pltpu.SMEM),
                  pl.BlockSpec((tt, d), lambda i: (i, 0)),
                  pl.BlockSpec((1, PEER_SLOTS, tt), lambda i: (i, 0, 0)),
                  pl.BlockSpec(memory_space=pl.ANY)],
        out_specs=pl.BlockSpec((tt, d), lambda i: (i, 0)),
        out_shape=jax.ShapeDtypeStruct((T, d), jnp.float32),
        scratch_shapes=[pltpu.VMEM((2, tt * PEER_SLOTS * PEER_SLAB_STRIDE, V7X_LANES), jnp.float32),
                        pltpu.SemaphoreType.DMA((2,))],
        compiler_params=pltpu.CompilerParams(dimension_semantics=("arbitrary",),
                                             vmem_limit_bytes=V7X_VMEM_LIMIT_BYTES),
        name="peer_experts",
    )(ids, ids, xn, gt, uv)


def peer_ffn(h, norm_g, w_q, keys1, keys2, u_tab, v_tab):
    b, s, d = h.shape
    xn, idsT, gateT = peer_route(h.reshape(b * s, d), norm_g, w_q, keys1, keys2)
    return peer_experts(xn, idsT, gateT, u_tab, v_tab).reshape(b, s, d)


def per_layer_embed(h, p_i, norm_g, w_up, w_gate):
    return _mm(p_i, None, w_up, False) * jax.nn.sigmoid(_mm(h, norm_g, w_gate, True))


def kernel(x, p, a_norm_g, a_w_in, a_conv_w, a_conv_b, a_dt_bias, a_log, a_d_skip, a_gnorm_g, a_w_out,
           kv_norm_g, w_kv, cmp_pos_k, cmp_pos_v, cmp_w1_k, cmp_w2_k, cmp_w1_v, cmp_w2_v, rel_bias,
           b_norm_g, b_w_in, b_w_out, c_norm_g, c_w_q, c_keys1, c_keys2, c_u, c_v,
           e_norm_g, e_w_up, e_w_gate, final_g):
    h = x
    shared = None
    for i in range(DEPTH):
        if i < N_A_LAYERS:
            h = ssd_mixer_residual(h, a_norm_g[i], a_w_in[i], a_conv_w[i], a_conv_b[i], a_dt_bias[i],
                                   a_log[i], a_d_skip[i], a_gnorm_g[i], a_w_out[i])
        else:
            if shared is None:
                shared = nsa_shared_kv(h, kv_norm_g, w_kv, cmp_pos_k, cmp_pos_v,
                                       cmp_w1_k, cmp_w2_k, cmp_w1_v, cmp_w2_v)
            j = i - N_A_LAYERS
            h = h + nsa_mixer(h, b_norm_g[j], b_w_in[j], b_w_out[j], rel_bias, *shared)
        h = h + peer_ffn(h, c_norm_g[i], c_w_q[i], c_keys1[i], c_keys2[i], c_u[i], c_v[i])
        h = h + per_layer_embed(h, p[i], e_norm_g[i], e_w_up[i], e_w_gate[i])
    return rms_norm(h, final_g)
```

```python
import functools
import math

import jax
import jax.numpy as jnp
import numpy as np
from jax import lax
from jax.experimental import pallas as pl
from jax.experimental.pallas import tpu as pltpu

D_MODEL = 1024
BATCH = 2
SEQ = 16384
DEPTH = 2
N_A_LAYERS = DEPTH // 2
N_B_LAYERS = DEPTH - N_A_LAYERS

SSD_D_INNER = 2 * D_MODEL
SSD_HEADDIM = 64
SSD_HEADS = SSD_D_INNER // SSD_HEADDIM
SSD_GROUPS = 8
SSD_HEADS_PER_GROUP = SSD_HEADS // SSD_GROUPS
SSD_STATE = 128
SSD_CONV = 4
SSD_CHUNK = 128
SSD_CONV_DIM = SSD_D_INNER + 2 * SSD_GROUPS * SSD_STATE
SSD_IN_DIM = SSD_D_INNER + SSD_CONV_DIM + SSD_HEADS

NSA_HEADS = 16
NSA_KV_HEADS = 4
NSA_GROUP = NSA_HEADS // NSA_KV_HEADS
NSA_HEAD_DIM = 64
CMP_BLOCK = 32
CMP_STRIDE = 16
CMP_HIDDEN = 256
SLC_BLOCK = 64
SLC_TOPN = 16
SLC_FORCE = 1e4
WINDOW = 512
Q_BLOCK = 128
NSA_IN_DIM = NSA_HEADS * NSA_HEAD_DIM + 3 * NSA_HEADS
KV_DIM = 6 * NSA_KV_HEADS * NSA_HEAD_DIM

REL_BUCKETS = 32
REL_MAX_DISTANCE = 4096

PEER_HEADS = 8
PEER_N_KEYS = 128
PEER_EXPERTS = PEER_N_KEYS * PEER_N_KEYS
PEER_QDIM = 256
PEER_TOPK = 16
PEER_TOKEN_CHUNK = 128
PEER_V_SCALE = PEER_HEADS ** -0.5

PLE_DIM = 256
NORM_EPS = 1e-6
NEG_INF = -1e30

V7X_LANES = 128
V7X_VMEM_LIMIT_BYTES = 56 * 1024 * 1024


def _norm_matmul_kernel(x_ref, g_ref, w_ref, o_ref, *, normalize):
    x = x_ref[...]
    if normalize:
        x = x * lax.rsqrt(jnp.mean(x * x, axis=-1, keepdims=True) + NORM_EPS) * g_ref[...]
    o_ref[...] = jnp.dot(x.astype(jnp.bfloat16), w_ref[...], preferred_element_type=jnp.float32)


def norm_matmul(x, g, w, *, normalize, tm=512, tn=None):
    m, k = x.shape
    n = w.shape[1]
    n_pad = -(-n // V7X_LANES) * V7X_LANES
    wb = w.astype(jnp.bfloat16)
    if n_pad != n:
        wb = jnp.pad(wb, ((0, 0), (0, n_pad - n)))
    if tn is None:
        tn = n_pad
    assert m % tm == 0 and n_pad % tn == 0
    out = pl.pallas_call(
        functools.partial(_norm_matmul_kernel, normalize=normalize),
        grid=(m // tm, n_pad // tn),
        in_specs=[
            pl.BlockSpec((tm, k), lambda i, j: (i, 0)),
            pl.BlockSpec((1, k), lambda i, j: (0, 0)),
            pl.BlockSpec((k, tn), lambda i, j: (0, j)),
        ],
        out_specs=pl.BlockSpec((tm, tn), lambda i, j: (i, j)),
        out_shape=jax.ShapeDtypeStruct((m, n_pad), jnp.float32),
        compiler_params=pltpu.CompilerParams(
            dimension_semantics=("arbitrary", "arbitrary"),
            vmem_limit_bytes=V7X_VMEM_LIMIT_BYTES),
        name="norm_matmul" if normalize else "matmul",
    )(x, g.reshape(1, k), wb)
    return out[:, :n] if n_pad != n else out


def _mm(x3, g, w, normalize, **kw):
    b, s, k = x3.shape
    if g is None:
        g = jnp.ones((k,), jnp.float32)
    return norm_matmul(x3.reshape(b * s, k), g, w, normalize=normalize, **kw).reshape(b, s, w.shape[1])


def rms_norm(x, g):
    xf = x.astype(jnp.float32)
    y = xf * lax.rsqrt(jnp.mean(xf * xf, axis=-1, keepdims=True) + NORM_EPS)
    return (y * g.astype(jnp.float32)).astype(x.dtype)


def rel_bucket(dist):
    dist = jnp.maximum(dist, 0)
    max_exact = REL_BUCKETS // 2
    d = jnp.maximum(dist, 1).astype(jnp.float32)
    large = max_exact + (jnp.log(d / max_exact) / math.log(REL_MAX_DISTANCE / max_exact)
                         * (REL_BUCKETS - max_exact)).astype(jnp.int32)
    large = jnp.minimum(large, REL_BUCKETS - 1)
    return jnp.where(dist < max_exact, dist, large)


SSD_GROUP_W = SSD_HEADS_PER_GROUP * SSD_HEADDIM
SSD_IN_PAD = -(-SSD_IN_DIM // V7X_LANES) * V7X_LANES
SSD_TAIL = 8


def _split3(x):
    bf16, f32 = jnp.bfloat16, jnp.float32
    hi = x.astype(bf16)
    r1 = x - hi.astype(f32)
    mid = r1.astype(bf16)
    lo = (r1 - mid.astype(f32)).astype(bf16)
    return hi, mid, lo


def _ssd_kernel(zx_ref, h_ref, cw_ref, cb_ref, dtb_ref, alog_ref, dskip_ref, gn_ref, wout_ref, tri_ref, exp_ref,
                o_ref, xs_scr, state_scr):
    f32, bf16 = jnp.float32, jnp.bfloat16
    L, N, GW = SSD_CHUNK, SSD_STATE, SSD_GROUP_W
    c = pl.program_id(1)

    @pl.when(c == 0)
    def _():
        xs_scr[0:SSD_TAIL, :] = jnp.zeros((SSD_TAIL, SSD_CONV_DIM), f32)
        state_scr[...] = jnp.zeros_like(state_scr)

    xs_scr[SSD_TAIL:SSD_TAIL + L, :] = zx_ref[:, SSD_D_INNER:SSD_D_INNER + SSD_CONV_DIM]
    conv = cb_ref[...]
    for w in range(SSD_CONV):
        conv = conv + cw_ref[w:w + 1, :] * xs_scr[pl.ds(SSD_TAIL - (SSD_CONV - 1) + w, L), :]
    xs_scr[0:SSD_TAIL, :] = xs_scr[L:L + SSD_TAIL, :]
    xbc = conv * jax.nn.sigmoid(conv)

    dt_in = zx_ref[:, SSD_D_INNER + SSD_CONV_DIM:SSD_IN_PAD] + dtb_ref[...]
    dt = jnp.maximum(dt_in, 0.0) + jnp.log1p(jnp.exp(-jnp.abs(dt_in)))
    a_dt = dt * -jnp.exp(alog_ref[...])
    a_cs = sum(jnp.dot(tri_ref[...], t, preferred_element_type=f32) for t in _split3(a_dt))
    a_csT = a_cs.T
    a_end = a_cs[L - 1:L, :]
    grow = jnp.exp(a_cs)
    to_end = jnp.exp(a_end - a_cs)

    def per_channel(q):
        return sum(jnp.dot(t, exp_ref[...], preferred_element_type=f32) for t in _split3(q)[:2])

    dt_c, grow_c, to_end_c = per_channel(dt), per_channel(grow), per_channel(to_end)
    end_c = per_channel(jnp.broadcast_to(jnp.exp(a_end), (8, V7X_LANES)))[0:1]
    dskip_c = per_channel(jnp.broadcast_to(dskip_ref[...], (8, V7X_LANES)))[0:1]

    x = xbc[:, :SSD_D_INNER]
    xd = x * dt_c
    xd_bf = xd.astype(bf16)
    xe_bf = (xd * to_end_c).astype(bf16)
    row_i = lax.broadcasted_iota(jnp.int32, (L, L), 0)
    col_i = lax.broadcasted_iota(jnp.int32, (L, L), 1)
    lane_head = lax.broadcasted_iota(jnp.int32, (L, GW), 1) // SSD_HEADDIM
    ys = []
    for g in range(SSD_GROUPS):
        bm = xbc[:, SSD_D_INNER + g * N:SSD_D_INNER + (g + 1) * N]
        cm = xbc[:, SSD_D_INNER + SSD_GROUPS * N + g * N:SSD_D_INNER + SSD_GROUPS * N + (g + 1) * N].astype(bf16)
        bmT = bm.T.astype(bf16)
        cb = jnp.dot(cm, bmT, preferred_element_type=f32)
        xd_g = xd_bf[:, g * GW:(g + 1) * GW]
        y_g = jnp.zeros((L, GW), f32)
        for r in range(SSD_HEADS_PER_GROUP):
            hd = g * SSD_HEADS_PER_GROUP + r
            seg = a_cs[:, hd:hd + 1] - a_csT[hd:hd + 1, :]
            decay = jnp.where(row_i >= col_i, jnp.exp(seg), 0.0)
            xr = jnp.where(lane_head == r, xd_g, jnp.zeros_like(xd_g))
            y_g = y_g + jnp.dot((cb * decay).astype(bf16), xr, preferred_element_type=f32)
        hT = state_scr[g]
        y_off = jnp.dot(cm, hT.astype(bf16), preferred_element_type=f32) * grow_c[:, g * GW:(g + 1) * GW]
        new_states = jnp.dot(bmT, xe_bf[:, g * GW:(g + 1) * GW], preferred_element_type=f32)
        state_scr[g] = hT * end_c[:, g * GW:(g + 1) * GW] + new_states
        ys.append(y_g + y_off)
    y = jnp.concatenate(ys, axis=1) + x * dskip_c

    z = zx_ref[:, :SSD_D_INNER]
    yz = y * (z * jax.nn.sigmoid(z))
    parts = []
    for g in range(SSD_GROUPS):
        t = yz[:, g * GW:(g + 1) * GW]
        parts.append(t * lax.rsqrt(jnp.mean(t * t, axis=-1, keepdims=True) + NORM_EPS))
    yn = jnp.concatenate(parts, axis=1) * gn_ref[...]
    o_ref[...] = h_ref[...] + jnp.dot(yn.astype(bf16), wout_ref[...], preferred_element_type=f32)


def ssd_mixer_residual(h, norm_g, w_in, conv_w, conv_b, dt_bias, a_log, d_skip, gnorm_g, w_out):
    b, s, d = h.shape
    L = SSD_CHUNK
    nc = s // L
    w_in_p = jnp.pad(w_in, ((0, 0), (0, SSD_IN_PAD - SSD_IN_DIM)))
    zx = norm_matmul(h.reshape(b * s, d), norm_g, w_in_p, normalize=True, tm=256)
    lane_pad = lambda v: jnp.pad(v.reshape(1, -1), ((0, 0), (0, V7X_LANES - v.shape[-1])))
    tri = jnp.asarray(np.tril(np.ones((L, L))), jnp.bfloat16)
    head_of = np.arange(SSD_D_INNER) // SSD_HEADDIM
    expand = jnp.asarray(np.arange(V7X_LANES)[:, None] == head_of[None, :], jnp.bfloat16)
    const = lambda *shape: pl.BlockSpec(shape, lambda bi, ci: (0,) * len(shape))
    out = pl.pallas_call(
        _ssd_kernel,
        grid=(b, nc),
        in_specs=[pl.BlockSpec((L, SSD_IN_PAD), lambda bi, ci: (bi * nc + ci, 0)),
                  pl.BlockSpec((L, d), lambda bi, ci: (bi * nc + ci, 0)),
                  const(SSD_CONV, SSD_CONV_DIM), const(1, SSD_CONV_DIM),
                  const(1, V7X_LANES), const(1, V7X_LANES), const(1, V7X_LANES),
                  const(1, SSD_D_INNER), const(SSD_D_INNER, d), const(L, L), const(V7X_LANES, SSD_D_INNER)],
        out_specs=pl.BlockSpec((L, d), lambda bi, ci: (bi * nc + ci, 0)),
        out_shape=jax.ShapeDtypeStruct((b * s, d), jnp.float32),
        scratch_shapes=[pltpu.VMEM((L + SSD_TAIL, SSD_CONV_DIM), jnp.float32),
                        pltpu.VMEM((SSD_GROUPS, SSD_STATE, SSD_GROUP_W), jnp.float32)],
        compiler_params=pltpu.CompilerParams(dimension_semantics=("arbitrary", "arbitrary"),
                                             vmem_limit_bytes=V7X_VMEM_LIMIT_BYTES),
        name="ssd_mixer",
    )(zx, h.reshape(b * s, d), conv_w, conv_b.reshape(1, -1), lane_pad(dt_bias), lane_pad(a_log), lane_pad(d_skip),
      gnorm_g.reshape(1, -1), w_out.astype(jnp.bfloat16), tri, expand)
    return out.reshape(b, s, d)


def _mlp_kernel(x_ref, w1_ref, w2_ref, o_ref):
    hid = jnp.dot(x_ref[...].astype(jnp.bfloat16), w1_ref[...], preferred_element_type=jnp.float32)
    hid = jax.nn.gelu(hid)
    o_ref[...] = jnp.dot(hid.astype(jnp.bfloat16), w2_ref[...], preferred_element_type=jnp.float32)


def mlp2(x, w1, w2, *, tm=512):
    m, k = x.shape
    hdim, n = w2.shape
    assert m % tm == 0
    return pl.pallas_call(
        _mlp_kernel,
        grid=(m // tm,),
        in_specs=[pl.BlockSpec((tm, k), lambda i: (i, 0)),
                  pl.BlockSpec((k, hdim), lambda i: (0, 0)),
                  pl.BlockSpec((hdim, n), lambda i: (0, 0))],
        out_specs=pl.BlockSpec((tm, n), lambda i: (i, 0)),
        out_shape=jax.ShapeDtypeStruct((m, n), jnp.float32),
        compiler_params=pltpu.CompilerParams(dimension_semantics=("arbitrary",),
                                             vmem_limit_bytes=V7X_VMEM_LIMIT_BYTES),
        name="cmp_mlp",
    )(x, w1.astype(jnp.bfloat16), w2.astype(jnp.bfloat16))


def compress_blocks(t, pos, w1, w2):
    b, s, g, d = t.shape
    halves = t.transpose(0, 2, 1, 3).reshape(b, g, s // CMP_STRIDE, CMP_STRIDE, d)
    nxt = jnp.concatenate([halves[:, :, 1:], jnp.zeros_like(halves[:, :, :1])], axis=2)
    blocks = jnp.concatenate([halves, nxt], axis=3) + pos
    flat = blocks.reshape(b * g * (s // CMP_STRIDE), CMP_BLOCK * d)
    return mlp2(flat, w1, w2).reshape(b, g, s // CMP_STRIDE, d)


N_SEL = SEQ // SLC_BLOCK
N_KT = SEQ // Q_BLOCK
CMP_FRONT = 256
CMP_ROWS = CMP_FRONT + SEQ // CMP_STRIDE
CMP_WIN = 256
REL_TABLE = 4096
NEAR_TILES = 24
WIN_TILES = WINDOW // Q_BLOCK + 1
SEL_GROUP = 8
QL = NSA_GROUP * Q_BLOCK


def _nsa_kernel(qT_ref, gT_ref, kc_ref, vcT_ref, kk_ref, vsT_ref, vwT_ref, at_ref, bcT_ref, bT_ref, bTw_ref,
                o_ref, s_scr, sel_scr):
    f32, bf16 = jnp.float32, jnp.bfloat16
    i = pl.program_id(2)
    qT = (qT_ref[0, 0, 0] * (NSA_HEAD_DIM ** -0.5)).astype(bf16)
    zq = jnp.zeros_like(qT)
    q_sel = jnp.concatenate([qT, zq], axis=0)
    q_win = jnp.concatenate([zq, qT], axis=0)

    r0 = pl.multiple_of(8 * i + 8, 8)
    s = jnp.dot(kc_ref[0, 0], qT, preferred_element_type=f32)
    rio = lax.broadcasted_iota(jnp.int32, (CMP_ROWS, Q_BLOCK), 0)
    rowmask = jnp.where((rio >= CMP_FRONT) & (rio < r0 + CMP_WIN), 0.0, NEG_INF)
    s_scr[...] = s + jnp.concatenate([rowmask] * NSA_GROUP, axis=1)
    s_scr[pl.ds(r0, CMP_WIN), :] = s_scr[pl.ds(r0, CMP_WIN), :] + bcT_ref[0]
    s = s_scr[...]
    m = jnp.max(s, axis=0, keepdims=True)
    p = jnp.exp(s - m)
    l = jnp.sum(p, axis=0, keepdims=True)
    inv = jnp.where(m > 0.1 * NEG_INF, 1.0 / l, 0.0)
    pn = p * inv
    o_c = jnp.dot(vcT_ref[0, 0], pn.astype(bf16), preferred_element_type=f32)

    psum = pn[:, 0:Q_BLOCK]
    for r in range(1, NSA_GROUP):
        psum = psum + pn[:, r * Q_BLOCK:(r + 1) * Q_BLOCK]
    hi = psum.astype(bf16)
    lo = (psum - hi.astype(f32)).astype(bf16)
    imp = (jnp.dot(at_ref[...], hi, preferred_element_type=f32)
           + jnp.dot(at_ref[...], lo, preferred_element_type=f32))
    blk = lax.broadcasted_iota(jnp.int32, (N_SEL, Q_BLOCK), 0)
    qi = lax.broadcasted_iota(jnp.int32, (N_SEL, Q_BLOCK), 1)
    cur = 2 * i + (qi >= SLC_BLOCK).astype(jnp.int32)
    forced = (blk == 0) | (blk == cur) | (blk == cur - 1)
    valid = blk <= cur
    score = jnp.where(forced, SLC_FORCE, jnp.where(valid, imp, -SLC_FORCE))
    blkf = blk.astype(f32)
    chosen = jnp.zeros((N_SEL, Q_BLOCK), f32)
    for _ in range(SLC_TOPN):
        mx = jnp.max(score, axis=0, keepdims=True)
        first = jnp.min(jnp.where(score == mx, blkf, float(N_SEL)), axis=0, keepdims=True)
        hit = blkf == first
        chosen = jnp.where(hit, 1.0, chosen)
        score = jnp.where(hit, -jnp.inf, score)
    sel_scr[...] = jnp.where((chosen > 0.5) & valid, 0.0, NEG_INF)

    def attend(kt, carry, q_aug, vT_ref, bias):
        m_run, l_run, acc = carry
        scs = [jnp.dot(kk_ref[0, 0, k], q_aug, preferred_element_type=f32) + b for k, b in zip(kt, bias)]
        m_new = m_run
        for sc in scs:
            m_new = jnp.maximum(m_new, jnp.max(sc, axis=0, keepdims=True))
        alpha = jnp.exp(m_run - m_new)
        l_new = alpha * l_run
        acc_new = alpha * acc
        for k, sc in zip(kt, scs):
            pt = jnp.exp(sc - m_new)
            l_new = l_new + jnp.sum(pt, axis=0, keepdims=True)
            acc_new = acc_new + jnp.dot(vT_ref[0, 0, k], pt.astype(bf16), preferred_element_type=f32)
        return m_new, l_new, acc_new

    def sel_mask(kt):
        m0 = jnp.broadcast_to(sel_scr[pl.ds(2 * kt, 1), :], (SLC_BLOCK, Q_BLOCK))
        m1 = jnp.broadcast_to(sel_scr[pl.ds(2 * kt + 1, 1), :], (SLC_BLOCK, Q_BLOCK))
        mk = jnp.concatenate([m0, m1], axis=0)
        return jnp.concatenate([mk] * NSA_GROUP, axis=1)

    init = (jnp.full((1, QL), NEG_INF, f32), jnp.zeros((1, QL), f32), jnp.zeros((NSA_HEAD_DIM, QL), f32))
    U = SEL_GROUP

    def far_group(g, c):
        kts = [g * U + u for u in range(U)]
        return attend(kts, c, q_sel, vsT_ref, [sel_mask(k) for k in kts])

    def near_group(g, c):
        kts, biases = [], []
        for u in range(U):
            delta = i - (g * U + u)
            k = jnp.minimum(g * U + u, i)
            tile = jnp.where(delta < 0, NEAR_TILES + 1, jnp.minimum(delta, NEAR_TILES))
            kts.append(k)
            biases.append(sel_mask(k) + bT_ref[0, tile])
        return attend(kts, c, q_sel, vsT_ref, biases)

    n_far = jnp.maximum((i - (NEAR_TILES - 1)) // U, 0)
    carry = lax.fori_loop(0, n_far, far_group, init)
    _, l_s, acc_s = lax.fori_loop(n_far, i // U + 1, near_group, carry)
    o_s = acc_s / l_s

    kts, biases = [], []
    for u in range(WIN_TILES):
        k = i - (WIN_TILES - 1) + u
        kts.append(jnp.maximum(k, 0))
        biases.append(bTw_ref[0, jnp.where(k >= 0, WIN_TILES - 1 - u, WIN_TILES)])
    _, l_w, acc_w = attend(kts, init, q_win, vwT_ref, biases)
    o_w = acc_w / l_w

    gate = jax.nn.sigmoid(gT_ref[0, 0, 0])
    o_ref[0, 0, 0] = gate[0:1] * o_c + gate[1:2] * o_s + gate[2:3] * o_w


BIAS_EXT = REL_TABLE + 2 * Q_BLOCK


CMP_OFF = CMP_STRIDE * (CMP_WIN - 8) - (CMP_BLOCK - 1)


def _bias_tile_kernel(bd_ref, bdc_ref, cmp_ref, near_ref, win_ref):
    row = bd_ref[0]
    kj = lax.broadcasted_iota(jnp.int32, (Q_BLOCK, Q_BLOCK), 0)
    qi = lax.broadcasted_iota(jnp.int32, (Q_BLOCK, Q_BLOCK), 1)
    for t in range(NEAR_TILES):
        seg = jnp.broadcast_to(row[:, t * Q_BLOCK:(t + 2) * Q_BLOCK], (Q_BLOCK, 2 * Q_BLOCK))
        tile = pltpu.roll(seg, 0, 1, stride=1, stride_axis=0)[:, Q_BLOCK:]
        near_ref[0, t] = tile
        if t < WIN_TILES:
            win_ref[0, t] = jnp.where(t * Q_BLOCK + qi - kj < WINDOW, tile, NEG_INF)
    near_ref[0, NEAR_TILES] = jnp.zeros((Q_BLOCK, Q_BLOCK), jnp.float32)
    near_ref[0, NEAR_TILES + 1] = jnp.full((Q_BLOCK, Q_BLOCK), NEG_INF, jnp.float32)
    win_ref[0, WIN_TILES] = jnp.full((Q_BLOCK, Q_BLOCK), NEG_INF, jnp.float32)
    full = jnp.broadcast_to(bdc_ref[0], (CMP_WIN, BIAS_EXT))
    cmp_ref[0] = pltpu.roll(full, 0, 1, stride=CMP_STRIDE, stride_axis=0)[:, REL_TABLE:REL_TABLE + Q_BLOCK]


def _bias_tables(rel_bias):
    G, R = NSA_KV_HEADS, NSA_GROUP
    bd = rel_bias[rel_bucket(jnp.arange(REL_TABLE))]
    bd = (bd - bd[REL_TABLE - 1]).T
    def padded(front):
        pad = lambda n: jnp.full((NSA_HEADS, n), NEG_INF, jnp.float32)
        return jnp.concatenate([pad(front), bd, pad(BIAS_EXT - REL_TABLE - front)], axis=1).reshape(NSA_HEADS, 1, BIAS_EXT)

    bd_ext, bd_cmp = padded(Q_BLOCK), padded(REL_TABLE - CMP_OFF)
    head = lambda *blk: pl.BlockSpec((1,) + blk + (Q_BLOCK,), lambda hd: (hd // R,) + (0,) * len(blk) + (hd % R,))
    return pl.pallas_call(
        _bias_tile_kernel,
        grid=(NSA_HEADS,),
        in_specs=[pl.BlockSpec((1, 1, BIAS_EXT), lambda hd: (hd, 0, 0))] * 2,
        out_specs=[head(CMP_WIN), head(NEAR_TILES + 2, Q_BLOCK), head(WIN_TILES + 1, Q_BLOCK)],
        out_shape=[jax.ShapeDtypeStruct((G, CMP_WIN, QL), jnp.float32),
                   jax.ShapeDtypeStruct((G, NEAR_TILES + 2, Q_BLOCK, QL), jnp.float32),
                   jax.ShapeDtypeStruct((G, WIN_TILES + 1, Q_BLOCK, QL), jnp.float32)],
        compiler_params=pltpu.CompilerParams(dimension_semantics=("arbitrary",),
                                             vmem_limit_bytes=V7X_VMEM_LIMIT_BYTES),
        name="nsa_bias_tiles",
    )(bd_ext, bd_cmp)


def _pool_matrix():
    per = SLC_BLOCK // CMP_STRIDE
    n_span = CMP_BLOCK // CMP_STRIDE
    k = np.arange(CMP_ROWS)[None, :] - CMP_FRONT
    j = np.arange(N_SEL)[:, None]
    return jnp.asarray((k >= per * j - (n_span - 1)) & (k <= per * j + per - 1), jnp.bfloat16)


def nsa_shared_kv(h, kv_norm_g, w_kv, cmp_pos_k, cmp_pos_v, cmp_w1_k, cmp_w2_k, cmp_w1_v, cmp_w2_v):
    b, s, _ = h.shape
    G, DH = NSA_KV_HEADS, NSA_HEAD_DIM
    bf16 = jnp.bfloat16
    kv = _mm(h, kv_norm_g, w_kv, True).reshape(b, s, 6, G, DH)
    k_cmp = compress_blocks(kv[:, :, 0], cmp_pos_k, cmp_w1_k, cmp_w2_k)
    v_cmp = compress_blocks(kv[:, :, 1], cmp_pos_v, cmp_w1_v, cmp_w2_v)
    front = jnp.zeros((b, G, CMP_FRONT, DH), bf16)
    kc = jnp.concatenate([front, k_cmp.astype(bf16)], axis=2)
    vcT = jnp.concatenate([front, v_cmp.astype(bf16)], axis=2).transpose(0, 1, 3, 2)
    kk = jnp.concatenate([kv[:, :, 2], kv[:, :, 4]], axis=-1).astype(bf16)
    kk = kk.transpose(0, 2, 1, 3).reshape(b, G, N_KT, Q_BLOCK, 2 * DH)

    def vt(v):
        return v.astype(bf16).transpose(0, 2, 1, 3).reshape(b, G, N_KT, Q_BLOCK, DH).transpose(0, 1, 2, 4, 3)

    return kc, vcT, kk, vt(kv[:, :, 3]), vt(kv[:, :, 5])


def nsa_mixer(h, norm_g, w_in, w_out, rel_bias, kc, vcT, kk, vsT, vwT):
    b, s, _ = h.shape
    G, R, DH = NSA_KV_HEADS, NSA_GROUP, NSA_HEAD_DIM
    nb = s // Q_BLOCK
    proj = _mm(h, norm_g, w_in, True)
    qT = proj[..., :NSA_HEADS * DH].reshape(b, nb, Q_BLOCK, G, R, DH).transpose(0, 3, 1, 5, 4, 2)
    qT = qT.reshape(b, G, nb, DH, QL)
    gT = proj[..., NSA_HEADS * DH:].reshape(b, nb, Q_BLOCK, G, R, 3).transpose(0, 3, 1, 5, 4, 2)
    gT = gT.reshape(b, G, nb, 3, QL)
    b_cmp, b_near, b_win = _bias_tables(rel_bias)
    at = _pool_matrix()

    per_bg = lambda *blk: pl.BlockSpec((1, 1) + blk, lambda bi, gi, i: (bi, gi) + (0,) * len(blk))
    per_g = lambda *blk: pl.BlockSpec((1,) + blk, lambda bi, gi, i: (gi,) + (0,) * len(blk))
    oT = pl.pallas_call(
        _nsa_kernel,
        grid=(b, G, nb),
        in_specs=[
            pl.BlockSpec((1, 1, 1, DH, QL), lambda bi, gi, i: (bi, gi, i, 0, 0)),
            pl.BlockSpec((1, 1, 1, 3, QL), lambda bi, gi, i: (bi, gi, i, 0, 0)),
            per_bg(CMP_ROWS, DH),
            per_bg(DH, CMP_ROWS),
            per_bg(N_KT, Q_BLOCK, 2 * DH),
            per_bg(N_KT, DH, Q_BLOCK),
            per_bg(N_KT, DH, Q_BLOCK),
            pl.BlockSpec((N_SEL, CMP_ROWS), lambda bi, gi, i: (0, 0)),
            per_g(CMP_WIN, QL),
            per_g(NEAR_TILES + 2, Q_BLOCK, QL),
            per_g(WIN_TILES + 1, Q_BLOCK, QL),
        ],
        out_specs=pl.BlockSpec((1, 1, 1, DH, QL), lambda bi, gi, i: (bi, gi, i, 0, 0)),
        out_shape=jax.ShapeDtypeStruct((b, G, nb, DH, QL), jnp.float32),
        scratch_shapes=[pltpu.VMEM((CMP_ROWS, QL), jnp.float32), pltpu.VMEM((N_SEL, Q_BLOCK), jnp.float32)],
        compiler_params=pltpu.CompilerParams(
            dimension_semantics=("arbitrary", "arbitrary", "arbitrary"),
            vmem_limit_bytes=V7X_VMEM_LIMIT_BYTES),
        name="nsa_attention",
    )(qT, gT, kc, vcT, kk, vsT, vwT, at, b_cmp, b_near, b_win)
    o = oT.reshape(b, G, nb, DH, R, Q_BLOCK).transpose(0, 2, 5, 1, 4, 3).reshape(b, s, NSA_HEADS * DH)
    return _mm(o, None, w_out, False)


PEER_SLOTS = PEER_HEADS * PEER_TOPK
PEER_ROUTE_TM = 256
PEER_TT = 16
PEER_CHUNKS = D_MODEL // V7X_LANES
PEER_WORDS = PEER_CHUNKS // 2
PEER_TILE_PITCH = PEER_SLOTS + 8


def _top_rows(s, rowid, n, payload=None):
    vals, picks = [], []
    for _ in range(n):
        mx = jnp.max(s, axis=0, keepdims=True)
        first = jnp.min(jnp.where(s == mx, rowid, float(s.shape[0])), axis=0, keepdims=True)
        hit = rowid == first
        vals.append(mx)
        picks.append(first if payload is None else jnp.max(jnp.where(hit, payload, -1.0), axis=0, keepdims=True))
        s = jnp.where(hit, -jnp.inf, s)
    return vals, picks


def _peer_route_kernel(h_ref, g_ref, wqT_ref, k1_ref, k2_ref, xn_ref, ids_ref, gate_ref):
    f32, bf16 = jnp.float32, jnp.bfloat16
    x = h_ref[...]
    xn = x * lax.rsqrt(jnp.mean(x * x, axis=-1, keepdims=True) + NORM_EPS) * g_ref[...]
    xn_ref[...] = xn
    qT = lax.dot_general(wqT_ref[...], xn.astype(bf16), (((1,), (1,)), ((), ())), preferred_element_type=f32)
    half = PEER_QDIM // 2
    tm = x.shape[0]
    key_id = lax.broadcasted_iota(jnp.int32, (PEER_N_KEYS, tm), 0).astype(f32)
    cand_pos = lax.broadcasted_iota(jnp.int32, (PEER_TOPK * PEER_TOPK, tm), 0).astype(f32)
    for hd in range(PEER_HEADS):
        q1 = qT[hd * PEER_QDIM: hd * PEER_QDIM + half].astype(bf16)
        q2 = qT[hd * PEER_QDIM + half: (hd + 1) * PEER_QDIM].astype(bf16)
        s1 = jnp.dot(k1_ref[hd], q1, preferred_element_type=f32)
        s2 = jnp.dot(k2_ref[hd], q2, preferred_element_type=f32)
        v1, i1 = _top_rows(s1, key_id, PEER_TOPK)
        v2, i2 = _top_rows(s2, key_id, PEER_TOPK)
        v2m = jnp.concatenate(v2, axis=0)
        i2m = jnp.concatenate(i2, axis=0)
        cand = jnp.concatenate([v1[a] + v2m for a in range(PEER_TOPK)], axis=0)
        cand_id = jnp.concatenate([i1[a] * float(PEER_N_KEYS) + i2m for a in range(PEER_TOPK)], axis=0)
        top_s, ids = _top_rows(cand, cand_pos, PEER_TOPK, payload=cand_id)
        top_s = jnp.concatenate(top_s, axis=0)
        e = jnp.exp(top_s - top_s[0:1])
        gate_ref[hd * PEER_TOPK:(hd + 1) * PEER_TOPK, :] = e / jnp.sum(e, axis=0, keepdims=True)
        ids_ref[hd * PEER_TOPK:(hd + 1) * PEER_TOPK, :] = jnp.concatenate(ids, axis=0).astype(jnp.int32)


def peer_route(h2, norm_g, w_q, keys1, keys2):
    T, d = h2.shape
    tm = PEER_ROUTE_TM
    bf16 = jnp.bfloat16
    nq = PEER_HEADS * PEER_QDIM
    half = PEER_QDIM // 2
    return pl.pallas_call(
        _peer_route_kernel,
        grid=(T // tm,),
        in_specs=[pl.BlockSpec((tm, d), lambda i: (i, 0)),
                  pl.BlockSpec((1, d), lambda i: (0, 0)),
                  pl.BlockSpec((nq, d), lambda i: (0, 0)),
                  pl.BlockSpec((PEER_HEADS, PEER_N_KEYS, half), lambda i: (0, 0, 0)),
                  pl.BlockSpec((PEER_HEADS, PEER_N_KEYS, half), lambda i: (0, 0, 0))],
        out_specs=[pl.BlockSpec((tm, d), lambda i: (i, 0)),
                   pl.BlockSpec((PEER_SLOTS, tm), lambda i: (0, i)),
                   pl.BlockSpec((PEER_SLOTS, tm), lambda i: (0, i))],
        out_shape=[jax.ShapeDtypeStruct((T, d), jnp.float32),
                   jax.ShapeDtypeStruct((PEER_SLOTS, T), jnp.int32),
                   jax.ShapeDtypeStruct((PEER_SLOTS, T), jnp.float32)],
        compiler_params=pltpu.CompilerParams(dimension_semantics=("arbitrary",),
                                             vmem_limit_bytes=V7X_VMEM_LIMIT_BYTES),
        name="peer_route",
    )(h2, norm_g.reshape(1, d), w_q.T.astype(bf16), keys1.astype(bf16), keys2.astype(bf16))


def _unpack_pair(words):
    lo = lax.bitcast_convert_type(lax.shift_left(words, 16), jnp.float32)
    hi = lax.bitcast_convert_type(lax.bitwise_and(words, -65536), jnp.float32)
    return lo, hi


def _pack_rows(tab):
    bits = lax.bitcast_convert_type(tab.astype(jnp.bfloat16), jnp.uint16).astype(jnp.uint32)
    bits = bits.reshape(tab.shape[0], PEER_WORDS, 2, V7X_LANES)
    words = lax.bitcast_convert_type(bits[:, :, 0] | (bits[:, :, 1] << 16), jnp.int32)
    return words.reshape(tab.shape[0] * PEER_WORDS, V7X_LANES)


def _peer_gather_token(ids_ref, tab, tile, t):
    for k in range(PEER_SLOTS):
        row = pl.multiple_of(ids_ref[t * PEER_SLOTS + k] * PEER_WORDS, PEER_WORDS)
        tile[pl.ds(k, PEER_WORDS, stride=PEER_TILE_PITCH), :] = tab[pl.ds(row, PEER_WORDS), :]


def _peer_chunk_pair(tile, j):
    return _unpack_pair(tile[j * PEER_TILE_PITCH:j * PEER_TILE_PITCH + PEER_SLOTS, :])


def _peer_load_table(tab_hbm, tab, sem):
    @pl.when(pl.program_id(0) == 0)
    def _():
        copy = pltpu.make_async_copy(tab_hbm, tab, sem)
        copy.start()
        copy.wait()


def _peer_act_kernel(ids_ref, xn_ref, gt_ref, tab_hbm, w_ref, tab, tile_a, tile_b, sem):
    _peer_load_table(tab_hbm, tab, sem)
    for t in range(PEER_TT):
        tile = (tile_a, tile_b)[t % 2]
        _peer_gather_token(ids_ref, tab, tile, t)
        xrow = xn_ref[t:t + 1, :]
        acc = jnp.zeros((PEER_SLOTS, V7X_LANES), jnp.float32)
        for j in range(PEER_WORDS):
            lo, hi = _peer_chunk_pair(tile, j)
            acc = acc + lo * xrow[:, 2 * j * V7X_LANES:(2 * j + 1) * V7X_LANES]
            acc = acc + hi * xrow[:, (2 * j + 1) * V7X_LANES:(2 * j + 2) * V7X_LANES]
        act = jnp.sum(acc, axis=1, keepdims=True)
        w_ref[0, :, t:t + 1] = gt_ref[0, :, t:t + 1] * jax.nn.gelu(act)


def _peer_out_kernel(ids_ref, w_ref, tab_hbm, o_ref, tab, tile_a, tile_b, sem):
    _peer_load_table(tab_hbm, tab, sem)
    for t in range(PEER_TT):
        tile = (tile_a, tile_b)[t % 2]
        _peer_gather_token(ids_ref, tab, tile, t)
        w = w_ref[0, :, t:t + 1]
        for j in range(PEER_WORDS):
            lo, hi = _peer_chunk_pair(tile, j)
            o_ref[t:t + 1, 2 * j * V7X_LANES:(2 * j + 1) * V7X_LANES] = jnp.sum(lo * w, axis=0, keepdims=True)
            o_ref[t:t + 1, (2 * j + 1) * V7X_LANES:(2 * j + 2) * V7X_LANES] = jnp.sum(hi * w, axis=0, keepdims=True)


def peer_experts(xn, idsT, gateT, u_tab, v_tab):
    T, d = xn.shape
    tt = PEER_TT
    nt = T // tt
    ids = idsT.T.reshape(T * PEER_SLOTS)
    gt = gateT.reshape(PEER_SLOTS, nt, tt).transpose(1, 0, 2)
    ids_spec = pl.BlockSpec((tt * PEER_SLOTS,), lambda i: (i,), memory_space=pltpu.SMEM)
    col_spec = pl.BlockSpec((1, PEER_SLOTS, tt), lambda i: (i, 0, 0))
    row_spec = pl.BlockSpec((tt, d), lambda i: (i, 0))
    tile = pltpu.VMEM((PEER_WORDS * PEER_TILE_PITCH, V7X_LANES), jnp.int32)
    scratch = [pltpu.VMEM((PEER_EXPERTS * PEER_WORDS, V7X_LANES), jnp.int32), tile, tile, pltpu.SemaphoreType.DMA(())]
    params = pltpu.CompilerParams(dimension_semantics=("arbitrary",), vmem_limit_bytes=V7X_VMEM_LIMIT_BYTES)
    w = pl.pallas_call(
        _peer_act_kernel,
        grid=(nt,),
        in_specs=[ids_spec, row_spec, col_spec, pl.BlockSpec(memory_space=pl.ANY)],
        out_specs=col_spec,
        out_shape=jax.ShapeDtypeStruct((nt, PEER_SLOTS, tt), jnp.float32),
        scratch_shapes=scratch, compiler_params=params, name="peer_act",
    )(ids, xn, gt, _pack_rows(u_tab))
    return pl.pallas_call(
        _peer_out_kernel,
        grid=(nt,),
        in_specs=[ids_spec, col_spec, pl.BlockSpec(memory_space=pl.ANY)],
        out_specs=row_spec,
        out_shape=jax.ShapeDtypeStruct((T, d), jnp.float32),
        scratch_shapes=scratch, compiler_params=params, name="peer_out",
    )(ids, w, _pack_rows(v_tab))


def peer_ffn(h, norm_g, w_q, keys1, keys2, u_tab, v_tab):
    b, s, d = h.shape
    xn, idsT, gateT = peer_route(h.reshape(b * s, d), norm_g, w_q, keys1, keys2)
    return peer_experts(xn, idsT, gateT, u_tab, v_tab).reshape(b, s, d)


def per_layer_embed(h, p_i, norm_g, w_up, w_gate):
    return _mm(p_i, None, w_up, False) * jax.nn.sigmoid(_mm(h, norm_g, w_gate, True))


def kernel(x, p, a_norm_g, a_w_in, a_conv_w, a_conv_b, a_dt_bias, a_log, a_d_skip, a_gnorm_g, a_w_out,
           kv_norm_g, w_kv, cmp_pos_k, cmp_pos_v, cmp_w1_k, cmp_w2_k, cmp_w1_v, cmp_w2_v, rel_bias,
           b_norm_g, b_w_in, b_w_out, c_norm_g, c_w_q, c_keys1, c_keys2, c_u, c_v,
           e_norm_g, e_w_up, e_w_gate, final_g):
    h = x
    shared = None
    for i in range(DEPTH):
        if i < N_A_LAYERS:
            h = ssd_mixer_residual(h, a_norm_g[i], a_w_in[i], a_conv_w[i], a_conv_b[i], a_dt_bias[i],
                                   a_log[i], a_d_skip[i], a_gnorm_g[i], a_w_out[i])
        else:
            if shared is None:
                shared = nsa_shared_kv(h, kv_norm_g, w_kv, cmp_pos_k, cmp_pos_v,
                                       cmp_w1_k, cmp_w2_k, cmp_w1_v, cmp_w2_v)
            j = i - N_A_LAYERS
            h = h + nsa_mixer(h, b_norm_g[j], b_w_in[j], b_w_out[j], rel_bias, *shared)
        h = h + peer_ffn(h, c_norm_g[i], c_w_q[i], c_keys1[i], c_keys2[i], c_u[i], c_v[i])
        h = h + per_layer_embed(h, p[i], e_norm_g[i], e_w_up[i], e_w_gate[i])
    return rms_norm(h, final_g)
```

```python
import functools
import math

import jax
import jax.numpy as jnp
import numpy as np
from jax import lax
from jax.experimental import pallas as pl
from jax.experimental.pallas import tpu as pltpu

D_MODEL = 1024
BATCH = 2
SEQ = 16384
DEPTH = 2
N_A_LAYERS = DEPTH // 2
N_B_LAYERS = DEPTH - N_A_LAYERS

SSD_D_INNER = 2 * D_MODEL
SSD_HEADDIM = 64
SSD_HEADS = SSD_D_INNER // SSD_HEADDIM
SSD_GROUPS = 8
SSD_HEADS_PER_GROUP = SSD_HEADS // SSD_GROUPS
SSD_STATE = 128
SSD_CONV = 4
SSD_CHUNK = 128
SSD_CONV_DIM = SSD_D_INNER + 2 * SSD_GROUPS * SSD_STATE
SSD_IN_DIM = SSD_D_INNER + SSD_CONV_DIM + SSD_HEADS

NSA_HEADS = 16
NSA_KV_HEADS = 4
NSA_GROUP = NSA_HEADS // NSA_KV_HEADS
NSA_HEAD_DIM = 64
CMP_BLOCK = 32
CMP_STRIDE = 16
CMP_HIDDEN = 256
SLC_BLOCK = 64
SLC_TOPN = 16
SLC_FORCE = 1e4
WINDOW = 512
Q_BLOCK = 128
NSA_IN_DIM = NSA_HEADS * NSA_HEAD_DIM + 3 * NSA_HEADS
KV_DIM = 6 * NSA_KV_HEADS * NSA_HEAD_DIM

REL_BUCKETS = 32
REL_MAX_DISTANCE = 4096

PEER_HEADS = 8
PEER_N_KEYS = 128
PEER_EXPERTS = PEER_N_KEYS * PEER_N_KEYS
PEER_QDIM = 256
PEER_TOPK = 16
PEER_TOKEN_CHUNK = 128
PEER_V_SCALE = PEER_HEADS ** -0.5

PLE_DIM = 256
NORM_EPS = 1e-6
NEG_INF = -1e30

V7X_LANES = 128
V7X_VMEM_LIMIT_BYTES = 56 * 1024 * 1024


def _norm_matmul_kernel(x_ref, g_ref, w_ref, o_ref, *, normalize):
    x = x_ref[...]
    if normalize:
        x = x * lax.rsqrt(jnp.mean(x * x, axis=-1, keepdims=True) + NORM_EPS) * g_ref[...]
    o_ref[...] = jnp.dot(x.astype(jnp.bfloat16), w_ref[...], preferred_element_type=jnp.float32)


def norm_matmul(x, g, w, *, normalize, tm=512, tn=None):
    m, k = x.shape
    n = w.shape[1]
    n_pad = -(-n // V7X_LANES) * V7X_LANES
    wb = w.astype(jnp.bfloat16)
    if n_pad != n:
        wb = jnp.pad(wb, ((0, 0), (0, n_pad - n)))
    if tn is None:
        tn = n_pad
    assert m % tm == 0 and n_pad % tn == 0
    out = pl.pallas_call(
        functools.partial(_norm_matmul_kernel, normalize=normalize),
        grid=(m // tm, n_pad // tn),
        in_specs=[
            pl.BlockSpec((tm, k), lambda i, j: (i, 0)),
            pl.BlockSpec((1, k), lambda i, j: (0, 0)),
            pl.BlockSpec((k, tn), lambda i, j: (0, j)),
        ],
        out_specs=pl.BlockSpec((tm, tn), lambda i, j: (i, j)),
        out_shape=jax.ShapeDtypeStruct((m, n_pad), jnp.float32),
        compiler_params=pltpu.CompilerParams(
            dimension_semantics=("arbitrary", "arbitrary"),
            vmem_limit_bytes=V7X_VMEM_LIMIT_BYTES),
        name="norm_matmul" if normalize else "matmul",
    )(x, g.reshape(1, k), wb)
    return out[:, :n] if n_pad != n else out


def _mm(x3, g, w, normalize, **kw):
    b, s, k = x3.shape
    if g is None:
        g = jnp.ones((k,), jnp.float32)
    return norm_matmul(x3.reshape(b * s, k), g, w, normalize=normalize, **kw).reshape(b, s, w.shape[1])


def rms_norm(x, g):
    xf = x.astype(jnp.float32)
    y = xf * lax.rsqrt(jnp.mean(xf * xf, axis=-1, keepdims=True) + NORM_EPS)
    return (y * g.astype(jnp.float32)).astype(x.dtype)


def rel_bucket(dist):
    dist = jnp.maximum(dist, 0)
    max_exact = REL_BUCKETS // 2
    d = jnp.maximum(dist, 1).astype(jnp.float32)
    large = max_exact + (jnp.log(d / max_exact) / math.log(REL_MAX_DISTANCE / max_exact)
                         * (REL_BUCKETS - max_exact)).astype(jnp.int32)
    large = jnp.minimum(large, REL_BUCKETS - 1)
    return jnp.where(dist < max_exact, dist, large)


SSD_GROUP_W = SSD_HEADS_PER_GROUP * SSD_HEADDIM
SSD_IN_PAD = -(-SSD_IN_DIM // V7X_LANES) * V7X_LANES
SSD_TAIL = 8


def _split3(x):
    bf16, f32 = jnp.bfloat16, jnp.float32
    hi = x.astype(bf16)
    r1 = x - hi.astype(f32)
    mid = r1.astype(bf16)
    lo = (r1 - mid.astype(f32)).astype(bf16)
    return hi, mid, lo


def _ssd_kernel(zx_ref, h_ref, cw_ref, cb_ref, dtb_ref, alog_ref, dskip_ref, gn_ref, wout_ref, tri_ref, exp_ref,
                o_ref, xs_scr, state_scr):
    f32, bf16 = jnp.float32, jnp.bfloat16
    L, N, GW = SSD_CHUNK, SSD_STATE, SSD_GROUP_W
    c = pl.program_id(1)

    @pl.when(c == 0)
    def _():
        xs_scr[0:SSD_TAIL, :] = jnp.zeros((SSD_TAIL, SSD_CONV_DIM), f32)
        state_scr[...] = jnp.zeros_like(state_scr)

    xs_scr[SSD_TAIL:SSD_TAIL + L, :] = zx_ref[:, SSD_D_INNER:SSD_D_INNER + SSD_CONV_DIM]
    conv = cb_ref[...]
    for w in range(SSD_CONV):
        conv = conv + cw_ref[w:w + 1, :] * xs_scr[pl.ds(SSD_TAIL - (SSD_CONV - 1) + w, L), :]
    xs_scr[0:SSD_TAIL, :] = xs_scr[L:L + SSD_TAIL, :]
    xbc = conv * jax.nn.sigmoid(conv)

    dt_in = zx_ref[:, SSD_D_INNER + SSD_CONV_DIM:SSD_IN_PAD] + dtb_ref[...]
    dt = jnp.maximum(dt_in, 0.0) + jnp.log1p(jnp.exp(-jnp.abs(dt_in)))
    a_dt = dt * -jnp.exp(alog_ref[...])
    a_cs = sum(jnp.dot(tri_ref[...], t, preferred_element_type=f32) for t in _split3(a_dt))
    a_csT = a_cs.T
    a_end = a_cs[L - 1:L, :]
    grow = jnp.exp(a_cs)
    to_end = jnp.exp(a_end - a_cs)

    def per_channel(q):
        return sum(jnp.dot(t, exp_ref[...], preferred_element_type=f32) for t in _split3(q)[:2])

    dt_c, grow_c, to_end_c = per_channel(dt), per_channel(grow), per_channel(to_end)
    end_c = per_channel(jnp.broadcast_to(jnp.exp(a_end), (8, V7X_LANES)))[0:1]
    dskip_c = per_channel(jnp.broadcast_to(dskip_ref[...], (8, V7X_LANES)))[0:1]

    x = xbc[:, :SSD_D_INNER]
    xd = x * dt_c
    xd_bf = xd.astype(bf16)
    xe_bf = (xd * to_end_c).astype(bf16)
    row_i = lax.broadcasted_iota(jnp.int32, (L, L), 0)
    col_i = lax.broadcasted_iota(jnp.int32, (L, L), 1)
    lane_head = lax.broadcasted_iota(jnp.int32, (L, GW), 1) // SSD_HEADDIM
    ys = []
    for g in range(SSD_GROUPS):
        bm = xbc[:, SSD_D_INNER + g * N:SSD_D_INNER + (g + 1) * N]
        cm = xbc[:, SSD_D_INNER + SSD_GROUPS * N + g * N:SSD_D_INNER + SSD_GROUPS * N + (g + 1) * N].astype(bf16)
        bmT = bm.T.astype(bf16)
        cb = jnp.dot(cm, bmT, preferred_element_type=f32)
        xd_g = xd_bf[:, g * GW:(g + 1) * GW]
        y_g = jnp.zeros((L, GW), f32)
        for r in range(SSD_HEADS_PER_GROUP):
            hd = g * SSD_HEADS_PER_GROUP + r
            seg = a_cs[:, hd:hd + 1] - a_csT[hd:hd + 1, :]
            decay = jnp.where(row_i >= col_i, jnp.exp(seg), 0.0)
            xr = jnp.where(lane_head == r, xd_g, jnp.zeros_like(xd_g))
            y_g = y_g + jnp.dot((cb * decay).astype(bf16), xr, preferred_element_type=f32)
        hT = state_scr[g]
        y_off = jnp.dot(cm, hT.astype(bf16), preferred_element_type=f32) * grow_c[:, g * GW:(g + 1) * GW]
        new_states = jnp.dot(bmT, xe_bf[:, g * GW:(g + 1) * GW], preferred_element_type=f32)
        state_scr[g] = hT * end_c[:, g * GW:(g + 1) * GW] + new_states
        ys.append(y_g + y_off)
    y = jnp.concatenate(ys, axis=1) + x * dskip_c

    z = zx_ref[:, :SSD_D_INNER]
    yz = y * (z * jax.nn.sigmoid(z))
    parts = []
    for g in range(SSD_GROUPS):
        t = yz[:, g * GW:(g + 1) * GW]
        parts.append(t * lax.rsqrt(jnp.mean(t * t, axis=-1, keepdims=True) + NORM_EPS))
    yn = jnp.concatenate(parts, axis=1) * gn_ref[...]
    o_ref[...] = h_ref[...] + jnp.dot(yn.astype(bf16), wout_ref[...], preferred_element_type=f32)


def ssd_mixer_residual(h, norm_g, w_in, conv_w, conv_b, dt_bias, a_log, d_skip, gnorm_g, w_out):
    b, s, d = h.shape
    L = SSD_CHUNK
    nc = s // L
    w_in_p = jnp.pad(w_in, ((0, 0), (0, SSD_IN_PAD - SSD_IN_DIM)))
    zx = norm_matmul(h.reshape(b * s, d), norm_g, w_in_p, normalize=True, tm=256)
    lane_pad = lambda v: jnp.pad(v.reshape(1, -1), ((0, 0), (0, V7X_LANES - v.shape[-1])))
    tri = jnp.asarray(np.tril(np.ones((L, L))), jnp.bfloat16)
    head_of = np.arange(SSD_D_INNER) // SSD_HEADDIM
    expand = jnp.asarray(np.arange(V7X_LANES)[:, None] == head_of[None, :], jnp.bfloat16)
    const = lambda *shape: pl.BlockSpec(shape, lambda bi, ci: (0,) * len(shape))
    out = pl.pallas_call(
        _ssd_kernel,
        grid=(b, nc),
        in_specs=[pl.BlockSpec((L, SSD_IN_PAD), lambda bi, ci: (bi * nc + ci, 0)),
                  pl.BlockSpec((L, d), lambda bi, ci: (bi * nc + ci, 0)),
                  const(SSD_CONV, SSD_CONV_DIM), const(1, SSD_CONV_DIM),
                  const(1, V7X_LANES), const(1, V7X_LANES), const(1, V7X_LANES),
                  const(1, SSD_D_INNER), const(SSD_D_INNER, d), const(L, L), const(V7X_LANES, SSD_D_INNER)],
        out_specs=pl.BlockSpec((L, d), lambda bi, ci: (bi * nc + ci, 0)),
        out_shape=jax.ShapeDtypeStruct((b * s, d), jnp.float32),
        scratch_shapes=[pltpu.VMEM((L + SSD_TAIL, SSD_CONV_DIM), jnp.float32),
                        pltpu.VMEM((SSD_GROUPS, SSD_STATE, SSD_GROUP_W), jnp.float32)],
        compiler_params=pltpu.CompilerParams(dimension_semantics=("arbitrary", "arbitrary"),
                                             vmem_limit_bytes=V7X_VMEM_LIMIT_BYTES),
        name="ssd_mixer",
    )(zx, h.reshape(b * s, d), conv_w, conv_b.reshape(1, -1), lane_pad(dt_bias), lane_pad(a_log), lane_pad(d_skip),
      gnorm_g.reshape(1, -1), w_out.astype(jnp.bfloat16), tri, expand)
    return out.reshape(b, s, d)


def _mlp_kernel(x_ref, w1_ref, w2_ref, o_ref):
    hid = jnp.dot(x_ref[...].astype(jnp.bfloat16), w1_ref[...], preferred_element_type=jnp.float32)
    hid = jax.nn.gelu(hid)
    o_ref[...] = jnp.dot(hid.astype(jnp.bfloat16), w2_ref[...], preferred_element_type=jnp.float32)


def mlp2(x, w1, w2, *, tm=512):
    m, k = x.shape
    hdim, n = w2.shape
    assert m % tm == 0
    return pl.pallas_call(
        _mlp_kernel,
        grid=(m // tm,),
        in_specs=[pl.BlockSpec((tm, k), lambda i: (i, 0)),
                  pl.BlockSpec((k, hdim), lambda i: (0, 0)),
                  pl.BlockSpec((hdim, n), lambda i: (0, 0))],
        out_specs=pl.BlockSpec((tm, n), lambda i: (i, 0)),
        out_shape=jax.ShapeDtypeStruct((m, n), jnp.float32),
        compiler_params=pltpu.CompilerParams(dimension_semantics=("arbitrary",),
                                             vmem_limit_bytes=V7X_VMEM_LIMIT_BYTES),
        name="cmp_mlp",
    )(x, w1.astype(jnp.bfloat16), w2.astype(jnp.bfloat16))


def compress_blocks(t, pos, w1, w2):
    b, s, g, d = t.shape
    halves = t.transpose(0, 2, 1, 3).reshape(b, g, s // CMP_STRIDE, CMP_STRIDE, d)
    nxt = jnp.concatenate([halves[:, :, 1:], jnp.zeros_like(halves[:, :, :1])], axis=2)
    blocks = jnp.concatenate([halves, nxt], axis=3) + pos
    flat = blocks.reshape(b * g * (s // CMP_STRIDE), CMP_BLOCK * d)
    return mlp2(flat, w1, w2).reshape(b, g, s // CMP_STRIDE, d)


N_SEL = SEQ // SLC_BLOCK
N_KT = SEQ // Q_BLOCK
CMP_FRONT = 256
CMP_ROWS = CMP_FRONT + SEQ // CMP_STRIDE
CMP_WIN = 256
REL_TABLE = 4096
NEAR_TILES = 24
WIN_TILES = WINDOW // Q_BLOCK + 1
SEL_GROUP = 8
QL = NSA_GROUP * Q_BLOCK


def _nsa_kernel(qT_ref, gT_ref, kc_ref, vcT_ref, kk_ref, vsT_ref, vwT_ref, at_ref, bcT_ref, bT_ref, bTw_ref,
                o_ref, s_scr, sel_scr):
    f32, bf16 = jnp.float32, jnp.bfloat16
    i = pl.program_id(2)
    qT = (qT_ref[0, 0, 0] * (NSA_HEAD_DIM ** -0.5)).astype(bf16)
    zq = jnp.zeros_like(qT)
    q_sel = jnp.concatenate([qT, zq], axis=0)
    q_win = jnp.concatenate([zq, qT], axis=0)

    r0 = pl.multiple_of(8 * i + 8, 8)
    s = jnp.dot(kc_ref[0, 0], qT, preferred_element_type=f32)
    rio = lax.broadcasted_iota(jnp.int32, (CMP_ROWS, Q_BLOCK), 0)
    rowmask = jnp.where((rio >= CMP_FRONT) & (rio < r0 + CMP_WIN), 0.0, NEG_INF)
    s_scr[...] = s + jnp.concatenate([rowmask] * NSA_GROUP, axis=1)
    s_scr[pl.ds(r0, CMP_WIN), :] = s_scr[pl.ds(r0, CMP_WIN), :] + bcT_ref[0]
    s = s_scr[...]
    m = jnp.max(s, axis=0, keepdims=True)
    p = jnp.exp(s - m)
    l = jnp.sum(p, axis=0, keepdims=True)
    inv = jnp.where(m > 0.1 * NEG_INF, 1.0 / l, 0.0)
    pn = p * inv
    o_c = jnp.dot(vcT_ref[0, 0], pn.astype(bf16), preferred_element_type=f32)

    psum = pn[:, 0:Q_BLOCK]
    for r in range(1, NSA_GROUP):
        psum = psum + pn[:, r * Q_BLOCK:(r + 1) * Q_BLOCK]
    hi = psum.astype(bf16)
    lo = (psum - hi.astype(f32)).astype(bf16)
    imp = (jnp.dot(at_ref[...], hi, preferred_element_type=f32)
           + jnp.dot(at_ref[...], lo, preferred_element_type=f32))
    blk = lax.broadcasted_iota(jnp.int32, (N_SEL, Q_BLOCK), 0)
    qi = lax.broadcasted_iota(jnp.int32, (N_SEL, Q_BLOCK), 1)
    cur = 2 * i + (qi >= SLC_BLOCK).astype(jnp.int32)
    forced = (blk == 0) | (blk == cur) | (blk == cur - 1)
    valid = blk <= cur
    score = jnp.where(forced, SLC_FORCE, jnp.where(valid, imp, -SLC_FORCE))
    blkf = blk.astype(f32)
    chosen = jnp.zeros((N_SEL, Q_BLOCK), f32)
    for _ in range(SLC_TOPN):
        mx = jnp.max(score, axis=0, keepdims=True)
        first = jnp.min(jnp.where(score == mx, blkf, float(N_SEL)), axis=0, keepdims=True)
        hit = blkf == first
        chosen = jnp.where(hit, 1.0, chosen)
        score = jnp.where(hit, -jnp.inf, score)
    sel_scr[...] = jnp.where((chosen > 0.5) & valid, 0.0, NEG_INF)

    def attend(kt, carry, q_aug, vT_ref, bias):
        m_run, l_run, acc = carry
        scs = [jnp.dot(kk_ref[0, 0, k], q_aug, preferred_element_type=f32) + b for k, b in zip(kt, bias)]
        for k, sc in zip(kt, scs):
            m_new = jnp.maximum(m_run, jnp.max(sc, axis=0, keepdims=True))
            alpha = jnp.exp(m_run - m_new)
            pt = jnp.exp(sc - m_new)
            l_run = alpha * l_run + jnp.sum(pt, axis=0, keepdims=True)
            acc = alpha * acc + jnp.dot(vT_ref[0, 0, k], pt.astype(bf16), preferred_element_type=f32)
            m_run = m_new
        return m_run, l_run, acc

    def sel_mask(kt):
        m0 = jnp.broadcast_to(sel_scr[pl.ds(2 * kt, 1), :], (SLC_BLOCK, Q_BLOCK))
        m1 = jnp.broadcast_to(sel_scr[pl.ds(2 * kt + 1, 1), :], (SLC_BLOCK, Q_BLOCK))
        mk = jnp.concatenate([m0, m1], axis=0)
        return jnp.concatenate([mk] * NSA_GROUP, axis=1)

    init = (jnp.full((1, QL), NEG_INF, f32), jnp.zeros((1, QL), f32), jnp.zeros((NSA_HEAD_DIM, QL), f32))
    U = SEL_GROUP

    def far_group(g, c):
        kts = [g * U + u for u in range(U)]
        return attend(kts, c, q_sel, vsT_ref, [sel_mask(k) for k in kts])

    def near_group(g, c):
        kts, biases = [], []
        for u in range(U):
            delta = i - (g * U + u)
            k = jnp.minimum(g * U + u, i)
            tile = jnp.where(delta < 0, NEAR_TILES + 1, jnp.minimum(delta, NEAR_TILES))
            kts.append(k)
            biases.append(sel_mask(k) + bT_ref[0, tile])
        return attend(kts, c, q_sel, vsT_ref, biases)

    n_far = jnp.maximum((i - (NEAR_TILES - 1)) // U, 0)
    carry = lax.fori_loop(0, n_far, far_group, init)
    _, l_s, acc_s = lax.fori_loop(n_far, i // U + 1, near_group, carry)
    o_s = acc_s / l_s

    kts, biases = [], []
    for u in range(WIN_TILES):
        k = i - (WIN_TILES - 1) + u
        kts.append(jnp.maximum(k, 0))
        biases.append(bTw_ref[0, jnp.where(k >= 0, WIN_TILES - 1 - u, WIN_TILES)])
    _, l_w, acc_w = attend(kts, init, q_win, vwT_ref, biases)
    o_w = acc_w / l_w

    gate = jax.nn.sigmoid(gT_ref[0, 0, 0])
    o_ref[0, 0, 0] = gate[0:1] * o_c + gate[1:2] * o_s + gate[2:3] * o_w


BIAS_EXT = REL_TABLE + 2 * Q_BLOCK


CMP_OFF = CMP_STRIDE * (CMP_WIN - 8) - (CMP_BLOCK - 1)


def _bias_tile_kernel(bd_ref, bdc_ref, cmp_ref, near_ref, win_ref):
    row = bd_ref[0]
    kj = lax.broadcasted_iota(jnp.int32, (Q_BLOCK, Q_BLOCK), 0)
    qi = lax.broadcasted_iota(jnp.int32, (Q_BLOCK, Q_BLOCK), 1)
    for t in range(NEAR_TILES):
        seg = jnp.broadcast_to(row[:, t * Q_BLOCK:(t + 2) * Q_BLOCK], (Q_BLOCK, 2 * Q_BLOCK))
        tile = pltpu.roll(seg, 0, 1, stride=1, stride_axis=0)[:, Q_BLOCK:]
        near_ref[0, t] = tile
        if t < WIN_TILES:
            win_ref[0, t] = jnp.where(t * Q_BLOCK + qi - kj < WINDOW, tile, NEG_INF)
    near_ref[0, NEAR_TILES] = jnp.zeros((Q_BLOCK, Q_BLOCK), jnp.float32)
    near_ref[0, NEAR_TILES + 1] = jnp.full((Q_BLOCK, Q_BLOCK), NEG_INF, jnp.float32)
    win_ref[0, WIN_TILES] = jnp.full((Q_BLOCK, Q_BLOCK), NEG_INF, jnp.float32)
    full = jnp.broadcast_to(bdc_ref[0], (CMP_WIN, BIAS_EXT))
    cmp_ref[0] = pltpu.roll(full, 0, 1, stride=CMP_STRIDE, stride_axis=0)[:, REL_TABLE:REL_TABLE + Q_BLOCK]


def _bias_tables(rel_bias):
    G, R = NSA_KV_HEADS, NSA_GROUP
    bd = rel_bias[rel_bucket(jnp.arange(REL_TABLE))]
    bd = (bd - bd[REL_TABLE - 1]).T
    def padded(front):
        pad = lambda n: jnp.full((NSA_HEADS, n), NEG_INF, jnp.float32)
        return jnp.concatenate([pad(front), bd, pad(BIAS_EXT - REL_TABLE - front)], axis=1).reshape(NSA_HEADS, 1, BIAS_EXT)

    bd_ext, bd_cmp = padded(Q_BLOCK), padded(REL_TABLE - CMP_OFF)
    head = lambda *blk: pl.BlockSpec((1,) + blk + (Q_BLOCK,), lambda hd: (hd // R,) + (0,) * len(blk) + (hd % R,))
    return pl.pallas_call(
        _bias_tile_kernel,
        grid=(NSA_HEADS,),
        in_specs=[pl.BlockSpec((1, 1, BIAS_EXT), lambda hd: (hd, 0, 0))] * 2,
        out_specs=[head(CMP_WIN), head(NEAR_TILES + 2, Q_BLOCK), head(WIN_TILES + 1, Q_BLOCK)],
        out_shape=[jax.ShapeDtypeStruct((G, CMP_WIN, QL), jnp.float32),
                   jax.ShapeDtypeStruct((G, NEAR_TILES + 2, Q_BLOCK, QL), jnp.float32),
                   jax.ShapeDtypeStruct((G, WIN_TILES + 1, Q_BLOCK, QL), jnp.float32)],
        compiler_params=pltpu.CompilerParams(dimension_semantics=("arbitrary",),
                                             vmem_limit_bytes=V7X_VMEM_LIMIT_BYTES),
        name="nsa_bias_tiles",
    )(bd_ext, bd_cmp)


def _pool_matrix():
    per = SLC_BLOCK // CMP_STRIDE
    n_span = CMP_BLOCK // CMP_STRIDE
    k = np.arange(CMP_ROWS)[None, :] - CMP_FRONT
    j = np.arange(N_SEL)[:, None]
    return jnp.asarray((k >= per * j - (n_span - 1)) & (k <= per * j + per - 1), jnp.bfloat16)


def nsa_shared_kv(h, kv_norm_g, w_kv, cmp_pos_k, cmp_pos_v, cmp_w1_k, cmp_w2_k, cmp_w1_v, cmp_w2_v):
    b, s, _ = h.shape
    G, DH = NSA_KV_HEADS, NSA_HEAD_DIM
    bf16 = jnp.bfloat16
    kv = _mm(h, kv_norm_g, w_kv, True).reshape(b, s, 6, G, DH)
    k_cmp = compress_blocks(kv[:, :, 0], cmp_pos_k, cmp_w1_k, cmp_w2_k)
    v_cmp = compress_blocks(kv[:, :, 1], cmp_pos_v, cmp_w1_v, cmp_w2_v)
    front = jnp.zeros((b, G, CMP_FRONT, DH), bf16)
    kc = jnp.concatenate([front, k_cmp.astype(bf16)], axis=2)
    vcT = jnp.concatenate([front, v_cmp.astype(bf16)], axis=2).transpose(0, 1, 3, 2)
    kk = jnp.concatenate([kv[:, :, 2], kv[:, :, 4]], axis=-1).astype(bf16)
    kk = kk.transpose(0, 2, 1, 3).reshape(b, G, N_KT, Q_BLOCK, 2 * DH)

    def vt(v):
        return v.astype(bf16).transpose(0, 2, 1, 3).reshape(b, G, N_KT, Q_BLOCK, DH).transpose(0, 1, 2, 4, 3)

    return kc, vcT, kk, vt(kv[:, :, 3]), vt(kv[:, :, 5])


def nsa_mixer(h, norm_g, w_in, w_out, rel_bias, kc, vcT, kk, vsT, vwT):
    b, s, _ = h.shape
    G, R, DH = NSA_KV_HEADS, NSA_GROUP, NSA_HEAD_DIM
    nb = s // Q_BLOCK
    proj = _mm(h, norm_g, w_in, True)
    qT = proj[..., :NSA_HEADS * DH].reshape(b, nb, Q_BLOCK, G, R, DH).transpose(0, 3, 1, 5, 4, 2)
    qT = qT.reshape(b, G, nb, DH, QL)
    gT = proj[..., NSA_HEADS * DH:].reshape(b, nb, Q_BLOCK, G, R, 3).transpose(0, 3, 1, 5, 4, 2)
    gT = gT.reshape(b, G, nb, 3, QL)
    b_cmp, b_near, b_win = _bias_tables(rel_bias)
    at = _pool_matrix()

    per_bg = lambda *blk: pl.BlockSpec((1, 1) + blk, lambda bi, gi, i: (bi, gi) + (0,) * len(blk))
    per_g = lambda *blk: pl.BlockSpec((1,) + blk, lambda bi, gi, i: (gi,) + (0,) * len(blk))
    oT = pl.pallas_call(
        _nsa_kernel,
        grid=(b, G, nb),
        in_specs=[
            pl.BlockSpec((1, 1, 1, DH, QL), lambda bi, gi, i: (bi, gi, i, 0, 0)),
            pl.BlockSpec((1, 1, 1, 3, QL), lambda bi, gi, i: (bi, gi, i, 0, 0)),
            per_bg(CMP_ROWS, DH),
            per_bg(DH, CMP_ROWS),
            per_bg(N_KT, Q_BLOCK, 2 * DH),
            per_bg(N_KT, DH, Q_BLOCK),
            per_bg(N_KT, DH, Q_BLOCK),
            pl.BlockSpec((N_SEL, CMP_ROWS), lambda bi, gi, i: (0, 0)),
            per_g(CMP_WIN, QL),
            per_g(NEAR_TILES + 2, Q_BLOCK, QL),
            per_g(WIN_TILES + 1, Q_BLOCK, QL),
        ],
        out_specs=pl.BlockSpec((1, 1, 1, DH, QL), lambda bi, gi, i: (bi, gi, i, 0, 0)),
        out_shape=jax.ShapeDtypeStruct((b, G, nb, DH, QL), jnp.float32),
        scratch_shapes=[pltpu.VMEM((CMP_ROWS, QL), jnp.float32), pltpu.VMEM((N_SEL, Q_BLOCK), jnp.float32)],
        compiler_params=pltpu.CompilerParams(
            dimension_semantics=("arbitrary", "arbitrary", "arbitrary"),
            vmem_limit_bytes=V7X_VMEM_LIMIT_BYTES),
        name="nsa_attention",
    )(qT, gT, kc, vcT, kk, vsT, vwT, at, b_cmp, b_near, b_win)
    o = oT.reshape(b, G, nb, DH, R, Q_BLOCK).transpose(0, 2, 5, 1, 4, 3).reshape(b, s, NSA_HEADS * DH)
    return _mm(o, None, w_out, False)


PEER_SLOTS = PEER_HEADS * PEER_TOPK
PEER_ROUTE_TM = 256
PEER_TT = 16
PEER_CHUNKS = D_MODEL // V7X_LANES
PEER_WORDS = PEER_CHUNKS // 2
PEER_TILE_PITCH = PEER_SLOTS + 8


def _top_rows(s, rowid, n, payload=None):
    vals, picks = [], []
    for _ in range(n):
        mx = jnp.max(s, axis=0, keepdims=True)
        first = jnp.min(jnp.where(s == mx, rowid, float(s.shape[0])), axis=0, keepdims=True)
        hit = rowid == first
        vals.append(mx)
        picks.append(first if payload is None else jnp.max(jnp.where(hit, payload, -1.0), axis=0, keepdims=True))
        s = jnp.where(hit, -jnp.inf, s)
    return vals, picks


def _peer_route_kernel(h_ref, g_ref, wqT_ref, k1_ref, k2_ref, xn_ref, ids_ref, gate_ref):
    f32, bf16 = jnp.float32, jnp.bfloat16
    x = h_ref[...]
    xn = x * lax.rsqrt(jnp.mean(x * x, axis=-1, keepdims=True) + NORM_EPS) * g_ref[...]
    xn_ref[...] = xn
    qT = lax.dot_general(wqT_ref[...], xn.astype(bf16), (((1,), (1,)), ((), ())), preferred_element_type=f32)
    half = PEER_QDIM // 2
    tm = x.shape[0]
    key_id = lax.broadcasted_iota(jnp.int32, (PEER_N_KEYS, tm), 0).astype(f32)
    cand_pos = lax.broadcasted_iota(jnp.int32, (PEER_TOPK * PEER_TOPK, tm), 0).astype(f32)
    for hd in range(PEER_HEADS):
        q1 = qT[hd * PEER_QDIM: hd * PEER_QDIM + half].astype(bf16)
        q2 = qT[hd * PEER_QDIM + half: (hd + 1) * PEER_QDIM].astype(bf16)
        s1 = jnp.dot(k1_ref[hd], q1, preferred_element_type=f32)
        s2 = jnp.dot(k2_ref[hd], q2, preferred_element_type=f32)
        v1, i1 = _top_rows(s1, key_id, PEER_TOPK)
        v2, i2 = _top_rows(s2, key_id, PEER_TOPK)
        v2m = jnp.concatenate(v2, axis=0)
        i2m = jnp.concatenate(i2, axis=0)
        cand = jnp.concatenate([v1[a] + v2m for a in range(PEER_TOPK)], axis=0)
        cand_id = jnp.concatenate([i1[a] * float(PEER_N_KEYS) + i2m for a in range(PEER_TOPK)], axis=0)
        top_s, ids = _top_rows(cand, cand_pos, PEER_TOPK, payload=cand_id)
        top_s = jnp.concatenate(top_s, axis=0)
        e = jnp.exp(top_s - top_s[0:1])
        gate_ref[hd * PEER_TOPK:(hd + 1) * PEER_TOPK, :] = e / jnp.sum(e, axis=0, keepdims=True)
        ids_ref[hd * PEER_TOPK:(hd + 1) * PEER_TOPK, :] = (jnp.concatenate(ids, axis=0) * PEER_WORDS).astype(jnp.int32)


def peer_route(h2, norm_g, w_q, keys1, keys2):
    T, d = h2.shape
    tm = PEER_ROUTE_TM
    bf16 = jnp.bfloat16
    nq = PEER_HEADS * PEER_QDIM
    half = PEER_QDIM // 2
    return pl.pallas_call(
        _peer_route_kernel,
        grid=(T // tm,),
        in_specs=[pl.BlockSpec((tm, d), lambda i: (i, 0)),
                  pl.BlockSpec((1, d), lambda i: (0, 0)),
                  pl.BlockSpec((nq, d), lambda i: (0, 0)),
                  pl.BlockSpec((PEER_HEADS, PEER_N_KEYS, half), lambda i: (0, 0, 0)),
                  pl.BlockSpec((PEER_HEADS, PEER_N_KEYS, half), lambda i: (0, 0, 0))],
        out_specs=[pl.BlockSpec((tm, d), lambda i: (i, 0)),
                   pl.BlockSpec((PEER_SLOTS, tm), lambda i: (0, i)),
                   pl.BlockSpec((PEER_SLOTS, tm), lambda i: (0, i))],
        out_shape=[jax.ShapeDtypeStruct((T, d), jnp.float32),
                   jax.ShapeDtypeStruct((PEER_SLOTS, T), jnp.int32),
                   jax.ShapeDtypeStruct((PEER_SLOTS, T), jnp.float32)],
        compiler_params=pltpu.CompilerParams(dimension_semantics=("arbitrary",),
                                             vmem_limit_bytes=V7X_VMEM_LIMIT_BYTES),
        name="peer_route",
    )(h2, norm_g.reshape(1, d), w_q.T.astype(bf16), keys1.astype(bf16), keys2.astype(bf16))


def _unpack_pair(words):
    lo = lax.bitcast_convert_type(lax.shift_left(words, 16), jnp.float32)
    hi = lax.bitcast_convert_type(lax.bitwise_and(words, -65536), jnp.float32)
    return lo, hi


def _pack_rows(tab):
    bits = lax.bitcast_convert_type(tab.astype(jnp.bfloat16), jnp.uint16).astype(jnp.uint32)
    bits = bits.reshape(tab.shape[0], PEER_WORDS, 2, V7X_LANES)
    words = lax.bitcast_convert_type(bits[:, :, 0] | (bits[:, :, 1] << 16), jnp.int32)
    return words.reshape(tab.shape[0] * PEER_WORDS, V7X_LANES)


def _peer_gather_token(ids_ref, tab, tile, t):
    for k in range(PEER_SLOTS):
        row = pl.multiple_of(ids_ref[t * PEER_SLOTS + k], PEER_WORDS)
        tile[pl.ds(k, PEER_WORDS, stride=PEER_TILE_PITCH), :] = tab[pl.ds(row, PEER_WORDS), :]


def _peer_chunk_pair(tile, j):
    return _unpack_pair(tile[j * PEER_TILE_PITCH:j * PEER_TILE_PITCH + PEER_SLOTS, :])


def _peer_load_table(tab_hbm, tab, sem):
    @pl.when(pl.program_id(0) == 0)
    def _():
        copy = pltpu.make_async_copy(tab_hbm, tab, sem)
        copy.start()
        copy.wait()


def _peer_act_kernel(ids_ref, xn_ref, gt_ref, tab_hbm, w_ref, tab, tile_a, tile_b, sem):
    _peer_load_table(tab_hbm, tab, sem)
    for t in range(PEER_TT):
        tile = (tile_a, tile_b)[t % 2]
        _peer_gather_token(ids_ref, tab, tile, t)
        xrow = xn_ref[t:t + 1, :]
        acc = jnp.zeros((PEER_SLOTS, V7X_LANES), jnp.float32)
        for j in range(PEER_WORDS):
            lo, hi = _peer_chunk_pair(tile, j)
            acc = acc + lo * xrow[:, 2 * j * V7X_LANES:(2 * j + 1) * V7X_LANES]
            acc = acc + hi * xrow[:, (2 * j + 1) * V7X_LANES:(2 * j + 2) * V7X_LANES]
        act = jnp.sum(acc, axis=1, keepdims=True)
        w_ref[0, :, t:t + 1] = gt_ref[0, :, t:t + 1] * jax.nn.gelu(act)


def _peer_out_kernel(ids_ref, w_ref, tab_hbm, o_ref, tab, tile_a, tile_b, sem):
    _peer_load_table(tab_hbm, tab, sem)
    for t in range(PEER_TT):
        tile = (tile_a, tile_b)[t % 2]
        _peer_gather_token(ids_ref, tab, tile, t)
        w = w_ref[0, :, t:t + 1]
        for j in range(PEER_WORDS):
            lo, hi = _peer_chunk_pair(tile, j)
            o_ref[t:t + 1, 2 * j * V7X_LANES:(2 * j + 1) * V7X_LANES] = jnp.sum(lo * w, axis=0, keepdims=True)
            o_ref[t:t + 1, (2 * j + 1) * V7X_LANES:(2 * j + 2) * V7X_LANES] = jnp.sum(hi * w, axis=0, keepdims=True)


def peer_experts(xn, idsT, gateT, u_tab, v_tab):
    T, d = xn.shape
    tt = PEER_TT
    nt = T // tt
    ids = idsT.T.reshape(T * PEER_SLOTS)
    gt = gateT.reshape(PEER_SLOTS, nt, tt).transpose(1, 0, 2)
    ids_spec = pl.BlockSpec((tt * PEER_SLOTS,), lambda i: (i,), memory_space=pltpu.SMEM)
    col_spec = pl.BlockSpec((1, PEER_SLOTS, tt), lambda i: (i, 0, 0))
    row_spec = pl.BlockSpec((tt, d), lambda i: (i, 0))
    tile = pltpu.VMEM((PEER_WORDS * PEER_TILE_PITCH, V7X_LANES), jnp.int32)
    scratch = [pltpu.VMEM((PEER_EXPERTS * PEER_WORDS, V7X_LANES), jnp.int32), tile, tile, pltpu.SemaphoreType.DMA(())]
    params = pltpu.CompilerParams(dimension_semantics=("arbitrary",), vmem_limit_bytes=V7X_VMEM_LIMIT_BYTES)
    w = pl.pallas_call(
        _peer_act_kernel,
        grid=(nt,),
        in_specs=[ids_spec, row_spec, col_spec, pl.BlockSpec(memory_space=pl.ANY)],
        out_specs=col_spec,
        out_shape=jax.ShapeDtypeStruct((nt, PEER_SLOTS, tt), jnp.float32),
        scratch_shapes=scratch, compiler_params=params, name="peer_act",
    )(ids, xn, gt, _pack_rows(u_tab))
    return pl.pallas_call(
        _peer_out_kernel,
        grid=(nt,),
        in_specs=[ids_spec, col_spec, pl.BlockSpec(memory_space=pl.ANY)],
        out_specs=row_spec,
        out_shape=jax.ShapeDtypeStruct((T, d), jnp.float32),
        scratch_shapes=scratch, compiler_params=params, name="peer_out",
    )(ids, w, _pack_rows(v_tab))


def peer_ffn(h, norm_g, w_q, keys1, keys2, u_tab, v_tab):
    b, s, d = h.shape
    xn, idsT, gateT = peer_route(h.reshape(b * s, d), norm_g, w_q, keys1, keys2)
    return peer_experts(xn, idsT, gateT, u_tab, v_tab).reshape(b, s, d)


def per_layer_embed(h, p_i, norm_g, w_up, w_gate):
    return _mm(p_i, None, w_up, False) * jax.nn.sigmoid(_mm(h, norm_g, w_gate, True))


def kernel(x, p, a_norm_g, a_w_in, a_conv_w, a_conv_b, a_dt_bias, a_log, a_d_skip, a_gnorm_g, a_w_out,
           kv_norm_g, w_kv, cmp_pos_k, cmp_pos_v, cmp_w1_k, cmp_w2_k, cmp_w1_v, cmp_w2_v, rel_bias,
           b_norm_g, b_w_in, b_w_out, c_norm_g, c_w_q, c_keys1, c_keys2, c_u, c_v,
           e_norm_g, e_w_up, e_w_gate, final_g):
    h = x
    shared = None
    for i in range(DEPTH):
        if i < N_A_LAYERS:
            h = ssd_mixer_residual(h, a_norm_g[i], a_w_in[i], a_conv_w[i], a_conv_b[i], a_dt_bias[i],
                                   a_log[i], a_d_skip[i], a_gnorm_g[i], a_w_out[i])
        else:
            if shared is None:
                shared = nsa_shared_kv(h, kv_norm_g, w_kv, cmp_pos_k, cmp_pos_v,
                                       cmp_w1_k, cmp_w2_k, cmp_w1_v, cmp_w2_v)
            j = i - N_A_LAYERS
            h = h + nsa_mixer(h, b_norm_g[j], b_w_in[j], b_w_out[j], rel_bias, *shared)
        h = h + peer_ffn(h, c_norm_g[i], c_w_q[i], c_keys1[i], c_keys2[i], c_u[i], c_v[i])
        h = h + per_layer_embed(h, p[i], e_norm_g[i], e_w_up[i], e_w_gate[i])
    return rms_norm(h, final_g)
```

```python
import functools
import math

import jax
import jax.numpy as jnp
import numpy as np
from jax import lax
from jax.experimental import pallas as pl
from jax.experimental.pallas import tpu as pltpu

D_MODEL = 1024
BATCH = 2
SEQ = 16384
DEPTH = 2
N_A_LAYERS = DEPTH // 2
N_B_LAYERS = DEPTH - N_A_LAYERS

SSD_D_INNER = 2 * D_MODEL
SSD_HEADDIM = 64
SSD_HEADS = SSD_D_INNER // SSD_HEADDIM
SSD_GROUPS = 8
SSD_HEADS_PER_GROUP = SSD_HEADS // SSD_GROUPS
SSD_STATE = 128
SSD_CONV = 4
SSD_CHUNK = 128
SSD_CONV_DIM = SSD_D_INNER + 2 * SSD_GROUPS * SSD_STATE
SSD_IN_DIM = SSD_D_INNER + SSD_CONV_DIM + SSD_HEADS

NSA_HEADS = 16
NSA_KV_HEADS = 4
NSA_GROUP = NSA_HEADS // NSA_KV_HEADS
NSA_HEAD_DIM = 64
CMP_BLOCK = 32
CMP_STRIDE = 16
CMP_HIDDEN = 256
SLC_BLOCK = 64
SLC_TOPN = 16
SLC_FORCE = 1e4
WINDOW = 512
Q_BLOCK = 128
NSA_IN_DIM = NSA_HEADS * NSA_HEAD_DIM + 3 * NSA_HEADS
KV_DIM = 6 * NSA_KV_HEADS * NSA_HEAD_DIM

REL_BUCKETS = 32
REL_MAX_DISTANCE = 4096

PEER_HEADS = 8
PEER_N_KEYS = 128
PEER_EXPERTS = PEER_N_KEYS * PEER_N_KEYS
PEER_QDIM = 256
PEER_TOPK = 16
PEER_TOKEN_CHUNK = 128
PEER_V_SCALE = PEER_HEADS ** -0.5

PLE_DIM = 256
NORM_EPS = 1e-6
NEG_INF = -1e30

V7X_LANES = 128
V7X_VMEM_LIMIT_BYTES = 56 * 1024 * 1024


def _norm_matmul_kernel(x_ref, g_ref, w_ref, o_ref, *, normalize):
    x = x_ref[...]
    if normalize:
        x = x * lax.rsqrt(jnp.mean(x * x, axis=-1, keepdims=True) + NORM_EPS) * g_ref[...]
    o_ref[...] = jnp.dot(x.astype(jnp.bfloat16), w_ref[...], preferred_element_type=jnp.float32)


def norm_matmul(x, g, w, *, normalize, tm=512, tn=None):
    m, k = x.shape
    n = w.shape[1]
    n_pad = -(-n // V7X_LANES) * V7X_LANES
    wb = w.astype(jnp.bfloat16)
    if n_pad != n:
        wb = jnp.pad(wb, ((0, 0), (0, n_pad - n)))
    if tn is None:
        tn = n_pad
    assert m % tm == 0 and n_pad % tn == 0
    out = pl.pallas_call(
        functools.partial(_norm_matmul_kernel, normalize=normalize),
        grid=(m // tm, n_pad // tn),
        in_specs=[
            pl.BlockSpec((tm, k), lambda i, j: (i, 0)),
            pl.BlockSpec((1, k), lambda i, j: (0, 0)),
            pl.BlockSpec((k, tn), lambda i, j: (0, j)),
        ],
        out_specs=pl.BlockSpec((tm, tn), lambda i, j: (i, j)),
        out_shape=jax.ShapeDtypeStruct((m, n_pad), jnp.float32),
        compiler_params=pltpu.CompilerParams(
            dimension_semantics=("arbitrary", "arbitrary"),
            vmem_limit_bytes=V7X_VMEM_LIMIT_BYTES),
        name="norm_matmul" if normalize else "matmul",
    )(x, g.reshape(1, k), wb)
    return out[:, :n] if n_pad != n else out


def _mm(x3, g, w, normalize, **kw):
    b, s, k = x3.shape
    if g is None:
        g = jnp.ones((k,), jnp.float32)
    return norm_matmul(x3.reshape(b * s, k), g, w, normalize=normalize, **kw).reshape(b, s, w.shape[1])


def rms_norm(x, g):
    xf = x.astype(jnp.float32)
    y = xf * lax.rsqrt(jnp.mean(xf * xf, axis=-1, keepdims=True) + NORM_EPS)
    return (y * g.astype(jnp.float32)).astype(x.dtype)


def rel_bucket(dist):
    dist = jnp.maximum(dist, 0)
    max_exact = REL_BUCKETS // 2
    d = jnp.maximum(dist, 1).astype(jnp.float32)
    large = max_exact + (jnp.log(d / max_exact) / math.log(REL_MAX_DISTANCE / max_exact)
                         * (REL_BUCKETS - max_exact)).astype(jnp.int32)
    large = jnp.minimum(large, REL_BUCKETS - 1)
    return jnp.where(dist < max_exact, dist, large)


SSD_GROUP_W = SSD_HEADS_PER_GROUP * SSD_HEADDIM
SSD_IN_PAD = -(-SSD_IN_DIM // V7X_LANES) * V7X_LANES
SSD_TAIL = 8


def _split3(x):
    bf16, f32 = jnp.bfloat16, jnp.float32
    hi = x.astype(bf16)
    r1 = x - hi.astype(f32)
    mid = r1.astype(bf16)
    lo = (r1 - mid.astype(f32)).astype(bf16)
    return hi, mid, lo


def _ssd_kernel(zx_ref, h_ref, cw_ref, cb_ref, dtb_ref, alog_ref, dskip_ref, gn_ref, wout_ref, tri_ref, exp_ref,
                o_ref, xs_scr, state_scr):
    f32, bf16 = jnp.float32, jnp.bfloat16
    L, N, GW = SSD_CHUNK, SSD_STATE, SSD_GROUP_W
    c = pl.program_id(1)

    @pl.when(c == 0)
    def _():
        xs_scr[0:SSD_TAIL, :] = jnp.zeros((SSD_TAIL, SSD_CONV_DIM), f32)
        state_scr[...] = jnp.zeros_like(state_scr)

    xs_scr[SSD_TAIL:SSD_TAIL + L, :] = zx_ref[:, SSD_D_INNER:SSD_D_INNER + SSD_CONV_DIM]
    conv = cb_ref[...]
    for w in range(SSD_CONV):
        conv = conv + cw_ref[w:w + 1, :] * xs_scr[pl.ds(SSD_TAIL - (SSD_CONV - 1) + w, L), :]
    xs_scr[0:SSD_TAIL, :] = xs_scr[L:L + SSD_TAIL, :]
    xbc = conv * jax.nn.sigmoid(conv)

    dt_in = zx_ref[:, SSD_D_INNER + SSD_CONV_DIM:SSD_IN_PAD] + dtb_ref[...]
    dt = jnp.maximum(dt_in, 0.0) + jnp.log1p(jnp.exp(-jnp.abs(dt_in)))
    a_dt = dt * -jnp.exp(alog_ref[...])
    a_cs = sum(jnp.dot(tri_ref[...], t, preferred_element_type=f32) for t in _split3(a_dt))
    a_csT = a_cs.T
    a_end = a_cs[L - 1:L, :]
    grow = jnp.exp(a_cs)
    to_end = jnp.exp(a_end - a_cs)

    def per_channel(q):
        return sum(jnp.dot(t, exp_ref[...], preferred_element_type=f32) for t in _split3(q)[:2])

    dt_c, grow_c, to_end_c = per_channel(dt), per_channel(grow), per_channel(to_end)
    end_c = per_channel(jnp.broadcast_to(jnp.exp(a_end), (8, V7X_LANES)))[0:1]
    dskip_c = per_channel(jnp.broadcast_to(dskip_ref[...], (8, V7X_LANES)))[0:1]

    x = xbc[:, :SSD_D_INNER]
    xd = x * dt_c
    xd_bf = xd.astype(bf16)
    xe_bf = (xd * to_end_c).astype(bf16)
    row_i = lax.broadcasted_iota(jnp.int32, (L, L), 0)
    col_i = lax.broadcasted_iota(jnp.int32, (L, L), 1)
    lane_head = lax.broadcasted_iota(jnp.int32, (L, GW), 1) // SSD_HEADDIM
    ys = []
    for g in range(SSD_GROUPS):
        bm = xbc[:, SSD_D_INNER + g * N:SSD_D_INNER + (g + 1) * N]
        cm = xbc[:, SSD_D_INNER + SSD_GROUPS * N + g * N:SSD_D_INNER + SSD_GROUPS * N + (g + 1) * N].astype(bf16)
        bmT = bm.T.astype(bf16)
        cb = jnp.dot(cm, bmT, preferred_element_type=f32)
        xd_g = xd_bf[:, g * GW:(g + 1) * GW]
        y_g = jnp.zeros((L, GW), f32)
        for r in range(SSD_HEADS_PER_GROUP):
            hd = g * SSD_HEADS_PER_GROUP + r
            seg = a_cs[:, hd:hd + 1] - a_csT[hd:hd + 1, :]
            decay = jnp.where(row_i >= col_i, jnp.exp(seg), 0.0)
            xr = jnp.where(lane_head == r, xd_g, jnp.zeros_like(xd_g))
            y_g = y_g + jnp.dot((cb * decay).astype(bf16), xr, preferred_element_type=f32)
        hT = state_scr[g]
        y_off = jnp.dot(cm, hT.astype(bf16), preferred_element_type=f32) * grow_c[:, g * GW:(g + 1) * GW]
        new_states = jnp.dot(bmT, xe_bf[:, g * GW:(g + 1) * GW], preferred_element_type=f32)
        state_scr[g] = hT * end_c[:, g * GW:(g + 1) * GW] + new_states
        ys.append(y_g + y_off)
    y = jnp.concatenate(ys, axis=1) + x * dskip_c

    z = zx_ref[:, :SSD_D_INNER]
    yz = y * (z * jax.nn.sigmoid(z))
    parts = []
    for g in range(SSD_GROUPS):
        t = yz[:, g * GW:(g + 1) * GW]
        parts.append(t * lax.rsqrt(jnp.mean(t * t, axis=-1, keepdims=True) + NORM_EPS))
    yn = jnp.concatenate(parts, axis=1) * gn_ref[...]
    o_ref[...] = h_ref[...] + jnp.dot(yn.astype(bf16), wout_ref[...], preferred_element_type=f32)


def ssd_mixer_residual(h, norm_g, w_in, conv_w, conv_b, dt_bias, a_log, d_skip, gnorm_g, w_out):
    b, s, d = h.shape
    L = SSD_CHUNK
    nc = s // L
    w_in_p = jnp.pad(w_in, ((0, 0), (0, SSD_IN_PAD - SSD_IN_DIM)))
    zx = norm_matmul(h.reshape(b * s, d), norm_g, w_in_p, normalize=True, tm=256)
    lane_pad = lambda v: jnp.pad(v.reshape(1, -1), ((0, 0), (0, V7X_LANES - v.shape[-1])))
    tri = jnp.asarray(np.tril(np.ones((L, L))), jnp.bfloat16)
    head_of = np.arange(SSD_D_INNER) // SSD_HEADDIM
    expand = jnp.asarray(np.arange(V7X_LANES)[:, None] == head_of[None, :], jnp.bfloat16)
    const = lambda *shape: pl.BlockSpec(shape, lambda bi, ci: (0,) * len(shape))
    out = pl.pallas_call(
        _ssd_kernel,
        grid=(b, nc),
        in_specs=[pl.BlockSpec((L, SSD_IN_PAD), lambda bi, ci: (bi * nc + ci, 0)),
                  pl.BlockSpec((L, d), lambda bi, ci: (bi * nc + ci, 0)),
                  const(SSD_CONV, SSD_CONV_DIM), const(1, SSD_CONV_DIM),
                  const(1, V7X_LANES), const(1, V7X_LANES), const(1, V7X_LANES),
                  const(1, SSD_D_INNER), const(SSD_D_INNER, d), const(L, L), const(V7X_LANES, SSD_D_INNER)],
        out_specs=pl.BlockSpec((L, d), lambda bi, ci: (bi * nc + ci, 0)),
        out_shape=jax.ShapeDtypeStruct((b * s, d), jnp.float32),
        scratch_shapes=[pltpu.VMEM((L + SSD_TAIL, SSD_CONV_DIM), jnp.float32),
                        pltpu.VMEM((SSD_GROUPS, SSD_STATE, SSD_GROUP_W), jnp.float32)],
        compiler_params=pltpu.CompilerParams(dimension_semantics=("arbitrary", "arbitrary"),
                                             vmem_limit_bytes=V7X_VMEM_LIMIT_BYTES),
        name="ssd_mixer",
    )(zx, h.reshape(b * s, d), conv_w, conv_b.reshape(1, -1), lane_pad(dt_bias), lane_pad(a_log), lane_pad(d_skip),
      gnorm_g.reshape(1, -1), w_out.astype(jnp.bfloat16), tri, expand)
    return out.reshape(b, s, d)


def _mlp_kernel(x_ref, w1_ref, w2_ref, o_ref):
    hid = jnp.dot(x_ref[...].astype(jnp.bfloat16), w1_ref[...], preferred_element_type=jnp.float32)
    hid = jax.nn.gelu(hid)
    o_ref[...] = jnp.dot(hid.astype(jnp.bfloat16), w2_ref[...], preferred_element_type=jnp.float32)


def mlp2(x, w1, w2, *, tm=512):
    m, k = x.shape
    hdim, n = w2.shape
    assert m % tm == 0
    return pl.pallas_call(
        _mlp_kernel,
        grid=(m // tm,),
        in_specs=[pl.BlockSpec((tm, k), lambda i: (i, 0)),
                  pl.BlockSpec((k, hdim), lambda i: (0, 0)),
                  pl.BlockSpec((hdim, n), lambda i: (0, 0))],
        out_specs=pl.BlockSpec((tm, n), lambda i: (i, 0)),
        out_shape=jax.ShapeDtypeStruct((m, n), jnp.float32),
        compiler_params=pltpu.CompilerParams(dimension_semantics=("arbitrary",),
                                             vmem_limit_bytes=V7X_VMEM_LIMIT_BYTES),
        name="cmp_mlp",
    )(x, w1.astype(jnp.bfloat16), w2.astype(jnp.bfloat16))


def compress_blocks(t, pos, w1, w2):
    b, s, g, d = t.shape
    halves = t.transpose(0, 2, 1, 3).reshape(b, g, s // CMP_STRIDE, CMP_STRIDE, d)
    nxt = jnp.concatenate([halves[:, :, 1:], jnp.zeros_like(halves[:, :, :1])], axis=2)
    blocks = jnp.concatenate([halves, nxt], axis=3) + pos
    flat = blocks.reshape(b * g * (s // CMP_STRIDE), CMP_BLOCK * d)
    return mlp2(flat, w1, w2).reshape(b, g, s // CMP_STRIDE, d)


N_SEL = SEQ // SLC_BLOCK
N_KT = SEQ // Q_BLOCK
CMP_FRONT = 256
CMP_ROWS = CMP_FRONT + SEQ // CMP_STRIDE
CMP_WIN = 256
REL_TABLE = 4096
NEAR_TILES = 24
WIN_TILES = WINDOW // Q_BLOCK + 1
SEL_GROUP = 8
QL = NSA_GROUP * Q_BLOCK


def _nsa_kernel(qT_ref, gT_ref, kc_ref, vcT_ref, kk_ref, vsT_ref, vwT_ref, at_ref, bcT_ref, bT_ref, bTw_ref,
                o_ref, s_scr, sel_scr):
    f32, bf16 = jnp.float32, jnp.bfloat16
    i = pl.program_id(2)
    qT = (qT_ref[0, 0, 0] * (NSA_HEAD_DIM ** -0.5)).astype(bf16)
    zq = jnp.zeros_like(qT)
    q_sel = jnp.concatenate([qT, zq], axis=0)
    q_win = jnp.concatenate([zq, qT], axis=0)

    r0 = pl.multiple_of(8 * i + 8, 8)
    s = jnp.dot(kc_ref[0, 0], qT, preferred_element_type=f32)
    rio = lax.broadcasted_iota(jnp.int32, (CMP_ROWS, Q_BLOCK), 0)
    rowmask = jnp.where((rio >= CMP_FRONT) & (rio < r0 + CMP_WIN), 0.0, NEG_INF)
    s_scr[...] = s + jnp.concatenate([rowmask] * NSA_GROUP, axis=1)
    s_scr[pl.ds(r0, CMP_WIN), :] = s_scr[pl.ds(r0, CMP_WIN), :] + bcT_ref[0]
    s = s_scr[...]
    m = jnp.max(s, axis=0, keepdims=True)
    p = jnp.exp(s - m)
    l = jnp.sum(p, axis=0, keepdims=True)
    inv = jnp.where(m > 0.1 * NEG_INF, 1.0 / l, 0.0)
    pn = p * inv
    o_c = jnp.dot(vcT_ref[0, 0], pn.astype(bf16), preferred_element_type=f32)

    psum = pn[:, 0:Q_BLOCK]
    for r in range(1, NSA_GROUP):
        psum = psum + pn[:, r * Q_BLOCK:(r + 1) * Q_BLOCK]
    hi = psum.astype(bf16)
    lo = (psum - hi.astype(f32)).astype(bf16)
    imp = (jnp.dot(at_ref[...], hi, preferred_element_type=f32)
           + jnp.dot(at_ref[...], lo, preferred_element_type=f32))
    blk = lax.broadcasted_iota(jnp.int32, (N_SEL, Q_BLOCK), 0)
    qi = lax.broadcasted_iota(jnp.int32, (N_SEL, Q_BLOCK), 1)
    cur = 2 * i + (qi >= SLC_BLOCK).astype(jnp.int32)
    forced = (blk == 0) | (blk == cur) | (blk == cur - 1)
    valid = blk <= cur
    score = jnp.where(forced, SLC_FORCE, jnp.where(valid, imp, -SLC_FORCE))
    blkf = blk.astype(f32)
    chosen = jnp.zeros((N_SEL, Q_BLOCK), f32)
    for _ in range(SLC_TOPN):
        mx = jnp.max(score, axis=0, keepdims=True)
        first = jnp.min(jnp.where(score == mx, blkf, float(N_SEL)), axis=0, keepdims=True)
        hit = blkf == first
        chosen = jnp.where(hit, 1.0, chosen)
        score = jnp.where(hit, -jnp.inf, score)
    sel_scr[...] = jnp.where((chosen > 0.5) & valid, 0.0, NEG_INF)

    def attend(kt, carry, q_aug, vT_ref, bias):
        m_run, l_run, acc = carry
        scs = [jnp.dot(kk_ref[0, 0, k], q_aug, preferred_element_type=f32) + b for k, b in zip(kt, bias)]
        for k, sc in zip(kt, scs):
            m_new = jnp.maximum(m_run, jnp.max(sc, axis=0, keepdims=True))
            alpha = jnp.exp(m_run - m_new)
            pt = jnp.exp(sc - m_new)
            l_run = alpha * l_run + jnp.sum(pt, axis=0, keepdims=True)
            acc = alpha * acc + jnp.dot(vT_ref[0, 0, k], pt.astype(bf16), preferred_element_type=f32)
            m_run = m_new
        return m_run, l_run, acc

    def sel_mask(kt):
        m0 = jnp.broadcast_to(sel_scr[pl.ds(2 * kt, 1), :], (SLC_BLOCK, Q_BLOCK))
        m1 = jnp.broadcast_to(sel_scr[pl.ds(2 * kt + 1, 1), :], (SLC_BLOCK, Q_BLOCK))
        mk = jnp.concatenate([m0, m1], axis=0)
        return jnp.concatenate([mk] * NSA_GROUP, axis=1)

    init = (jnp.full((1, QL), NEG_INF, f32), jnp.zeros((1, QL), f32), jnp.zeros((NSA_HEAD_DIM, QL), f32))
    U = SEL_GROUP

    def far_group(g, c):
        kts = [g * U + u for u in range(U)]
        return attend(kts, c, q_sel, vsT_ref, [sel_mask(k) for k in kts])

    def near_group(g, c):
        kts, biases = [], []
        for u in range(U):
            delta = i - (g * U + u)
            k = jnp.minimum(g * U + u, i)
            tile = jnp.where(delta < 0, NEAR_TILES + 1, jnp.minimum(delta, NEAR_TILES))
            kts.append(k)
            biases.append(sel_mask(k) + bT_ref[0, tile])
        return attend(kts, c, q_sel, vsT_ref, biases)

    n_far = jnp.maximum((i - (NEAR_TILES - 1)) // U, 0)
    carry = lax.fori_loop(0, n_far, far_group, init)
    _, l_s, acc_s = lax.fori_loop(n_far, i // U + 1, near_group, carry)
    o_s = acc_s / l_s

    kts, biases = [], []
    for u in range(WIN_TILES):
        k = i - (WIN_TILES - 1) + u
        kts.append(jnp.maximum(k, 0))
        biases.append(bTw_ref[0, jnp.where(k >= 0, WIN_TILES - 1 - u, WIN_TILES)])
    _, l_w, acc_w = attend(kts, init, q_win, vwT_ref, biases)
    o_w = acc_w / l_w

    gate = jax.nn.sigmoid(gT_ref[0, 0, 0])
    o_ref[0, 0, 0] = gate[0:1] * o_c + gate[1:2] * o_s + gate[2:3] * o_w


BIAS_EXT = REL_TABLE + 2 * Q_BLOCK


CMP_OFF = CMP_STRIDE * (CMP_WIN - 8) - (CMP_BLOCK - 1)


def _bias_tile_kernel(bd_ref, bdc_ref, cmp_ref, near_ref, win_ref):
    row = bd_ref[0]
    kj = lax.broadcasted_iota(jnp.int32, (Q_BLOCK, Q_BLOCK), 0)
    qi = lax.broadcasted_iota(jnp.int32, (Q_BLOCK, Q_BLOCK), 1)
    for t in range(NEAR_TILES):
        seg = jnp.broadcast_to(row[:, t * Q_BLOCK:(t + 2) * Q_BLOCK], (Q_BLOCK, 2 * Q_BLOCK))
        tile = pltpu.roll(seg, 0, 1, stride=1, stride_axis=0)[:, Q_BLOCK:]
        near_ref[0, t] = tile
        if t < WIN_TILES:
            win_ref[0, t] = jnp.where(t * Q_BLOCK + qi - kj < WINDOW, tile, NEG_INF)
    near_ref[0, NEAR_TILES] = jnp.zeros((Q_BLOCK, Q_BLOCK), jnp.float32)
    near_ref[0, NEAR_TILES + 1] = jnp.full((Q_BLOCK, Q_BLOCK), NEG_INF, jnp.float32)
    win_ref[0, WIN_TILES] = jnp.full((Q_BLOCK, Q_BLOCK), NEG_INF, jnp.float32)
    full = jnp.broadcast_to(bdc_ref[0], (CMP_WIN, BIAS_EXT))
    cmp_ref[0] = pltpu.roll(full, 0, 1, stride=CMP_STRIDE, stride_axis=0)[:, REL_TABLE:REL_TABLE + Q_BLOCK]


def _bias_tables(rel_bias):
    G, R = NSA_KV_HEADS, NSA_GROUP
    bd = rel_bias[rel_bucket(jnp.arange(REL_TABLE))]
    bd = (bd - bd[REL_TABLE - 1]).T
    def padded(front):
        pad = lambda n: jnp.full((NSA_HEADS, n), NEG_INF, jnp.float32)
        return jnp.concatenate([pad(front), bd, pad(BIAS_EXT - REL_TABLE - front)], axis=1).reshape(NSA_HEADS, 1, BIAS_EXT)

    bd_ext, bd_cmp = padded(Q_BLOCK), padded(REL_TABLE - CMP_OFF)
    head = lambda *blk: pl.BlockSpec((1,) + blk + (Q_BLOCK,), lambda hd: (hd // R,) + (0,) * len(blk) + (hd % R,))
    return pl.pallas_call(
        _bias_tile_kernel,
        grid=(NSA_HEADS,),
        in_specs=[pl.BlockSpec((1, 1, BIAS_EXT), lambda hd: (hd, 0, 0))] * 2,
        out_specs=[head(CMP_WIN), head(NEAR_TILES + 2, Q_BLOCK), head(WIN_TILES + 1, Q_BLOCK)],
        out_shape=[jax.ShapeDtypeStruct((G, CMP_WIN, QL), jnp.float32),
                   jax.ShapeDtypeStruct((G, NEAR_TILES + 2, Q_BLOCK, QL), jnp.float32),
                   jax.ShapeDtypeStruct((G, WIN_TILES + 1, Q_BLOCK, QL), jnp.float32)],
        compiler_params=pltpu.CompilerParams(dimension_semantics=("arbitrary",),
                                             vmem_limit_bytes=V7X_VMEM_LIMIT_BYTES),
        name="nsa_bias_tiles",
    )(bd_ext, bd_cmp)


def _pool_matrix():
    per = SLC_BLOCK // CMP_STRIDE
    n_span = CMP_BLOCK // CMP_STRIDE
    k = np.arange(CMP_ROWS)[None, :] - CMP_FRONT
    j = np.arange(N_SEL)[:, None]
    return jnp.asarray((k >= per * j - (n_span - 1)) & (k <= per * j + per - 1), jnp.bfloat16)


def nsa_shared_kv(h, kv_norm_g, w_kv, cmp_pos_k, cmp_pos_v, cmp_w1_k, cmp_w2_k, cmp_w1_v, cmp_w2_v):
    b, s, _ = h.shape
    G, DH = NSA_KV_HEADS, NSA_HEAD_DIM
    bf16 = jnp.bfloat16
    kv = _mm(h, kv_norm_g, w_kv, True).reshape(b, s, 6, G, DH)
    k_cmp = compress_blocks(kv[:, :, 0], cmp_pos_k, cmp_w1_k, cmp_w2_k)
    v_cmp = compress_blocks(kv[:, :, 1], cmp_pos_v, cmp_w1_v, cmp_w2_v)
    front = jnp.zeros((b, G, CMP_FRONT, DH), bf16)
    kc = jnp.concatenate([front, k_cmp.astype(bf16)], axis=2)
    vcT = jnp.concatenate([front, v_cmp.astype(bf16)], axis=2).transpose(0, 1, 3, 2)
    kk = jnp.concatenate([kv[:, :, 2], kv[:, :, 4]], axis=-1).astype(bf16)
    kk = kk.transpose(0, 2, 1, 3).reshape(b, G, N_KT, Q_BLOCK, 2 * DH)

    def vt(v):
        return v.astype(bf16).transpose(0, 2, 1, 3).reshape(b, G, N_KT, Q_BLOCK, DH).transpose(0, 1, 2, 4, 3)

    return kc, vcT, kk, vt(kv[:, :, 3]), vt(kv[:, :, 5])


def nsa_mixer(h, norm_g, w_in, w_out, rel_bias, kc, vcT, kk, vsT, vwT):
    b, s, _ = h.shape
    G, R, DH = NSA_KV_HEADS, NSA_GROUP, NSA_HEAD_DIM
    nb = s // Q_BLOCK
    proj = _mm(h, norm_g, w_in, True)
    qT = proj[..., :NSA_HEADS * DH].reshape(b, nb, Q_BLOCK, G, R, DH).transpose(0, 3, 1, 5, 4, 2)
    qT = qT.reshape(b, G, nb, DH, QL)
    gT = proj[..., NSA_HEADS * DH:].reshape(b, nb, Q_BLOCK, G, R, 3).transpose(0, 3, 1, 5, 4, 2)
    gT = gT.reshape(b, G, nb, 3, QL)
    b_cmp, b_near, b_win = _bias_tables(rel_bias)
    at = _pool_matrix()

    per_bg = lambda *blk: pl.BlockSpec((1, 1) + blk, lambda bi, gi, i: (bi, gi) + (0,) * len(blk))
    per_g = lambda *blk: pl.BlockSpec((1,) + blk, lambda bi, gi, i: (gi,) + (0,) * len(blk))
    oT = pl.pallas_call(
        _nsa_kernel,
        grid=(b, G, nb),
        in_specs=[
            pl.BlockSpec((1, 1, 1, DH, QL), lambda bi, gi, i: (bi, gi, i, 0, 0)),
            pl.BlockSpec((1, 1, 1, 3, QL), lambda bi, gi, i: (bi, gi, i, 0, 0)),
            per_bg(CMP_ROWS, DH),
            per_bg(DH, CMP_ROWS),
            per_bg(N_KT, Q_BLOCK, 2 * DH),
            per_bg(N_KT, DH, Q_BLOCK),
            per_bg(N_KT, DH, Q_BLOCK),
            pl.BlockSpec((N_SEL, CMP_ROWS), lambda bi, gi, i: (0, 0)),
            per_g(CMP_WIN, QL),
            per_g(NEAR_TILES + 2, Q_BLOCK, QL),
            per_g(WIN_TILES + 1, Q_BLOCK, QL),
        ],
        out_specs=pl.BlockSpec((1, 1, 1, DH, QL), lambda bi, gi, i: (bi, gi, i, 0, 0)),
        out_shape=jax.ShapeDtypeStruct((b, G, nb, DH, QL), jnp.float32),
        scratch_shapes=[pltpu.VMEM((CMP_ROWS, QL), jnp.float32), pltpu.VMEM((N_SEL, Q_BLOCK), jnp.float32)],
        compiler_params=pltpu.CompilerParams(
            dimension_semantics=("arbitrary", "arbitrary", "arbitrary"),
            vmem_limit_bytes=V7X_VMEM_LIMIT_BYTES),
        name="nsa_attention",
    )(qT, gT, kc, vcT, kk, vsT, vwT, at, b_cmp, b_near, b_win)
    o = oT.reshape(b, G, nb, DH, R, Q_BLOCK).transpose(0, 2, 5, 1, 4, 3).reshape(b, s, NSA_HEADS * DH)
    return _mm(o, None, w_out, False)


PEER_SLOTS = PEER_HEADS * PEER_TOPK
PEER_ROUTE_TM = 256
PEER_ROUTE_LOCKSTEP = 2
PEER_TT = 16
PEER_CHUNKS = D_MODEL // V7X_LANES
PEER_WORDS = PEER_CHUNKS // 2
PEER_TILE_PITCH = PEER_SLOTS + 8


def _top_rows(arrays, rowid, n, payloads=None):
    arrays = list(arrays)
    vals = [[] for _ in arrays]
    picks = [[] for _ in arrays]
    for _ in range(n):
        for a, s in enumerate(arrays):
            mx = jnp.max(s, axis=0, keepdims=True)
            first = jnp.min(jnp.where(s == mx, rowid, jnp.inf), axis=0, keepdims=True)
            hit = rowid == first
            vals[a].append(mx)
            picks[a].append(first if payloads is None
                            else jnp.max(jnp.where(hit, payloads[a], -1.0), axis=0, keepdims=True))
            arrays[a] = jnp.where(hit, -jnp.inf, s)
    return vals, picks


def _peer_route_kernel(h_ref, g_ref, wqT_ref, k1_ref, k2_ref, xn_ref, ids_ref, gate_ref):
    f32, bf16 = jnp.float32, jnp.bfloat16
    x = h_ref[...]
    xn = x * lax.rsqrt(jnp.mean(x * x, axis=-1, keepdims=True) + NORM_EPS) * g_ref[...]
    xn_ref[...] = xn
    qT = lax.dot_general(wqT_ref[...], xn.astype(bf16), (((1,), (1,)), ((), ())), preferred_element_type=f32)
    half = PEER_QDIM // 2
    tm = x.shape[0]
    key_id = lax.broadcasted_iota(jnp.int32, (PEER_N_KEYS, tm), 0).astype(f32)
    stair = [PEER_TOPK // (a + 1) for a in range(PEER_TOPK)]
    n_pad = -sum(stair) % 8
    cand_pos = jnp.concatenate(
        [lax.broadcasted_iota(jnp.int32, (nb, tm), 0).astype(f32) + float(a * PEER_TOPK) for a, nb in enumerate(stair)]
        + [jnp.full((n_pad, tm), float(PEER_TOPK * PEER_TOPK), f32)], axis=0)
    for h0 in range(0, PEER_HEADS, PEER_ROUTE_LOCKSTEP):
        heads = range(h0, h0 + PEER_ROUTE_LOCKSTEP)
        scores = []
        for hd in heads:
            q1 = qT[hd * PEER_QDIM: hd * PEER_QDIM + half].astype(bf16)
            q2 = qT[hd * PEER_QDIM + half: (hd + 1) * PEER_QDIM].astype(bf16)
            scores.append(jnp.dot(k1_ref[hd], q1, preferred_element_type=f32))
            scores.append(jnp.dot(k2_ref[hd], q2, preferred_element_type=f32))
        vals, idxs = _top_rows(scores, key_id, PEER_TOPK)
        cands, cand_ids = [], []
        for a in range(PEER_ROUTE_LOCKSTEP):
            v1, i1 = vals[2 * a], idxs[2 * a]
            v2m = jnp.concatenate(vals[2 * a + 1], axis=0)
            i2m = jnp.concatenate(idxs[2 * a + 1], axis=0)
            cands.append(jnp.concatenate([v1[r] + v2m[0:nb] for r, nb in enumerate(stair)]
                                         + [jnp.full((n_pad, tm), -jnp.inf, f32)], axis=0))
            cand_ids.append(jnp.concatenate([i1[r] * float(PEER_N_KEYS) + i2m[0:nb] for r, nb in enumerate(stair)]
                                            + [jnp.full((n_pad, tm), -1.0, f32)], axis=0))
        tops, ids = _top_rows(cands, cand_pos, PEER_TOPK, payloads=cand_ids)
        for a, hd in enumerate(heads):
            top_s = jnp.concatenate(tops[a], axis=0)
            e = jnp.exp(top_s - top_s[0:1])
            gate_ref[hd * PEER_TOPK:(hd + 1) * PEER_TOPK, :] = e / jnp.sum(e, axis=0, keepdims=True)
            ids_ref[hd * PEER_TOPK:(hd + 1) * PEER_TOPK, :] = (jnp.concatenate(ids[a], axis=0)
                                                               * PEER_WORDS).astype(jnp.int32)


def peer_route(h2, norm_g, w_q, keys1, keys2):
    T, d = h2.shape
    tm = PEER_ROUTE_TM
    bf16 = jnp.bfloat16
    nq = PEER_HEADS * PEER_QDIM
    half = PEER_QDIM // 2
    return pl.pallas_call(
        _peer_route_kernel,
        grid=(T // tm,),
        in_specs=[pl.BlockSpec((tm, d), lambda i: (i, 0)),
                  pl.BlockSpec((1, d), lambda i: (0, 0)),
                  pl.BlockSpec((nq, d), lambda i: (0, 0)),
                  pl.BlockSpec((PEER_HEADS, PEER_N_KEYS, half), lambda i: (0, 0, 0)),
                  pl.BlockSpec((PEER_HEADS, PEER_N_KEYS, half), lambda i: (0, 0, 0))],
        out_specs=[pl.BlockSpec((tm, d), lambda i: (i, 0)),
                   pl.BlockSpec((PEER_SLOTS, tm), lambda i: (0, i)),
                   pl.BlockSpec((PEER_SLOTS, tm), lambda i: (0, i))],
        out_shape=[jax.ShapeDtypeStruct((T, d), jnp.float32),
                   jax.ShapeDtypeStruct((PEER_SLOTS, T), jnp.int32),
                   jax.ShapeDtypeStruct((PEER_SLOTS, T), jnp.float32)],
        compiler_params=pltpu.CompilerParams(dimension_semantics=("arbitrary",),
                                             vmem_limit_bytes=V7X_VMEM_LIMIT_BYTES),
        name="peer_route",
    )(h2, norm_g.reshape(1, d), w_q.T.astype(bf16), keys1.astype(bf16), keys2.astype(bf16))


def _unpack_pair(words):
    lo = lax.bitcast_convert_type(lax.shift_left(words, 16), jnp.float32)
    hi = lax.bitcast_convert_type(lax.bitwise_and(words, -65536), jnp.float32)
    return lo, hi


def _pack_rows(tab):
    bits = lax.bitcast_convert_type(tab.astype(jnp.bfloat16), jnp.uint16).astype(jnp.uint32)
    bits = bits.reshape(tab.shape[0], PEER_WORDS, 2, V7X_LANES)
    words = lax.bitcast_convert_type(bits[:, :, 0] | (bits[:, :, 1] << 16), jnp.int32)
    return words.reshape(tab.shape[0] * PEER_WORDS, V7X_LANES)


def _peer_gather_token(ids_ref, tab, tile, t):
    for k in range(PEER_SLOTS):
        row = pl.multiple_of(ids_ref[t * PEER_SLOTS + k], PEER_WORDS)
        tile[pl.ds(k, PEER_WORDS, stride=PEER_TILE_PITCH), :] = tab[pl.ds(row, PEER_WORDS), :]


def _peer_chunk_pair(tile, j):
    return _unpack_pair(tile[j * PEER_TILE_PITCH:j * PEER_TILE_PITCH + PEER_SLOTS, :])


def _peer_load_table(tab_hbm, tab, sem):
    @pl.when(pl.program_id(0) == 0)
    def _():
        copy = pltpu.make_async_copy(tab_hbm, tab, sem)
        copy.start()
        copy.wait()


def _peer_act_kernel(ids_ref, xn_ref, gt_ref, tab_hbm, w_ref, tab, tile_a, tile_b, sem):
    _peer_load_table(tab_hbm, tab, sem)
    for t in range(PEER_TT):
        tile = (tile_a, tile_b)[t % 2]
        _peer_gather_token(ids_ref, tab, tile, t)
        xrow = xn_ref[t:t + 1, :]
        acc = jnp.zeros((PEER_SLOTS, V7X_LANES), jnp.float32)
        for j in range(PEER_WORDS):
            lo, hi = _peer_chunk_pair(tile, j)
            acc = acc + lo * xrow[:, 2 * j * V7X_LANES:(2 * j + 1) * V7X_LANES]
            acc = acc + hi * xrow[:, (2 * j + 1) * V7X_LANES:(2 * j + 2) * V7X_LANES]
        act = jnp.sum(acc, axis=1, keepdims=True)
        w_ref[0, :, t:t + 1] = gt_ref[0, :, t:t + 1] * jax.nn.gelu(act)


def _peer_out_kernel(ids_ref, w_ref, tab_hbm, o_ref, tab, tile_a, tile_b, sem):
    _peer_load_table(tab_hbm, tab, sem)
    for t in range(PEER_TT):
        tile = (tile_a, tile_b)[t % 2]
        _peer_gather_token(ids_ref, tab, tile, t)
        w = w_ref[0, :, t:t + 1]
        for j in range(PEER_WORDS):
            lo, hi = _peer_chunk_pair(tile, j)
            o_ref[t:t + 1, 2 * j * V7X_LANES:(2 * j + 1) * V7X_LANES] = jnp.sum(lo * w, axis=0, keepdims=True)
            o_ref[t:t + 1, (2 * j + 1) * V7X_LANES:(2 * j + 2) * V7X_LANES] = jnp.sum(hi * w, axis=0, keepdims=True)


def peer_experts(xn, idsT, gateT, u_tab, v_tab):
    T, d = xn.shape
    tt = PEER_TT
    nt = T // tt
    ids = idsT.T.reshape(T * PEER_SLOTS)
    gt = gateT.reshape(PEER_SLOTS, nt, tt).transpose(1, 0, 2)
    ids_spec = pl.BlockSpec((tt * PEER_SLOTS,), lambda i: (i,), memory_space=pltpu.SMEM)
    col_spec = pl.BlockSpec((1, PEER_SLOTS, tt), lambda i: (i, 0, 0))
    row_spec = pl.BlockSpec((tt, d), lambda i: (i, 0))
    tile = pltpu.VMEM((PEER_WORDS * PEER_TILE_PITCH, V7X_LANES), jnp.int32)
    scratch = [pltpu.VMEM((PEER_EXPERTS * PEER_WORDS, V7X_LANES), jnp.int32), tile, tile, pltpu.SemaphoreType.DMA(())]
    params = pltpu.CompilerParams(dimension_semantics=("arbitrary",), vmem_limit_bytes=V7X_VMEM_LIMIT_BYTES)
    w = pl.pallas_call(
        _peer_act_kernel,
        grid=(nt,),
        in_specs=[ids_spec, row_spec, col_spec, pl.BlockSpec(memory_space=pl.ANY)],
        out_specs=col_spec,
        out_shape=jax.ShapeDtypeStruct((nt, PEER_SLOTS, tt), jnp.float32),
        scratch_shapes=scratch, compiler_params=params, name="peer_act",
    )(ids, xn, gt, _pack_rows(u_tab))
    return pl.pallas_call(
        _peer_out_kernel,
        grid=(nt,),
        in_specs=[ids_spec, col_spec, pl.BlockSpec(memory_space=pl.ANY)],
        out_specs=row_spec,
        out_shape=jax.ShapeDtypeStruct((T, d), jnp.float32),
        scratch_shapes=scratch, compiler_params=params, name="peer_out",
    )(ids, w, _pack_rows(v_tab))


def peer_ffn(h, norm_g, w_q, keys1, keys2, u_tab, v_tab):
    b, s, d = h.shape
    xn, idsT, gateT = peer_route(h.reshape(b * s, d), norm_g, w_q, keys1, keys2)
    return peer_experts(xn, idsT, gateT, u_tab, v_tab).reshape(b, s, d)


def per_layer_embed(h, p_i, norm_g, w_up, w_gate):
    return _mm(p_i, None, w_up, False) * jax.nn.sigmoid(_mm(h, norm_g, w_gate, True))


def kernel(x, p, a_norm_g, a_w_in, a_conv_w, a_conv_b, a_dt_bias, a_log, a_d_skip, a_gnorm_g, a_w_out,
           kv_norm_g, w_kv, cmp_pos_k, cmp_pos_v, cmp_w1_k, cmp_w2_k, cmp_w1_v, cmp_w2_v, rel_bias,
           b_norm_g, b_w_in, b_w_out, c_norm_g, c_w_q, c_keys1, c_keys2, c_u, c_v,
           e_norm_g, e_w_up, e_w_gate, final_g):
    h = x
    shared = None
    for i in range(DEPTH):
        if i < N_A_LAYERS:
            h = ssd_mixer_residual(h, a_norm_g[i], a_w_in[i], a_conv_w[i], a_conv_b[i], a_dt_bias[i],
                                   a_log[i], a_d_skip[i], a_gnorm_g[i], a_w_out[i])
        else:
            if shared is None:
                shared = nsa_shared_kv(h, kv_norm_g, w_kv, cmp_pos_k, cmp_pos_v,
                                       cmp_w1_k, cmp_w2_k, cmp_w1_v, cmp_w2_v)
            j = i - N_A_LAYERS
            h = h + nsa_mixer(h, b_norm_g[j], b_w_in[j], b_w_out[j], rel_bias, *shared)
        h = h + peer_ffn(h, c_norm_g[i], c_w_q[i], c_keys1[i], c_keys2[i], c_u[i], c_v[i])
        h = h + per_layer_embed(h, p[i], e_norm_g[i], e_w_up[i], e_w_gate[i])
    return rms_norm(h, final_g)
```

```python
import functools
import math

import jax
import jax.numpy as jnp
import numpy as np
from jax import lax
from jax.experimental import pallas as pl
from jax.experimental.pallas import tpu as pltpu

D_MODEL = 1024
BATCH = 2
SEQ = 16384
DEPTH = 2
N_A_LAYERS = DEPTH // 2
N_B_LAYERS = DEPTH - N_A_LAYERS

SSD_D_INNER = 2 * D_MODEL
SSD_HEADDIM = 64
SSD_HEADS = SSD_D_INNER // SSD_HEADDIM
SSD_GROUPS = 8
SSD_HEADS_PER_GROUP = SSD_HEADS // SSD_GROUPS
SSD_STATE = 128
SSD_CONV = 4
SSD_CHUNK = 128
SSD_CONV_DIM = SSD_D_INNER + 2 * SSD_GROUPS * SSD_STATE
SSD_IN_DIM = SSD_D_INNER + SSD_CONV_DIM + SSD_HEADS

NSA_HEADS = 16
NSA_KV_HEADS = 4
NSA_GROUP = NSA_HEADS // NSA_KV_HEADS
NSA_HEAD_DIM = 64
CMP_BLOCK = 32
CMP_STRIDE = 16
CMP_HIDDEN = 256
SLC_BLOCK = 64
SLC_TOPN = 16
SLC_FORCE = 1e4
WINDOW = 512
Q_BLOCK = 128
NSA_IN_DIM = NSA_HEADS * NSA_HEAD_DIM + 3 * NSA_HEADS
KV_DIM = 6 * NSA_KV_HEADS * NSA_HEAD_DIM

REL_BUCKETS = 32
REL_MAX_DISTANCE = 4096

PEER_HEADS = 8
PEER_N_KEYS = 128
PEER_EXPERTS = PEER_N_KEYS * PEER_N_KEYS
PEER_QDIM = 256
PEER_TOPK = 16
PEER_TOKEN_CHUNK = 128
PEER_V_SCALE = PEER_HEADS ** -0.5

PLE_DIM = 256
NORM_EPS = 1e-6
NEG_INF = -1e30

V7X_LANES = 128
V7X_VMEM_LIMIT_BYTES = 56 * 1024 * 1024


def _norm_matmul_kernel(x_ref, g_ref, w_ref, o_ref, *, normalize):
    x = x_ref[...]
    if normalize:
        x = x * lax.rsqrt(jnp.mean(x * x, axis=-1, keepdims=True) + NORM_EPS) * g_ref[...]
    o_ref[...] = jnp.dot(x.astype(jnp.bfloat16), w_ref[...], preferred_element_type=jnp.float32)


def norm_matmul(x, g, w, *, normalize, tm=512, tn=None):
    m, k = x.shape
    n = w.shape[1]
    n_pad = -(-n // V7X_LANES) * V7X_LANES
    wb = w.astype(jnp.bfloat16)
    if n_pad != n:
        wb = jnp.pad(wb, ((0, 0), (0, n_pad - n)))
    if tn is None:
        tn = n_pad
    assert m % tm == 0 and n_pad % tn == 0
    out = pl.pallas_call(
        functools.partial(_norm_matmul_kernel, normalize=normalize),
        grid=(m // tm, n_pad // tn),
        in_specs=[
            pl.BlockSpec((tm, k), lambda i, j: (i, 0)),
            pl.BlockSpec((1, k), lambda i, j: (0, 0)),
            pl.BlockSpec((k, tn), lambda i, j: (0, j)),
        ],
        out_specs=pl.BlockSpec((tm, tn), lambda i, j: (i, j)),
        out_shape=jax.ShapeDtypeStruct((m, n_pad), jnp.float32),
        compiler_params=pltpu.CompilerParams(
            dimension_semantics=("arbitrary", "arbitrary"),
            vmem_limit_bytes=V7X_VMEM_LIMIT_BYTES),
        name="norm_matmul" if normalize else "matmul",
    )(x, g.reshape(1, k), wb)
    return out[:, :n] if n_pad != n else out


def _mm(x3, g, w, normalize, **kw):
    b, s, k = x3.shape
    if g is None:
        g = jnp.ones((k,), jnp.float32)
    return norm_matmul(x3.reshape(b * s, k), g, w, normalize=normalize, **kw).reshape(b, s, w.shape[1])


def rms_norm(x, g):
    xf = x.astype(jnp.float32)
    y = xf * lax.rsqrt(jnp.mean(xf * xf, axis=-1, keepdims=True) + NORM_EPS)
    return (y * g.astype(jnp.float32)).astype(x.dtype)


def rel_bucket(dist):
    dist = jnp.maximum(dist, 0)
    max_exact = REL_BUCKETS // 2
    d = jnp.maximum(dist, 1).astype(jnp.float32)
    large = max_exact + (jnp.log(d / max_exact) / math.log(REL_MAX_DISTANCE / max_exact)
                         * (REL_BUCKETS - max_exact)).astype(jnp.int32)
    large = jnp.minimum(large, REL_BUCKETS - 1)
    return jnp.where(dist < max_exact, dist, large)


SSD_GROUP_W = SSD_HEADS_PER_GROUP * SSD_HEADDIM
SSD_IN_PAD = -(-SSD_IN_DIM // V7X_LANES) * V7X_LANES
SSD_TAIL = 8


def _split3(x):
    bf16, f32 = jnp.bfloat16, jnp.float32
    hi = x.astype(bf16)
    r1 = x - hi.astype(f32)
    mid = r1.astype(bf16)
    lo = (r1 - mid.astype(f32)).astype(bf16)
    return hi, mid, lo


def _ssd_kernel(zx_ref, h_ref, cw_ref, cb_ref, dtb_ref, alog_ref, dskip_ref, gn_ref, wout_ref, tri_ref, exp_ref,
                o_ref, xs_scr, state_scr):
    f32, bf16 = jnp.float32, jnp.bfloat16
    L, N, GW = SSD_CHUNK, SSD_STATE, SSD_GROUP_W
    c = pl.program_id(1)

    @pl.when(c == 0)
    def _():
        xs_scr[0:SSD_TAIL, :] = jnp.zeros((SSD_TAIL, SSD_CONV_DIM), f32)
        state_scr[...] = jnp.zeros_like(state_scr)

    xs_scr[SSD_TAIL:SSD_TAIL + L, :] = zx_ref[:, SSD_D_INNER:SSD_D_INNER + SSD_CONV_DIM]
    conv = cb_ref[...]
    for w in range(SSD_CONV):
        conv = conv + cw_ref[w:w + 1, :] * xs_scr[pl.ds(SSD_TAIL - (SSD_CONV - 1) + w, L), :]
    xs_scr[0:SSD_TAIL, :] = xs_scr[L:L + SSD_TAIL, :]
    xbc = conv * jax.nn.sigmoid(conv)

    dt_in = zx_ref[:, SSD_D_INNER + SSD_CONV_DIM:SSD_IN_PAD] + dtb_ref[...]
    dt = jnp.maximum(dt_in, 0.0) + jnp.log1p(jnp.exp(-jnp.abs(dt_in)))
    a_dt = dt * -jnp.exp(alog_ref[...])
    a_cs = sum(jnp.dot(tri_ref[...], t, preferred_element_type=f32) for t in _split3(a_dt))
    a_csT = a_cs.T
    a_end = a_cs[L - 1:L, :]
    grow = jnp.exp(a_cs)
    to_end = jnp.exp(a_end - a_cs)

    def per_channel(q):
        return sum(jnp.dot(t, exp_ref[...], preferred_element_type=f32) for t in _split3(q)[:2])

    dt_c, grow_c, to_end_c = per_channel(dt), per_channel(grow), per_channel(to_end)
    end_c = per_channel(jnp.broadcast_to(jnp.exp(a_end), (8, V7X_LANES)))[0:1]
    dskip_c = per_channel(jnp.broadcast_to(dskip_ref[...], (8, V7X_LANES)))[0:1]

    x = xbc[:, :SSD_D_INNER]
    xd = x * dt_c
    xd_bf = xd.astype(bf16)
    xe_bf = (xd * to_end_c).astype(bf16)
    row_i = lax.broadcasted_iota(jnp.int32, (L, L), 0)
    col_i = lax.broadcasted_iota(jnp.int32, (L, L), 1)
    lane_head = lax.broadcasted_iota(jnp.int32, (L, GW), 1) // SSD_HEADDIM
    ys = []
    for g in range(SSD_GROUPS):
        bm = xbc[:, SSD_D_INNER + g * N:SSD_D_INNER + (g + 1) * N]
        cm = xbc[:, SSD_D_INNER + SSD_GROUPS * N + g * N:SSD_D_INNER + SSD_GROUPS * N + (g + 1) * N].astype(bf16)
        bmT = bm.T.astype(bf16)
        cb = jnp.dot(cm, bmT, preferred_element_type=f32)
        xd_g = xd_bf[:, g * GW:(g + 1) * GW]
        y_g = jnp.zeros((L, GW), f32)
        for r in range(SSD_HEADS_PER_GROUP):
            hd = g * SSD_HEADS_PER_GROUP + r
            seg = a_cs[:, hd:hd + 1] - a_csT[hd:hd + 1, :]
            decay = jnp.where(row_i >= col_i, jnp.exp(seg), 0.0)
            xr = jnp.where(lane_head == r, xd_g, jnp.zeros_like(xd_g))
            y_g = y_g + jnp.dot((cb * decay).astype(bf16), xr, preferred_element_type=f32)
        hT = state_scr[g]
        y_off = jnp.dot(cm, hT.astype(bf16), preferred_element_type=f32) * grow_c[:, g * GW:(g + 1) * GW]
        new_states = jnp.dot(bmT, xe_bf[:, g * GW:(g + 1) * GW], preferred_element_type=f32)
        state_scr[g] = hT * end_c[:, g * GW:(g + 1) * GW] + new_states
        ys.append(y_g + y_off)
    y = jnp.concatenate(ys, axis=1) + x * dskip_c

    z = zx_ref[:, :SSD_D_INNER]
    yz = y * (z * jax.nn.sigmoid(z))
    parts = []
    for g in range(SSD_GROUPS):
        t = yz[:, g * GW:(g + 1) * GW]
        parts.append(t * lax.rsqrt(jnp.mean(t * t, axis=-1, keepdims=True) + NORM_EPS))
    yn = jnp.concatenate(parts, axis=1) * gn_ref[...]
    o_ref[...] = h_ref[...] + jnp.dot(yn.astype(bf16), wout_ref[...], preferred_element_type=f32)


def ssd_mixer_residual(h, norm_g, w_in, conv_w, conv_b, dt_bias, a_log, d_skip, gnorm_g, w_out):
    b, s, d = h.shape
    L = SSD_CHUNK
    nc = s // L
    w_in_p = jnp.pad(w_in, ((0, 0), (0, SSD_IN_PAD - SSD_IN_DIM)))
    zx = norm_matmul(h.reshape(b * s, d), norm_g, w_in_p, normalize=True, tm=256)
    lane_pad = lambda v: jnp.pad(v.reshape(1, -1), ((0, 0), (0, V7X_LANES - v.shape[-1])))
    tri = jnp.asarray(np.tril(np.ones((L, L))), jnp.bfloat16)
    head_of = np.arange(SSD_D_INNER) // SSD_HEADDIM
    expand = jnp.asarray(np.arange(V7X_LANES)[:, None] == head_of[None, :], jnp.bfloat16)
    const = lambda *shape: pl.BlockSpec(shape, lambda bi, ci: (0,) * len(shape))
    out = pl.pallas_call(
        _ssd_kernel,
        grid=(b, nc),
        in_specs=[pl.BlockSpec((L, SSD_IN_PAD), lambda bi, ci: (bi * nc + ci, 0)),
                  pl.BlockSpec((L, d), lambda bi, ci: (bi * nc + ci, 0)),
                  const(SSD_CONV, SSD_CONV_DIM), const(1, SSD_CONV_DIM),
                  const(1, V7X_LANES), const(1, V7X_LANES), const(1, V7X_LANES),
                  const(1, SSD_D_INNER), const(SSD_D_INNER, d), const(L, L), const(V7X_LANES, SSD_D_INNER)],
        out_specs=pl.BlockSpec((L, d), lambda bi, ci: (bi * nc + ci, 0)),
        out_shape=jax.ShapeDtypeStruct((b * s, d), jnp.float32),
        scratch_shapes=[pltpu.VMEM((L + SSD_TAIL, SSD_CONV_DIM), jnp.float32),
                        pltpu.VMEM((SSD_GROUPS, SSD_STATE, SSD_GROUP_W), jnp.float32)],
        compiler_params=pltpu.CompilerParams(dimension_semantics=("arbitrary", "arbitrary"),
                                             vmem_limit_bytes=V7X_VMEM_LIMIT_BYTES),
        name="ssd_mixer",
    )(zx, h.reshape(b * s, d), conv_w, conv_b.reshape(1, -1), lane_pad(dt_bias), lane_pad(a_log), lane_pad(d_skip),
      gnorm_g.reshape(1, -1), w_out.astype(jnp.bfloat16), tri, expand)
    return out.reshape(b, s, d)


def _mlp_kernel(x_ref, w1_ref, w2_ref, o_ref):
    hid = jnp.dot(x_ref[...].astype(jnp.bfloat16), w1_ref[...], preferred_element_type=jnp.float32)
    hid = jax.nn.gelu(hid)
    o_ref[...] = jnp.dot(hid.astype(jnp.bfloat16), w2_ref[...], preferred_element_type=jnp.float32)


def mlp2(x, w1, w2, *, tm=512):
    m, k = x.shape
    hdim, n = w2.shape
    assert m % tm == 0
    return pl.pallas_call(
        _mlp_kernel,
        grid=(m // tm,),
        in_specs=[pl.BlockSpec((tm, k), lambda i: (i, 0)),
                  pl.BlockSpec((k, hdim), lambda i: (0, 0)),
                  pl.BlockSpec((hdim, n), lambda i: (0, 0))],
        out_specs=pl.BlockSpec((tm, n), lambda i: (i, 0)),
        out_shape=jax.ShapeDtypeStruct((m, n), jnp.float32),
        compiler_params=pltpu.CompilerParams(dimension_semantics=("arbitrary",),
                                             vmem_limit_bytes=V7X_VMEM_LIMIT_BYTES),
        name="cmp_mlp",
    )(x, w1.astype(jnp.bfloat16), w2.astype(jnp.bfloat16))


def compress_blocks(t, pos, w1, w2):
    b, s, g, d = t.shape
    halves = t.transpose(0, 2, 1, 3).reshape(b, g, s // CMP_STRIDE, CMP_STRIDE, d)
    nxt = jnp.concatenate([halves[:, :, 1:], jnp.zeros_like(halves[:, :, :1])], axis=2)
    blocks = jnp.concatenate([halves, nxt], axis=3) + pos
    flat = blocks.reshape(b * g * (s // CMP_STRIDE), CMP_BLOCK * d)
    return mlp2(flat, w1, w2).reshape(b, g, s // CMP_STRIDE, d)


N_SEL = SEQ // SLC_BLOCK
N_KT = SEQ // Q_BLOCK
CMP_FRONT = 256
CMP_ROWS = CMP_FRONT + SEQ // CMP_STRIDE
CMP_WIN = 256
REL_TABLE = 4096
NEAR_TILES = 24
WIN_TILES = WINDOW // Q_BLOCK + 1
SEL_GROUP = 8
QL = NSA_GROUP * Q_BLOCK


def _nsa_kernel(qT_ref, gT_ref, kc_ref, vcT_ref, kk_ref, vsT_ref, vwT_ref, at_ref, bcT_ref, bT_ref, bTw_ref,
                o_ref, s_scr, sel_scr):
    f32, bf16 = jnp.float32, jnp.bfloat16
    i = pl.program_id(2)
    qT = (qT_ref[0, 0, 0] * (NSA_HEAD_DIM ** -0.5)).astype(bf16)
    zq = jnp.zeros_like(qT)
    q_sel = jnp.concatenate([qT, zq], axis=0)
    q_win = jnp.concatenate([zq, qT], axis=0)

    r0 = pl.multiple_of(8 * i + 8, 8)
    s = jnp.dot(kc_ref[0, 0], qT, preferred_element_type=f32)
    rio = lax.broadcasted_iota(jnp.int32, (CMP_ROWS, Q_BLOCK), 0)
    rowmask = jnp.where((rio >= CMP_FRONT) & (rio < r0 + CMP_WIN), 0.0, NEG_INF)
    s_scr[...] = s + jnp.concatenate([rowmask] * NSA_GROUP, axis=1)
    s_scr[pl.ds(r0, CMP_WIN), :] = s_scr[pl.ds(r0, CMP_WIN), :] + bcT_ref[0]
    s = s_scr[...]
    m = jnp.max(s, axis=0, keepdims=True)
    p = jnp.exp(s - m)
    l = jnp.sum(p, axis=0, keepdims=True)
    inv = jnp.where(m > 0.1 * NEG_INF, 1.0 / l, 0.0)
    pn = p * inv
    o_c = jnp.dot(vcT_ref[0, 0], pn.astype(bf16), preferred_element_type=f32)

    psum = pn[:, 0:Q_BLOCK]
    for r in range(1, NSA_GROUP):
        psum = psum + pn[:, r * Q_BLOCK:(r + 1) * Q_BLOCK]
    hi = psum.astype(bf16)
    lo = (psum - hi.astype(f32)).astype(bf16)
    imp = (jnp.dot(at_ref[...], hi, preferred_element_type=f32)
           + jnp.dot(at_ref[...], lo, preferred_element_type=f32))
    blk = lax.broadcasted_iota(jnp.int32, (N_SEL, Q_BLOCK), 0)
    qi = lax.broadcasted_iota(jnp.int32, (N_SEL, Q_BLOCK), 1)
    cur = 2 * i + (qi >= SLC_BLOCK).astype(jnp.int32)
    forced = (blk == 0) | (blk == cur) | (blk == cur - 1)
    valid = blk <= cur
    score = jnp.where(forced, SLC_FORCE, jnp.where(valid, imp, -SLC_FORCE))
    blkf = blk.astype(f32)
    chosen = jnp.zeros((N_SEL, Q_BLOCK), f32)
    for _ in range(SLC_TOPN):
        mx = jnp.max(score, axis=0, keepdims=True)
        first = jnp.min(jnp.where(score == mx, blkf, float(N_SEL)), axis=0, keepdims=True)
        hit = blkf == first
        chosen = jnp.where(hit, 1.0, chosen)
        score = jnp.where(hit, -jnp.inf, score)
    sel_scr[...] = jnp.where((chosen > 0.5) & valid, 0.0, NEG_INF)

    def attend(kt, carry, q_aug, vT_ref, bias):
        m_run, l_run, acc = carry
        scs = [jnp.dot(kk_ref[0, 0, k], q_aug, preferred_element_type=f32) + b for k, b in zip(kt, bias)]
        for k, sc in zip(kt, scs):
            m_new = jnp.maximum(m_run, jnp.max(sc, axis=0, keepdims=True))
            alpha = jnp.exp(m_run - m_new)
            pt = jnp.exp(sc - m_new)
            l_run = alpha * l_run + jnp.sum(pt, axis=0, keepdims=True)
            acc = alpha * acc + jnp.dot(vT_ref[0, 0, k], pt.astype(bf16), preferred_element_type=f32)
            m_run = m_new
        return m_run, l_run, acc

    def sel_mask(kt):
        m0 = jnp.broadcast_to(sel_scr[pl.ds(2 * kt, 1), :], (SLC_BLOCK, Q_BLOCK))
        m1 = jnp.broadcast_to(sel_scr[pl.ds(2 * kt + 1, 1), :], (SLC_BLOCK, Q_BLOCK))
        mk = jnp.concatenate([m0, m1], axis=0)
        return jnp.concatenate([mk] * NSA_GROUP, axis=1)

    init = (jnp.full((1, QL), NEG_INF, f32), jnp.zeros((1, QL), f32), jnp.zeros((NSA_HEAD_DIM, QL), f32))
    U = SEL_GROUP

    def far_group(g, c):
        kts = [g * U + u for u in range(U)]
        return attend(kts, c, q_sel, vsT_ref, [sel_mask(k) for k in kts])

    def near_group(g, c):
        kts, biases = [], []
        for u in range(U):
            delta = i - (g * U + u)
            k = jnp.minimum(g * U + u, i)
            tile = jnp.where(delta < 0, NEAR_TILES + 1, jnp.minimum(delta, NEAR_TILES))
            kts.append(k)
            biases.append(sel_mask(k) + bT_ref[0, tile])
        return attend(kts, c, q_sel, vsT_ref, biases)

    n_far = jnp.maximum((i - (NEAR_TILES - 1)) // U, 0)
    carry = lax.fori_loop(0, n_far, far_group, init)
    _, l_s, acc_s = lax.fori_loop(n_far, i // U + 1, near_group, carry)
    o_s = acc_s / l_s

    kts, biases = [], []
    for u in range(WIN_TILES):
        k = i - (WIN_TILES - 1) + u
        kts.append(jnp.maximum(k, 0))
        biases.append(bTw_ref[0, jnp.where(k >= 0, WIN_TILES - 1 - u, WIN_TILES)])
    _, l_w, acc_w = attend(kts, init, q_win, vwT_ref, biases)
    o_w = acc_w / l_w

    gate = jax.nn.sigmoid(gT_ref[0, 0, 0])
    o_ref[0, 0, 0] = gate[0:1] * o_c + gate[1:2] * o_s + gate[2:3] * o_w


BIAS_EXT = REL_TABLE + 2 * Q_BLOCK


CMP_OFF = CMP_STRIDE * (CMP_WIN - 8) - (CMP_BLOCK - 1)


def _bias_tile_kernel(bd_ref, bdc_ref, cmp_ref, near_ref, win_ref):
    row = bd_ref[0]
    kj = lax.broadcasted_iota(jnp.int32, (Q_BLOCK, Q_BLOCK), 0)
    qi = lax.broadcasted_iota(jnp.int32, (Q_BLOCK, Q_BLOCK), 1)
    for t in range(NEAR_TILES):
        seg = jnp.broadcast_to(row[:, t * Q_BLOCK:(t + 2) * Q_BLOCK], (Q_BLOCK, 2 * Q_BLOCK))
        tile = pltpu.roll(seg, 0, 1, stride=1, stride_axis=0)[:, Q_BLOCK:]
        near_ref[0, t] = tile
        if t < WIN_TILES:
            win_ref[0, t] = jnp.where(t * Q_BLOCK + qi - kj < WINDOW, tile, NEG_INF)
    near_ref[0, NEAR_TILES] = jnp.zeros((Q_BLOCK, Q_BLOCK), jnp.float32)
    near_ref[0, NEAR_TILES + 1] = jnp.full((Q_BLOCK, Q_BLOCK), NEG_INF, jnp.float32)
    win_ref[0, WIN_TILES] = jnp.full((Q_BLOCK, Q_BLOCK), NEG_INF, jnp.float32)
    full = jnp.broadcast_to(bdc_ref[0], (CMP_WIN, BIAS_EXT))
    cmp_ref[0] = pltpu.roll(full, 0, 1, stride=CMP_STRIDE, stride_axis=0)[:, REL_TABLE:REL_TABLE + Q_BLOCK]


def _bias_tables(rel_bias):
    G, R = NSA_KV_HEADS, NSA_GROUP
    bd = rel_bias[rel_bucket(jnp.arange(REL_TABLE))]
    bd = (bd - bd[REL_TABLE - 1]).T
    def padded(front):
        pad = lambda n: jnp.full((NSA_HEADS, n), NEG_INF, jnp.float32)
        return jnp.concatenate([pad(front), bd, pad(BIAS_EXT - REL_TABLE - front)], axis=1).reshape(NSA_HEADS, 1, BIAS_EXT)

    bd_ext, bd_cmp = padded(Q_BLOCK), padded(REL_TABLE - CMP_OFF)
    head = lambda *blk: pl.BlockSpec((1,) + blk + (Q_BLOCK,), lambda hd: (hd // R,) + (0,) * len(blk) + (hd % R,))
    return pl.pallas_call(
        _bias_tile_kernel,
        grid=(NSA_HEADS,),
        in_specs=[pl.BlockSpec((1, 1, BIAS_EXT), lambda hd: (hd, 0, 0))] * 2,
        out_specs=[head(CMP_WIN), head(NEAR_TILES + 2, Q_BLOCK), head(WIN_TILES + 1, Q_BLOCK)],
        out_shape=[jax.ShapeDtypeStruct((G, CMP_WIN, QL), jnp.float32),
                   jax.ShapeDtypeStruct((G, NEAR_TILES + 2, Q_BLOCK, QL), jnp.float32),
                   jax.ShapeDtypeStruct((G, WIN_TILES + 1, Q_BLOCK, QL), jnp.float32)],
        compiler_params=pltpu.CompilerParams(dimension_semantics=("arbitrary",),
                                             vmem_limit_bytes=V7X_VMEM_LIMIT_BYTES),
        name="nsa_bias_tiles",
    )(bd_ext, bd_cmp)


def _pool_matrix():
    per = SLC_BLOCK // CMP_STRIDE
    n_span = CMP_BLOCK // CMP_STRIDE
    k = np.arange(CMP_ROWS)[None, :] - CMP_FRONT
    j = np.arange(N_SEL)[:, None]
    return jnp.asarray((k >= per * j - (n_span - 1)) & (k <= per * j + per - 1), jnp.bfloat16)


def nsa_shared_kv(h, kv_norm_g, w_kv, cmp_pos_k, cmp_pos_v, cmp_w1_k, cmp_w2_k, cmp_w1_v, cmp_w2_v):
    b, s, _ = h.shape
    G, DH = NSA_KV_HEADS, NSA_HEAD_DIM
    bf16 = jnp.bfloat16
    kv = _mm(h, kv_norm_g, w_kv, True).reshape(b, s, 6, G, DH)
    k_cmp = compress_blocks(kv[:, :, 0], cmp_pos_k, cmp_w1_k, cmp_w2_k)
    v_cmp = compress_blocks(kv[:, :, 1], cmp_pos_v, cmp_w1_v, cmp_w2_v)
    front = jnp.zeros((b, G, CMP_FRONT, DH), bf16)
    kc = jnp.concatenate([front, k_cmp.astype(bf16)], axis=2)
    vcT = jnp.concatenate([front, v_cmp.astype(bf16)], axis=2).transpose(0, 1, 3, 2)
    kk = jnp.concatenate([kv[:, :, 2], kv[:, :, 4]], axis=-1).astype(bf16)
    kk = kk.transpose(0, 2, 1, 3).reshape(b, G, N_KT, Q_BLOCK, 2 * DH)

    def vt(v):
        return v.astype(bf16).transpose(0, 2, 1, 3).reshape(b, G, N_KT, Q_BLOCK, DH).transpose(0, 1, 2, 4, 3)

    return kc, vcT, kk, vt(kv[:, :, 3]), vt(kv[:, :, 5])


def nsa_mixer(h, norm_g, w_in, w_out, rel_bias, kc, vcT, kk, vsT, vwT):
    b, s, _ = h.shape
    G, R, DH = NSA_KV_HEADS, NSA_GROUP, NSA_HEAD_DIM
    nb = s // Q_BLOCK
    proj = _mm(h, norm_g, w_in, True)
    qT = proj[..., :NSA_HEADS * DH].reshape(b, nb, Q_BLOCK, G, R, DH).transpose(0, 3, 1, 5, 4, 2)
    qT = qT.reshape(b, G, nb, DH, QL)
    gT = proj[..., NSA_HEADS * DH:].reshape(b, nb, Q_BLOCK, G, R, 3).transpose(0, 3, 1, 5, 4, 2)
    gT = gT.reshape(b, G, nb, 3, QL)
    b_cmp, b_near, b_win = _bias_tables(rel_bias)
    at = _pool_matrix()

    per_bg = lambda *blk: pl.BlockSpec((1, 1) + blk, lambda bi, gi, i: (bi, gi) + (0,) * len(blk))
    per_g = lambda *blk: pl.BlockSpec((1,) + blk, lambda bi, gi, i: (gi,) + (0,) * len(blk))
    oT = pl.pallas_call(
        _nsa_kernel,
        grid=(b, G, nb),
        in_specs=[
            pl.BlockSpec((1, 1, 1, DH, QL), lambda bi, gi, i: (bi, gi, i, 0, 0)),
            pl.BlockSpec((1, 1, 1, 3, QL), lambda bi, gi, i: (bi, gi, i, 0, 0)),
            per_bg(CMP_ROWS, DH),
            per_bg(DH, CMP_ROWS),
            per_bg(N_KT, Q_BLOCK, 2 * DH),
            per_bg(N_KT, DH, Q_BLOCK),
            per_bg(N_KT, DH, Q_BLOCK),
            pl.BlockSpec((N_SEL, CMP_ROWS), lambda bi, gi, i: (0, 0)),
            per_g(CMP_WIN, QL),
            per_g(NEAR_TILES + 2, Q_BLOCK, QL),
            per_g(WIN_TILES + 1, Q_BLOCK, QL),
        ],
        out_specs=pl.BlockSpec((1, 1, 1, DH, QL), lambda bi, gi, i: (bi, gi, i, 0, 0)),
        out_shape=jax.ShapeDtypeStruct((b, G, nb, DH, QL), jnp.float32),
        scratch_shapes=[pltpu.VMEM((CMP_ROWS, QL), jnp.float32), pltpu.VMEM((N_SEL, Q_BLOCK), jnp.float32)],
        compiler_params=pltpu.CompilerParams(
            dimension_semantics=("arbitrary", "arbitrary", "arbitrary"),
            vmem_limit_bytes=V7X_VMEM_LIMIT_BYTES),
        name="nsa_attention",
    )(qT, gT, kc, vcT, kk, vsT, vwT, at, b_cmp, b_near, b_win)
    o = oT.reshape(b, G, nb, DH, R, Q_BLOCK).transpose(0, 2, 5, 1, 4, 3).reshape(b, s, NSA_HEADS * DH)
    return _mm(o, None, w_out, False)


PEER_SLOTS = PEER_HEADS * PEER_TOPK
PEER_ROUTE_TM = 256
PEER_ROUTE_LOCKSTEP = 2
PEER_TT = 16
PEER_CHUNKS = D_MODEL // V7X_LANES
PEER_WORDS = PEER_CHUNKS // 2
PEER_TILE_PITCH = PEER_SLOTS + 8


def _top_rows(arrays, rowid, n, payloads=None):
    arrays = list(arrays)
    vals = [[] for _ in arrays]
    picks = [[] for _ in arrays]
    for _ in range(n):
        for a, s in enumerate(arrays):
            mx = jnp.max(s, axis=0, keepdims=True)
            first = jnp.min(jnp.where(s == mx, rowid, jnp.inf), axis=0, keepdims=True)
            hit = rowid == first
            vals[a].append(mx)
            picks[a].append(first if payloads is None
                            else jnp.max(jnp.where(hit, payloads[a], -1.0), axis=0, keepdims=True))
            arrays[a] = jnp.where(hit, -jnp.inf, s)
    return vals, picks


def _peer_route_kernel(h_ref, g_ref, wqT_ref, k1_ref, k2_ref, xn_ref, ids_ref, gate_ref):
    f32, bf16 = jnp.float32, jnp.bfloat16
    x = h_ref[...]
    xn = x * lax.rsqrt(jnp.mean(x * x, axis=-1, keepdims=True) + NORM_EPS) * g_ref[...]
    xn_ref[...] = xn
    qT = lax.dot_general(wqT_ref[...], xn.astype(bf16), (((1,), (1,)), ((), ())), preferred_element_type=f32)
    half = PEER_QDIM // 2
    tm = x.shape[0]
    key_id = lax.broadcasted_iota(jnp.int32, (PEER_N_KEYS, tm), 0).astype(f32)
    stair = [PEER_TOPK // (a + 1) for a in range(PEER_TOPK)]
    n_pad = -sum(stair) % 8
    cand_pos = jnp.concatenate(
        [lax.broadcasted_iota(jnp.int32, (nb, tm), 0).astype(f32) + float(a * PEER_TOPK) for a, nb in enumerate(stair)]
        + [jnp.full((n_pad, tm), float(PEER_TOPK * PEER_TOPK), f32)], axis=0)
    for h0 in range(0, PEER_HEADS, PEER_ROUTE_LOCKSTEP):
        heads = range(h0, h0 + PEER_ROUTE_LOCKSTEP)
        scores = []
        for hd in heads:
            q1 = qT[hd * PEER_QDIM: hd * PEER_QDIM + half].astype(bf16)
            q2 = qT[hd * PEER_QDIM + half: (hd + 1) * PEER_QDIM].astype(bf16)
            scores.append(jnp.dot(k1_ref[hd], q1, preferred_element_type=f32))
            scores.append(jnp.dot(k2_ref[hd], q2, preferred_element_type=f32))
        vals, idxs = _top_rows(scores, key_id, PEER_TOPK)
        cands, cand_ids = [], []
        for a in range(PEER_ROUTE_LOCKSTEP):
            v1, i1 = vals[2 * a], idxs[2 * a]
            v2m = jnp.concatenate(vals[2 * a + 1], axis=0)
            i2m = jnp.concatenate(idxs[2 * a + 1], axis=0)
            cands.append(jnp.concatenate([v1[r] + v2m[0:nb] for r, nb in enumerate(stair)]
                                         + [jnp.full((n_pad, tm), -jnp.inf, f32)], axis=0))
            cand_ids.append(jnp.concatenate([i1[r] * float(PEER_N_KEYS) + i2m[0:nb] for r, nb in enumerate(stair)]
                                            + [jnp.full((n_pad, tm), -1.0, f32)], axis=0))
        tops, ids = _top_rows(cands, cand_pos, PEER_TOPK, payloads=cand_ids)
        for a, hd in enumerate(heads):
            top_s = jnp.concatenate(tops[a], axis=0)
            e = jnp.exp(top_s - top_s[0:1])
            gate_ref[hd * PEER_TOPK:(hd + 1) * PEER_TOPK, :] = e / jnp.sum(e, axis=0, keepdims=True)
            ids_ref[hd * PEER_TOPK:(hd + 1) * PEER_TOPK, :] = (jnp.concatenate(ids[a], axis=0)
                                                               * PEER_WORDS).astype(jnp.int32)


def peer_route(h2, norm_g, w_q, keys1, keys2):
    T, d = h2.shape
    tm = PEER_ROUTE_TM
    bf16 = jnp.bfloat16
    nq = PEER_HEADS * PEER_QDIM
    half = PEER_QDIM // 2
    return pl.pallas_call(
        _peer_route_kernel,
        grid=(T // tm,),
        in_specs=[pl.BlockSpec((tm, d), lambda i: (i, 0)),
                  pl.BlockSpec((1, d), lambda i: (0, 0)),
                  pl.BlockSpec((nq, d), lambda i: (0, 0)),
                  pl.BlockSpec((PEER_HEADS, PEER_N_KEYS, half), lambda i: (0, 0, 0)),
                  pl.BlockSpec((PEER_HEADS, PEER_N_KEYS, half), lambda i: (0, 0, 0))],
        out_specs=[pl.BlockSpec((tm, d), lambda i: (i, 0)),
                   pl.BlockSpec((PEER_SLOTS, tm), lambda i: (0, i)),
                   pl.BlockSpec((PEER_SLOTS, tm), lambda i: (0, i))],
        out_shape=[jax.ShapeDtypeStruct((T, d), jnp.float32),
                   jax.ShapeDtypeStruct((PEER_SLOTS, T), jnp.int32),
                   jax.ShapeDtypeStruct((PEER_SLOTS, T), jnp.float32)],
        compiler_params=pltpu.CompilerParams(dimension_semantics=("arbitrary",),
                                             vmem_limit_bytes=V7X_VMEM_LIMIT_BYTES),
        name="peer_route",
    )(h2, norm_g.reshape(1, d), w_q.T.astype(bf16), keys1.astype(bf16), keys2.astype(bf16))


def _unpack_pair(words):
    lo = lax.bitcast_convert_type(lax.shift_left(words, 16), jnp.float32)
    hi = lax.bitcast_convert_type(lax.bitwise_and(words, -65536), jnp.float32)
    return lo, hi


def _pack_rows(tab):
    bits = lax.bitcast_convert_type(tab.astype(jnp.bfloat16), jnp.uint16).astype(jnp.uint32)
    bits = bits.reshape(tab.shape[0], PEER_WORDS, 2, V7X_LANES)
    words = lax.bitcast_convert_type(bits[:, :, 0] | (bits[:, :, 1] << 16), jnp.int32)
    return words.reshape(tab.shape[0] * PEER_WORDS, V7X_LANES)


def _peer_gather_token(ids_ref, tab, tile, t):
    for k in range(PEER_SLOTS):
        row = pl.multiple_of(ids_ref[t * PEER_SLOTS + k], PEER_WORDS)
        tile[pl.ds(k, PEER_WORDS, stride=PEER_TILE_PITCH), :] = tab[pl.ds(row, PEER_WORDS), :]


def _peer_chunk_pair(tile, j):
    return _unpack_pair(tile[j * PEER_TILE_PITCH:j * PEER_TILE_PITCH + PEER_SLOTS, :])


def _peer_load_table(tab_hbm, tab, sem):
    @pl.when(pl.program_id(0) == 0)
    def _():
        copy = pltpu.make_async_copy(tab_hbm, tab, sem)
        copy.start()
        copy.wait()


def _peer_two_tiles(ids_hbm, ids_a, ids_b, sems, run_tile):
    i = pl.program_id(0)
    n = pl.num_programs(0)
    rows = PEER_TT * PEER_SLOTS

    def ids_copy(tile_index, dst, sem):
        return pltpu.make_async_copy(ids_hbm.at[pl.ds(pl.multiple_of(tile_index * rows, rows), rows)], dst, sem)

    @pl.when(i == 0)
    def _():
        first = ids_copy(0, ids_a, sems.at[0])
        first.start()
        first.wait()

    ids_copy(2 * i + 1, ids_b, sems.at[1]).start()
    run_tile(ids_a, 0)
    ids_copy(2 * i + 1, ids_b, sems.at[1]).wait()
    nxt = jnp.minimum(2 * i + 2, 2 * n - 2)
    ids_copy(nxt, ids_a, sems.at[0]).start()
    run_tile(ids_b, PEER_TT)
    ids_copy(nxt, ids_a, sems.at[0]).wait()


def _peer_act_kernel(ids_hbm, xn_ref, gt_ref, tab_hbm, w_ref, tab, tile_a, tile_b, ids_a, ids_b, sem, ids_sems):
    _peer_load_table(tab_hbm, tab, sem)

    def run_tile(ids_ref, first_token):
        for t in range(PEER_TT):
            tile = (tile_a, tile_b)[t % 2]
            _peer_gather_token(ids_ref, tab, tile, t)
            col = first_token + t
            xrow = xn_ref[col:col + 1, :]
            acc = jnp.zeros((PEER_SLOTS, V7X_LANES), jnp.float32)
            for j in range(PEER_WORDS):
                lo, hi = _peer_chunk_pair(tile, j)
                acc = acc + lo * xrow[:, 2 * j * V7X_LANES:(2 * j + 1) * V7X_LANES]
                acc = acc + hi * xrow[:, (2 * j + 1) * V7X_LANES:(2 * j + 2) * V7X_LANES]
            act = jnp.sum(acc, axis=1, keepdims=True)
            w_ref[0, :, col:col + 1] = gt_ref[0, :, col:col + 1] * jax.nn.gelu(act)

    _peer_two_tiles(ids_hbm, ids_a, ids_b, ids_sems, run_tile)


def _peer_out_kernel(ids_hbm, w_ref, tab_hbm, o_ref, tab, tile_a, tile_b, ids_a, ids_b, sem, ids_sems):
    _peer_load_table(tab_hbm, tab, sem)

    def run_tile(ids_ref, first_token):
        for t in range(PEER_TT):
            tile = (tile_a, tile_b)[t % 2]
            _peer_gather_token(ids_ref, tab, tile, t)
            col = first_token + t
            w = w_ref[0, :, col:col + 1]
            for j in range(PEER_WORDS):
                lo, hi = _peer_chunk_pair(tile, j)
                o_ref[col:col + 1, 2 * j * V7X_LANES:(2 * j + 1) * V7X_LANES] = jnp.sum(lo * w, axis=0, keepdims=True)
                o_ref[col:col + 1, (2 * j + 1) * V7X_LANES:(2 * j + 2) * V7X_LANES] = jnp.sum(hi * w, axis=0,
                                                                                               keepdims=True)

    _peer_two_tiles(ids_hbm, ids_a, ids_b, ids_sems, run_tile)


def peer_experts(xn, idsT, gateT, u_tab, v_tab):
    T, d = xn.shape
    tt = 2 * PEER_TT
    nt = T // tt
    ids = idsT.T.reshape(T * PEER_SLOTS)
    gt = gateT.reshape(PEER_SLOTS, nt, tt).transpose(1, 0, 2)
    any_spec = pl.BlockSpec(memory_space=pl.ANY)
    col_spec = pl.BlockSpec((1, PEER_SLOTS, tt), lambda i: (i, 0, 0))
    row_spec = pl.BlockSpec((tt, d), lambda i: (i, 0))
    tile = pltpu.VMEM((PEER_WORDS * PEER_TILE_PITCH, V7X_LANES), jnp.int32)
    tile_ids = pltpu.SMEM((PEER_TT * PEER_SLOTS,), jnp.int32)
    scratch = [pltpu.VMEM((PEER_EXPERTS * PEER_WORDS, V7X_LANES), jnp.int32), tile, tile, tile_ids, tile_ids,
               pltpu.SemaphoreType.DMA(()), pltpu.SemaphoreType.DMA((2,))]
    params = pltpu.CompilerParams(dimension_semantics=("arbitrary",), vmem_limit_bytes=V7X_VMEM_LIMIT_BYTES)
    w = pl.pallas_call(
        _peer_act_kernel,
        grid=(nt,),
        in_specs=[any_spec, row_spec, col_spec, any_spec],
        out_specs=col_spec,
        out_shape=jax.ShapeDtypeStruct((nt, PEER_SLOTS, tt), jnp.float32),
        scratch_shapes=scratch, compiler_params=params, name="peer_act",
    )(ids, xn, gt, _pack_rows(u_tab))
    return pl.pallas_call(
        _peer_out_kernel,
        grid=(nt,),
        in_specs=[any_spec, col_spec, any_spec],
        out_specs=row_spec,
        out_shape=jax.ShapeDtypeStruct((T, d), jnp.float32),
        scratch_shapes=scratch, compiler_params=params, name="peer_out",
    )(ids, w, _pack_rows(v_tab))


def peer_ffn(h, norm_g, w_q, keys1, keys2, u_tab, v_tab):
    b, s, d = h.shape
    xn, idsT, gateT = peer_route(h.reshape(b * s, d), norm_g, w_q, keys1, keys2)
    return peer_experts(xn, idsT, gateT, u_tab, v_tab).reshape(b, s, d)


def per_layer_embed(h, p_i, norm_g, w_up, w_gate):
    return _mm(p_i, None, w_up, False) * jax.nn.sigmoid(_mm(h, norm_g, w_gate, True))


def kernel(x, p, a_norm_g, a_w_in, a_conv_w, a_conv_b, a_dt_bias, a_log, a_d_skip, a_gnorm_g, a_w_out,
           kv_norm_g, w_kv, cmp_pos_k, cmp_pos_v, cmp_w1_k, cmp_w2_k, cmp_w1_v, cmp_w2_v, rel_bias,
           b_norm_g, b_w_in, b_w_out, c_norm_g, c_w_q, c_keys1, c_keys2, c_u, c_v,
           e_norm_g, e_w_up, e_w_gate, final_g):
    h = x
    shared = None
    for i in range(DEPTH):
        if i < N_A_LAYERS:
            h = ssd_mixer_residual(h, a_norm_g[i], a_w_in[i], a_conv_w[i], a_conv_b[i], a_dt_bias[i],
                                   a_log[i], a_d_skip[i], a_gnorm_g[i], a_w_out[i])
        else:
            if shared is None:
                shared = nsa_shared_kv(h, kv_norm_g, w_kv, cmp_pos_k, cmp_pos_v,
                                       cmp_w1_k, cmp_w2_k, cmp_w1_v, cmp_w2_v)
            j = i - N_A_LAYERS
            h = h + nsa_mixer(h, b_norm_g[j], b_w_in[j], b_w_out[j], rel_bias, *shared)
        h = h + peer_ffn(h, c_norm_g[i], c_w_q[i], c_keys1[i], c_keys2[i], c_u[i], c_v[i])
        h = h + per_layer_embed(h, p[i], e_norm_g[i], e_w_up[i], e_w_gate[i])
    return rms_norm(h, final_g)
```

```python
import functools
import math

import jax
import jax.numpy as jnp
import numpy as np
from jax import lax
from jax.experimental import pallas as pl
from jax.experimental.pallas import tpu as pltpu

D_MODEL = 1024
BATCH = 2
SEQ = 16384
DEPTH = 2
N_A_LAYERS = DEPTH // 2
N_B_LAYERS = DEPTH - N_A_LAYERS

SSD_D_INNER = 2 * D_MODEL
SSD_HEADDIM = 64
SSD_HEADS = SSD_D_INNER // SSD_HEADDIM
SSD_GROUPS = 8
SSD_HEADS_PER_GROUP = SSD_HEADS // SSD_GROUPS
SSD_STATE = 128
SSD_CONV = 4
SSD_CHUNK = 128
SSD_CONV_DIM = SSD_D_INNER + 2 * SSD_GROUPS * SSD_STATE
SSD_IN_DIM = SSD_D_INNER + SSD_CONV_DIM + SSD_HEADS

NSA_HEADS = 16
NSA_KV_HEADS = 4
NSA_GROUP = NSA_HEADS // NSA_KV_HEADS
NSA_HEAD_DIM = 64
CMP_BLOCK = 32
CMP_STRIDE = 16
CMP_HIDDEN = 256
SLC_BLOCK = 64
SLC_TOPN = 16
SLC_FORCE = 1e4
WINDOW = 512
Q_BLOCK = 128
NSA_IN_DIM = NSA_HEADS * NSA_HEAD_DIM + 3 * NSA_HEADS
KV_DIM = 6 * NSA_KV_HEADS * NSA_HEAD_DIM

REL_BUCKETS = 32
REL_MAX_DISTANCE = 4096

PEER_HEADS = 8
PEER_N_KEYS = 128
PEER_EXPERTS = PEER_N_KEYS * PEER_N_KEYS
PEER_QDIM = 256
PEER_TOPK = 16
PEER_TOKEN_CHUNK = 128
PEER_V_SCALE = PEER_HEADS ** -0.5

PLE_DIM = 256
NORM_EPS = 1e-6
NEG_INF = -1e30

V7X_LANES = 128
V7X_VMEM_LIMIT_BYTES = 56 * 1024 * 1024


def _norm_matmul_kernel(x_ref, g_ref, w_ref, o_ref, *, normalize):
    x = x_ref[...]
    if normalize:
        x = x * lax.rsqrt(jnp.mean(x * x, axis=-1, keepdims=True) + NORM_EPS) * g_ref[...]
    o_ref[...] = jnp.dot(x.astype(jnp.bfloat16), w_ref[...], preferred_element_type=jnp.float32)


def _matmul_residual_kernel(x_ref, r_ref, w_ref, o_ref):
    o_ref[...] = r_ref[...] + jnp.dot(x_ref[...].astype(jnp.bfloat16), w_ref[...], preferred_element_type=jnp.float32)


def matmul_residual(x, r, w, *, tm=512):
    m, k = x.shape
    n = w.shape[1]
    return pl.pallas_call(
        _matmul_residual_kernel,
        grid=(m // tm,),
        in_specs=[pl.BlockSpec((tm, k), lambda i: (i, 0)), pl.BlockSpec((tm, n), lambda i: (i, 0)),
                  pl.BlockSpec((k, n), lambda i: (0, 0))],
        out_specs=pl.BlockSpec((tm, n), lambda i: (i, 0)),
        out_shape=jax.ShapeDtypeStruct((m, n), jnp.float32),
        compiler_params=pltpu.CompilerParams(dimension_semantics=("arbitrary",),
                                             vmem_limit_bytes=V7X_VMEM_LIMIT_BYTES),
        name="matmul_residual",
    )(x, r, w.astype(jnp.bfloat16))


def norm_matmul(x, g, w, *, normalize, tm=512, tn=None):
    m, k = x.shape
    n = w.shape[1]
    n_pad = -(-n // V7X_LANES) * V7X_LANES
    wb = w.astype(jnp.bfloat16)
    if n_pad != n:
        wb = jnp.pad(wb, ((0, 0), (0, n_pad - n)))
    if tn is None:
        tn = n_pad
    assert m % tm == 0 and n_pad % tn == 0
    out = pl.pallas_call(
        functools.partial(_norm_matmul_kernel, normalize=normalize),
        grid=(m // tm, n_pad // tn),
        in_specs=[
            pl.BlockSpec((tm, k), lambda i, j: (i, 0)),
            pl.BlockSpec((1, k), lambda i, j: (0, 0)),
            pl.BlockSpec((k, tn), lambda i, j: (0, j)),
        ],
        out_specs=pl.BlockSpec((tm, tn), lambda i, j: (i, j)),
        out_shape=jax.ShapeDtypeStruct((m, n_pad), jnp.float32),
        compiler_params=pltpu.CompilerParams(
            dimension_semantics=("arbitrary", "arbitrary"),
            vmem_limit_bytes=V7X_VMEM_LIMIT_BYTES),
        name="norm_matmul" if normalize else "matmul",
    )(x, g.reshape(1, k), wb)
    return out[:, :n] if n_pad != n else out


def _mm(x3, g, w, normalize, **kw):
    b, s, k = x3.shape
    if g is None:
        g = jnp.ones((k,), jnp.float32)
    return norm_matmul(x3.reshape(b * s, k), g, w, normalize=normalize, **kw).reshape(b, s, w.shape[1])


def rel_bucket(dist):
    dist = jnp.maximum(dist, 0)
    max_exact = REL_BUCKETS // 2
    d = jnp.maximum(dist, 1).astype(jnp.float32)
    large = max_exact + (jnp.log(d / max_exact) / math.log(REL_MAX_DISTANCE / max_exact)
                         * (REL_BUCKETS - max_exact)).astype(jnp.int32)
    large = jnp.minimum(large, REL_BUCKETS - 1)
    return jnp.where(dist < max_exact, dist, large)


SSD_GROUP_W = SSD_HEADS_PER_GROUP * SSD_HEADDIM
SSD_IN_PAD = -(-SSD_IN_DIM // V7X_LANES) * V7X_LANES
SSD_TAIL = 8


def _split3(x):
    bf16, f32 = jnp.bfloat16, jnp.float32
    hi = x.astype(bf16)
    r1 = x - hi.astype(f32)
    mid = r1.astype(bf16)
    lo = (r1 - mid.astype(f32)).astype(bf16)
    return hi, mid, lo


def _ssd_kernel(zx_ref, h_ref, cw_ref, cb_ref, dtb_ref, alog_ref, dskip_ref, gn_ref, wout_ref, tri_ref, exp_ref,
                o_ref, xs_scr, state_scr):
    f32, bf16 = jnp.float32, jnp.bfloat16
    L, N, GW = SSD_CHUNK, SSD_STATE, SSD_GROUP_W
    c = pl.program_id(1)

    @pl.when(c == 0)
    def _():
        xs_scr[0:SSD_TAIL, :] = jnp.zeros((SSD_TAIL, SSD_CONV_DIM), f32)
        state_scr[...] = jnp.zeros_like(state_scr)

    xs_scr[SSD_TAIL:SSD_TAIL + L, :] = zx_ref[:, SSD_D_INNER:SSD_D_INNER + SSD_CONV_DIM]
    conv = cb_ref[...]
    for w in range(SSD_CONV):
        conv = conv + cw_ref[w:w + 1, :] * xs_scr[pl.ds(SSD_TAIL - (SSD_CONV - 1) + w, L), :]
    xs_scr[0:SSD_TAIL, :] = xs_scr[L:L + SSD_TAIL, :]
    xbc = conv * jax.nn.sigmoid(conv)

    dt_in = zx_ref[:, SSD_D_INNER + SSD_CONV_DIM:SSD_IN_PAD] + dtb_ref[...]
    dt = jnp.maximum(dt_in, 0.0) + jnp.log1p(jnp.exp(-jnp.abs(dt_in)))
    a_dt = dt * -jnp.exp(alog_ref[...])
    a_cs = sum(jnp.dot(tri_ref[...], t, preferred_element_type=f32) for t in _split3(a_dt))
    a_csT = a_cs.T
    a_end = a_cs[L - 1:L, :]
    grow = jnp.exp(a_cs)
    to_end = jnp.exp(a_end - a_cs)

    def per_channel(q):
        return sum(jnp.dot(t, exp_ref[...], preferred_element_type=f32) for t in _split3(q)[:2])

    dt_c, grow_c, to_end_c = per_channel(dt), per_channel(grow), per_channel(to_end)
    end_c = per_channel(jnp.broadcast_to(jnp.exp(a_end), (8, V7X_LANES)))[0:1]
    dskip_c = per_channel(jnp.broadcast_to(dskip_ref[...], (8, V7X_LANES)))[0:1]

    x = xbc[:, :SSD_D_INNER]
    xd = x * dt_c
    xd_bf = xd.astype(bf16)
    xe_bf = (xd * to_end_c).astype(bf16)
    row_i = lax.broadcasted_iota(jnp.int32, (L, L), 0)
    col_i = lax.broadcasted_iota(jnp.int32, (L, L), 1)
    lane_head = lax.broadcasted_iota(jnp.int32, (L, GW), 1) // SSD_HEADDIM
    ys = []
    for g in range(SSD_GROUPS):
        bm = xbc[:, SSD_D_INNER + g * N:SSD_D_INNER + (g + 1) * N]
        cm = xbc[:, SSD_D_INNER + SSD_GROUPS * N + g * N:SSD_D_INNER + SSD_GROUPS * N + (g + 1) * N].astype(bf16)
        bmT = bm.T.astype(bf16)
        cb = jnp.dot(cm, bmT, preferred_element_type=f32)
        xd_g = xd_bf[:, g * GW:(g + 1) * GW]
        y_g = jnp.zeros((L, GW), f32)
        for r in range(SSD_HEADS_PER_GROUP):
            hd = g * SSD_HEADS_PER_GROUP + r
            seg = a_cs[:, hd:hd + 1] - a_csT[hd:hd + 1, :]
            decay = jnp.where(row_i >= col_i, jnp.exp(seg), 0.0)
            xr = jnp.where(lane_head == r, xd_g, jnp.zeros_like(xd_g))
            y_g = y_g + jnp.dot((cb * decay).astype(bf16), xr, preferred_element_type=f32)
        hT = state_scr[g]
        y_off = jnp.dot(cm, hT.astype(bf16), preferred_element_type=f32) * grow_c[:, g * GW:(g + 1) * GW]
        new_states = jnp.dot(bmT, xe_bf[:, g * GW:(g + 1) * GW], preferred_element_type=f32)
        state_scr[g] = hT * end_c[:, g * GW:(g + 1) * GW] + new_states
        ys.append(y_g + y_off)
    y = jnp.concatenate(ys, axis=1) + x * dskip_c

    z = zx_ref[:, :SSD_D_INNER]
    yz = y * (z * jax.nn.sigmoid(z))
    parts = []
    for g in range(SSD_GROUPS):
        t = yz[:, g * GW:(g + 1) * GW]
        parts.append(t * lax.rsqrt(jnp.mean(t * t, axis=-1, keepdims=True) + NORM_EPS))
    yn = jnp.concatenate(parts, axis=1) * gn_ref[...]
    o_ref[...] = h_ref[...] + jnp.dot(yn.astype(bf16), wout_ref[...], preferred_element_type=f32)


def ssd_mixer_residual(h, norm_g, w_in, conv_w, conv_b, dt_bias, a_log, d_skip, gnorm_g, w_out):
    b, s, d = h.shape
    L = SSD_CHUNK
    nc = s // L
    w_in_p = jnp.pad(w_in, ((0, 0), (0, SSD_IN_PAD - SSD_IN_DIM)))
    zx = norm_matmul(h.reshape(b * s, d), norm_g, w_in_p, normalize=True, tm=256)
    lane_pad = lambda v: jnp.pad(v.reshape(1, -1), ((0, 0), (0, V7X_LANES - v.shape[-1])))
    tri = jnp.asarray(np.tril(np.ones((L, L))), jnp.bfloat16)
    head_of = np.arange(SSD_D_INNER) // SSD_HEADDIM
    expand = jnp.asarray(np.arange(V7X_LANES)[:, None] == head_of[None, :], jnp.bfloat16)
    const = lambda *shape: pl.BlockSpec(shape, lambda bi, ci: (0,) * len(shape))
    out = pl.pallas_call(
        _ssd_kernel,
        grid=(b, nc),
        in_specs=[pl.BlockSpec((L, SSD_IN_PAD), lambda bi, ci: (bi * nc + ci, 0)),
                  pl.BlockSpec((L, d), lambda bi, ci: (bi * nc + ci, 0)),
                  const(SSD_CONV, SSD_CONV_DIM), const(1, SSD_CONV_DIM),
                  const(1, V7X_LANES), const(1, V7X_LANES), const(1, V7X_LANES),
                  const(1, SSD_D_INNER), const(SSD_D_INNER, d), const(L, L), const(V7X_LANES, SSD_D_INNER)],
        out_specs=pl.BlockSpec((L, d), lambda bi, ci: (bi * nc + ci, 0)),
        out_shape=jax.ShapeDtypeStruct((b * s, d), jnp.float32),
        scratch_shapes=[pltpu.VMEM((L + SSD_TAIL, SSD_CONV_DIM), jnp.float32),
                        pltpu.VMEM((SSD_GROUPS, SSD_STATE, SSD_GROUP_W), jnp.float32)],
        compiler_params=pltpu.CompilerParams(dimension_semantics=("arbitrary", "arbitrary"),
                                             vmem_limit_bytes=V7X_VMEM_LIMIT_BYTES),
        name="ssd_mixer",
    )(zx, h.reshape(b * s, d), conv_w, conv_b.reshape(1, -1), lane_pad(dt_bias), lane_pad(a_log), lane_pad(d_skip),
      gnorm_g.reshape(1, -1), w_out.astype(jnp.bfloat16), tri, expand)
    return out.reshape(b, s, d)


def _mlp_kernel(x_ref, w1_ref, w2_ref, o_ref):
    hid = jnp.dot(x_ref[...].astype(jnp.bfloat16), w1_ref[...], preferred_element_type=jnp.float32)
    hid = jax.nn.gelu(hid)
    o_ref[...] = jnp.dot(hid.astype(jnp.bfloat16), w2_ref[...], preferred_element_type=jnp.float32)


def mlp2(x, w1, w2, *, tm=512):
    m, k = x.shape
    hdim, n = w2.shape
    assert m % tm == 0
    return pl.pallas_call(
        _mlp_kernel,
        grid=(m // tm,),
        in_specs=[pl.BlockSpec((tm, k), lambda i: (i, 0)),
                  pl.BlockSpec((k, hdim), lambda i: (0, 0)),
                  pl.BlockSpec((hdim, n), lambda i: (0, 0))],
        out_specs=pl.BlockSpec((tm, n), lambda i: (i, 0)),
        out_shape=jax.ShapeDtypeStruct((m, n), jnp.float32),
        compiler_params=pltpu.CompilerParams(dimension_semantics=("arbitrary",),
                                             vmem_limit_bytes=V7X_VMEM_LIMIT_BYTES),
        name="cmp_mlp",
    )(x, w1.astype(jnp.bfloat16), w2.astype(jnp.bfloat16))


def compress_blocks(t, pos, w1, w2):
    b, s, g, d = t.shape
    halves = t.transpose(0, 2, 1, 3).reshape(b, g, s // CMP_STRIDE, CMP_STRIDE, d)
    nxt = jnp.concatenate([halves[:, :, 1:], jnp.zeros_like(halves[:, :, :1])], axis=2)
    blocks = jnp.concatenate([halves, nxt], axis=3) + pos
    flat = blocks.reshape(b * g * (s // CMP_STRIDE), CMP_BLOCK * d)
    return mlp2(flat, w1, w2).reshape(b, g, s // CMP_STRIDE, d)


N_SEL = SEQ // SLC_BLOCK
N_KT = SEQ // Q_BLOCK
CMP_FRONT = 256
CMP_ROWS = CMP_FRONT + SEQ // CMP_STRIDE
CMP_WIN = 256
REL_TABLE = 4096
NEAR_TILES = 24
WIN_TILES = WINDOW // Q_BLOCK + 1
SEL_GROUP = 8
QL = NSA_GROUP * Q_BLOCK


def _nsa_kernel(qT_ref, gT_ref, kc_ref, vcT_ref, kk_ref, vsT_ref, vwT_ref, at_ref, bcT_ref, bT_ref, bTw_ref,
                o_ref, s_scr, sel_scr):
    f32, bf16 = jnp.float32, jnp.bfloat16
    i = pl.program_id(2)
    qT = (qT_ref[0, 0, 0] * (NSA_HEAD_DIM ** -0.5)).astype(bf16)
    zq = jnp.zeros_like(qT)
    q_sel = jnp.concatenate([qT, zq], axis=0)
    q_win = jnp.concatenate([zq, qT], axis=0)

    r0 = pl.multiple_of(8 * i + 8, 8)
    s = jnp.dot(kc_ref[0, 0], qT, preferred_element_type=f32)
    rio = lax.broadcasted_iota(jnp.int32, (CMP_ROWS, Q_BLOCK), 0)
    rowmask = jnp.where((rio >= CMP_FRONT) & (rio < r0 + CMP_WIN), 0.0, NEG_INF)
    s_scr[...] = s + jnp.concatenate([rowmask] * NSA_GROUP, axis=1)
    s_scr[pl.ds(r0, CMP_WIN), :] = s_scr[pl.ds(r0, CMP_WIN), :] + bcT_ref[0]
    s = s_scr[...]
    m = jnp.max(s, axis=0, keepdims=True)
    p = jnp.exp(s - m)
    l = jnp.sum(p, axis=0, keepdims=True)
    inv = jnp.where(m > 0.1 * NEG_INF, 1.0 / l, 0.0)
    pn = p * inv
    o_c = jnp.dot(vcT_ref[0, 0], pn.astype(bf16), preferred_element_type=f32)

    psum = pn[:, 0:Q_BLOCK]
    for r in range(1, NSA_GROUP):
        psum = psum + pn[:, r * Q_BLOCK:(r + 1) * Q_BLOCK]
    hi = psum.astype(bf16)
    lo = (psum - hi.astype(f32)).astype(bf16)
    imp = (jnp.dot(at_ref[...], hi, preferred_element_type=f32)
           + jnp.dot(at_ref[...], lo, preferred_element_type=f32))
    blk = lax.broadcasted_iota(jnp.int32, (N_SEL, Q_BLOCK), 0)
    qi = lax.broadcasted_iota(jnp.int32, (N_SEL, Q_BLOCK), 1)
    cur = 2 * i + (qi >= SLC_BLOCK).astype(jnp.int32)
    forced = (blk == 0) | (blk == cur) | (blk == cur - 1)
    valid = blk <= cur
    score = jnp.where(forced, SLC_FORCE, jnp.where(valid, imp, -SLC_FORCE))
    blkf = blk.astype(f32)
    chosen = jnp.zeros((N_SEL, Q_BLOCK), f32)
    for _ in range(SLC_TOPN):
        mx = jnp.max(score, axis=0, keepdims=True)
        first = jnp.min(jnp.where(score == mx, blkf, float(N_SEL)), axis=0, keepdims=True)
        hit = blkf == first
        chosen = jnp.where(hit, 1.0, chosen)
        score = jnp.where(hit, -jnp.inf, score)
    sel_scr[...] = jnp.where((chosen > 0.5) & valid, 0.0, NEG_INF)

    def attend(kt, carry, q_aug, vT_ref, bias):
        m_run, l_run, acc = carry
        scs = [jnp.dot(kk_ref[0, 0, k], q_aug, preferred_element_type=f32) + b for k, b in zip(kt, bias)]
        for k, sc in zip(kt, scs):
            m_new = jnp.maximum(m_run, jnp.max(sc, axis=0, keepdims=True))
            alpha = jnp.exp(m_run - m_new)
            pt = jnp.exp(sc - m_new)
            l_run = alpha * l_run + jnp.sum(pt, axis=0, keepdims=True)
            acc = alpha * acc + jnp.dot(vT_ref[0, 0, k], pt.astype(bf16), preferred_element_type=f32)
            m_run = m_new
        return m_run, l_run, acc

    def sel_mask(kt):
        m0 = jnp.broadcast_to(sel_scr[pl.ds(2 * kt, 1), :], (SLC_BLOCK, Q_BLOCK))
        m1 = jnp.broadcast_to(sel_scr[pl.ds(2 * kt + 1, 1), :], (SLC_BLOCK, Q_BLOCK))
        mk = jnp.concatenate([m0, m1], axis=0)
        return jnp.concatenate([mk] * NSA_GROUP, axis=1)

    init = (jnp.full((1, QL), NEG_INF, f32), jnp.zeros((1, QL), f32), jnp.zeros((NSA_HEAD_DIM, QL), f32))
    U = SEL_GROUP

    def far_group(g, c):
        kts = [g * U + u for u in range(U)]
        return attend(kts, c, q_sel, vsT_ref, [sel_mask(k) for k in kts])

    def near_group(g, c):
        kts, biases = [], []
        for u in range(U):
            delta = i - (g * U + u)
            k = jnp.minimum(g * U + u, i)
            tile = jnp.where(delta < 0, NEAR_TILES + 1, jnp.minimum(delta, NEAR_TILES))
            kts.append(k)
            biases.append(sel_mask(k) + bT_ref[0, tile])
        return attend(kts, c, q_sel, vsT_ref, biases)

    n_far = jnp.maximum((i - (NEAR_TILES - 1)) // U, 0)
    carry = lax.fori_loop(0, n_far, far_group, init)
    _, l_s, acc_s = lax.fori_loop(n_far, i // U + 1, near_group, carry)
    o_s = acc_s / l_s

    kts, biases = [], []
    for u in range(WIN_TILES):
        k = i - (WIN_TILES - 1) + u
        kts.append(jnp.maximum(k, 0))
        biases.append(bTw_ref[0, jnp.where(k >= 0, WIN_TILES - 1 - u, WIN_TILES)])
    _, l_w, acc_w = attend(kts, init, q_win, vwT_ref, biases)
    o_w = acc_w / l_w

    gate = jax.nn.sigmoid(gT_ref[0, 0, 0])
    o_ref[0, 0, 0] = gate[0:1] * o_c + gate[1:2] * o_s + gate[2:3] * o_w


BIAS_EXT = REL_TABLE + 2 * Q_BLOCK


CMP_OFF = CMP_STRIDE * (CMP_WIN - 8) - (CMP_BLOCK - 1)


def _bias_tile_kernel(bd_ref, bdc_ref, cmp_ref, near_ref, win_ref):
    row = bd_ref[0]
    kj = lax.broadcasted_iota(jnp.int32, (Q_BLOCK, Q_BLOCK), 0)
    qi = lax.broadcasted_iota(jnp.int32, (Q_BLOCK, Q_BLOCK), 1)
    for t in range(NEAR_TILES):
        seg = jnp.broadcast_to(row[:, t * Q_BLOCK:(t + 2) * Q_BLOCK], (Q_BLOCK, 2 * Q_BLOCK))
        tile = pltpu.roll(seg, 0, 1, stride=1, stride_axis=0)[:, Q_BLOCK:]
        near_ref[0, t] = tile
        if t < WIN_TILES:
            win_ref[0, t] = jnp.where(t * Q_BLOCK + qi - kj < WINDOW, tile, NEG_INF)
    near_ref[0, NEAR_TILES] = jnp.zeros((Q_BLOCK, Q_BLOCK), jnp.float32)
    near_ref[0, NEAR_TILES + 1] = jnp.full((Q_BLOCK, Q_BLOCK), NEG_INF, jnp.float32)
    win_ref[0, WIN_TILES] = jnp.full((Q_BLOCK, Q_BLOCK), NEG_INF, jnp.float32)
    full = jnp.broadcast_to(bdc_ref[0], (CMP_WIN, BIAS_EXT))
    cmp_ref[0] = pltpu.roll(full, 0, 1, stride=CMP_STRIDE, stride_axis=0)[:, REL_TABLE:REL_TABLE + Q_BLOCK]


def _bias_tables(rel_bias):
    G, R = NSA_KV_HEADS, NSA_GROUP
    bd = rel_bias[rel_bucket(jnp.arange(REL_TABLE))]
    bd = (bd - bd[REL_TABLE - 1]).T
    def padded(front):
        pad = lambda n: jnp.full((NSA_HEADS, n), NEG_INF, jnp.float32)
        return jnp.concatenate([pad(front), bd, pad(BIAS_EXT - REL_TABLE - front)], axis=1).reshape(NSA_HEADS, 1, BIAS_EXT)

    bd_ext, bd_cmp = padded(Q_BLOCK), padded(REL_TABLE - CMP_OFF)
    head = lambda *blk: pl.BlockSpec((1,) + blk + (Q_BLOCK,), lambda hd: (hd // R,) + (0,) * len(blk) + (hd % R,))
    return pl.pallas_call(
        _bias_tile_kernel,
        grid=(NSA_HEADS,),
        in_specs=[pl.BlockSpec((1, 1, BIAS_EXT), lambda hd: (hd, 0, 0))] * 2,
        out_specs=[head(CMP_WIN), head(NEAR_TILES + 2, Q_BLOCK), head(WIN_TILES + 1, Q_BLOCK)],
        out_shape=[jax.ShapeDtypeStruct((G, CMP_WIN, QL), jnp.float32),
                   jax.ShapeDtypeStruct((G, NEAR_TILES + 2, Q_BLOCK, QL), jnp.float32),
                   jax.ShapeDtypeStruct((G, WIN_TILES + 1, Q_BLOCK, QL), jnp.float32)],
        compiler_params=pltpu.CompilerParams(dimension_semantics=("arbitrary",),
                                             vmem_limit_bytes=V7X_VMEM_LIMIT_BYTES),
        name="nsa_bias_tiles",
    )(bd_ext, bd_cmp)


def _pool_matrix():
    per = SLC_BLOCK // CMP_STRIDE
    n_span = CMP_BLOCK // CMP_STRIDE
    k = np.arange(CMP_ROWS)[None, :] - CMP_FRONT
    j = np.arange(N_SEL)[:, None]
    return jnp.asarray((k >= per * j - (n_span - 1)) & (k <= per * j + per - 1), jnp.bfloat16)


def nsa_shared_kv(h, kv_norm_g, w_kv, cmp_pos_k, cmp_pos_v, cmp_w1_k, cmp_w2_k, cmp_w1_v, cmp_w2_v):
    b, s, _ = h.shape
    G, DH = NSA_KV_HEADS, NSA_HEAD_DIM
    bf16 = jnp.bfloat16
    kv = _mm(h, kv_norm_g, w_kv, True).reshape(b, s, 6, G, DH)
    k_cmp = compress_blocks(kv[:, :, 0], cmp_pos_k, cmp_w1_k, cmp_w2_k)
    v_cmp = compress_blocks(kv[:, :, 1], cmp_pos_v, cmp_w1_v, cmp_w2_v)
    front = jnp.zeros((b, G, CMP_FRONT, DH), bf16)
    kc = jnp.concatenate([front, k_cmp.astype(bf16)], axis=2)
    vcT = jnp.concatenate([front, v_cmp.astype(bf16)], axis=2).transpose(0, 1, 3, 2)
    kk = jnp.concatenate([kv[:, :, 2], kv[:, :, 4]], axis=-1).astype(bf16)
    kk = kk.transpose(0, 2, 1, 3).reshape(b, G, N_KT, Q_BLOCK, 2 * DH)

    def vt(v):
        return v.astype(bf16).transpose(0, 2, 1, 3).reshape(b, G, N_KT, Q_BLOCK, DH).transpose(0, 1, 2, 4, 3)

    return kc, vcT, kk, vt(kv[:, :, 3]), vt(kv[:, :, 5])


def nsa_mixer(h, norm_g, w_in, w_out, rel_bias, kc, vcT, kk, vsT, vwT):
    b, s, _ = h.shape
    G, R, DH = NSA_KV_HEADS, NSA_GROUP, NSA_HEAD_DIM
    nb = s // Q_BLOCK
    proj = _mm(h, norm_g, w_in, True)
    qT = proj[..., :NSA_HEADS * DH].reshape(b, nb, Q_BLOCK, G, R, DH).transpose(0, 3, 1, 5, 4, 2)
    qT = qT.reshape(b, G, nb, DH, QL)
    gT = proj[..., NSA_HEADS * DH:].reshape(b, nb, Q_BLOCK, G, R, 3).transpose(0, 3, 1, 5, 4, 2)
    gT = gT.reshape(b, G, nb, 3, QL)
    b_cmp, b_near, b_win = _bias_tables(rel_bias)
    at = _pool_matrix()

    per_bg = lambda *blk: pl.BlockSpec((1, 1) + blk, lambda bi, gi, i: (bi, gi) + (0,) * len(blk))
    per_g = lambda *blk: pl.BlockSpec((1,) + blk, lambda bi, gi, i: (gi,) + (0,) * len(blk))
    oT = pl.pallas_call(
        _nsa_kernel,
        grid=(b, G, nb),
        in_specs=[
            pl.BlockSpec((1, 1, 1, DH, QL), lambda bi, gi, i: (bi, gi, i, 0, 0)),
            pl.BlockSpec((1, 1, 1, 3, QL), lambda bi, gi, i: (bi, gi, i, 0, 0)),
            per_bg(CMP_ROWS, DH),
            per_bg(DH, CMP_ROWS),
            per_bg(N_KT, Q_BLOCK, 2 * DH),
            per_bg(N_KT, DH, Q_BLOCK),
            per_bg(N_KT, DH, Q_BLOCK),
            pl.BlockSpec((N_SEL, CMP_ROWS), lambda bi, gi, i: (0, 0)),
            per_g(CMP_WIN, QL),
            per_g(NEAR_TILES + 2, Q_BLOCK, QL),
            per_g(WIN_TILES + 1, Q_BLOCK, QL),
        ],
        out_specs=pl.BlockSpec((1, 1, 1, DH, QL), lambda bi, gi, i: (bi, gi, i, 0, 0)),
        out_shape=jax.ShapeDtypeStruct((b, G, nb, DH, QL), jnp.float32),
        scratch_shapes=[pltpu.VMEM((CMP_ROWS, QL), jnp.float32), pltpu.VMEM((N_SEL, Q_BLOCK), jnp.float32)],
        compiler_params=pltpu.CompilerParams(
            dimension_semantics=("arbitrary", "arbitrary", "arbitrary"),
            vmem_limit_bytes=V7X_VMEM_LIMIT_BYTES),
        name="nsa_attention",
    )(qT, gT, kc, vcT, kk, vsT, vwT, at, b_cmp, b_near, b_win)
    o = oT.reshape(b, G, nb, DH, R, Q_BLOCK).transpose(0, 2, 5, 1, 4, 3).reshape(b, s, NSA_HEADS * DH)
    return matmul_residual(o.reshape(b * s, NSA_HEADS * DH), h.reshape(b * s, -1), w_out).reshape(h.shape)


PEER_SLOTS = PEER_HEADS * PEER_TOPK
PEER_ROUTE_TM = 256
PEER_ROUTE_LOCKSTEP = 2
PEER_TT = 16
PEER_CHUNKS = D_MODEL // V7X_LANES
PEER_WORDS = PEER_CHUNKS // 2
PEER_TILE_PITCH = PEER_SLOTS + 8


def _top_rows(arrays, rowid, n, payloads=None):
    arrays = list(arrays)
    vals = [[] for _ in arrays]
    picks = [[] for _ in arrays]
    for _ in range(n):
        for a, s in enumerate(arrays):
            mx = jnp.max(s, axis=0, keepdims=True)
            first = jnp.min(jnp.where(s == mx, rowid, jnp.inf), axis=0, keepdims=True)
            hit = rowid == first
            vals[a].append(mx)
            picks[a].append(first if payloads is None
                            else jnp.max(jnp.where(hit, payloads[a], -1.0), axis=0, keepdims=True))
            arrays[a] = jnp.where(hit, -jnp.inf, s)
    return vals, picks


def _peer_route_kernel(h_ref, g_ref, wqT_ref, k1_ref, k2_ref, xn_ref, ids_ref, gate_ref):
    f32, bf16 = jnp.float32, jnp.bfloat16
    x = h_ref[...]
    xn = x * lax.rsqrt(jnp.mean(x * x, axis=-1, keepdims=True) + NORM_EPS) * g_ref[...]
    xn_ref[...] = xn
    qT = lax.dot_general(wqT_ref[...], xn.astype(bf16), (((1,), (1,)), ((), ())), preferred_element_type=f32)
    half = PEER_QDIM // 2
    tm = x.shape[0]
    key_id = lax.broadcasted_iota(jnp.int32, (PEER_N_KEYS, tm), 0).astype(f32)
    stair = [PEER_TOPK // (a + 1) for a in range(PEER_TOPK)]
    n_pad = -sum(stair) % 8
    cand_pos = jnp.concatenate(
        [lax.broadcasted_iota(jnp.int32, (nb, tm), 0).astype(f32) + float(a * PEER_TOPK) for a, nb in enumerate(stair)]
        + [jnp.full((n_pad, tm), float(PEER_TOPK * PEER_TOPK), f32)], axis=0)
    for h0 in range(0, PEER_HEADS, PEER_ROUTE_LOCKSTEP):
        heads = range(h0, h0 + PEER_ROUTE_LOCKSTEP)
        scores = []
        for hd in heads:
            q1 = qT[hd * PEER_QDIM: hd * PEER_QDIM + half].astype(bf16)
            q2 = qT[hd * PEER_QDIM + half: (hd + 1) * PEER_QDIM].astype(bf16)
            scores.append(jnp.dot(k1_ref[hd], q1, preferred_element_type=f32))
            scores.append(jnp.dot(k2_ref[hd], q2, preferred_element_type=f32))
        vals, idxs = _top_rows(scores, key_id, PEER_TOPK)
        cands, cand_ids = [], []
        for a in range(PEER_ROUTE_LOCKSTEP):
            v1, i1 = vals[2 * a], idxs[2 * a]
            v2m = jnp.concatenate(vals[2 * a + 1], axis=0)
            i2m = jnp.concatenate(idxs[2 * a + 1], axis=0)
            cands.append(jnp.concatenate([v1[r] + v2m[0:nb] for r, nb in enumerate(stair)]
                                         + [jnp.full((n_pad, tm), -jnp.inf, f32)], axis=0))
            cand_ids.append(jnp.concatenate([i1[r] * float(PEER_N_KEYS) + i2m[0:nb] for r, nb in enumerate(stair)]
                                            + [jnp.full((n_pad, tm), -1.0, f32)], axis=0))
        tops, ids = _top_rows(cands, cand_pos, PEER_TOPK, payloads=cand_ids)
        for a, hd in enumerate(heads):
            top_s = jnp.concatenate(tops[a], axis=0)
            e = jnp.exp(top_s - top_s[0:1])
            gate_ref[hd * PEER_TOPK:(hd + 1) * PEER_TOPK, :] = e / jnp.sum(e, axis=0, keepdims=True)
            ids_ref[hd * PEER_TOPK:(hd + 1) * PEER_TOPK, :] = (jnp.concatenate(ids[a], axis=0)
                                                               * PEER_WORDS).astype(jnp.int32)


def peer_route(h2, norm_g, w_q, keys1, keys2):
    T, d = h2.shape
    tm = PEER_ROUTE_TM
    bf16 = jnp.bfloat16
    nq = PEER_HEADS * PEER_QDIM
    half = PEER_QDIM // 2
    return pl.pallas_call(
        _peer_route_kernel,
        grid=(T // tm,),
        in_specs=[pl.BlockSpec((tm, d), lambda i: (i, 0)),
                  pl.BlockSpec((1, d), lambda i: (0, 0)),
                  pl.BlockSpec((nq, d), lambda i: (0, 0)),
                  pl.BlockSpec((PEER_HEADS, PEER_N_KEYS, half), lambda i: (0, 0, 0)),
                  pl.BlockSpec((PEER_HEADS, PEER_N_KEYS, half), lambda i: (0, 0, 0))],
        out_specs=[pl.BlockSpec((tm, d), lambda i: (i, 0)),
                   pl.BlockSpec((PEER_SLOTS, tm), lambda i: (0, i)),
                   pl.BlockSpec((PEER_SLOTS, tm), lambda i: (0, i))],
        out_shape=[jax.ShapeDtypeStruct((T, d), jnp.float32),
                   jax.ShapeDtypeStruct((PEER_SLOTS, T), jnp.int32),
                   jax.ShapeDtypeStruct((PEER_SLOTS, T), jnp.float32)],
        compiler_params=pltpu.CompilerParams(dimension_semantics=("arbitrary",),
                                             vmem_limit_bytes=V7X_VMEM_LIMIT_BYTES),
        name="peer_route",
    )(h2, norm_g.reshape(1, d), w_q.T.astype(bf16), keys1.astype(bf16), keys2.astype(bf16))


def _unpack_pair(words):
    lo = lax.bitcast_convert_type(lax.shift_left(words, 16), jnp.float32)
    hi = lax.bitcast_convert_type(lax.bitwise_and(words, -65536), jnp.float32)
    return lo, hi


def _pack_rows(tab):
    bits = lax.bitcast_convert_type(tab.astype(jnp.bfloat16), jnp.uint16).astype(jnp.uint32)
    bits = bits.reshape(tab.shape[0], PEER_WORDS, 2, V7X_LANES)
    words = lax.bitcast_convert_type(bits[:, :, 0] | (bits[:, :, 1] << 16), jnp.int32)
    return words.reshape(tab.shape[0] * PEER_WORDS, V7X_LANES)


def _peer_gather_token(ids_ref, tab, tile, t):
    for k in range(PEER_SLOTS):
        row = pl.multiple_of(ids_ref[t * PEER_SLOTS + k], PEER_WORDS)
        tile[pl.ds(k, PEER_WORDS, stride=PEER_TILE_PITCH), :] = tab[pl.ds(row, PEER_WORDS), :]


def _peer_chunk_pair(tile, j):
    return _unpack_pair(tile[j * PEER_TILE_PITCH:j * PEER_TILE_PITCH + PEER_SLOTS, :])


def _peer_load_table(tab_hbm, tab, sem):
    @pl.when(pl.program_id(0) == 0)
    def _():
        copy = pltpu.make_async_copy(tab_hbm, tab, sem)
        copy.start()
        copy.wait()


def _peer_act_kernel(ids_ref, xn_ref, gt_ref, tab_hbm, w_ref, tab, tile_a, tile_b, sem):
    _peer_load_table(tab_hbm, tab, sem)
    for t in range(PEER_TT):
        tile = (tile_a, tile_b)[t % 2]
        _peer_gather_token(ids_ref, tab, tile, t)
        xrow = xn_ref[t:t + 1, :]
        acc = jnp.zeros((PEER_SLOTS, V7X_LANES), jnp.float32)
        for j in range(PEER_WORDS):
            lo, hi = _peer_chunk_pair(tile, j)
            acc = acc + lo * xrow[:, 2 * j * V7X_LANES:(2 * j + 1) * V7X_LANES]
            acc = acc + hi * xrow[:, (2 * j + 1) * V7X_LANES:(2 * j + 2) * V7X_LANES]
        act = jnp.sum(acc, axis=1, keepdims=True)
        w_ref[0, :, t:t + 1] = gt_ref[0, :, t:t + 1] * jax.nn.gelu(act)


def _peer_out_kernel(ids_ref, w_ref, h_ref, tab_hbm, o_ref, tab, tile_a, tile_b, sem):
    _peer_load_table(tab_hbm, tab, sem)
    for t in range(PEER_TT):
        tile = (tile_a, tile_b)[t % 2]
        _peer_gather_token(ids_ref, tab, tile, t)
        w = w_ref[0, :, t:t + 1]
        for j in range(PEER_WORDS):
            for c, part in zip((2 * j, 2 * j + 1), _peer_chunk_pair(tile, j)):
                lanes = slice(c * V7X_LANES, (c + 1) * V7X_LANES)
                o_ref[t:t + 1, lanes] = h_ref[t:t + 1, lanes] + jnp.sum(part * w, axis=0, keepdims=True)


def peer_experts(xn, idsT, gateT, u_tab, v_tab, h2):
    T, d = xn.shape
    tt = PEER_TT
    nt = T // tt
    ids = idsT.T.reshape(T * PEER_SLOTS)
    gt = gateT.reshape(PEER_SLOTS, nt, tt).transpose(1, 0, 2)
    ids_spec = pl.BlockSpec((tt * PEER_SLOTS,), lambda i: (i,), memory_space=pltpu.SMEM)
    col_spec = pl.BlockSpec((1, PEER_SLOTS, tt), lambda i: (i, 0, 0))
    row_spec = pl.BlockSpec((tt, d), lambda i: (i, 0))
    tile = pltpu.VMEM((PEER_WORDS * PEER_TILE_PITCH, V7X_LANES), jnp.int32)
    scratch = [pltpu.VMEM((PEER_EXPERTS * PEER_WORDS, V7X_LANES), jnp.int32), tile, tile, pltpu.SemaphoreType.DMA(())]
    params = pltpu.CompilerParams(dimension_semantics=("arbitrary",), vmem_limit_bytes=V7X_VMEM_LIMIT_BYTES)
    w = pl.pallas_call(
        _peer_act_kernel,
        grid=(nt,),
        in_specs=[ids_spec, row_spec, col_spec, pl.BlockSpec(memory_space=pl.ANY)],
        out_specs=col_spec,
        out_shape=jax.ShapeDtypeStruct((nt, PEER_SLOTS, tt), jnp.float32),
        scratch_shapes=scratch, compiler_params=params, name="peer_act",
    )(ids, xn, gt, _pack_rows(u_tab))
    return pl.pallas_call(
        _peer_out_kernel,
        grid=(nt,),
        in_specs=[ids_spec, col_spec, row_spec, pl.BlockSpec(memory_space=pl.ANY)],
        out_specs=row_spec,
        out_shape=jax.ShapeDtypeStruct((T, d), jnp.float32),
        scratch_shapes=scratch, compiler_params=params, name="peer_out",
    )(ids, w, h2, _pack_rows(v_tab))


def peer_ffn_residual(h, norm_g, w_q, keys1, keys2, u_tab, v_tab):
    b, s, d = h.shape
    h2 = h.reshape(b * s, d)
    xn, idsT, gateT = peer_route(h2, norm_g, w_q, keys1, keys2)
    return peer_experts(xn, idsT, gateT, u_tab, v_tab, h2).reshape(b, s, d)


def _ple_kernel(h_ref, p_ref, g_ref, wup_ref, wgate_ref, fg_ref, o_ref, *, final_norm):
    bf16, f32 = jnp.bfloat16, jnp.float32

    def rms(v, gain):
        return v * lax.rsqrt(jnp.mean(v * v, axis=-1, keepdims=True) + NORM_EPS) * gain

    h = h_ref[...]
    gate = jax.nn.sigmoid(jnp.dot(rms(h, g_ref[...]).astype(bf16), wgate_ref[...], preferred_element_type=f32))
    y = h + jnp.dot(p_ref[...].astype(bf16), wup_ref[...], preferred_element_type=f32) * gate
    o_ref[...] = rms(y, fg_ref[...]) if final_norm else y


def per_layer_embed_residual(h, p_i, norm_g, w_up, w_gate, final_g, *, final_norm, tm=512):
    b, s, d = h.shape
    pd = p_i.shape[-1]
    const = lambda *shape: pl.BlockSpec(shape, lambda i: (0,) * len(shape))
    out = pl.pallas_call(
        functools.partial(_ple_kernel, final_norm=final_norm),
        grid=(b * s // tm,),
        in_specs=[pl.BlockSpec((tm, d), lambda i: (i, 0)), pl.BlockSpec((tm, pd), lambda i: (i, 0)),
                  const(1, d), const(pd, d), const(d, d), const(1, d)],
        out_specs=pl.BlockSpec((tm, d), lambda i: (i, 0)),
        out_shape=jax.ShapeDtypeStruct((b * s, d), jnp.float32),
        compiler_params=pltpu.CompilerParams(dimension_semantics=("arbitrary",),
                                             vmem_limit_bytes=V7X_VMEM_LIMIT_BYTES),
        name="per_layer_embed",
    )(h.reshape(b * s, d), p_i.reshape(b * s, pd), norm_g.reshape(1, d), w_up.astype(jnp.bfloat16),
      w_gate.astype(jnp.bfloat16), final_g.reshape(1, d))
    return out.reshape(b, s, d)


def kernel(x, p, a_norm_g, a_w_in, a_conv_w, a_conv_b, a_dt_bias, a_log, a_d_skip, a_gnorm_g, a_w_out,
           kv_norm_g, w_kv, cmp_pos_k, cmp_pos_v, cmp_w1_k, cmp_w2_k, cmp_w1_v, cmp_w2_v, rel_bias,
           b_norm_g, b_w_in, b_w_out, c_norm_g, c_w_q, c_keys1, c_keys2, c_u, c_v,
           e_norm_g, e_w_up, e_w_gate, final_g):
    h = x
    shared = None
    for i in range(DEPTH):
        if i < N_A_LAYERS:
            h = ssd_mixer_residual(h, a_norm_g[i], a_w_in[i], a_conv_w[i], a_conv_b[i], a_dt_bias[i],
                                   a_log[i], a_d_skip[i], a_gnorm_g[i], a_w_out[i])
        else:
            if shared is None:
                shared = nsa_shared_kv(h, kv_norm_g, w_kv, cmp_pos_k, cmp_pos_v,
                                       cmp_w1_k, cmp_w2_k, cmp_w1_v, cmp_w2_v)
            j = i - N_A_LAYERS
            h = nsa_mixer(h, b_norm_g[j], b_w_in[j], b_w_out[j], rel_bias, *shared)
        h = peer_ffn_residual(h, c_norm_g[i], c_w_q[i], c_keys1[i], c_keys2[i], c_u[i], c_v[i])
        h = per_layer_embed_residual(h, p[i], e_norm_g[i], e_w_up[i], e_w_gate[i], final_g,
                                     final_norm=(i == DEPTH - 1))
    return h
```

```python
import functools
import math

import jax
import jax.numpy as jnp
import numpy as np
from jax import lax
from jax.experimental import pallas as pl
from jax.experimental.pallas import tpu as pltpu

D_MODEL = 1024
BATCH = 2
SEQ = 16384
DEPTH = 2
N_A_LAYERS = DEPTH // 2
N_B_LAYERS = DEPTH - N_A_LAYERS

SSD_D_INNER = 2 * D_MODEL
SSD_HEADDIM = 64
SSD_HEADS = SSD_D_INNER // SSD_HEADDIM
SSD_GROUPS = 8
SSD_HEADS_PER_GROUP = SSD_HEADS // SSD_GROUPS
SSD_STATE = 128
SSD_CONV = 4
SSD_CHUNK = 128
SSD_CONV_DIM = SSD_D_INNER + 2 * SSD_GROUPS * SSD_STATE
SSD_IN_DIM = SSD_D_INNER + SSD_CONV_DIM + SSD_HEADS

NSA_HEADS = 16
NSA_KV_HEADS = 4
NSA_GROUP = NSA_HEADS // NSA_KV_HEADS
NSA_HEAD_DIM = 64
CMP_BLOCK = 32
CMP_STRIDE = 16
CMP_HIDDEN = 256
SLC_BLOCK = 64
SLC_TOPN = 16
SLC_FORCE = 1e4
WINDOW = 512
Q_BLOCK = 128
NSA_IN_DIM = NSA_HEADS * NSA_HEAD_DIM + 3 * NSA_HEADS
KV_DIM = 6 * NSA_KV_HEADS * NSA_HEAD_DIM

REL_BUCKETS = 32
REL_MAX_DISTANCE = 4096

PEER_HEADS = 8
PEER_N_KEYS = 128
PEER_EXPERTS = PEER_N_KEYS * PEER_N_KEYS
PEER_QDIM = 256
PEER_TOPK = 16
PEER_TOKEN_CHUNK = 128
PEER_V_SCALE = PEER_HEADS ** -0.5

PLE_DIM = 256
NORM_EPS = 1e-6
NEG_INF = -1e30

V7X_LANES = 128
V7X_VMEM_LIMIT_BYTES = 56 * 1024 * 1024


def _norm_matmul_kernel(x_ref, g_ref, w_ref, o_ref, *, normalize):
    x = x_ref[...]
    if normalize:
        x = x * lax.rsqrt(jnp.mean(x * x, axis=-1, keepdims=True) + NORM_EPS) * g_ref[...]
    o_ref[...] = jnp.dot(x.astype(jnp.bfloat16), w_ref[...], preferred_element_type=jnp.float32)


def _matmul_residual_kernel(x_ref, r_ref, w_ref, o_ref):
    o_ref[...] = r_ref[...] + jnp.dot(x_ref[...].astype(jnp.bfloat16), w_ref[...], preferred_element_type=jnp.float32)


def matmul_residual(x, r, w, *, tm=512):
    m, k = x.shape
    n = w.shape[1]
    return pl.pallas_call(
        _matmul_residual_kernel,
        grid=(m // tm,),
        in_specs=[pl.BlockSpec((tm, k), lambda i: (i, 0)), pl.BlockSpec((tm, n), lambda i: (i, 0)),
                  pl.BlockSpec((k, n), lambda i: (0, 0))],
        out_specs=pl.BlockSpec((tm, n), lambda i: (i, 0)),
        out_shape=jax.ShapeDtypeStruct((m, n), jnp.float32),
        compiler_params=pltpu.CompilerParams(dimension_semantics=("arbitrary",),
                                             vmem_limit_bytes=V7X_VMEM_LIMIT_BYTES),
        name="matmul_residual",
    )(x, r, w.astype(jnp.bfloat16))


def norm_matmul(x, g, w, *, normalize, tm=512, tn=None):
    m, k = x.shape
    n = w.shape[1]
    n_pad = -(-n // V7X_LANES) * V7X_LANES
    wb = w.astype(jnp.bfloat16)
    if n_pad != n:
        wb = jnp.pad(wb, ((0, 0), (0, n_pad - n)))
    if tn is None:
        tn = n_pad
    assert m % tm == 0 and n_pad % tn == 0
    out = pl.pallas_call(
        functools.partial(_norm_matmul_kernel, normalize=normalize),
        grid=(m // tm, n_pad // tn),
        in_specs=[
            pl.BlockSpec((tm, k), lambda i, j: (i, 0)),
            pl.BlockSpec((1, k), lambda i, j: (0, 0)),
            pl.BlockSpec((k, tn), lambda i, j: (0, j)),
        ],
        out_specs=pl.BlockSpec((tm, tn), lambda i, j: (i, j)),
        out_shape=jax.ShapeDtypeStruct((m, n_pad), jnp.float32),
        compiler_params=pltpu.CompilerParams(
            dimension_semantics=("arbitrary", "arbitrary"),
            vmem_limit_bytes=V7X_VMEM_LIMIT_BYTES),
        name="norm_matmul" if normalize else "matmul",
    )(x, g.reshape(1, k), wb)
    return out[:, :n] if n_pad != n else out


def _mm(x3, g, w, normalize, **kw):
    b, s, k = x3.shape
    if g is None:
        g = jnp.ones((k,), jnp.float32)
    return norm_matmul(x3.reshape(b * s, k), g, w, normalize=normalize, **kw).reshape(b, s, w.shape[1])


def rel_bucket(dist):
    dist = jnp.maximum(dist, 0)
    max_exact = REL_BUCKETS // 2
    d = jnp.maximum(dist, 1).astype(jnp.float32)
    large = max_exact + (jnp.log(d / max_exact) / math.log(REL_MAX_DISTANCE / max_exact)
                         * (REL_BUCKETS - max_exact)).astype(jnp.int32)
    large = jnp.minimum(large, REL_BUCKETS - 1)
    return jnp.where(dist < max_exact, dist, large)


SSD_GROUP_W = SSD_HEADS_PER_GROUP * SSD_HEADDIM
SSD_IN_PAD = -(-SSD_IN_DIM // V7X_LANES) * V7X_LANES
SSD_TAIL = 8


def _split3(x):
    bf16, f32 = jnp.bfloat16, jnp.float32
    hi = x.astype(bf16)
    r1 = x - hi.astype(f32)
    mid = r1.astype(bf16)
    lo = (r1 - mid.astype(f32)).astype(bf16)
    return hi, mid, lo


def _ssd_kernel(zx_ref, h_ref, cw_ref, cb_ref, dtb_ref, alog_ref, dskip_ref, gn_ref, wout_ref, tri_ref, exp_ref,
                o_ref, xs_scr, state_scr):
    f32, bf16 = jnp.float32, jnp.bfloat16
    L, N, GW = SSD_CHUNK, SSD_STATE, SSD_GROUP_W
    c = pl.program_id(1)

    @pl.when(c == 0)
    def _():
        xs_scr[0:SSD_TAIL, :] = jnp.zeros((SSD_TAIL, SSD_CONV_DIM), f32)
        state_scr[...] = jnp.zeros_like(state_scr)

    xs_scr[SSD_TAIL:SSD_TAIL + L, :] = zx_ref[:, SSD_D_INNER:SSD_D_INNER + SSD_CONV_DIM]
    conv = cb_ref[...]
    for w in range(SSD_CONV):
        conv = conv + cw_ref[w:w + 1, :] * xs_scr[pl.ds(SSD_TAIL - (SSD_CONV - 1) + w, L), :]
    xs_scr[0:SSD_TAIL, :] = xs_scr[L:L + SSD_TAIL, :]
    xbc = conv * jax.nn.sigmoid(conv)

    dt_in = zx_ref[:, SSD_D_INNER + SSD_CONV_DIM:SSD_IN_PAD] + dtb_ref[...]
    dt = jnp.maximum(dt_in, 0.0) + jnp.log1p(jnp.exp(-jnp.abs(dt_in)))
    a_dt = dt * -jnp.exp(alog_ref[...])
    a_cs = sum(jnp.dot(tri_ref[...], t, preferred_element_type=f32) for t in _split3(a_dt))
    a_csT = a_cs.T
    a_end = a_cs[L - 1:L, :]
    grow = jnp.exp(a_cs)
    to_end = jnp.exp(a_end - a_cs)

    def per_channel(q):
        return sum(jnp.dot(t, exp_ref[...], preferred_element_type=f32) for t in _split3(q)[:2])

    dt_c, grow_c, to_end_c = per_channel(dt), per_channel(grow), per_channel(to_end)
    end_c = per_channel(jnp.broadcast_to(jnp.exp(a_end), (8, V7X_LANES)))[0:1]
    dskip_c = per_channel(jnp.broadcast_to(dskip_ref[...], (8, V7X_LANES)))[0:1]

    x = xbc[:, :SSD_D_INNER]
    xd = x * dt_c
    xd_bf = xd.astype(bf16)
    xe_bf = (xd * to_end_c).astype(bf16)
    row_i = lax.broadcasted_iota(jnp.int32, (L, L), 0)
    col_i = lax.broadcasted_iota(jnp.int32, (L, L), 1)
    lane_head = lax.broadcasted_iota(jnp.int32, (L, GW), 1) // SSD_HEADDIM
    ys = []
    for g in range(SSD_GROUPS):
        bm = xbc[:, SSD_D_INNER + g * N:SSD_D_INNER + (g + 1) * N]
        cm = xbc[:, SSD_D_INNER + SSD_GROUPS * N + g * N:SSD_D_INNER + SSD_GROUPS * N + (g + 1) * N].astype(bf16)
        bmT = bm.T.astype(bf16)
        cb = jnp.dot(cm, bmT, preferred_element_type=f32)
        xd_g = xd_bf[:, g * GW:(g + 1) * GW]
        y_g = jnp.zeros((L, GW), f32)
        for r in range(SSD_HEADS_PER_GROUP):
            hd = g * SSD_HEADS_PER_GROUP + r
            seg = a_cs[:, hd:hd + 1] - a_csT[hd:hd + 1, :]
            decay = jnp.where(row_i >= col_i, jnp.exp(seg), 0.0)
            xr = jnp.where(lane_head == r, xd_g, jnp.zeros_like(xd_g))
            y_g = y_g + jnp.dot((cb * decay).astype(bf16), xr, preferred_element_type=f32)
        hT = state_scr[g]
        y_off = jnp.dot(cm, hT.astype(bf16), preferred_element_type=f32) * grow_c[:, g * GW:(g + 1) * GW]
        new_states = jnp.dot(bmT, xe_bf[:, g * GW:(g + 1) * GW], preferred_element_type=f32)
        state_scr[g] = hT * end_c[:, g * GW:(g + 1) * GW] + new_states
        ys.append(y_g + y_off)
    y = jnp.concatenate(ys, axis=1) + x * dskip_c

    z = zx_ref[:, :SSD_D_INNER]
    yz = y * (z * jax.nn.sigmoid(z))
    parts = []
    for g in range(SSD_GROUPS):
        t = yz[:, g * GW:(g + 1) * GW]
        parts.append(t * lax.rsqrt(jnp.mean(t * t, axis=-1, keepdims=True) + NORM_EPS))
    yn = jnp.concatenate(parts, axis=1) * gn_ref[...]
    o_ref[...] = h_ref[...] + jnp.dot(yn.astype(bf16), wout_ref[...], preferred_element_type=f32)


def ssd_mixer_residual(h, norm_g, w_in, conv_w, conv_b, dt_bias, a_log, d_skip, gnorm_g, w_out):
    b, s, d = h.shape
    L = SSD_CHUNK
    nc = s // L
    w_in_p = jnp.pad(w_in, ((0, 0), (0, SSD_IN_PAD - SSD_IN_DIM)))
    zx = norm_matmul(h.reshape(b * s, d), norm_g, w_in_p, normalize=True, tm=256)
    lane_pad = lambda v: jnp.pad(v.reshape(1, -1), ((0, 0), (0, V7X_LANES - v.shape[-1])))
    tri = jnp.asarray(np.tril(np.ones((L, L))), jnp.bfloat16)
    head_of = np.arange(SSD_D_INNER) // SSD_HEADDIM
    expand = jnp.asarray(np.arange(V7X_LANES)[:, None] == head_of[None, :], jnp.bfloat16)
    const = lambda *shape: pl.BlockSpec(shape, lambda bi, ci: (0,) * len(shape))
    out = pl.pallas_call(
        _ssd_kernel,
        grid=(b, nc),
        in_specs=[pl.BlockSpec((L, SSD_IN_PAD), lambda bi, ci: (bi * nc + ci, 0)),
                  pl.BlockSpec((L, d), lambda bi, ci: (bi * nc + ci, 0)),
                  const(SSD_CONV, SSD_CONV_DIM), const(1, SSD_CONV_DIM),
                  const(1, V7X_LANES), const(1, V7X_LANES), const(1, V7X_LANES),
                  const(1, SSD_D_INNER), const(SSD_D_INNER, d), const(L, L), const(V7X_LANES, SSD_D_INNER)],
        out_specs=pl.BlockSpec((L, d), lambda bi, ci: (bi * nc + ci, 0)),
        out_shape=jax.ShapeDtypeStruct((b * s, d), jnp.float32),
        scratch_shapes=[pltpu.VMEM((L + SSD_TAIL, SSD_CONV_DIM), jnp.float32),
                        pltpu.VMEM((SSD_GROUPS, SSD_STATE, SSD_GROUP_W), jnp.float32)],
        compiler_params=pltpu.CompilerParams(dimension_semantics=("arbitrary", "arbitrary"),
                                             vmem_limit_bytes=V7X_VMEM_LIMIT_BYTES),
        name="ssd_mixer",
    )(zx, h.reshape(b * s, d), conv_w, conv_b.reshape(1, -1), lane_pad(dt_bias), lane_pad(a_log), lane_pad(d_skip),
      gnorm_g.reshape(1, -1), w_out.astype(jnp.bfloat16), tri, expand)
    return out.reshape(b, s, d)


def _mlp_kernel(x_ref, w1_ref, w2_ref, o_ref):
    hid = jnp.dot(x_ref[...].astype(jnp.bfloat16), w1_ref[...], preferred_element_type=jnp.float32)
    hid = jax.nn.gelu(hid)
    o_ref[...] = jnp.dot(hid.astype(jnp.bfloat16), w2_ref[...], preferred_element_type=jnp.float32)


def mlp2(x, w1, w2, *, tm=512):
    m, k = x.shape
    hdim, n = w2.shape
    assert m % tm == 0
    return pl.pallas_call(
        _mlp_kernel,
        grid=(m // tm,),
        in_specs=[pl.BlockSpec((tm, k), lambda i: (i, 0)),
                  pl.BlockSpec((k, hdim), lambda i: (0, 0)),
                  pl.BlockSpec((hdim, n), lambda i: (0, 0))],
        out_specs=pl.BlockSpec((tm, n), lambda i: (i, 0)),
        out_shape=jax.ShapeDtypeStruct((m, n), jnp.float32),
        compiler_params=pltpu.CompilerParams(dimension_semantics=("arbitrary",),
                                             vmem_limit_bytes=V7X_VMEM_LIMIT_BYTES),
        name="cmp_mlp",
    )(x, w1.astype(jnp.bfloat16), w2.astype(jnp.bfloat16))


def compress_blocks(t, pos, w1, w2):
    b, s, g, d = t.shape
    halves = t.transpose(0, 2, 1, 3).reshape(b, g, s // CMP_STRIDE, CMP_STRIDE, d)
    nxt = jnp.concatenate([halves[:, :, 1:], jnp.zeros_like(halves[:, :, :1])], axis=2)
    blocks = jnp.concatenate([halves, nxt], axis=3) + pos
    flat = blocks.reshape(b * g * (s // CMP_STRIDE), CMP_BLOCK * d)
    return mlp2(flat, w1, w2).reshape(b, g, s // CMP_STRIDE, d)


N_SEL = SEQ // SLC_BLOCK
N_KT = SEQ // Q_BLOCK
CMP_FRONT = 256
CMP_ROWS = CMP_FRONT + SEQ // CMP_STRIDE
CMP_WIN = 256
REL_TABLE = 4096
NEAR_TILES = 24
WIN_TILES = WINDOW // Q_BLOCK + 1
SEL_GROUP = 8
QL = NSA_GROUP * Q_BLOCK


def _nsa_kernel(qT_ref, gT_ref, kc_ref, vcT_ref, kk_ref, vsT_ref, vwT_ref, at_ref, bcT_ref, bT_ref, bTw_ref,
                o_ref, s_scr, sel_scr):
    f32, bf16 = jnp.float32, jnp.bfloat16
    i = pl.program_id(2)
    qT = (qT_ref[0, 0, 0] * (NSA_HEAD_DIM ** -0.5)).astype(bf16)
    zq = jnp.zeros_like(qT)
    q_sel = jnp.concatenate([qT, zq], axis=0)
    q_win = jnp.concatenate([zq, qT], axis=0)

    r0 = pl.multiple_of(8 * i + 8, 8)
    s = jnp.dot(kc_ref[0, 0], qT, preferred_element_type=f32)
    rio = lax.broadcasted_iota(jnp.int32, (CMP_ROWS, Q_BLOCK), 0)
    rowmask = jnp.where((rio >= CMP_FRONT) & (rio < r0 + CMP_WIN), 0.0, NEG_INF)
    s_scr[...] = s + jnp.concatenate([rowmask] * NSA_GROUP, axis=1)
    s_scr[pl.ds(r0, CMP_WIN), :] = s_scr[pl.ds(r0, CMP_WIN), :] + bcT_ref[0]
    s = s_scr[...]
    m = jnp.max(s, axis=0, keepdims=True)
    p = jnp.exp(s - m)
    l = jnp.sum(p, axis=0, keepdims=True)
    inv = jnp.where(m > 0.1 * NEG_INF, 1.0 / l, 0.0)
    pn = p * inv
    o_c = jnp.dot(vcT_ref[0, 0], pn.astype(bf16), preferred_element_type=f32)

    psum = pn[:, 0:Q_BLOCK]
    for r in range(1, NSA_GROUP):
        psum = psum + pn[:, r * Q_BLOCK:(r + 1) * Q_BLOCK]
    hi = psum.astype(bf16)
    lo = (psum - hi.astype(f32)).astype(bf16)
    imp = (jnp.dot(at_ref[...], hi, preferred_element_type=f32)
           + jnp.dot(at_ref[...], lo, preferred_element_type=f32))
    blk = lax.broadcasted_iota(jnp.int32, (N_SEL, Q_BLOCK), 0)
    qi = lax.broadcasted_iota(jnp.int32, (N_SEL, Q_BLOCK), 1)
    cur = 2 * i + (qi >= SLC_BLOCK).astype(jnp.int32)
    forced = (blk == 0) | (blk == cur) | (blk == cur - 1)
    valid = blk <= cur
    score = jnp.where(forced, SLC_FORCE, jnp.where(valid, imp, -SLC_FORCE))
    blkf = blk.astype(f32)
    chosen = jnp.zeros((N_SEL, Q_BLOCK), f32)
    for _ in range(SLC_TOPN):
        mx = jnp.max(score, axis=0, keepdims=True)
        first = jnp.min(jnp.where(score == mx, blkf, float(N_SEL)), axis=0, keepdims=True)
        hit = blkf == first
        chosen = jnp.where(hit, 1.0, chosen)
        score = jnp.where(hit, -jnp.inf, score)
    sel_scr[...] = jnp.where((chosen > 0.5) & valid, 0.0, NEG_INF)

    def attend(kt, carry, q_aug, vT_ref, bias):
        m_run, l_run, acc = carry
        scs = [jnp.dot(kk_ref[0, 0, k], q_aug, preferred_element_type=f32) + b for k, b in zip(kt, bias)]
        for k, sc in zip(kt, scs):
            m_new = jnp.maximum(m_run, jnp.max(sc, axis=0, keepdims=True))
            alpha = jnp.exp(m_run - m_new)
            pt = jnp.exp(sc - m_new)
            l_run = alpha * l_run + jnp.sum(pt, axis=0, keepdims=True)
            acc = alpha * acc + jnp.dot(vT_ref[0, 0, k], pt.astype(bf16), preferred_element_type=f32)
            m_run = m_new
        return m_run, l_run, acc

    def sel_mask(kt):
        m0 = jnp.broadcast_to(sel_scr[pl.ds(2 * kt, 1), :], (SLC_BLOCK, Q_BLOCK))
        m1 = jnp.broadcast_to(sel_scr[pl.ds(2 * kt + 1, 1), :], (SLC_BLOCK, Q_BLOCK))
        mk = jnp.concatenate([m0, m1], axis=0)
        return jnp.concatenate([mk] * NSA_GROUP, axis=1)

    init = (jnp.full((1, QL), NEG_INF, f32), jnp.zeros((1, QL), f32), jnp.zeros((NSA_HEAD_DIM, QL), f32))
    U = SEL_GROUP

    def far_group(g, c):
        kts = [g * U + u for u in range(U)]
        return attend(kts, c, q_sel, vsT_ref, [sel_mask(k) for k in kts])

    def near_group(g, c):
        kts, biases = [], []
        for u in range(U):
            delta = i - (g * U + u)
            k = jnp.minimum(g * U + u, i)
            tile = jnp.where(delta < 0, NEAR_TILES + 1, jnp.minimum(delta, NEAR_TILES))
            kts.append(k)
            biases.append(sel_mask(k) + bT_ref[0, tile])
        return attend(kts, c, q_sel, vsT_ref, biases)

    n_far = jnp.maximum((i - (NEAR_TILES - 1)) // U, 0)
    carry = lax.fori_loop(0, n_far, far_group, init)
    _, l_s, acc_s = lax.fori_loop(n_far, i // U + 1, near_group, carry)
    o_s = acc_s / l_s

    kts, biases = [], []
    for u in range(WIN_TILES):
        k = i - (WIN_TILES - 1) + u
        kts.append(jnp.maximum(k, 0))
        biases.append(bTw_ref[0, jnp.where(k >= 0, WIN_TILES - 1 - u, WIN_TILES)])
    _, l_w, acc_w = attend(kts, init, q_win, vwT_ref, biases)
    o_w = acc_w / l_w

    gate = jax.nn.sigmoid(gT_ref[0, 0, 0])
    o_ref[0, 0, 0] = gate[0:1] * o_c + gate[1:2] * o_s + gate[2:3] * o_w


BIAS_EXT = REL_TABLE + 2 * Q_BLOCK


CMP_OFF = CMP_STRIDE * (CMP_WIN - 8) - (CMP_BLOCK - 1)


def _bias_tile_kernel(bd_ref, bdc_ref, cmp_ref, near_ref, win_ref):
    row = bd_ref[0]
    kj = lax.broadcasted_iota(jnp.int32, (Q_BLOCK, Q_BLOCK), 0)
    qi = lax.broadcasted_iota(jnp.int32, (Q_BLOCK, Q_BLOCK), 1)
    for t in range(NEAR_TILES):
        seg = jnp.broadcast_to(row[:, t * Q_BLOCK:(t + 2) * Q_BLOCK], (Q_BLOCK, 2 * Q_BLOCK))
        tile = pltpu.roll(seg, 0, 1, stride=1, stride_axis=0)[:, Q_BLOCK:]
        near_ref[0, t] = tile
        if t < WIN_TILES:
            win_ref[0, t] = jnp.where(t * Q_BLOCK + qi - kj < WINDOW, tile, NEG_INF)
    near_ref[0, NEAR_TILES] = jnp.zeros((Q_BLOCK, Q_BLOCK), jnp.float32)
    near_ref[0, NEAR_TILES + 1] = jnp.full((Q_BLOCK, Q_BLOCK), NEG_INF, jnp.float32)
    win_ref[0, WIN_TILES] = jnp.full((Q_BLOCK, Q_BLOCK), NEG_INF, jnp.float32)
    full = jnp.broadcast_to(bdc_ref[0], (CMP_WIN, BIAS_EXT))
    cmp_ref[0] = pltpu.roll(full, 0, 1, stride=CMP_STRIDE, stride_axis=0)[:, REL_TABLE:REL_TABLE + Q_BLOCK]


def _bias_tables(rel_bias):
    G, R = NSA_KV_HEADS, NSA_GROUP
    bd = rel_bias[rel_bucket(jnp.arange(REL_TABLE))]
    bd = (bd - bd[REL_TABLE - 1]).T
    def padded(front):
        pad = lambda n: jnp.full((NSA_HEADS, n), NEG_INF, jnp.float32)
        return jnp.concatenate([pad(front), bd, pad(BIAS_EXT - REL_TABLE - front)], axis=1).reshape(NSA_HEADS, 1, BIAS_EXT)

    bd_ext, bd_cmp = padded(Q_BLOCK), padded(REL_TABLE - CMP_OFF)
    head = lambda *blk: pl.BlockSpec((1,) + blk + (Q_BLOCK,), lambda hd: (hd // R,) + (0,) * len(blk) + (hd % R,))
    return pl.pallas_call(
        _bias_tile_kernel,
        grid=(NSA_HEADS,),
        in_specs=[pl.BlockSpec((1, 1, BIAS_EXT), lambda hd: (hd, 0, 0))] * 2,
        out_specs=[head(CMP_WIN), head(NEAR_TILES + 2, Q_BLOCK), head(WIN_TILES + 1, Q_BLOCK)],
        out_shape=[jax.ShapeDtypeStruct((G, CMP_WIN, QL), jnp.float32),
                   jax.ShapeDtypeStruct((G, NEAR_TILES + 2, Q_BLOCK, QL), jnp.float32),
                   jax.ShapeDtypeStruct((G, WIN_TILES + 1, Q_BLOCK, QL), jnp.float32)],
        compiler_params=pltpu.CompilerParams(dimension_semantics=("arbitrary",),
                                             vmem_limit_bytes=V7X_VMEM_LIMIT_BYTES),
        name="nsa_bias_tiles",
    )(bd_ext, bd_cmp)


def _pool_matrix():
    per = SLC_BLOCK // CMP_STRIDE
    n_span = CMP_BLOCK // CMP_STRIDE
    k = np.arange(CMP_ROWS)[None, :] - CMP_FRONT
    j = np.arange(N_SEL)[:, None]
    return jnp.asarray((k >= per * j - (n_span - 1)) & (k <= per * j + per - 1), jnp.bfloat16)


def nsa_shared_kv(h, kv_norm_g, w_kv, cmp_pos_k, cmp_pos_v, cmp_w1_k, cmp_w2_k, cmp_w1_v, cmp_w2_v):
    b, s, _ = h.shape
    G, DH = NSA_KV_HEADS, NSA_HEAD_DIM
    bf16 = jnp.bfloat16
    kv = _mm(h, kv_norm_g, w_kv, True).reshape(b, s, 6, G, DH)
    k_cmp = compress_blocks(kv[:, :, 0], cmp_pos_k, cmp_w1_k, cmp_w2_k)
    v_cmp = compress_blocks(kv[:, :, 1], cmp_pos_v, cmp_w1_v, cmp_w2_v)
    front = jnp.zeros((b, G, CMP_FRONT, DH), bf16)
    kc = jnp.concatenate([front, k_cmp.astype(bf16)], axis=2)
    vcT = jnp.concatenate([front, v_cmp.astype(bf16)], axis=2).transpose(0, 1, 3, 2)
    kk = jnp.concatenate([kv[:, :, 2], kv[:, :, 4]], axis=-1).astype(bf16)
    kk = kk.transpose(0, 2, 1, 3).reshape(b, G, N_KT, Q_BLOCK, 2 * DH)

    def vt(v):
        return v.astype(bf16).transpose(0, 2, 1, 3).reshape(b, G, N_KT, Q_BLOCK, DH).transpose(0, 1, 2, 4, 3)

    return kc, vcT, kk, vt(kv[:, :, 3]), vt(kv[:, :, 5])


def nsa_mixer(h, norm_g, w_in, w_out, rel_bias, kc, vcT, kk, vsT, vwT):
    b, s, _ = h.shape
    G, R, DH = NSA_KV_HEADS, NSA_GROUP, NSA_HEAD_DIM
    nb = s // Q_BLOCK
    proj = _mm(h, norm_g, w_in, True)
    qT = proj[..., :NSA_HEADS * DH].reshape(b, nb, Q_BLOCK, G, R, DH).transpose(0, 3, 1, 5, 4, 2)
    qT = qT.reshape(b, G, nb, DH, QL)
    gT = proj[..., NSA_HEADS * DH:].reshape(b, nb, Q_BLOCK, G, R, 3).transpose(0, 3, 1, 5, 4, 2)
    gT = gT.reshape(b, G, nb, 3, QL)
    b_cmp, b_near, b_win = _bias_tables(rel_bias)
    at = _pool_matrix()

    per_bg = lambda *blk: pl.BlockSpec((1, 1) + blk, lambda bi, gi, i: (bi, gi) + (0,) * len(blk))
    per_g = lambda *blk: pl.BlockSpec((1,) + blk, lambda bi, gi, i: (gi,) + (0,) * len(blk))
    oT = pl.pallas_call(
        _nsa_kernel,
        grid=(b, G, nb),
        in_specs=[
            pl.BlockSpec((1, 1, 1, DH, QL), lambda bi, gi, i: (bi, gi, i, 0, 0)),
            pl.BlockSpec((1, 1, 1, 3, QL), lambda bi, gi, i: (bi, gi, i, 0, 0)),
            per_bg(CMP_ROWS, DH),
            per_bg(DH, CMP_ROWS),
            per_bg(N_KT, Q_BLOCK, 2 * DH),
            per_bg(N_KT, DH, Q_BLOCK),
            per_bg(N_KT, DH, Q_BLOCK),
            pl.BlockSpec((N_SEL, CMP_ROWS), lambda bi, gi, i: (0, 0)),
            per_g(CMP_WIN, QL),
            per_g(NEAR_TILES + 2, Q_BLOCK, QL),
            per_g(WIN_TILES + 1, Q_BLOCK, QL),
        ],
        out_specs=pl.BlockSpec((1, 1, 1, DH, QL), lambda bi, gi, i: (bi, gi, i, 0, 0)),
        out_shape=jax.ShapeDtypeStruct((b, G, nb, DH, QL), jnp.float32),
        scratch_shapes=[pltpu.VMEM((CMP_ROWS, QL), jnp.float32), pltpu.VMEM((N_SEL, Q_BLOCK), jnp.float32)],
        compiler_params=pltpu.CompilerParams(
            dimension_semantics=("arbitrary", "arbitrary", "arbitrary"),
            vmem_limit_bytes=V7X_VMEM_LIMIT_BYTES),
        name="nsa_attention",
    )(qT, gT, kc, vcT, kk, vsT, vwT, at, b_cmp, b_near, b_win)
    o = oT.reshape(b, G, nb, DH, R, Q_BLOCK).transpose(0, 2, 5, 1, 4, 3).reshape(b, s, NSA_HEADS * DH)
    return matmul_residual(o.reshape(b * s, NSA_HEADS * DH), h.reshape(b * s, -1), w_out).reshape(h.shape)


PEER_SLOTS = PEER_HEADS * PEER_TOPK
PEER_ROUTE_TM = 256
PEER_ROUTE_LOCKSTEP = 2
PEER_TT = 32
PEER_CHUNKS = D_MODEL // V7X_LANES
PEER_WORDS = PEER_CHUNKS // 2
PEER_TILE_PITCH = PEER_SLOTS + 8


def _top_rows(arrays, rowid, n, payloads=None):
    arrays = list(arrays)
    vals = [[] for _ in arrays]
    picks = [[] for _ in arrays]
    for _ in range(n):
        for a, s in enumerate(arrays):
            mx = jnp.max(s, axis=0, keepdims=True)
            first = jnp.min(jnp.where(s == mx, rowid, jnp.inf), axis=0, keepdims=True)
            hit = rowid == first
            vals[a].append(mx)
            picks[a].append(first if payloads is None
                            else jnp.max(jnp.where(hit, payloads[a], -1.0), axis=0, keepdims=True))
            arrays[a] = jnp.where(hit, -jnp.inf, s)
    return vals, picks


def _peer_route_kernel(h_ref, g_ref, wqT_ref, k1_ref, k2_ref, xn_ref, ids_ref, gate_ref):
    f32, bf16 = jnp.float32, jnp.bfloat16
    x = h_ref[...]
    xn = x * lax.rsqrt(jnp.mean(x * x, axis=-1, keepdims=True) + NORM_EPS) * g_ref[...]
    xn_ref[...] = xn
    qT = lax.dot_general(wqT_ref[...], xn.astype(bf16), (((1,), (1,)), ((), ())), preferred_element_type=f32)
    half = PEER_QDIM // 2
    tm = x.shape[0]
    key_id = lax.broadcasted_iota(jnp.int32, (PEER_N_KEYS, tm), 0).astype(f32)
    stair = [PEER_TOPK // (a + 1) for a in range(PEER_TOPK)]
    n_pad = -sum(stair) % 8
    cand_pos = jnp.concatenate(
        [lax.broadcasted_iota(jnp.int32, (nb, tm), 0).astype(f32) + float(a * PEER_TOPK) for a, nb in enumerate(stair)]
        + [jnp.full((n_pad, tm), float(PEER_TOPK * PEER_TOPK), f32)], axis=0)
    for h0 in range(0, PEER_HEADS, PEER_ROUTE_LOCKSTEP):
        heads = range(h0, h0 + PEER_ROUTE_LOCKSTEP)
        scores = []
        for hd in heads:
            q1 = qT[hd * PEER_QDIM: hd * PEER_QDIM + half].astype(bf16)
            q2 = qT[hd * PEER_QDIM + half: (hd + 1) * PEER_QDIM].astype(bf16)
            scores.append(jnp.dot(k1_ref[hd], q1, preferred_element_type=f32))
            scores.append(jnp.dot(k2_ref[hd], q2, preferred_element_type=f32))
        vals, idxs = _top_rows(scores, key_id, PEER_TOPK)
        cands, cand_ids = [], []
        for a in range(PEER_ROUTE_LOCKSTEP):
            v1, i1 = vals[2 * a], idxs[2 * a]
            v2m = jnp.concatenate(vals[2 * a + 1], axis=0)
            i2m = jnp.concatenate(idxs[2 * a + 1], axis=0)
            cands.append(jnp.concatenate([v1[r] + v2m[0:nb] for r, nb in enumerate(stair)]
                                         + [jnp.full((n_pad, tm), -jnp.inf, f32)], axis=0))
            cand_ids.append(jnp.concatenate([i1[r] * float(PEER_N_KEYS) + i2m[0:nb] for r, nb in enumerate(stair)]
                                            + [jnp.full((n_pad, tm), -1.0, f32)], axis=0))
        tops, ids = _top_rows(cands, cand_pos, PEER_TOPK, payloads=cand_ids)
        for a, hd in enumerate(heads):
            top_s = jnp.concatenate(tops[a], axis=0)
            e = jnp.exp(top_s - top_s[0:1])
            gate_ref[hd * PEER_TOPK:(hd + 1) * PEER_TOPK, :] = e / jnp.sum(e, axis=0, keepdims=True)
            ids_ref[hd * PEER_TOPK:(hd + 1) * PEER_TOPK, :] = (jnp.concatenate(ids[a], axis=0)
                                                               * PEER_WORDS).astype(jnp.int32)


def peer_route(h2, norm_g, w_q, keys1, keys2):
    T, d = h2.shape
    tm = PEER_ROUTE_TM
    bf16 = jnp.bfloat16
    nq = PEER_HEADS * PEER_QDIM
    half = PEER_QDIM // 2
    return pl.pallas_call(
        _peer_route_kernel,
        grid=(T // tm,),
        in_specs=[pl.BlockSpec((tm, d), lambda i: (i, 0)),
                  pl.BlockSpec((1, d), lambda i: (0, 0)),
                  pl.BlockSpec((nq, d), lambda i: (0, 0)),
                  pl.BlockSpec((PEER_HEADS, PEER_N_KEYS, half), lambda i: (0, 0, 0)),
                  pl.BlockSpec((PEER_HEADS, PEER_N_KEYS, half), lambda i: (0, 0, 0))],
        out_specs=[pl.BlockSpec((tm, d), lambda i: (i, 0)),
                   pl.BlockSpec((PEER_SLOTS, tm), lambda i: (0, i)),
                   pl.BlockSpec((PEER_SLOTS, tm), lambda i: (0, i))],
        out_shape=[jax.ShapeDtypeStruct((T, d), jnp.float32),
                   jax.ShapeDtypeStruct((PEER_SLOTS, T), jnp.int32),
                   jax.ShapeDtypeStruct((PEER_SLOTS, T), jnp.float32)],
        compiler_params=pltpu.CompilerParams(dimension_semantics=("arbitrary",),
                                             vmem_limit_bytes=V7X_VMEM_LIMIT_BYTES),
        name="peer_route",
    )(h2, norm_g.reshape(1, d), w_q.T.astype(bf16), keys1.astype(bf16), keys2.astype(bf16))


def _unpack_pair(words):
    lo = lax.bitcast_convert_type(lax.shift_left(words, 16), jnp.float32)
    hi = lax.bitcast_convert_type(lax.bitwise_and(words, -65536), jnp.float32)
    return lo, hi


def _pack_rows(tab):
    bits = lax.bitcast_convert_type(tab.astype(jnp.bfloat16), jnp.uint16).astype(jnp.uint32)
    bits = bits.reshape(tab.shape[0], PEER_WORDS, 2, V7X_LANES)
    words = lax.bitcast_convert_type(bits[:, :, 0] | (bits[:, :, 1] << 16), jnp.int32)
    return words.reshape(tab.shape[0] * PEER_WORDS, V7X_LANES)


def _peer_gather_token(ids_ref, tab, tile, t):
    for k in range(PEER_SLOTS):
        row = pl.multiple_of(ids_ref[t * PEER_SLOTS + k], PEER_WORDS)
        tile[pl.ds(k, PEER_WORDS, stride=PEER_TILE_PITCH), :] = tab[pl.ds(row, PEER_WORDS), :]


def _peer_chunk_pair(tile, j):
    return _unpack_pair(tile[j * PEER_TILE_PITCH:j * PEER_TILE_PITCH + PEER_SLOTS, :])


def _peer_load_table(tab_hbm, tab, sem):
    @pl.when(pl.program_id(0) == 0)
    def _():
        copy = pltpu.make_async_copy(tab_hbm, tab, sem)
        copy.start()
        copy.wait()


def _peer_act_kernel(ids_ref, xn_ref, gt_ref, tab_hbm, w_ref, tab, tile_a, tile_b, sem):
    _peer_load_table(tab_hbm, tab, sem)
    for t in range(PEER_TT):
        tile = (tile_a, tile_b)[t % 2]
        _peer_gather_token(ids_ref, tab, tile, t)
        xrow = xn_ref[t:t + 1, :]
        acc = jnp.zeros((PEER_SLOTS, V7X_LANES), jnp.float32)
        for j in range(PEER_WORDS):
            lo, hi = _peer_chunk_pair(tile, j)
            acc = acc + lo * xrow[:, 2 * j * V7X_LANES:(2 * j + 1) * V7X_LANES]
            acc = acc + hi * xrow[:, (2 * j + 1) * V7X_LANES:(2 * j + 2) * V7X_LANES]
        act = jnp.sum(acc, axis=1, keepdims=True)
        w_ref[0, :, t:t + 1] = gt_ref[0, :, t:t + 1] * jax.nn.gelu(act)


def _peer_out_kernel(ids_ref, w_ref, h_ref, tab_hbm, o_ref, tab, tile_a, tile_b, sem):
    _peer_load_table(tab_hbm, tab, sem)
    for t in range(PEER_TT):
        tile = (tile_a, tile_b)[t % 2]
        _peer_gather_token(ids_ref, tab, tile, t)
        w = w_ref[0, :, t:t + 1]
        for j in range(PEER_WORDS):
            for c, part in zip((2 * j, 2 * j + 1), _peer_chunk_pair(tile, j)):
                lanes = slice(c * V7X_LANES, (c + 1) * V7X_LANES)
                o_ref[t:t + 1, lanes] = h_ref[t:t + 1, lanes] + jnp.sum(part * w, axis=0, keepdims=True)


def peer_experts(xn, idsT, gateT, u_tab, v_tab, h2):
    T, d = xn.shape
    tt = PEER_TT
    nt = T // tt
    ids = idsT.T.reshape(T * PEER_SLOTS)
    gt = gateT.reshape(PEER_SLOTS, nt, tt).transpose(1, 0, 2)
    ids_spec = pl.BlockSpec((tt * PEER_SLOTS,), lambda i: (i,), memory_space=pltpu.SMEM)
    col_spec = pl.BlockSpec((1, PEER_SLOTS, tt), lambda i: (i, 0, 0))
    row_spec = pl.BlockSpec((tt, d), lambda i: (i, 0))
    tile = pltpu.VMEM((PEER_WORDS * PEER_TILE_PITCH, V7X_LANES), jnp.int32)
    scratch = [pltpu.VMEM((PEER_EXPERTS * PEER_WORDS, V7X_LANES), jnp.int32), tile, tile, pltpu.SemaphoreType.DMA(())]
    params = pltpu.CompilerParams(dimension_semantics=("arbitrary",), vmem_limit_bytes=V7X_VMEM_LIMIT_BYTES)
    w = pl.pallas_call(
        _peer_act_kernel,
        grid=(nt,),
        in_specs=[ids_spec, row_spec, col_spec, pl.BlockSpec(memory_space=pl.ANY)],
        out_specs=col_spec,
        out_shape=jax.ShapeDtypeStruct((nt, PEER_SLOTS, tt), jnp.float32),
        scratch_shapes=scratch, compiler_params=params, name="peer_act",
    )(ids, xn, gt, _pack_rows(u_tab))
    return pl.pallas_call(
        _peer_out_kernel,
        grid=(nt,),
        in_specs=[ids_spec, col_spec, row_spec, pl.BlockSpec(memory_space=pl.ANY)],
        out_specs=row_spec,
        out_shape=jax.ShapeDtypeStruct((T, d), jnp.float32),
        scratch_shapes=scratch, compiler_params=params, name="peer_out",
    )(ids, w, h2, _pack_rows(v_tab))


def peer_ffn_residual(h, norm_g, w_q, keys1, keys2, u_tab, v_tab):
    b, s, d = h.shape
    h2 = h.reshape(b * s, d)
    xn, idsT, gateT = peer_route(h2, norm_g, w_q, keys1, keys2)
    return peer_experts(xn, idsT, gateT, u_tab, v_tab, h2).reshape(b, s, d)


def _ple_kernel(h_ref, p_ref, g_ref, wup_ref, wgate_ref, fg_ref, o_ref, *, final_norm):
    bf16, f32 = jnp.bfloat16, jnp.float32

    def rms(v, gain):
        return v * lax.rsqrt(jnp.mean(v * v, axis=-1, keepdims=True) + NORM_EPS) * gain

    h = h_ref[...]
    gate = jax.nn.sigmoid(jnp.dot(rms(h, g_ref[...]).astype(bf16), wgate_ref[...], preferred_element_type=f32))
    y = h + jnp.dot(p_ref[...].astype(bf16), wup_ref[...], preferred_element_type=f32) * gate
    o_ref[...] = rms(y, fg_ref[...]) if final_norm else y


def per_layer_embed_residual(h, p_i, norm_g, w_up, w_gate, final_g, *, final_norm, tm=512):
    b, s, d = h.shape
    pd = p_i.shape[-1]
    const = lambda *shape: pl.BlockSpec(shape, lambda i: (0,) * len(shape))
    out = pl.pallas_call(
        functools.partial(_ple_kernel, final_norm=final_norm),
        grid=(b * s // tm,),
        in_specs=[pl.BlockSpec((tm, d), lambda i: (i, 0)), pl.BlockSpec((tm, pd), lambda i: (i, 0)),
                  const(1, d), const(pd, d), const(d, d), const(1, d)],
        out_specs=pl.BlockSpec((tm, d), lambda i: (i, 0)),
        out_shape=jax.ShapeDtypeStruct((b * s, d), jnp.float32),
        compiler_params=pltpu.CompilerParams(dimension_semantics=("arbitrary",),
                                             vmem_limit_bytes=V7X_VMEM_LIMIT_BYTES),
        name="per_layer_embed",
    )(h.reshape(b * s, d), p_i.reshape(b * s, pd), norm_g.reshape(1, d), w_up.astype(jnp.bfloat16),
      w_gate.astype(jnp.bfloat16), final_g.reshape(1, d))
    return out.reshape(b, s, d)


def kernel(x, p, a_norm_g, a_w_in, a_conv_w, a_conv_b, a_dt_bias, a_log, a_d_skip, a_gnorm_g, a_w_out,
           kv_norm_g, w_kv, cmp_pos_k, cmp_pos_v, cmp_w1_k, cmp_w2_k, cmp_w1_v, cmp_w2_v, rel_bias,
           b_norm_g, b_w_in, b_w_out, c_norm_g, c_w_q, c_keys1, c_keys2, c_u, c_v,
           e_norm_g, e_w_up, e_w_gate, final_g):
    h = x
    shared = None
    for i in range(DEPTH):
        if i < N_A_LAYERS:
            h = ssd_mixer_residual(h, a_norm_g[i], a_w_in[i], a_conv_w[i], a_conv_b[i], a_dt_bias[i],
                                   a_log[i], a_d_skip[i], a_gnorm_g[i], a_w_out[i])
        else:
            if shared is None:
                shared = nsa_shared_kv(h, kv_norm_g, w_kv, cmp_pos_k, cmp_pos_v,
                                       cmp_w1_k, cmp_w2_k, cmp_w1_v, cmp_w2_v)
            j = i - N_A_LAYERS
            h = nsa_mixer(h, b_norm_g[j], b_w_in[j], b_w_out[j], rel_bias, *shared)
        h = peer_ffn_residual(h, c_norm_g[i], c_w_q[i], c_keys1[i], c_keys2[i], c_u[i], c_v[i])
        h = per_layer_embed_residual(h, p[i], e_norm_g[i], e_w_up[i], e_w_gate[i], final_g,
                                     final_norm=(i == DEPTH - 1))
    return h
```

```python
import functools
import math

import jax
import jax.numpy as jnp
import numpy as np
from jax import lax
from jax.experimental import pallas as pl
from jax.experimental.pallas import tpu as pltpu

D_MODEL = 1024
BATCH = 2
SEQ = 16384
DEPTH = 2
N_A_LAYERS = DEPTH // 2
N_B_LAYERS = DEPTH - N_A_LAYERS

SSD_D_INNER = 2 * D_MODEL
SSD_HEADDIM = 64
SSD_HEADS = SSD_D_INNER // SSD_HEADDIM
SSD_GROUPS = 8
SSD_HEADS_PER_GROUP = SSD_HEADS // SSD_GROUPS
SSD_STATE = 128
SSD_CONV = 4
SSD_CHUNK = 128
SSD_CONV_DIM = SSD_D_INNER + 2 * SSD_GROUPS * SSD_STATE
SSD_IN_DIM = SSD_D_INNER + SSD_CONV_DIM + SSD_HEADS

NSA_HEADS = 16
NSA_KV_HEADS = 4
NSA_GROUP = NSA_HEADS // NSA_KV_HEADS
NSA_HEAD_DIM = 64
CMP_BLOCK = 32
CMP_STRIDE = 16
CMP_HIDDEN = 256
SLC_BLOCK = 64
SLC_TOPN = 16
SLC_FORCE = 1e4
WINDOW = 512
Q_BLOCK = 128
NSA_IN_DIM = NSA_HEADS * NSA_HEAD_DIM + 3 * NSA_HEADS
KV_DIM = 6 * NSA_KV_HEADS * NSA_HEAD_DIM

REL_BUCKETS = 32
REL_MAX_DISTANCE = 4096

PEER_HEADS = 8
PEER_N_KEYS = 128
PEER_EXPERTS = PEER_N_KEYS * PEER_N_KEYS
PEER_QDIM = 256
PEER_TOPK = 16
PEER_TOKEN_CHUNK = 128
PEER_V_SCALE = PEER_HEADS ** -0.5

PLE_DIM = 256
NORM_EPS = 1e-6
NEG_INF = -1e30

V7X_LANES = 128
V7X_VMEM_LIMIT_BYTES = 56 * 1024 * 1024


def _norm_matmul_kernel(x_ref, g_ref, w_ref, o_ref, *, normalize):
    x = x_ref[...]
    if normalize:
        x = x * lax.rsqrt(jnp.mean(x * x, axis=-1, keepdims=True) + NORM_EPS) * g_ref[...]
    o_ref[...] = jnp.dot(x.astype(jnp.bfloat16), w_ref[...], preferred_element_type=jnp.float32)


def _matmul_residual_kernel(x_ref, r_ref, w_ref, o_ref):
    o_ref[...] = r_ref[...] + jnp.dot(x_ref[...].astype(jnp.bfloat16), w_ref[...], preferred_element_type=jnp.float32)


def matmul_residual(x, r, w, *, tm=512):
    m, k = x.shape
    n = w.shape[1]
    return pl.pallas_call(
        _matmul_residual_kernel,
        grid=(m // tm,),
        in_specs=[pl.BlockSpec((tm, k), lambda i: (i, 0)), pl.BlockSpec((tm, n), lambda i: (i, 0)),
                  pl.BlockSpec((k, n), lambda i: (0, 0))],
        out_specs=pl.BlockSpec((tm, n), lambda i: (i, 0)),
        out_shape=jax.ShapeDtypeStruct((m, n), jnp.float32),
        compiler_params=pltpu.CompilerParams(dimension_semantics=("arbitrary",),
                                             vmem_limit_bytes=V7X_VMEM_LIMIT_BYTES),
        name="matmul_residual",
    )(x, r, w.astype(jnp.bfloat16))


def norm_matmul(x, g, w, *, normalize, tm=512, tn=None):
    m, k = x.shape
    n = w.shape[1]
    n_pad = -(-n // V7X_LANES) * V7X_LANES
    wb = w.astype(jnp.bfloat16)
    if n_pad != n:
        wb = jnp.pad(wb, ((0, 0), (0, n_pad - n)))
    if tn is None:
        tn = n_pad
    assert m % tm == 0 and n_pad % tn == 0
    out = pl.pallas_call(
        functools.partial(_norm_matmul_kernel, normalize=normalize),
        grid=(m // tm, n_pad // tn),
        in_specs=[
            pl.BlockSpec((tm, k), lambda i, j: (i, 0)),
            pl.BlockSpec((1, k), lambda i, j: (0, 0)),
            pl.BlockSpec((k, tn), lambda i, j: (0, j)),
        ],
        out_specs=pl.BlockSpec((tm, tn), lambda i, j: (i, j)),
        out_shape=jax.ShapeDtypeStruct((m, n_pad), jnp.float32),
        compiler_params=pltpu.CompilerParams(
            dimension_semantics=("arbitrary", "arbitrary"),
            vmem_limit_bytes=V7X_VMEM_LIMIT_BYTES),
        name="norm_matmul" if normalize else "matmul",
    )(x, g.reshape(1, k), wb)
    return out[:, :n] if n_pad != n else out


def _mm(x3, g, w, normalize, **kw):
    b, s, k = x3.shape
    if g is None:
        g = jnp.ones((k,), jnp.float32)
    return norm_matmul(x3.reshape(b * s, k), g, w, normalize=normalize, **kw).reshape(b, s, w.shape[1])


def rel_bucket(dist):
    dist = jnp.maximum(dist, 0)
    max_exact = REL_BUCKETS // 2
    d = jnp.maximum(dist, 1).astype(jnp.float32)
    large = max_exact + (jnp.log(d / max_exact) / math.log(REL_MAX_DISTANCE / max_exact)
                         * (REL_BUCKETS - max_exact)).astype(jnp.int32)
    large = jnp.minimum(large, REL_BUCKETS - 1)
    return jnp.where(dist < max_exact, dist, large)


SSD_GROUP_W = SSD_HEADS_PER_GROUP * SSD_HEADDIM
SSD_IN_PAD = -(-SSD_IN_DIM // V7X_LANES) * V7X_LANES
SSD_TAIL = 8


def _split3(x):
    bf16, f32 = jnp.bfloat16, jnp.float32
    hi = x.astype(bf16)
    r1 = x - hi.astype(f32)
    mid = r1.astype(bf16)
    lo = (r1 - mid.astype(f32)).astype(bf16)
    return hi, mid, lo


def _ssd_kernel(zx_ref, h_ref, cw_ref, cb_ref, dtb_ref, alog_ref, dskip_ref, gn_ref, wout_ref, tri_ref, exp_ref,
                o_ref, xs_scr, state_scr):
    f32, bf16 = jnp.float32, jnp.bfloat16
    L, N, GW = SSD_CHUNK, SSD_STATE, SSD_GROUP_W
    c = pl.program_id(1)

    @pl.when(c == 0)
    def _():
        xs_scr[0:SSD_TAIL, :] = jnp.zeros((SSD_TAIL, SSD_CONV_DIM), f32)
        state_scr[...] = jnp.zeros_like(state_scr)

    xs_scr[SSD_TAIL:SSD_TAIL + L, :] = zx_ref[:, SSD_D_INNER:SSD_D_INNER + SSD_CONV_DIM]
    conv = cb_ref[...]
    for w in range(SSD_CONV):
        conv = conv + cw_ref[w:w + 1, :] * xs_scr[pl.ds(SSD_TAIL - (SSD_CONV - 1) + w, L), :]
    xs_scr[0:SSD_TAIL, :] = xs_scr[L:L + SSD_TAIL, :]
    xbc = conv * jax.nn.sigmoid(conv)

    dt_in = zx_ref[:, SSD_D_INNER + SSD_CONV_DIM:SSD_IN_PAD] + dtb_ref[...]
    dt = jnp.maximum(dt_in, 0.0) + jnp.log1p(jnp.exp(-jnp.abs(dt_in)))
    a_dt = dt * -jnp.exp(alog_ref[...])
    a_cs = sum(jnp.dot(tri_ref[...], t, preferred_element_type=f32) for t in _split3(a_dt))
    a_csT = a_cs.T
    a_end = a_cs[L - 1:L, :]
    grow = jnp.exp(a_cs)
    to_end = jnp.exp(a_end - a_cs)

    def per_channel(q):
        return sum(jnp.dot(t, exp_ref[...], preferred_element_type=f32) for t in _split3(q)[:2])

    dt_c, grow_c, to_end_c = per_channel(dt), per_channel(grow), per_channel(to_end)
    end_c = per_channel(jnp.broadcast_to(jnp.exp(a_end), (8, V7X_LANES)))[0:1]
    dskip_c = per_channel(jnp.broadcast_to(dskip_ref[...], (8, V7X_LANES)))[0:1]

    x = xbc[:, :SSD_D_INNER]
    xd = x * dt_c
    xd_bf = xd.astype(bf16)
    xe_bf = (xd * to_end_c).astype(bf16)
    row_i = lax.broadcasted_iota(jnp.int32, (L, L), 0)
    col_i = lax.broadcasted_iota(jnp.int32, (L, L), 1)
    lane_head = lax.broadcasted_iota(jnp.int32, (L, GW), 1) // SSD_HEADDIM
    ys = []
    for g in range(SSD_GROUPS):
        bm = xbc[:, SSD_D_INNER + g * N:SSD_D_INNER + (g + 1) * N]
        cm = xbc[:, SSD_D_INNER + SSD_GROUPS * N + g * N:SSD_D_INNER + SSD_GROUPS * N + (g + 1) * N].astype(bf16)
        bmT = bm.T.astype(bf16)
        cb = jnp.dot(cm, bmT, preferred_element_type=f32)
        xd_g = xd_bf[:, g * GW:(g + 1) * GW]
        y_g = jnp.zeros((L, GW), f32)
        for r in range(SSD_HEADS_PER_GROUP):
            hd = g * SSD_HEADS_PER_GROUP + r
            seg = a_cs[:, hd:hd + 1] - a_csT[hd:hd + 1, :]
            decay = jnp.where(row_i >= col_i, jnp.exp(seg), 0.0)
            xr = jnp.where(lane_head == r, xd_g, jnp.zeros_like(xd_g))
            y_g = y_g + jnp.dot((cb * decay).astype(bf16), xr, preferred_element_type=f32)
        hT = state_scr[g]
        y_off = jnp.dot(cm, hT.astype(bf16), preferred_element_type=f32) * grow_c[:, g * GW:(g + 1) * GW]
        new_states = jnp.dot(bmT, xe_bf[:, g * GW:(g + 1) * GW], preferred_element_type=f32)
        state_scr[g] = hT * end_c[:, g * GW:(g + 1) * GW] + new_states
        ys.append(y_g + y_off)
    y = jnp.concatenate(ys, axis=1) + x * dskip_c

    z = zx_ref[:, :SSD_D_INNER]
    yz = y * (z * jax.nn.sigmoid(z))
    parts = []
    for g in range(SSD_GROUPS):
        t = yz[:, g * GW:(g + 1) * GW]
        parts.append(t * lax.rsqrt(jnp.mean(t * t, axis=-1, keepdims=True) + NORM_EPS))
    yn = jnp.concatenate(parts, axis=1) * gn_ref[...]
    o_ref[...] = h_ref[...] + jnp.dot(yn.astype(bf16), wout_ref[...], preferred_element_type=f32)


def ssd_mixer_residual(h, norm_g, w_in, conv_w, conv_b, dt_bias, a_log, d_skip, gnorm_g, w_out):
    b, s, d = h.shape
    L = SSD_CHUNK
    nc = s // L
    w_in_p = jnp.pad(w_in, ((0, 0), (0, SSD_IN_PAD - SSD_IN_DIM)))
    zx = norm_matmul(h.reshape(b * s, d), norm_g, w_in_p, normalize=True, tm=256)
    lane_pad = lambda v: jnp.pad(v.reshape(1, -1), ((0, 0), (0, V7X_LANES - v.shape[-1])))
    tri = jnp.asarray(np.tril(np.ones((L, L))), jnp.bfloat16)
    head_of = np.arange(SSD_D_INNER) // SSD_HEADDIM
    expand = jnp.asarray(np.arange(V7X_LANES)[:, None] == head_of[None, :], jnp.bfloat16)
    const = lambda *shape: pl.BlockSpec(shape, lambda bi, ci: (0,) * len(shape))
    out = pl.pallas_call(
        _ssd_kernel,
        grid=(b, nc),
        in_specs=[pl.BlockSpec((L, SSD_IN_PAD), lambda bi, ci: (bi * nc + ci, 0)),
                  pl.BlockSpec((L, d), lambda bi, ci: (bi * nc + ci, 0)),
                  const(SSD_CONV, SSD_CONV_DIM), const(1, SSD_CONV_DIM),
                  const(1, V7X_LANES), const(1, V7X_LANES), const(1, V7X_LANES),
                  const(1, SSD_D_INNER), const(SSD_D_INNER, d), const(L, L), const(V7X_LANES, SSD_D_INNER)],
        out_specs=pl.BlockSpec((L, d), lambda bi, ci: (bi * nc + ci, 0)),
        out_shape=jax.ShapeDtypeStruct((b * s, d), jnp.float32),
        scratch_shapes=[pltpu.VMEM((L + SSD_TAIL, SSD_CONV_DIM), jnp.float32),
                        pltpu.VMEM((SSD_GROUPS, SSD_STATE, SSD_GROUP_W), jnp.float32)],
        compiler_params=pltpu.CompilerParams(dimension_semantics=("arbitrary", "arbitrary"),
                                             vmem_limit_bytes=V7X_VMEM_LIMIT_BYTES),
        name="ssd_mixer",
    )(zx, h.reshape(b * s, d), conv_w, conv_b.reshape(1, -1), lane_pad(dt_bias), lane_pad(a_log), lane_pad(d_skip),
      gnorm_g.reshape(1, -1), w_out.astype(jnp.bfloat16), tri, expand)
    return out.reshape(b, s, d)


def _mlp_kernel(x_ref, w1_ref, w2_ref, o_ref):
    hid = jnp.dot(x_ref[...].astype(jnp.bfloat16), w1_ref[...], preferred_element_type=jnp.float32)
    hid = jax.nn.gelu(hid)
    o_ref[...] = jnp.dot(hid.astype(jnp.bfloat16), w2_ref[...], preferred_element_type=jnp.float32)


def mlp2(x, w1, w2, *, tm=512):
    m, k = x.shape
    hdim, n = w2.shape
    assert m % tm == 0
    return pl.pallas_call(
        _mlp_kernel,
        grid=(m // tm,),
        in_specs=[pl.BlockSpec((tm, k), lambda i: (i, 0)),
                  pl.BlockSpec((k, hdim), lambda i: (0, 0)),
                  pl.BlockSpec((hdim, n), lambda i: (0, 0))],
        out_specs=pl.BlockSpec((tm, n), lambda i: (i, 0)),
        out_shape=jax.ShapeDtypeStruct((m, n), jnp.float32),
        compiler_params=pltpu.CompilerParams(dimension_semantics=("arbitrary",),
                                             vmem_limit_bytes=V7X_VMEM_LIMIT_BYTES),
        name="cmp_mlp",
    )(x, w1.astype(jnp.bfloat16), w2.astype(jnp.bfloat16))


def compress_blocks(t, pos, w1, w2):
    b, s, g, d = t.shape
    halves = t.transpose(0, 2, 1, 3).reshape(b, g, s // CMP_STRIDE, CMP_STRIDE, d)
    nxt = jnp.concatenate([halves[:, :, 1:], jnp.zeros_like(halves[:, :, :1])], axis=2)
    blocks = jnp.concatenate([halves, nxt], axis=3) + pos
    flat = blocks.reshape(b * g * (s // CMP_STRIDE), CMP_BLOCK * d)
    return mlp2(flat, w1, w2).reshape(b, g, s // CMP_STRIDE, d)


N_SEL = SEQ // SLC_BLOCK
N_KT = SEQ // Q_BLOCK
CMP_FRONT = 256
CMP_ROWS = CMP_FRONT + SEQ // CMP_STRIDE
CMP_WIN = 256
REL_TABLE = 4096
NEAR_TILES = 24
WIN_TILES = WINDOW // Q_BLOCK + 1
SEL_GROUP = 8
QL = NSA_GROUP * Q_BLOCK


def _nsa_kernel(qT_ref, gT_ref, kc_ref, vcT_ref, kk_ref, vsT_ref, vwT_ref, at_ref, bcT_ref, bT_ref, bTw_ref,
                o_ref, s_scr, sel_scr):
    f32, bf16 = jnp.float32, jnp.bfloat16
    i = pl.program_id(2)
    qT = (qT_ref[0, 0, 0] * (NSA_HEAD_DIM ** -0.5)).astype(bf16)
    zq = jnp.zeros_like(qT)
    q_sel = jnp.concatenate([qT, zq], axis=0)
    q_win = jnp.concatenate([zq, qT], axis=0)

    r0 = pl.multiple_of(8 * i + 8, 8)
    s = jnp.dot(kc_ref[0, 0], qT, preferred_element_type=f32)
    rio = lax.broadcasted_iota(jnp.int32, (CMP_ROWS, Q_BLOCK), 0)
    rowmask = jnp.where((rio >= CMP_FRONT) & (rio < r0 + CMP_WIN), 0.0, NEG_INF)
    s_scr[...] = s + jnp.concatenate([rowmask] * NSA_GROUP, axis=1)
    s_scr[pl.ds(r0, CMP_WIN), :] = s_scr[pl.ds(r0, CMP_WIN), :] + bcT_ref[0]
    s = s_scr[...]
    m = jnp.max(s, axis=0, keepdims=True)
    p = jnp.exp(s - m)
    l = jnp.sum(p, axis=0, keepdims=True)
    inv = jnp.where(m > 0.1 * NEG_INF, 1.0 / l, 0.0)
    pn = p * inv
    o_c = jnp.dot(vcT_ref[0, 0], pn.astype(bf16), preferred_element_type=f32)

    psum = pn[:, 0:Q_BLOCK]
    for r in range(1, NSA_GROUP):
        psum = psum + pn[:, r * Q_BLOCK:(r + 1) * Q_BLOCK]
    hi = psum.astype(bf16)
    lo = (psum - hi.astype(f32)).astype(bf16)
    imp = (jnp.dot(at_ref[...], hi, preferred_element_type=f32)
           + jnp.dot(at_ref[...], lo, preferred_element_type=f32))
    blk = lax.broadcasted_iota(jnp.int32, (N_SEL, Q_BLOCK), 0)
    qi = lax.broadcasted_iota(jnp.int32, (N_SEL, Q_BLOCK), 1)
    cur = 2 * i + (qi >= SLC_BLOCK).astype(jnp.int32)
    forced = (blk == 0) | (blk == cur) | (blk == cur - 1)
    valid = blk <= cur
    score = jnp.where(forced, SLC_FORCE, jnp.where(valid, imp, -SLC_FORCE))
    blkf = blk.astype(f32)
    chosen = jnp.zeros((N_SEL, Q_BLOCK), f32)
    for _ in range(SLC_TOPN):
        mx = jnp.max(score, axis=0, keepdims=True)
        first = jnp.min(jnp.where(score == mx, blkf, float(N_SEL)), axis=0, keepdims=True)
        hit = blkf == first
        chosen = jnp.where(hit, 1.0, chosen)
        score = jnp.where(hit, -jnp.inf, score)
    sel_scr[...] = jnp.where((chosen > 0.5) & valid, 0.0, NEG_INF)

    def attend(kt, carry, q_aug, vT_ref, bias):
        m_run, l_run, acc = carry
        scs = [jnp.dot(kk_ref[0, 0, k], q_aug, preferred_element_type=f32) + b for k, b in zip(kt, bias)]
        for k, sc in zip(kt, scs):
            m_new = jnp.maximum(m_run, jnp.max(sc, axis=0, keepdims=True))
            alpha = jnp.exp(m_run - m_new)
            pt = jnp.exp(sc - m_new)
            l_run = alpha * l_run + jnp.sum(pt, axis=0, keepdims=True)
            acc = alpha * acc + jnp.dot(vT_ref[0, 0, k], pt.astype(bf16), preferred_element_type=f32)
            m_run = m_new
        return m_run, l_run, acc

    def sel_mask(kt):
        m0 = jnp.broadcast_to(sel_scr[pl.ds(2 * kt, 1), :], (SLC_BLOCK, Q_BLOCK))
        m1 = jnp.broadcast_to(sel_scr[pl.ds(2 * kt + 1, 1), :], (SLC_BLOCK, Q_BLOCK))
        mk = jnp.concatenate([m0, m1], axis=0)
        return jnp.concatenate([mk] * NSA_GROUP, axis=1)

    init = (jnp.full((1, QL), NEG_INF, f32), jnp.zeros((1, QL), f32), jnp.zeros((NSA_HEAD_DIM, QL), f32))
    U = SEL_GROUP

    def far_group(g, c):
        kts = [g * U + u for u in range(U)]
        return attend(kts, c, q_sel, vsT_ref, [sel_mask(k) for k in kts])

    def near_group(g, c):
        kts, biases = [], []
        for u in range(U):
            delta = i - (g * U + u)
            k = jnp.minimum(g * U + u, i)
            tile = jnp.where(delta < 0, NEAR_TILES + 1, jnp.minimum(delta, NEAR_TILES))
            kts.append(k)
            biases.append(sel_mask(k) + bT_ref[0, tile])
        return attend(kts, c, q_sel, vsT_ref, biases)

    n_far = jnp.maximum((i - (NEAR_TILES - 1)) // U, 0)
    carry = lax.fori_loop(0, n_far, far_group, init)
    _, l_s, acc_s = lax.fori_loop(n_far, i // U + 1, near_group, carry)
    o_s = acc_s / l_s

    kts, biases = [], []
    for u in range(WIN_TILES):
        k = i - (WIN_TILES - 1) + u
        kts.append(jnp.maximum(k, 0))
        biases.append(bTw_ref[0, jnp.where(k >= 0, WIN_TILES - 1 - u, WIN_TILES)])
    _, l_w, acc_w = attend(kts, init, q_win, vwT_ref, biases)
    o_w = acc_w / l_w

    gate = jax.nn.sigmoid(gT_ref[0, 0, 0])
    o_ref[0, 0, 0] = gate[0:1] * o_c + gate[1:2] * o_s + gate[2:3] * o_w


BIAS_EXT = REL_TABLE + 2 * Q_BLOCK


CMP_OFF = CMP_STRIDE * (CMP_WIN - 8) - (CMP_BLOCK - 1)


def _bias_tile_kernel(bd_ref, bdc_ref, cmp_ref, near_ref, win_ref):
    row = bd_ref[0]
    kj = lax.broadcasted_iota(jnp.int32, (Q_BLOCK, Q_BLOCK), 0)
    qi = lax.broadcasted_iota(jnp.int32, (Q_BLOCK, Q_BLOCK), 1)
    for t in range(NEAR_TILES):
        seg = jnp.broadcast_to(row[:, t * Q_BLOCK:(t + 2) * Q_BLOCK], (Q_BLOCK, 2 * Q_BLOCK))
        tile = pltpu.roll(seg, 0, 1, stride=1, stride_axis=0)[:, Q_BLOCK:]
        near_ref[0, t] = tile
        if t < WIN_TILES:
            win_ref[0, t] = jnp.where(t * Q_BLOCK + qi - kj < WINDOW, tile, NEG_INF)
    near_ref[0, NEAR_TILES] = jnp.zeros((Q_BLOCK, Q_BLOCK), jnp.float32)
    near_ref[0, NEAR_TILES + 1] = jnp.full((Q_BLOCK, Q_BLOCK), NEG_INF, jnp.float32)
    win_ref[0, WIN_TILES] = jnp.full((Q_BLOCK, Q_BLOCK), NEG_INF, jnp.float32)
    full = jnp.broadcast_to(bdc_ref[0], (CMP_WIN, BIAS_EXT))
    cmp_ref[0] = pltpu.roll(full, 0, 1, stride=CMP_STRIDE, stride_axis=0)[:, REL_TABLE:REL_TABLE + Q_BLOCK]


def _bias_tables(rel_bias):
    G, R = NSA_KV_HEADS, NSA_GROUP
    bd = rel_bias[rel_bucket(jnp.arange(REL_TABLE))]
    bd = (bd - bd[REL_TABLE - 1]).T
    def padded(front):
        pad = lambda n: jnp.full((NSA_HEADS, n), NEG_INF, jnp.float32)
        return jnp.concatenate([pad(front), bd, pad(BIAS_EXT - REL_TABLE - front)], axis=1).reshape(NSA_HEADS, 1, BIAS_EXT)

    bd_ext, bd_cmp = padded(Q_BLOCK), padded(REL_TABLE - CMP_OFF)
    head = lambda *blk: pl.BlockSpec((1,) + blk + (Q_BLOCK,), lambda hd: (hd // R,) + (0,) * len(blk) + (hd % R,))
    return pl.pallas_call(
        _bias_tile_kernel,
        grid=(NSA_HEADS,),
        in_specs=[pl.BlockSpec((1, 1, BIAS_EXT), lambda hd: (hd, 0, 0))] * 2,
        out_specs=[head(CMP_WIN), head(NEAR_TILES + 2, Q_BLOCK), head(WIN_TILES + 1, Q_BLOCK)],
        out_shape=[jax.ShapeDtypeStruct((G, CMP_WIN, QL), jnp.float32),
                   jax.ShapeDtypeStruct((G, NEAR_TILES + 2, Q_BLOCK, QL), jnp.float32),
                   jax.ShapeDtypeStruct((G, WIN_TILES + 1, Q_BLOCK, QL), jnp.float32)],
        compiler_params=pltpu.CompilerParams(dimension_semantics=("arbitrary",),
                                             vmem_limit_bytes=V7X_VMEM_LIMIT_BYTES),
        name="nsa_bias_tiles",
    )(bd_ext, bd_cmp)


def _pool_matrix():
    per = SLC_BLOCK // CMP_STRIDE
    n_span = CMP_BLOCK // CMP_STRIDE
    k = np.arange(CMP_ROWS)[None, :] - CMP_FRONT
    j = np.arange(N_SEL)[:, None]
    return jnp.asarray((k >= per * j - (n_span - 1)) & (k <= per * j + per - 1), jnp.bfloat16)


def nsa_shared_kv(h, kv_norm_g, w_kv, cmp_pos_k, cmp_pos_v, cmp_w1_k, cmp_w2_k, cmp_w1_v, cmp_w2_v):
    b, s, _ = h.shape
    G, DH = NSA_KV_HEADS, NSA_HEAD_DIM
    bf16 = jnp.bfloat16
    kv = _mm(h, kv_norm_g, w_kv, True).reshape(b, s, 6, G, DH)
    k_cmp = compress_blocks(kv[:, :, 0], cmp_pos_k, cmp_w1_k, cmp_w2_k)
    v_cmp = compress_blocks(kv[:, :, 1], cmp_pos_v, cmp_w1_v, cmp_w2_v)
    front = jnp.zeros((b, G, CMP_FRONT, DH), bf16)
    kc = jnp.concatenate([front, k_cmp.astype(bf16)], axis=2)
    vcT = jnp.concatenate([front, v_cmp.astype(bf16)], axis=2).transpose(0, 1, 3, 2)
    kk = jnp.concatenate([kv[:, :, 2], kv[:, :, 4]], axis=-1).astype(bf16)
    kk = kk.transpose(0, 2, 1, 3).reshape(b, G, N_KT, Q_BLOCK, 2 * DH)

    def vt(v):
        return v.astype(bf16).transpose(0, 2, 1, 3).reshape(b, G, N_KT, Q_BLOCK, DH).transpose(0, 1, 2, 4, 3)

    return kc, vcT, kk, vt(kv[:, :, 3]), vt(kv[:, :, 5])


def nsa_mixer(h, norm_g, w_in, w_out, rel_bias, kc, vcT, kk, vsT, vwT):
    b, s, _ = h.shape
    G, R, DH = NSA_KV_HEADS, NSA_GROUP, NSA_HEAD_DIM
    nb = s // Q_BLOCK
    proj = _mm(h, norm_g, w_in, True)
    qT = proj[..., :NSA_HEADS * DH].reshape(b, nb, Q_BLOCK, G, R, DH).transpose(0, 3, 1, 5, 4, 2)
    qT = qT.reshape(b, G, nb, DH, QL)
    gT = proj[..., NSA_HEADS * DH:].reshape(b, nb, Q_BLOCK, G, R, 3).transpose(0, 3, 1, 5, 4, 2)
    gT = gT.reshape(b, G, nb, 3, QL)
    b_cmp, b_near, b_win = _bias_tables(rel_bias)
    at = _pool_matrix()

    per_bg = lambda *blk: pl.BlockSpec((1, 1) + blk, lambda bi, gi, i: (bi, gi) + (0,) * len(blk))
    per_g = lambda *blk: pl.BlockSpec((1,) + blk, lambda bi, gi, i: (gi,) + (0,) * len(blk))
    oT = pl.pallas_call(
        _nsa_kernel,
        grid=(b, G, nb),
        in_specs=[
            pl.BlockSpec((1, 1, 1, DH, QL), lambda bi, gi, i: (bi, gi, i, 0, 0)),
            pl.BlockSpec((1, 1, 1, 3, QL), lambda bi, gi, i: (bi, gi, i, 0, 0)),
            per_bg(CMP_ROWS, DH),
            per_bg(DH, CMP_ROWS),
            per_bg(N_KT, Q_BLOCK, 2 * DH),
            per_bg(N_KT, DH, Q_BLOCK),
            per_bg(N_KT, DH, Q_BLOCK),
            pl.BlockSpec((N_SEL, CMP_ROWS), lambda bi, gi, i: (0, 0)),
            per_g(CMP_WIN, QL),
            per_g(NEAR_TILES + 2, Q_BLOCK, QL),
            per_g(WIN_TILES + 1, Q_BLOCK, QL),
        ],
        out_specs=pl.BlockSpec((1, 1, 1, DH, QL), lambda bi, gi, i: (bi, gi, i, 0, 0)),
        out_shape=jax.ShapeDtypeStruct((b, G, nb, DH, QL), jnp.float32),
        scratch_shapes=[pltpu.VMEM((CMP_ROWS, QL), jnp.float32), pltpu.VMEM((N_SEL, Q_BLOCK), jnp.float32)],
        compiler_params=pltpu.CompilerParams(
            dimension_semantics=("arbitrary", "arbitrary", "arbitrary"),
            vmem_limit_bytes=V7X_VMEM_LIMIT_BYTES),
        name="nsa_attention",
    )(qT, gT, kc, vcT, kk, vsT, vwT, at, b_cmp, b_near, b_win)
    o = oT.reshape(b, G, nb, DH, R, Q_BLOCK).transpose(0, 2, 5, 1, 4, 3).reshape(b, s, NSA_HEADS * DH)
    return matmul_residual(o.reshape(b * s, NSA_HEADS * DH), h.reshape(b * s, -1), w_out).reshape(h.shape)


PEER_SLOTS = PEER_HEADS * PEER_TOPK
PEER_ROUTE_TM = 256
PEER_ROUTE_LOCKSTEP = 2
PEER_TT = 64
PEER_CHUNKS = D_MODEL // V7X_LANES
PEER_WORDS = PEER_CHUNKS // 2
PEER_TILE_PITCH = PEER_SLOTS + 8


def _top_rows(arrays, rowid, n, payloads=None):
    arrays = list(arrays)
    vals = [[] for _ in arrays]
    picks = [[] for _ in arrays]
    for _ in range(n):
        for a, s in enumerate(arrays):
            mx = jnp.max(s, axis=0, keepdims=True)
            first = jnp.min(jnp.where(s == mx, rowid, jnp.inf), axis=0, keepdims=True)
            hit = rowid == first
            vals[a].append(mx)
            picks[a].append(first if payloads is None
                            else jnp.max(jnp.where(hit, payloads[a], -1.0), axis=0, keepdims=True))
            arrays[a] = jnp.where(hit, -jnp.inf, s)
    return vals, picks


def _peer_route_kernel(h_ref, g_ref, wqT_ref, k1_ref, k2_ref, xn_ref, ids_ref, gate_ref):
    f32, bf16 = jnp.float32, jnp.bfloat16
    x = h_ref[...]
    xn = x * lax.rsqrt(jnp.mean(x * x, axis=-1, keepdims=True) + NORM_EPS) * g_ref[...]
    xn_ref[...] = xn
    qT = lax.dot_general(wqT_ref[...], xn.astype(bf16), (((1,), (1,)), ((), ())), preferred_element_type=f32)
    half = PEER_QDIM // 2
    tm = x.shape[0]
    key_id = lax.broadcasted_iota(jnp.int32, (PEER_N_KEYS, tm), 0).astype(f32)
    stair = [PEER_TOPK // (a + 1) for a in range(PEER_TOPK)]
    n_pad = -sum(stair) % 8
    cand_pos = jnp.concatenate(
        [lax.broadcasted_iota(jnp.int32, (nb, tm), 0).astype(f32) + float(a * PEER_TOPK) for a, nb in enumerate(stair)]
        + [jnp.full((n_pad, tm), float(PEER_TOPK * PEER_TOPK), f32)], axis=0)
    for h0 in range(0, PEER_HEADS, PEER_ROUTE_LOCKSTEP):
        heads = range(h0, h0 + PEER_ROUTE_LOCKSTEP)
        scores = []
        for hd in heads:
            q1 = qT[hd * PEER_QDIM: hd * PEER_QDIM + half].astype(bf16)
            q2 = qT[hd * PEER_QDIM + half: (hd + 1) * PEER_QDIM].astype(bf16)
            scores.append(jnp.dot(k1_ref[hd], q1, preferred_element_type=f32))
            scores.append(jnp.dot(k2_ref[hd], q2, preferred_element_type=f32))
        vals, idxs = _top_rows(scores, key_id, PEER_TOPK)
        cands, cand_ids = [], []
        for a in range(PEER_ROUTE_LOCKSTEP):
            v1, i1 = vals[2 * a], idxs[2 * a]
            v2m = jnp.concatenate(vals[2 * a + 1], axis=0)
            i2m = jnp.concatenate(idxs[2 * a + 1], axis=0)
            cands.append(jnp.concatenate([v1[r] + v2m[0:nb] for r, nb in enumerate(stair)]
                                         + [jnp.full((n_pad, tm), -jnp.inf, f32)], axis=0))
            cand_ids.append(jnp.concatenate([i1[r] * float(PEER_N_KEYS) + i2m[0:nb] for r, nb in enumerate(stair)]
                                            + [jnp.full((n_pad, tm), -1.0, f32)], axis=0))
        tops, ids = _top_rows(cands, cand_pos, PEER_TOPK, payloads=cand_ids)
        for a, hd in enumerate(heads):
            top_s = jnp.concatenate(tops[a], axis=0)
            e = jnp.exp(top_s - top_s[0:1])
            gate_ref[hd * PEER_TOPK:(hd + 1) * PEER_TOPK, :] = e / jnp.sum(e, axis=0, keepdims=True)
            ids_ref[hd * PEER_TOPK:(hd + 1) * PEER_TOPK, :] = (jnp.concatenate(ids[a], axis=0)
                                                               * PEER_WORDS).astype(jnp.int32)


def peer_route(h2, norm_g, w_q, keys1, keys2):
    T, d = h2.shape
    tm = PEER_ROUTE_TM
    bf16 = jnp.bfloat16
    nq = PEER_HEADS * PEER_QDIM
    half = PEER_QDIM // 2
    return pl.pallas_call(
        _peer_route_kernel,
        grid=(T // tm,),
        in_specs=[pl.BlockSpec((tm, d), lambda i: (i, 0)),
                  pl.BlockSpec((1, d), lambda i: (0, 0)),
                  pl.BlockSpec((nq, d), lambda i: (0, 0)),
                  pl.BlockSpec((PEER_HEADS, PEER_N_KEYS, half), lambda i: (0, 0, 0)),
                  pl.BlockSpec((PEER_HEADS, PEER_N_KEYS, half), lambda i: (0, 0, 0))],
        out_specs=[pl.BlockSpec((tm, d), lambda i: (i, 0)),
                   pl.BlockSpec((PEER_SLOTS, tm), lambda i: (0, i)),
                   pl.BlockSpec((PEER_SLOTS, tm), lambda i: (0, i))],
        out_shape=[jax.ShapeDtypeStruct((T, d), jnp.float32),
                   jax.ShapeDtypeStruct((PEER_SLOTS, T), jnp.int32),
                   jax.ShapeDtypeStruct((PEER_SLOTS, T), jnp.float32)],
        compiler_params=pltpu.CompilerParams(dimension_semantics=("arbitrary",),
                                             vmem_limit_bytes=V7X_VMEM_LIMIT_BYTES),
        name="peer_route",
    )(h2, norm_g.reshape(1, d), w_q.T.astype(bf16), keys1.astype(bf16), keys2.astype(bf16))


def _unpack_pair(words):
    lo = lax.bitcast_convert_type(lax.shift_left(words, 16), jnp.float32)
    hi = lax.bitcast_convert_type(lax.bitwise_and(words, -65536), jnp.float32)
    return lo, hi


def _pack_rows(tab):
    bits = lax.bitcast_convert_type(tab.astype(jnp.bfloat16), jnp.uint16).astype(jnp.uint32)
    bits = bits.reshape(tab.shape[0], PEER_WORDS, 2, V7X_LANES)
    words = lax.bitcast_convert_type(bits[:, :, 0] | (bits[:, :, 1] << 16), jnp.int32)
    return words.reshape(tab.shape[0] * PEER_WORDS, V7X_LANES)


def _peer_gather_token(ids_ref, tab, tile, t):
    for k in range(PEER_SLOTS):
        row = pl.multiple_of(ids_ref[t * PEER_SLOTS + k], PEER_WORDS)
        tile[pl.ds(k, PEER_WORDS, stride=PEER_TILE_PITCH), :] = tab[pl.ds(row, PEER_WORDS), :]


def _peer_chunk_pair(tile, j):
    return _unpack_pair(tile[j * PEER_TILE_PITCH:j * PEER_TILE_PITCH + PEER_SLOTS, :])


def _peer_load_table(tab_hbm, tab, sem):
    @pl.when(pl.program_id(0) == 0)
    def _():
        copy = pltpu.make_async_copy(tab_hbm, tab, sem)
        copy.start()
        copy.wait()


def _peer_act_kernel(ids_ref, xn_ref, gt_ref, tab_hbm, w_ref, tab, tile_a, tile_b, sem):
    _peer_load_table(tab_hbm, tab, sem)
    for t in range(PEER_TT):
        tile = (tile_a, tile_b)[t % 2]
        _peer_gather_token(ids_ref, tab, tile, t)
        xrow = xn_ref[t:t + 1, :]
        acc = jnp.zeros((PEER_SLOTS, V7X_LANES), jnp.float32)
        for j in range(PEER_WORDS):
            lo, hi = _peer_chunk_pair(tile, j)
            acc = acc + lo * xrow[:, 2 * j * V7X_LANES:(2 * j + 1) * V7X_LANES]
            acc = acc + hi * xrow[:, (2 * j + 1) * V7X_LANES:(2 * j + 2) * V7X_LANES]
        act = jnp.sum(acc, axis=1, keepdims=True)
        w_ref[0, :, t:t + 1] = gt_ref[0, :, t:t + 1] * jax.nn.gelu(act)


def _peer_out_kernel(ids_ref, w_ref, h_ref, tab_hbm, o_ref, tab, tile_a, tile_b, sem):
    _peer_load_table(tab_hbm, tab, sem)
    for t in range(PEER_TT):
        tile = (tile_a, tile_b)[t % 2]
        _peer_gather_token(ids_ref, tab, tile, t)
        w = w_ref[0, :, t:t + 1]
        for j in range(PEER_WORDS):
            for c, part in zip((2 * j, 2 * j + 1), _peer_chunk_pair(tile, j)):
                lanes = slice(c * V7X_LANES, (c + 1) * V7X_LANES)
                o_ref[t:t + 1, lanes] = h_ref[t:t + 1, lanes] + jnp.sum(part * w, axis=0, keepdims=True)


def peer_experts(xn, idsT, gateT, u_tab, v_tab, h2):
    T, d = xn.shape
    tt = PEER_TT
    nt = T // tt
    ids = idsT.T.reshape(T * PEER_SLOTS)
    gt = gateT.reshape(PEER_SLOTS, nt, tt).transpose(1, 0, 2)
    ids_spec = pl.BlockSpec((tt * PEER_SLOTS,), lambda i: (i,), memory_space=pltpu.SMEM)
    col_spec = pl.BlockSpec((1, PEER_SLOTS, tt), lambda i: (i, 0, 0))
    row_spec = pl.BlockSpec((tt, d), lambda i: (i, 0))
    tile = pltpu.VMEM((PEER_WORDS * PEER_TILE_PITCH, V7X_LANES), jnp.int32)
    scratch = [pltpu.VMEM((PEER_EXPERTS * PEER_WORDS, V7X_LANES), jnp.int32), tile, tile, pltpu.SemaphoreType.DMA(())]
    params = pltpu.CompilerParams(dimension_semantics=("arbitrary",), vmem_limit_bytes=V7X_VMEM_LIMIT_BYTES)
    w = pl.pallas_call(
        _peer_act_kernel,
        grid=(nt,),
        in_specs=[ids_spec, row_spec, col_spec, pl.BlockSpec(memory_space=pl.ANY)],
        out_specs=col_spec,
        out_shape=jax.ShapeDtypeStruct((nt, PEER_SLOTS, tt), jnp.float32),
        scratch_shapes=scratch, compiler_params=params, name="peer_act",
    )(ids, xn, gt, _pack_rows(u_tab))
    return pl.pallas_call(
        _peer_out_kernel,
        grid=(nt,),
        in_specs=[ids_spec, col_spec, row_spec, pl.BlockSpec(memory_space=pl.ANY)],
        out_specs=row_spec,
        out_shape=jax.ShapeDtypeStruct((T, d), jnp.float32),
        scratch_shapes=scratch, compiler_params=params, name="peer_out",
    )(ids, w, h2, _pack_rows(v_tab))


def peer_ffn_residual(h, norm_g, w_q, keys1, keys2, u_tab, v_tab):
    b, s, d = h.shape
    h2 = h.reshape(b * s, d)
    xn, idsT, gateT = peer_route(h2, norm_g, w_q, keys1, keys2)
    return peer_experts(xn, idsT, gateT, u_tab, v_tab, h2).reshape(b, s, d)


def _ple_kernel(h_ref, p_ref, g_ref, wup_ref, wgate_ref, fg_ref, o_ref, *, final_norm):
    bf16, f32 = jnp.bfloat16, jnp.float32

    def rms(v, gain):
        return v * lax.rsqrt(jnp.mean(v * v, axis=-1, keepdims=True) + NORM_EPS) * gain

    h = h_ref[...]
    gate = jax.nn.sigmoid(jnp.dot(rms(h, g_ref[...]).astype(bf16), wgate_ref[...], preferred_element_type=f32))
    y = h + jnp.dot(p_ref[...].astype(bf16), wup_ref[...], preferred_element_type=f32) * gate
    o_ref[...] = rms(y, fg_ref[...]) if final_norm else y


def per_layer_embed_residual(h, p_i, norm_g, w_up, w_gate, final_g, *, final_norm, tm=512):
    b, s, d = h.shape
    pd = p_i.shape[-1]
    const = lambda *shape: pl.BlockSpec(shape, lambda i: (0,) * len(shape))
    out = pl.pallas_call(
        functools.partial(_ple_kernel, final_norm=final_norm),
        grid=(b * s // tm,),
        in_specs=[pl.BlockSpec((tm, d), lambda i: (i, 0)), pl.BlockSpec((tm, pd), lambda i: (i, 0)),
                  const(1, d), const(pd, d), const(d, d), const(1, d)],
        out_specs=pl.BlockSpec((tm, d), lambda i: (i, 0)),
        out_shape=jax.ShapeDtypeStruct((b * s, d), jnp.float32),
        compiler_params=pltpu.CompilerParams(dimension_semantics=("arbitrary",),
                                             vmem_limit_bytes=V7X_VMEM_LIMIT_BYTES),
        name="per_layer_embed",
    )(h.reshape(b * s, d), p_i.reshape(b * s, pd), norm_g.reshape(1, d), w_up.astype(jnp.bfloat16),
      w_gate.astype(jnp.bfloat16), final_g.reshape(1, d))
    return out.reshape(b, s, d)


def kernel(x, p, a_norm_g, a_w_in, a_conv_w, a_conv_b, a_dt_bias, a_log, a_d_skip, a_gnorm_g, a_w_out,
           kv_norm_g, w_kv, cmp_pos_k, cmp_pos_v, cmp_w1_k, cmp_w2_k, cmp_w1_v, cmp_w2_v, rel_bias,
           b_norm_g, b_w_in, b_w_out, c_norm_g, c_w_q, c_keys1, c_keys2, c_u, c_v,
           e_norm_g, e_w_up, e_w_gate, final_g):
    h = x
    shared = None
    for i in range(DEPTH):
        if i < N_A_LAYERS:
            h = ssd_mixer_residual(h, a_norm_g[i], a_w_in[i], a_conv_w[i], a_conv_b[i], a_dt_bias[i],
                                   a_log[i], a_d_skip[i], a_gnorm_g[i], a_w_out[i])
        else:
            if shared is None:
                shared = nsa_shared_kv(h, kv_norm_g, w_kv, cmp_pos_k, cmp_pos_v,
                                       cmp_w1_k, cmp_w2_k, cmp_w1_v, cmp_w2_v)
            j = i - N_A_LAYERS
            h = nsa_mixer(h, b_norm_g[j], b_w_in[j], b_w_out[j], rel_bias, *shared)
        h = peer_ffn_residual(h, c_norm_g[i], c_w_q[i], c_keys1[i], c_keys2[i], c_u[i], c_v[i])
        h = per_layer_embed_residual(h, p[i], e_norm_g[i], e_w_up[i], e_w_gate[i], final_g,
                                     final_norm=(i == DEPTH - 1))
    return h
```
